```python
import jax
import jax.numpy as jnp
from jax import lax
import numpy as np

D_MODEL = 1024
BATCH = 2
SEQ = 8192
DEPTH = 4

GRID_W = 64
CTX_LEN = 256

N_MIXERS = 3
N_LAYERS_A = (DEPTH + 2) // 3
N_LAYERS_B = (DEPTH + 1) // 3
N_LAYERS_C = DEPTH // 3

A_HEADS = 8
A_QK_DIM = D_MODEL // 2
A_V_DIM = D_MODEL
A_DK = A_QK_DIM // A_HEADS
A_DV = A_V_DIM // A_HEADS
A_CHUNK = 128

B_Q_HEADS = 16
B_KV_HEADS = 4
B_HEAD_DIM = D_MODEL // B_Q_HEADS
B_GROUP = B_Q_HEADS // B_KV_HEADS
B_Q_DIM = B_Q_HEADS * B_HEAD_DIM
B_KV_DIM = B_KV_HEADS * B_HEAD_DIM
B_WINDOW = 128
B_BLOCK = 128
ROPE_BASE = 10000.0

C_CONV_W = 3

MLP_HIDDEN = 4 * D_MODEL
NORM_EPS = 1e-6

kernel_name = 'hybrid_mlstm_swa_shortconv_dit'


def rmsnorm(x, g):
    xf = x.astype(jnp.float32)
    y = xf * lax.rsqrt(jnp.mean(xf * xf, axis=-1, keepdims=True) + NORM_EPS)
    return (y * g.astype(jnp.float32)).astype(x.dtype)


def modulate(x, shift, scale):
    return x * (1 + scale) + shift


def sq_relu_mlp(x, w1, w2):
    return jnp.square(jax.nn.relu(x @ w1)) @ w2


def axial_angles(rows):
    row = jnp.repeat(jnp.arange(rows, dtype=jnp.float32), GRID_W)
    col = jnp.tile(jnp.arange(GRID_W, dtype=jnp.float32), rows)
    n_freq = B_HEAD_DIM // 4
    inv_freq = ROPE_BASE ** (-jnp.arange(n_freq, dtype=jnp.float32) / n_freq)
    return row[:, None] * inv_freq, col[:, None] * inv_freq


def rope_half(x, ang):
    cos = jnp.cos(ang)[None, :, None, :]
    sin = jnp.sin(ang)[None, :, None, :]
    x1, x2 = jnp.split(x.astype(jnp.float32), 2, axis=-1)
    return jnp.concatenate([x1 * cos - x2 * sin, x2 * cos + x1 * sin], axis=-1)


def axial_rope(x, ang_row, ang_col):
    half = x.shape[-1] // 2
    y = jnp.concatenate([rope_half(x[..., :half], ang_row), rope_half(x[..., half:], ang_col)], axis=-1)
    return y.astype(x.dtype)


def sink_softmax(scores, sink):
    m = sink
    for s in scores:
        m = jnp.maximum(m, jnp.max(s, axis=-1))
    ps = [jnp.exp(s - m[..., None]) for s in scores]
    denom = jnp.exp(sink - m)
    for p in ps:
        denom = denom + jnp.sum(p, axis=-1)
    return [p / denom[..., None] for p in ps]


def mlstm_chunk_scan(q, k, v, log_i, log_f, state, with_output):
    n_chunks = q.shape[2] // A_CHUNK

    def chunks(a):
        a = a.reshape(a.shape[:2] + (n_chunks, A_CHUNK) + a.shape[3:])
        return jnp.moveaxis(a, 2, 0)

    tri = jnp.tril(jnp.ones((A_CHUNK, A_CHUNK), dtype=bool))

    def step(carry, inp):
        C, n, m = carry
        qc, kc, vc, li, lf = inp
        b = jnp.cumsum(lf, axis=-1)
        b_end = b[..., -1]
        w_end = b_end[..., None] - b + li
        m_new = jnp.maximum(b_end + m, jnp.max(w_end, axis=-1))
        decay = jnp.exp(b_end + m - m_new)
        ws = jnp.exp(w_end - m_new[..., None])
        C_new = decay[..., None, None] * C + jnp.einsum('bhs,bhsk,bhsv->bhkv', ws, kc, vc)
        n_new = decay[..., None] * n + jnp.einsum('bhs,bhsk->bhk', ws, kc)
        if with_output:
            log_d = jnp.where(tri, b[..., :, None] - b[..., None, :] + li[..., None, :], -jnp.inf)
            inter = b + m[..., None]
            m_comb = jnp.maximum(inter, jnp.max(log_d, axis=-1))
            d_mat = jnp.exp(log_d - m_comb[..., None])
            sc = jnp.einsum('bhtk,bhsk->bhts', qc, kc) * d_mat
            a_inter = jnp.exp(inter - m_comb)
            num = jnp.einsum('bhts,bhsv->bhtv', sc, vc) + a_inter[..., None] * jnp.einsum('bhtk,bhkv->bhtv', qc, C)
            den = jnp.sum(sc, axis=-1) + a_inter * jnp.einsum('bhtk,bhk->bht', qc, n)
            h = num / jnp.maximum(jnp.abs(den), jnp.exp(-m_comb))[..., None]
        else:
            h = None
        return (C_new, n_new, m_new), h

    state, hs = lax.scan(step, state, (chunks(q), chunks(k), chunks(v), chunks(log_i), chunks(log_f)))
    if with_output:
        hs = jnp.moveaxis(hs, 0, 2).reshape(q.shape[:3] + (v.shape[-1],))
    return state, hs


def mlstm_mixer(xc, xl, w_in, w_gate, b_gate, head_g, w_out, ctx_out):
    def project(x):
        bn, L, _ = x.shape
        q, k, v, o = jnp.split(x @ w_in, [A_QK_DIM, 2 * A_QK_DIM, 2 * A_QK_DIM + A_V_DIM], axis=-1)

        def heads(a, d):
            return a.reshape(bn, L, A_HEADS, d).transpose(0, 2, 1, 3).astype(jnp.float32)

        g = (x @ w_gate + b_gate).astype(jnp.float32).transpose(0, 2, 1)
        li_f, lf_f, li_b, lf_b = jnp.split(g, 4, axis=1)
        gates = ((li_f, jax.nn.log_sigmoid(lf_f)), (li_b, jax.nn.log_sigmoid(lf_b)))
        return heads(q, A_DK) * (A_DK ** -0.5), heads(k, A_DK), heads(v, A_DV), o, gates

    def flip(a):
        return jnp.flip(a, axis=2)

    def finish(h, o):
        h = h * lax.rsqrt(jnp.mean(h * h, axis=-1, keepdims=True) + NORM_EPS)
        bn, _, L, _ = h.shape
        h = h.transpose(0, 2, 1, 3).reshape(bn, L, A_V_DIM) * head_g.astype(jnp.float32)
        return (jax.nn.sigmoid(o) * h.astype(o.dtype)) @ w_out

    qc, kc, vc, oc, gc = project(xc)
    ql, kl, vl, ol, gl = project(xl)
    bn = xc.shape[0]
    init = (jnp.zeros((bn, A_HEADS, A_DK, A_DV), jnp.float32),
            jnp.zeros((bn, A_HEADS, A_DK), jnp.float32),
            jnp.full((bn, A_HEADS), -jnp.inf, jnp.float32))
    st_f, hc_f = mlstm_chunk_scan(qc, kc, vc, gc[0][0], gc[0][1], init, ctx_out)
    st_b, hc_b = mlstm_chunk_scan(flip(qc), flip(kc), flip(vc), flip(gc[1][0]), flip(gc[1][1]), init, ctx_out)
    _, hl_f = mlstm_chunk_scan(ql, kl, vl, gl[0][0], gl[0][1], st_f, True)
    _, hl_b = mlstm_chunk_scan(flip(ql), flip(kl), flip(vl), flip(gl[1][0]), flip(gl[1][1]), st_b, True)
    yl = finish(hl_f + flip(hl_b), ol)
    yc = finish(hc_f + flip(hc_b), oc) if ctx_out else None
    return yc, yl


def swa_mixer(xc, xl, w_qkv, sinks, w_out, ang_row, ang_col, ctx_out):
    scale = B_HEAD_DIM ** -0.5

    def project(x, rotate):
        bn, L, _ = x.shape
        q, k, v = jnp.split(x @ w_qkv, [B_Q_DIM, B_Q_DIM + B_KV_DIM], axis=-1)
        q = q.reshape(bn, L, B_Q_HEADS, B_HEAD_DIM)
        k = k.reshape(bn, L, B_KV_HEADS, B_HEAD_DIM)
        v = v.reshape(bn, L, B_KV_HEADS, B_HEAD_DIM)
        if rotate:
            q = axial_rope(q, ang_row, ang_col)
            k = axial_rope(k, ang_row, ang_col)
        return q.reshape(bn, L, B_KV_HEADS, B_GROUP, B_HEAD_DIM), k, v

    sink = sinks.reshape(B_KV_HEADS, B_GROUP).astype(jnp.float32)
    qc, kc, vc = project(xc, False)
    ql, kl, vl = project(xl, True)
    bn, S = xl.shape[:2]
    nb = S // B_BLOCK
    qb = ql.reshape(bn, nb, B_BLOCK, B_KV_HEADS, B_GROUP, B_HEAD_DIM)

    def band(a):
        ap = jnp.pad(a, ((0, 0), (B_BLOCK, B_BLOCK), (0, 0), (0, 0)))
        ap = ap.reshape(bn, nb + 2, B_BLOCK, B_KV_HEADS, B_HEAD_DIM)
        return jnp.concatenate([ap[:, :-2], ap[:, 1:-1], ap[:, 2:]], axis=2)

    kb, vb = band(kl), band(vl)
    blk = jnp.arange(nb)[:, None, None]
    qpos = blk * B_BLOCK + jnp.arange(B_BLOCK)[None, :, None]
    kpos = (blk - 1) * B_BLOCK + jnp.arange(3 * B_BLOCK)[None, None, :]
    mask = (jnp.abs(qpos - kpos) <= B_WINDOW) & (kpos >= 0) & (kpos < S)
    s_loc = jnp.einsum('bnqhgd,bnkhd->bnhgqk', qb, kb).astype(jnp.float32) * scale
    s_loc = jnp.where(mask[None, :, None, None], s_loc, -jnp.inf)
    s_ctx = jnp.einsum('bnqhgd,bchd->bnhgqc', qb, kc).astype(jnp.float32) * scale
    p_loc, p_ctx = sink_softmax([s_loc, s_ctx], sink[None, None, :, :, None])
    o_l = (jnp.einsum('bnhgqk,bnkhd->bnqhgd', p_loc.astype(vb.dtype), vb)
           + jnp.einsum('bnhgqc,bchd->bnqhgd', p_ctx.astype(vc.dtype), vc))
    yl = o_l.reshape(bn, S, D_MODEL) @ w_out
    if ctx_out:
        s_c = jnp.einsum('bqhgd,bkhd->bhgqk', qc, kc).astype(jnp.float32) * scale
        (p_c,) = sink_softmax([s_c], sink[None, :, :, None])
        o_c = jnp.einsum('bhgqk,bkhd->bqhgd', p_c.astype(vc.dtype), vc)
        yc = o_c.reshape(bn, xc.shape[1], D_MODEL) @ w_out
    else:
        yc = None
    return yc, yl


def shortconv_mixer(xc, xl, w_in, conv_w, conv_b, w_out, ctx_out):
    def run(x):
        bg, cg, xt = jnp.split(x @ w_in, 3, axis=-1)
        u = cg * xt
        u = lax.conv_general_dilated(
            u, conv_w[:, None, :].astype(u.dtype), window_strides=(1,),
            padding=((C_CONV_W // 2, C_CONV_W // 2),),
            dimension_numbers=('NWC', 'WIO', 'NWC'), feature_group_count=D_MODEL) + conv_b
        return (bg * u) @ w_out

    return (run(xc) if ctx_out else None), run(xl)


def setup_inputs(seed: int = 0) -> dict:
    key = jax.random.key(seed)
    ks = jax.random.split(key, 24)
    D = D_MODEL
    f32 = jnp.float32

    def nrm(k, shape, fan_in, mult=1.0):
        return jax.random.normal(k, shape, f32) * (mult * fan_in ** -0.5)

    gate_noise = jax.random.normal(ks[9], (N_LAYERS_A, 4, A_HEADS), f32)
    gate_center = jnp.array([0.0, 3.0, 0.0, 3.0], f32)[None, :, None]
    gate_spread = jnp.array([0.1, 0.5, 0.1, 0.5], f32)[None, :, None]
    a_b_gate = (gate_center + gate_spread * gate_noise).reshape(N_LAYERS_A, 4 * A_HEADS)
    return {
        'x': jax.random.normal(ks[0], (BATCH, SEQ, D), f32),
        'c': jax.random.normal(ks[1], (BATCH, D), f32),
        'ctx': jax.random.normal(ks[2], (BATCH, CTX_LEN, D), f32),
        'c_ctx': jax.random.normal(ks[3], (D,), f32),
        'ada_w': nrm(ks[4], (DEPTH, D, 6 * D), D, 0.5),
        'ada_b': 0.02 * jax.random.normal(ks[5], (DEPTH, 6 * D), f32),
        'norm_g': 1.0 + 0.05 * jax.random.normal(ks[6], (DEPTH, 2, D), f32),
        'final_g': 1.0 + 0.05 * jax.random.normal(ks[7], (D,), f32),
        'mlp_w1': nrm(ks[8], (DEPTH, D, MLP_HIDDEN), D),
        'mlp_w2': nrm(ks[10], (DEPTH, MLP_HIDDEN, D), MLP_HIDDEN),
        'a_w_in': nrm(ks[11], (N_LAYERS_A, D, 2 * A_QK_DIM + 2 * A_V_DIM), D),
        'a_w_gate': nrm(ks[12], (N_LAYERS_A, D, 4 * A_HEADS), D, 0.5),
        'a_b_gate': a_b_gate,
        'a_head_g': 1.0 + 0.05 * jax.random.normal(ks[13], (N_LAYERS_A, A_V_DIM), f32),
        'a_w_out': nrm(ks[14], (N_LAYERS_A, A_V_DIM, D), A_V_DIM),
        'b_w_qkv': nrm(ks[15], (N_LAYERS_B, D, B_Q_DIM + 2 * B_KV_DIM), D),
        'b_sinks': 0.5 * jax.random.normal(ks[16], (N_LAYERS_B, B_Q_HEADS), f32),
        'b_w_out': nrm(ks[17], (N_LAYERS_B, B_Q_DIM, D), B_Q_DIM),
        'c_w_in': nrm(ks[18], (N_LAYERS_C, D, 3 * D), D),
        'c_conv_w': nrm(ks[19], (N_LAYERS_C, C_CONV_W, D), C_CONV_W),
        'c_conv_b': 0.02 * jax.random.normal(ks[20], (N_LAYERS_C, D), f32),
        'c_w_out': nrm(ks[21], (N_LAYERS_C, D, D), D),
    }


def reference(x, c, ctx, c_ctx, ada_w, ada_b, norm_g, final_g, mlp_w1, mlp_w2,
              a_w_in, a_w_gate, a_b_gate, a_head_g, a_w_out,
              b_w_qkv, b_sinks, b_w_out, c_w_in, c_conv_w, c_conv_b, c_w_out):
    rows = x.shape[1] // GRID_W
    ang_row, ang_col = axial_angles(rows)
    silu_c = jax.nn.silu(c)
    silu_cc = jax.nn.silu(c_ctx)
    h, hc = x, ctx
    for l in range(DEPTH):
        kind, j = l % N_MIXERS, l // N_MIXERS
        ctx_out = l != DEPTH - 1
        ml = jnp.split(silu_c @ ada_w[l] + ada_b[l], 6, axis=-1)
        mc = jnp.split(silu_cc @ ada_w[l] + ada_b[l], 6, axis=-1)
        xl = modulate(rmsnorm(h, norm_g[l, 0]), ml[0][:, None], ml[1][:, None])
        need_ctx_input = ctx_out or kind != 2
        xc = modulate(rmsnorm(hc, norm_g[l, 0]), mc[0], mc[1]) if need_ctx_input else None
        if kind == 0:
            yc, yl = mlstm_mixer(xc, xl, a_w_in[j], a_w_gate[j], a_b_gate[j], a_head_g[j], a_w_out[j], ctx_out)
        elif kind == 1:
            yc, yl = swa_mixer(xc, xl, b_w_qkv[j], b_sinks[j], b_w_out[j], ang_row, ang_col, ctx_out)
        else:
            yc, yl = shortconv_mixer(xc, xl, c_w_in[j], c_conv_w[j], c_conv_b[j], c_w_out[j], ctx_out)
        h = h + ml[2][:, None] * yl
        xl = modulate(rmsnorm(h, norm_g[l, 1]), ml[3][:, None], ml[4][:, None])
        h = h + ml[5][:, None] * sq_relu_mlp(xl, mlp_w1[l], mlp_w2[l])
        if ctx_out:
            hc = hc + mc[2] * yc
            xc = modulate(rmsnorm(hc, norm_g[l, 1]), mc[3], mc[4])
            hc = hc + mc[5] * sq_relu_mlp(xc, mlp_w1[l], mlp_w2[l])
    return rmsnorm(h, final_g)
```

```python
import functools

import jax
import jax.numpy as jnp
from jax import lax
from jax.experimental import pallas as pl
from jax.experimental.pallas import tpu as pltpu

D_MODEL = 1024
GRID_W = 64
N_MIXERS = 3
A_HEADS = 8
A_QK_DIM = D_MODEL // 2
A_DK = A_QK_DIM // A_HEADS
A_DV = D_MODEL // A_HEADS
A_CHUNK = 128
B_Q_HEADS = 16
B_KV_HEADS = 4
B_HEAD_DIM = D_MODEL // B_Q_HEADS
B_GROUP = B_Q_HEADS // B_KV_HEADS
B_KV_DIM = B_KV_HEADS * B_HEAD_DIM
B_BLOCK = 128
ROPE_BASE = 10000.0
MLP_HIDDEN = 4 * D_MODEL
NORM_EPS = 1e-6

LANES = 128
SUBLANES = 8
VMEM_LIMIT_BYTES = 56 * 1024 * 1024

ROW_TILE = 512
MLP_HIDDEN_CHUNK = 1024
MOD_ROWS = 8
ADA_COL_TILE = 1536

_BF16 = jnp.bfloat16
_F32 = jnp.float32


def _cparams(*sem):
    return pltpu.CompilerParams(dimension_semantics=sem, vmem_limit_bytes=VMEM_LIMIT_BYTES)


def _resident(shape):
    nd = len(shape)
    return pl.BlockSpec(shape, lambda *_: (0,) * nd, pipeline_mode=pl.Buffered(1))


def _norm_mod(h, g, shift, scale):
    ms = jnp.mean(h * h, axis=-1, keepdims=True)
    y = h * lax.rsqrt(ms + NORM_EPS) * g
    return y * (1.0 + scale) + shift


def _mod_part(mod_ref, k):
    return mod_ref[:, k * D_MODEL:(k + 1) * D_MODEL]


def _mod_spec(n_ctx_rows, seq, tm):
    def idx(i):
        r0 = i * tm
        grp = jnp.where(r0 < n_ctx_rows, 0, 1 + (r0 - n_ctx_rows) // seq)
        return (grp, 0, 0)
    return pl.BlockSpec((None, 1, 6 * D_MODEL), idx)


def _row_spec(tm, width):
    return pl.BlockSpec((tm, width), lambda i: (i, 0))


def _ada_kernel(cs_ref, w_ref, b_ref, o_ref):
    cs = cs_ref[...]
    s = (cs * jax.nn.sigmoid(cs)).astype(_BF16)
    o_ref[...] = jnp.dot(s, w_ref[...].astype(_BF16), preferred_element_type=_F32) + b_ref[...]


def _ada_table(cs, ada_w, ada_b):
    depth = ada_w.shape[0]
    n = ada_w.shape[2]
    return pl.pallas_call(
        _ada_kernel,
        out_shape=jax.ShapeDtypeStruct((depth, MOD_ROWS, n), _F32),
        grid=(depth, n // ADA_COL_TILE),
        in_specs=[
            pl.BlockSpec((MOD_ROWS, D_MODEL), lambda l, j: (0, 0)),
            pl.BlockSpec((None, D_MODEL, ADA_COL_TILE), lambda l, j: (l, 0, j)),
            pl.BlockSpec((None, 1, ADA_COL_TILE), lambda l, j: (l, 0, j)),
        ],
        out_specs=pl.BlockSpec((None, MOD_ROWS, ADA_COL_TILE), lambda l, j: (l, 0, j)),
        compiler_params=_cparams("arbitrary", "arbitrary"),
        name="ada_table",
    )(cs, ada_w, ada_b.reshape(depth, 1, n))


def _mlp_kernel(h_ref, g_ref, mod_ref, w1_ref, w2_ref, fg_ref, o_ref, *, final_norm):
    h = h_ref[...]
    x = _norm_mod(h, g_ref[...], _mod_part(mod_ref, 3), _mod_part(mod_ref, 4)).astype(_BF16)
    acc = jnp.zeros(h.shape, _F32)
    for c in range(MLP_HIDDEN // MLP_HIDDEN_CHUNK):
        cols = slice(c * MLP_HIDDEN_CHUNK, (c + 1) * MLP_HIDDEN_CHUNK)
        u = jnp.dot(x, w1_ref[:, cols], preferred_element_type=_F32)
        u = jnp.square(jnp.maximum(u, 0.0)).astype(_BF16)
        acc = acc + jnp.dot(u, w2_ref[cols, :], preferred_element_type=_F32)
    out = h + _mod_part(mod_ref, 5) * acc
    if final_norm:
        ms = jnp.mean(out * out, axis=-1, keepdims=True)
        out = out * lax.rsqrt(ms + NORM_EPS) * fg_ref[...]
    o_ref[...] = out


def _mlp(h, g, mod_l, w1, w2, final_g, dims, final_norm):
    rows = h.shape[0]
    tm = ROW_TILE
    return pl.pallas_call(
        functools.partial(_mlp_kernel, final_norm=final_norm),
        out_shape=jax.ShapeDtypeStruct(h.shape, _F32),
        grid=(rows // tm,),
        in_specs=[
            _row_spec(tm, D_MODEL),
            _resident((1, D_MODEL)),
            _mod_spec(dims.n_ctx_rows, dims.seq, tm),
            _resident(w1.shape),
            _resident(w2.shape),
            _resident((1, D_MODEL)),
        ],
        out_specs=_row_spec(tm, D_MODEL),
        compiler_params=_cparams("arbitrary"),
        name="mlp",
    )(h, g, mod_l, w1, w2, final_g)


def _residual_out(h, mod_ref, y_bf16, w_ref):
    return h + _mod_part(mod_ref, 2) * jnp.dot(y_bf16, w_ref[...], preferred_element_type=_F32)


def _a_proj_kernel(h_ref, g_ref, mod_ref, wqvo_ref, wkt_ref, wgt_ref, bg_ref,
                   q_ref, kt_ref, v_ref, o_ref, gt_ref):
    x = _norm_mod(h_ref[...], g_ref[...], _mod_part(mod_ref, 0), _mod_part(mod_ref, 1)).astype(_BF16)
    y = jnp.dot(x, wqvo_ref[...], preferred_element_type=_F32)
    q_ref[...] = (y[:, :A_QK_DIM] * (A_DK ** -0.5)).astype(_BF16)
    v_ref[...] = y[:, A_QK_DIM:A_QK_DIM + D_MODEL].astype(_BF16)
    o_ref[...] = y[:, A_QK_DIM + D_MODEL:]
    nt = (((1,), (1,)), ((), ()))
    kt_ref[...] = lax.dot_general(wkt_ref[...], x, nt, preferred_element_type=_F32).astype(_BF16)
    gt = lax.dot_general(wgt_ref[...], x, nt, preferred_element_type=_F32) + bg_ref[...]
    row = lax.broadcasted_iota(jnp.int32, gt.shape, 0)
    log_sig = jnp.minimum(gt, 0.0) - jnp.log1p(jnp.exp(-jnp.abs(gt)))
    gt_ref[...] = jnp.where((row // A_HEADS) % 2 == 1, log_sig, gt)


def _a_proj(h, g, mod_l, wqvo, wkt, wgt, bgate, dims):
    rows = h.shape[0]
    tm = ROW_TILE
    n_gate = 4 * A_HEADS
    return pl.pallas_call(
        _a_proj_kernel,
        out_shape=(
            jax.ShapeDtypeStruct((rows, A_QK_DIM), _BF16),
            jax.ShapeDtypeStruct((A_QK_DIM, rows), _BF16),
            jax.ShapeDtypeStruct((rows, D_MODEL), _BF16),
            jax.ShapeDtypeStruct((rows, D_MODEL), _F32),
            jax.ShapeDtypeStruct((n_gate, rows), _F32),
        ),
        grid=(rows // tm,),
        in_specs=[
            _row_spec(tm, D_MODEL),
            _resident((1, D_MODEL)),
            _mod_spec(dims.n_ctx_rows, dims.seq, tm),
            _resident(wqvo.shape),
            _resident(wkt.shape),
            _resident(wgt.shape),
            _resident((n_gate, 1)),
        ],
        out_specs=(
            _row_spec(tm, A_QK_DIM),
            pl.BlockSpec((A_QK_DIM, tm), lambda i: (0, i)),
            _row_spec(tm, D_MODEL),
            _row_spec(tm, D_MODEL),
            pl.BlockSpec((n_gate, tm), lambda i: (0, i)),
        ),
        compiler_params=_cparams("arbitrary"),
        name="a_proj",
    )(h, g, mod_l, wqvo, wkt, wgt, bgate)


def _lane_scan(x, op, fill, reverse):
    lane = lax.broadcasted_iota(jnp.int32, x.shape, 1)
    k = 1
    while k < LANES:
        if reverse:
            shifted = jnp.where(lane < LANES - k, pltpu.roll(x, LANES - k, axis=1), fill)
        else:
            shifted = jnp.where(lane >= k, pltpu.roll(x, k, axis=1), fill)
        x = op(x, shifted)
        k *= 2
    return x


def _a_gate_terms(gates, m, reverse):
    li, lf = gates[:A_HEADS], gates[A_HEADS:]
    b = _lane_scan(lf, jnp.add, 0.0, reverse)
    a = li - b
    gg = jnp.maximum(m, _lane_scan(a, jnp.maximum, -jnp.inf, reverse))
    last = 0 if reverse else LANES - 1
    gg_end = gg[:, last:last + 1]
    b_end = b[:, last:last + 1]
    a_inter = jnp.exp(m - gg)
    clamp = jnp.exp(-(b + gg))
    ws = jnp.exp(a - gg_end)
    decay = jnp.exp(m[:, :1] - gg_end)
    m_new = jnp.broadcast_to(b_end + gg_end, m.shape)
    return a, gg, a_inter, clamp, ws, decay, m_new


def _a_scan_kernel(qf_ref, ktf_ref, vf_ref, gf_ref, qb_ref, ktb_ref, vb_ref, gb_ref,
                   hf_ref, hb_ref, cf_ref, cb_ref, mf_ref, mb_ref):
    @pl.when(pl.program_id(1) == 0)
    def _():
        cf_ref[...] = jnp.zeros(cf_ref.shape, _F32)
        cb_ref[...] = jnp.zeros(cb_ref.shape, _F32)
        mf_ref[...] = jnp.full(mf_ref.shape, -jnp.inf, _F32)
        mb_ref[...] = jnp.full(mb_ref.shape, -jnp.inf, _F32)

    n_gate = 2 * A_HEADS
    terms_f = _a_gate_terms(gf_ref[:n_gate, :], mf_ref[...], False)
    terms_b = _a_gate_terms(gb_ref[n_gate:, :], mb_ref[...], True)
    mf_ref[...] = terms_f[6]
    mb_ref[...] = terms_b[6]

    per_query = [terms_f[1], terms_f[2], terms_f[3], terms_b[1], terms_b[2], terms_b[3]]
    pad = jnp.zeros((LANES - len(per_query) * A_HEADS, LANES), _F32)
    cols = jnp.concatenate(per_query + [pad], axis=0).T

    t_idx = lax.broadcasted_iota(jnp.int32, (A_CHUNK, A_CHUNK), 0)
    s_idx = lax.broadcasted_iota(jnp.int32, (A_CHUNK, A_CHUNK), 1)
    ones_col = (lax.broadcasted_iota(jnp.int32, (A_CHUNK, LANES), 1) == 0).astype(_BF16)

    for d, (q_ref, kt_ref, v_ref, h_ref, c_ref, terms) in enumerate((
            (qf_ref, ktf_ref, vf_ref, hf_ref, cf_ref, terms_f),
            (qb_ref, ktb_ref, vb_ref, hb_ref, cb_ref, terms_b))):
        a, _, _, _, ws, decay, _ = terms
        mask = (s_idx <= t_idx) if d == 0 else (s_idx >= t_idx)
        for hd in range(A_HEADS):
            c0 = 3 * A_HEADS * d + hd
            gg_col = cols[:, c0:c0 + 1]
            a_inter_col = cols[:, c0 + A_HEADS:c0 + A_HEADS + 1]
            clamp_col = cols[:, c0 + 2 * A_HEADS:c0 + 2 * A_HEADS + 1]
            q = q_ref[:, hd * A_DK:(hd + 1) * A_DK]
            kt = kt_ref[hd * A_DK:(hd + 1) * A_DK, :]
            v = v_ref[:, hd * A_DV:(hd + 1) * A_DV]
            v_ext = jnp.concatenate([v, ones_col], axis=1)
            c_old = c_ref[hd]
            s = jnp.dot(q, kt, preferred_element_type=_F32)
            d_mat = jnp.exp(jnp.where(mask, a[hd:hd + 1, :] - gg_col, -jnp.inf))
            p = (s * d_mat).astype(_BF16)
            num = jnp.dot(p, v_ext, preferred_element_type=_F32)
            num = num + a_inter_col * jnp.dot(q, c_old.astype(_BF16), preferred_element_type=_F32)
            den = jnp.maximum(jnp.abs(num[:, A_DV:A_DV + 1]), clamp_col)
            h_ref[:, hd * A_DV:(hd + 1) * A_DV] = num[:, :A_DV] / den
            kw = (kt.astype(_F32) * ws[hd:hd + 1, :]).astype(_BF16)
            c_ref[hd] = decay[hd:hd + 1, :] * c_old + jnp.dot(kw, v_ext, preferred_element_type=_F32)


def _a_scan(q, kt, v, gt, dims):
    rows = q.shape[0]
    ncc = dims.ctx_len // A_CHUNK
    ncl = dims.seq // A_CHUNK
    lat0 = dims.batch * ncc

    def fwd_blk(b, j):
        return jnp.where(j < ncc, b * ncc + j, lat0 + b * ncl + (j - ncc))

    def bwd_blk(b, j):
        return jnp.where(j < ncc, b * ncc + (ncc - 1 - j), lat0 + b * ncl + (ncl - 1 - (j - ncc)))

    def specs(blk):
        return [
            pl.BlockSpec((A_CHUNK, A_QK_DIM), lambda b, j: (blk(b, j), 0)),
            pl.BlockSpec((A_QK_DIM, A_CHUNK), lambda b, j: (0, blk(b, j))),
            pl.BlockSpec((A_CHUNK, D_MODEL), lambda b, j: (blk(b, j), 0)),
            pl.BlockSpec((4 * A_HEADS, A_CHUNK), lambda b, j: (0, blk(b, j))),
        ]

    state = pltpu.VMEM((A_HEADS, A_DK, 2 * A_DV), _F32)
    stab = pltpu.VMEM((A_HEADS, LANES), _F32)
    return pl.pallas_call(
        _a_scan_kernel,
        out_shape=(jax.ShapeDtypeStruct((rows, D_MODEL), _F32),) * 2,
        grid=(dims.batch, ncc + ncl),
        in_specs=specs(fwd_blk) + specs(bwd_blk),
        out_specs=(
            pl.BlockSpec((A_CHUNK, D_MODEL), lambda b, j: (fwd_blk(b, j), 0)),
            pl.BlockSpec((A_CHUNK, D_MODEL), lambda b, j: (bwd_blk(b, j), 0)),
        ),
        scratch_shapes=[state, state, stab, stab],
        compiler_params=_cparams("arbitrary", "arbitrary"),
        name="a_scan",
    )(q, kt, v, gt, q, kt, v, gt)


def _a_out_kernel(h_ref, mod_ref, hf_ref, hb_ref, o_ref, hg_ref, w_ref, out_ref):
    hs = hf_ref[...] + hb_ref[...]
    parts = []
    for hd in range(A_HEADS):
        x = hs[:, hd * A_DV:(hd + 1) * A_DV]
        parts.append(x * lax.rsqrt(jnp.mean(x * x, axis=-1, keepdims=True) + NORM_EPS))
    y = jnp.concatenate(parts, axis=1) * hg_ref[...]
    z = (jax.nn.sigmoid(o_ref[...]) * y).astype(_BF16)
    out_ref[...] = _residual_out(h_ref[...], mod_ref, z, w_ref)


def _a_out(h, mod_l, hf, hb, o, head_g, w_out, dims):
    rows = h.shape[0]
    tm = ROW_TILE
    return pl.pallas_call(
        _a_out_kernel,
        out_shape=jax.ShapeDtypeStruct(h.shape, _F32),
        grid=(rows // tm,),
        in_specs=[
            _row_spec(tm, D_MODEL),
            _mod_spec(dims.n_ctx_rows, dims.seq, tm),
            _row_spec(tm, D_MODEL),
            _row_spec(tm, D_MODEL),
            _row_spec(tm, D_MODEL),
            _resident((1, D_MODEL)),
            _resident(w_out.shape),
        ],
        out_specs=_row_spec(tm, D_MODEL),
        compiler_params=_cparams("arbitrary"),
        name="a_out",
    )(h, mod_l, hf, hb, o, head_g, w_out)


ROPE_HALF = B_HEAD_DIM // 4


def _swap_halves_lanes(x):
    lane = lax.broadcasted_iota(jnp.int32, x.shape, 1)
    fwd = pltpu.roll(x, LANES - ROPE_HALF, axis=1)
    back = pltpu.roll(x, ROPE_HALF, axis=1)
    return jnp.where(lane % (2 * ROPE_HALF) < ROPE_HALF, fwd, back)


def _swap_halves_rows(x):
    parts = []
    for r0 in range(0, x.shape[0], 2 * ROPE_HALF):
        parts += [x[r0 + ROPE_HALF:r0 + 2 * ROPE_HALF], x[r0:r0 + ROPE_HALF]]
    return jnp.concatenate(parts, axis=0)


def _b_proj_kernel(h_ref, g_ref, mod_ref, wqv_ref, wkt_ref, cos_ref, sin_ref, cost_ref, sint_ref,
                   q_ref, kt_ref, v_ref):
    x = _norm_mod(h_ref[...], g_ref[...], _mod_part(mod_ref, 0), _mod_part(mod_ref, 1)).astype(_BF16)
    y = jnp.dot(x, wqv_ref[...], preferred_element_type=_F32)
    cos, sin = cos_ref[...], sin_ref[...]
    for c0 in range(0, D_MODEL, LANES):
        q = y[:, c0:c0 + LANES]
        q = (q * cos + _swap_halves_lanes(q) * sin) * (B_HEAD_DIM ** -0.5)
        q_ref[:, c0:c0 + LANES] = q.astype(_BF16)
    v_ref[...] = y[:, D_MODEL:].astype(_BF16)
    nt = (((1,), (1,)), ((), ()))
    kt = lax.dot_general(wkt_ref[...], x, nt, preferred_element_type=_F32)
    cost = jnp.concatenate([cost_ref[...]] * B_KV_HEADS, axis=0)
    sint = jnp.concatenate([sint_ref[...]] * B_KV_HEADS, axis=0)
    kt_ref[...] = (kt * cost + _swap_halves_rows(kt) * sint).astype(_BF16)


def _b_proj(h, g, mod_l, wqv, wkt, rope, dims):
    rows = h.shape[0]
    tm = ROW_TILE
    cos, sin, cost, sint = rope
    return pl.pallas_call(
        _b_proj_kernel,
        out_shape=(
            jax.ShapeDtypeStruct((rows, D_MODEL), _BF16),
            jax.ShapeDtypeStruct((B_KV_DIM, rows), _BF16),
            jax.ShapeDtypeStruct((rows, B_KV_DIM), _BF16),
        ),
        grid=(rows // tm,),
        in_specs=[
            _row_spec(tm, D_MODEL),
            _resident((1, D_MODEL)),
            _mod_spec(dims.n_ctx_rows, dims.seq, tm),
            _resident(wqv.shape),
            _resident(wkt.shape),
            _row_spec(tm, LANES),
            _row_spec(tm, LANES),
            pl.BlockSpec((B_HEAD_DIM, tm), lambda i: (0, i)),
            pl.BlockSpec((B_HEAD_DIM, tm), lambda i: (0, i)),
        ],
        out_specs=(
            _row_spec(tm, D_MODEL),
            pl.BlockSpec((B_KV_DIM, tm), lambda i: (0, i)),
            _row_spec(tm, B_KV_DIM),
        ),
        compiler_params=_cparams("arbitrary"),
        name="b_proj",
    )(h, g, mod_l, wqv, wkt, cos, sin, cost, sint)


def _b_attn_kernel(sink_ref, q_ref, ktl_ref, ktc_ref, ktr_ref, ktx_ref, vl_ref, vc_ref, vr_ref, vx_ref,
                   o_ref, *, n_ctx_blocks, blocks_per_seq):
    i = pl.program_id(0)
    is_lat = i >= n_ctx_blocks
    n = (i - n_ctx_blocks) % blocks_per_seq
    neg = -jnp.inf
    bias_l = jnp.where(jnp.logical_and(is_lat, n >= 1), 0.0, neg)
    bias_c = jnp.where(is_lat, 0.0, neg)
    bias_r = jnp.where(jnp.logical_and(is_lat, n <= blocks_per_seq - 2), 0.0, neg)

    rows = B_GROUP * B_BLOCK
    r = lax.broadcasted_iota(jnp.int32, (rows, B_BLOCK), 0) % B_BLOCK
    c = lax.broadcasted_iota(jnp.int32, (rows, B_BLOCK), 1)
    mask_l = c >= r
    mask_r = c <= r
    head_of_row = lax.broadcasted_iota(jnp.int32, (rows, 1), 0) // B_BLOCK

    for g in range(B_KV_HEADS):
        q = jnp.concatenate(
            [q_ref[:, (g * B_GROUP + j) * B_HEAD_DIM:(g * B_GROUP + j + 1) * B_HEAD_DIM]
             for j in range(B_GROUP)], axis=0)
        sink = jnp.zeros((rows, 1), _F32)
        for j in range(B_GROUP):
            sink = jnp.where(head_of_row == j, sink_ref[g * B_GROUP + j], sink)
        ks = slice(g * B_HEAD_DIM, (g + 1) * B_HEAD_DIM)
        s_l = jnp.where(mask_l, jnp.dot(q, ktl_ref[ks, :], preferred_element_type=_F32), neg) + bias_l
        s_c = jnp.dot(q, ktc_ref[ks, :], preferred_element_type=_F32) + bias_c
        s_r = jnp.where(mask_r, jnp.dot(q, ktr_ref[ks, :], preferred_element_type=_F32), neg) + bias_r
        s_x = jnp.dot(q, ktx_ref[ks, :], preferred_element_type=_F32)
        m = sink
        for s in (s_l, s_c, s_r, s_x):
            m = jnp.maximum(m, jnp.max(s, axis=-1, keepdims=True))
        denom = jnp.exp(sink - m)
        acc = jnp.zeros((rows, B_HEAD_DIM), _F32)
        for s, v_ref in ((s_l, vl_ref), (s_c, vc_ref), (s_r, vr_ref), (s_x, vx_ref)):
            p = jnp.exp(s - m)
            denom = denom + jnp.sum(p, axis=-1, keepdims=True)
            acc = acc + jnp.dot(p.astype(_BF16), v_ref[:, ks], preferred_element_type=_F32)
        out = (acc / denom).astype(_BF16)
        for j in range(B_GROUP):
            hd = g * B_GROUP + j
            o_ref[:, hd * B_HEAD_DIM:(hd + 1) * B_HEAD_DIM] = out[j * B_BLOCK:(j + 1) * B_BLOCK, :]


def _b_attn(sinks, q, kt, v, dims):
    rows = q.shape[0]
    ncc = dims.ctx_len // B_BLOCK
    ncl = dims.seq // B_BLOCK
    lat0 = dims.batch * ncc
    nblk = rows // B_BLOCK

    def batch_of(i):
        return jnp.where(i < lat0, i // ncc, (i - lat0) // ncl)

    def left(i):
        return jnp.maximum(i - 1, 0)

    def right(i):
        return jnp.minimum(i + 1, nblk - 1)

    def kt_spec(f):
        return pl.BlockSpec((B_KV_DIM, B_BLOCK), lambda i: (0, f(i)))

    def v_spec(f):
        return pl.BlockSpec((B_BLOCK, B_KV_DIM), lambda i: (f(i), 0))

    return pl.pallas_call(
        functools.partial(_b_attn_kernel, n_ctx_blocks=lat0, blocks_per_seq=ncl),
        out_shape=jax.ShapeDtypeStruct((rows, D_MODEL), _BF16),
        grid=(nblk,),
        in_specs=[
            pl.BlockSpec(memory_space=pltpu.SMEM),
            pl.BlockSpec((B_BLOCK, D_MODEL), lambda i: (i, 0)),
            kt_spec(left), kt_spec(lambda i: i), kt_spec(right),
            pl.BlockSpec((B_KV_DIM, dims.ctx_len), lambda i: (0, batch_of(i))),
            v_spec(left), v_spec(lambda i: i), v_spec(right),
            pl.BlockSpec((dims.ctx_len, B_KV_DIM), lambda i: (batch_of(i), 0)),
        ],
        out_specs=pl.BlockSpec((B_BLOCK, D_MODEL), lambda i: (i, 0)),
        compiler_params=_cparams("arbitrary"),
        name="b_attn",
    )(sinks, q, kt, kt, kt, kt, v, v, v, v)


def _b_out_kernel(h_ref, mod_ref, y_ref, w_ref, out_ref):
    out_ref[...] = _residual_out(h_ref[...], mod_ref, y_ref[...], w_ref)


def _b_out(h, mod_l, y, w_out, dims):
    rows = h.shape[0]
    tm = ROW_TILE
    return pl.pallas_call(
        _b_out_kernel,
        out_shape=jax.ShapeDtypeStruct(h.shape, _F32),
        grid=(rows // tm,),
        in_specs=[
            _row_spec(tm, D_MODEL),
            _mod_spec(dims.n_ctx_rows, dims.seq, tm),
            _row_spec(tm, D_MODEL),
            _resident(w_out.shape),
        ],
        out_specs=_row_spec(tm, D_MODEL),
        compiler_params=_cparams("arbitrary"),
        name="b_out",
    )(h, mod_l, y, w_out)


def _c_proj_kernel(h_ref, g_ref, mod_ref, w_ref, bg_ref, u_ref):
    x = _norm_mod(h_ref[...], g_ref[...], _mod_part(mod_ref, 0), _mod_part(mod_ref, 1)).astype(_BF16)
    y = jnp.dot(x, w_ref[...], preferred_element_type=_F32)
    bg_ref[...] = y[:, :D_MODEL]
    u_ref[...] = y[:, D_MODEL:2 * D_MODEL] * y[:, 2 * D_MODEL:]


def _c_proj(h, g, mod_l, w_in, dims):
    rows = h.shape[0]
    tm = ROW_TILE
    return pl.pallas_call(
        _c_proj_kernel,
        out_shape=(jax.ShapeDtypeStruct((rows, D_MODEL), _F32),) * 2,
        grid=(rows // tm,),
        in_specs=[
            _row_spec(tm, D_MODEL),
            _resident((1, D_MODEL)),
            _mod_spec(dims.n_ctx_rows, dims.seq, tm),
            _resident(w_in.shape),
        ],
        out_specs=(_row_spec(tm, D_MODEL),) * 2,
        compiler_params=_cparams("arbitrary"),
        name="c_proj",
    )(h, g, mod_l, w_in)


def _c_out_kernel(h_ref, mod_ref, bg_ref, u_ref, up_ref, un_ref, cw_ref, cb_ref, w_ref, out_ref,
                  *, n_ctx_rows, ctx_len, seq):
    tm = u_ref.shape[0]
    u = u_ref[...]
    row = lax.broadcasted_iota(jnp.int32, (tm, 1), 0)
    g_row = pl.program_id(0) * tm + row
    pos = jnp.where(g_row < n_ctx_rows, g_row % ctx_len, (g_row - n_ctx_rows) % seq)
    length = jnp.where(g_row < n_ctx_rows, ctx_len, seq)
    prev = jnp.where(row == 0, up_ref[SUBLANES - 1:SUBLANES, :], pltpu.roll(u, 1, axis=0))
    nxt = jnp.where(row == tm - 1, un_ref[0:1, :], pltpu.roll(u, tm - 1, axis=0))
    prev = jnp.where(pos == 0, 0.0, prev)
    nxt = jnp.where(pos == length - 1, 0.0, nxt)
    conv = prev * cw_ref[0:1, :] + u * cw_ref[1:2, :] + nxt * cw_ref[2:3, :] + cb_ref[...]
    y = (bg_ref[...] * conv).astype(_BF16)
    out_ref[...] = _residual_out(h_ref[...], mod_ref, y, w_ref)


def _c_out(h, mod_l, bg, u, conv_w, conv_b, w_out, dims):
    rows = h.shape[0]
    tm = ROW_TILE
    per = tm // SUBLANES
    last = rows // SUBLANES - 1
    return pl.pallas_call(
        functools.partial(_c_out_kernel, n_ctx_rows=dims.n_ctx_rows, ctx_len=dims.ctx_len, seq=dims.seq),
        out_shape=jax.ShapeDtypeStruct(h.shape, _F32),
        grid=(rows // tm,),
        in_specs=[
            _row_spec(tm, D_MODEL),
            _mod_spec(dims.n_ctx_rows, dims.seq, tm),
            _row_spec(tm, D_MODEL),
            _row_spec(tm, D_MODEL),
            pl.BlockSpec((SUBLANES, D_MODEL), lambda i: (jnp.maximum(i * per - 1, 0), 0)),
            pl.BlockSpec((SUBLANES, D_MODEL), lambda i: (jnp.minimum((i + 1) * per, last), 0)),
            _resident(conv_w.shape),
            _resident((1, D_MODEL)),
            _resident(w_out.shape),
        ],
        out_specs=_row_spec(tm, D_MODEL),
        compiler_params=_cparams("arbitrary"),
        name="c_out",
    )(h, mod_l, bg, u, u, u, conv_w, conv_b, w_out)


class _Dims:
    def __init__(self, batch, seq, ctx_len):
        self.batch = batch
        self.seq = seq
        self.ctx_len = ctx_len
        self.n_ctx_rows = batch * ctx_len


def _rope_tables(dims):
    n_freq = B_HEAD_DIM // 4
    t = jnp.arange(dims.seq)
    inv_freq = ROPE_BASE ** (-jnp.arange(n_freq, dtype=_F32) / n_freq)
    ang_row = (t // GRID_W).astype(_F32)[:, None] * inv_freq
    ang_col = (t % GRID_W).astype(_F32)[:, None] * inv_freq
    ang = jnp.concatenate([ang_row, ang_row, ang_col, ang_col], axis=1)
    sign = jnp.tile(jnp.concatenate([-jnp.ones(n_freq, _F32), jnp.ones(n_freq, _F32)]), 2)
    cos = jnp.tile(jnp.cos(ang), (dims.batch, 1))
    sin = jnp.tile(jnp.sin(ang) * sign, (dims.batch, 1))
    cos = jnp.concatenate([jnp.ones((dims.n_ctx_rows, B_HEAD_DIM), _F32), cos], axis=0)
    sin = jnp.concatenate([jnp.zeros((dims.n_ctx_rows, B_HEAD_DIM), _F32), sin], axis=0)
    return jnp.tile(cos, (1, 2)), jnp.tile(sin, (1, 2)), cos.T, sin.T


def kernel(x, c, ctx, c_ctx, ada_w, ada_b, norm_g, final_g, mlp_w1, mlp_w2,
           a_w_in, a_w_gate, a_b_gate, a_head_g, a_w_out,
           b_w_qkv, b_sinks, b_w_out, c_w_in, c_conv_w, c_conv_b, c_w_out):
    batch, seq, d = x.shape
    ctx_len = ctx.shape[1]
    depth = ada_w.shape[0]
    dims = _Dims(batch, seq, ctx_len)
    assert d == D_MODEL and seq % ROW_TILE == 0 and dims.n_ctx_rows % ROW_TILE == 0
    assert ctx_len % A_CHUNK == 0 and 1 + batch <= MOD_ROWS

    h = jnp.concatenate([ctx.reshape(-1, d), x.reshape(-1, d)], axis=0)
    cs = jnp.concatenate([c_ctx[None], c, jnp.zeros((MOD_ROWS - 1 - batch, d), _F32)], axis=0)
    mod = _ada_table(cs, ada_w, ada_b).reshape(depth, MOD_ROWS, 1, 6 * d)
    rope = _rope_tables(dims)
    fg = final_g.reshape(1, d)

    for l in range(depth):
        kind, j = l % N_MIXERS, l // N_MIXERS
        mod_l = mod[l]
        g0 = norm_g[l, 0].reshape(1, d)
        g1 = norm_g[l, 1].reshape(1, d)
        if kind == 0:
            w_in = a_w_in[j]
            wqvo = jnp.concatenate([w_in[:, :A_QK_DIM], w_in[:, 2 * A_QK_DIM:]], axis=1).astype(_BF16)
            wkt = w_in[:, A_QK_DIM:2 * A_QK_DIM].T.astype(_BF16)
            wgt = a_w_gate[j].T.astype(_BF16)
            q, kt, v, o, gt = _a_proj(h, g0, mod_l, wqvo, wkt, wgt, a_b_gate[j].reshape(-1, 1), dims)
            hf, hb = _a_scan(q, kt, v, gt, dims)
            h = _a_out(h, mod_l, hf, hb, o, a_head_g[j].reshape(1, d), a_w_out[j].astype(_BF16), dims)
        elif kind == 1:
            w = b_w_qkv[j]
            wqv = jnp.concatenate([w[:, :D_MODEL], w[:, D_MODEL + B_KV_DIM:]], axis=1).astype(_BF16)
            wkt = w[:, D_MODEL:D_MODEL + B_KV_DIM].T.astype(_BF16)
            q, kt, v = _b_proj(h, g0, mod_l, wqv, wkt, rope, dims)
            y = _b_attn(b_sinks[j], q, kt, v, dims)
            h = _b_out(h, mod_l, y, b_w_out[j].astype(_BF16), dims)
        else:
            bg, u = _c_proj(h, g0, mod_l, c_w_in[j].astype(_BF16), dims)
            h = _c_out(h, mod_l, bg, u, c_conv_w[j], c_conv_b[j].reshape(1, d), c_w_out[j].astype(_BF16), dims)
        h = _mlp(h, g1, mod_l, mlp_w1[l].astype(_BF16), mlp_w2[l].astype(_BF16), fg, dims,
                 final_norm=(l == depth - 1))
    return h[dims.n_ctx_rows:].reshape(batch, seq, d)
```

```python
import functools

import jax
import jax.numpy as jnp
from jax import lax
from jax.experimental import pallas as pl
from jax.experimental.pallas import tpu as pltpu

D_MODEL = 1024
GRID_W = 64
N_MIXERS = 3
A_HEADS = 8
A_QK_DIM = D_MODEL // 2
A_DK = A_QK_DIM // A_HEADS
A_DV = D_MODEL // A_HEADS
A_CHUNK = 128
B_Q_HEADS = 16
B_KV_HEADS = 4
B_HEAD_DIM = D_MODEL // B_Q_HEADS
B_GROUP = B_Q_HEADS // B_KV_HEADS
B_KV_DIM = B_KV_HEADS * B_HEAD_DIM
B_BLOCK = 128
ROPE_BASE = 10000.0
MLP_HIDDEN = 4 * D_MODEL
NORM_EPS = 1e-6

LANES = 128
SUBLANES = 8
VMEM_LIMIT_BYTES = 56 * 1024 * 1024

ROW_TILE = 512
MLP_HIDDEN_CHUNK = 1024
MOD_ROWS = 8
ADA_COL_TILE = 1536

_BF16 = jnp.bfloat16
_F32 = jnp.float32


def _cparams(*sem):
    return pltpu.CompilerParams(dimension_semantics=sem, vmem_limit_bytes=VMEM_LIMIT_BYTES)


def _resident(shape):
    nd = len(shape)
    return pl.BlockSpec(shape, lambda *_: (0,) * nd, pipeline_mode=pl.Buffered(1))


def _norm_mod(h, g, shift, scale):
    ms = jnp.mean(h * h, axis=-1, keepdims=True)
    y = h * lax.rsqrt(ms + NORM_EPS) * g
    return y * (1.0 + scale) + shift


def _mod_part(mod_ref, k):
    return mod_ref[:, k * D_MODEL:(k + 1) * D_MODEL]


def _mod_spec(n_ctx_rows, seq, tm):
    def idx(i):
        r0 = i * tm
        grp = jnp.where(r0 < n_ctx_rows, 0, 1 + (r0 - n_ctx_rows) // seq)
        return (grp, 0, 0)
    return pl.BlockSpec((None, 1, 6 * D_MODEL), idx)


def _row_spec(tm, width):
    return pl.BlockSpec((tm, width), lambda i: (i, 0))


def _ada_kernel(cs_ref, w_ref, b_ref, o_ref):
    cs = cs_ref[...]
    s = (cs * jax.nn.sigmoid(cs)).astype(_BF16)
    o_ref[...] = jnp.dot(s, w_ref[...].astype(_BF16), preferred_element_type=_F32) + b_ref[...]


def _ada_table(cs, ada_w, ada_b):
    depth = ada_w.shape[0]
    n = ada_w.shape[2]
    return pl.pallas_call(
        _ada_kernel,
        out_shape=jax.ShapeDtypeStruct((depth, MOD_ROWS, n), _F32),
        grid=(depth, n // ADA_COL_TILE),
        in_specs=[
            pl.BlockSpec((MOD_ROWS, D_MODEL), lambda l, j: (0, 0)),
            pl.BlockSpec((None, D_MODEL, ADA_COL_TILE), lambda l, j: (l, 0, j)),
            pl.BlockSpec((None, 1, ADA_COL_TILE), lambda l, j: (l, 0, j)),
        ],
        out_specs=pl.BlockSpec((None, MOD_ROWS, ADA_COL_TILE), lambda l, j: (l, 0, j)),
        compiler_params=_cparams("arbitrary", "arbitrary"),
        name="ada_table",
    )(cs, ada_w, ada_b.reshape(depth, 1, n))


def _mlp_kernel(h_ref, g_ref, mod_ref, w1_ref, w2_ref, fg_ref, o_ref, *, final_norm):
    h = h_ref[...]
    x = _norm_mod(h, g_ref[...], _mod_part(mod_ref, 3), _mod_part(mod_ref, 4)).astype(_BF16)
    acc = jnp.zeros(h.shape, _F32)
    for c in range(MLP_HIDDEN // MLP_HIDDEN_CHUNK):
        cols = slice(c * MLP_HIDDEN_CHUNK, (c + 1) * MLP_HIDDEN_CHUNK)
        u = jnp.dot(x, w1_ref[:, cols], preferred_element_type=_F32)
        u = jnp.square(jnp.maximum(u, 0.0)).astype(_BF16)
        acc = acc + jnp.dot(u, w2_ref[cols, :], preferred_element_type=_F32)
    out = h + _mod_part(mod_ref, 5) * acc
    if final_norm:
        ms = jnp.mean(out * out, axis=-1, keepdims=True)
        out = out * lax.rsqrt(ms + NORM_EPS) * fg_ref[...]
    o_ref[...] = out


def _mlp(h, g, mod_l, w1, w2, final_g, dims, final_norm):
    rows = h.shape[0]
    tm = ROW_TILE
    return pl.pallas_call(
        functools.partial(_mlp_kernel, final_norm=final_norm),
        out_shape=jax.ShapeDtypeStruct(h.shape, _F32),
        grid=(rows // tm,),
        in_specs=[
            _row_spec(tm, D_MODEL),
            _resident((1, D_MODEL)),
            _mod_spec(dims.n_ctx_rows, dims.seq, tm),
            _resident(w1.shape),
            _resident(w2.shape),
            _resident((1, D_MODEL)),
        ],
        out_specs=_row_spec(tm, D_MODEL),
        compiler_params=_cparams("arbitrary"),
        name="mlp",
    )(h, g, mod_l, w1, w2, final_g)


def _residual_out(h, mod_ref, y_bf16, w_ref):
    return h + _mod_part(mod_ref, 2) * jnp.dot(y_bf16, w_ref[...], preferred_element_type=_F32)


def _a_proj_kernel(h_ref, g_ref, mod_ref, wqvo_ref, wkt_ref, wgt_ref, bg_ref,
                   q_ref, kt_ref, v_ref, o_ref, gr_ref, gc_ref):
    x = _norm_mod(h_ref[...], g_ref[...], _mod_part(mod_ref, 0), _mod_part(mod_ref, 1)).astype(_BF16)
    y = jnp.dot(x, wqvo_ref[...], preferred_element_type=_F32)
    q_ref[...] = (y[:, :A_QK_DIM] * (A_DK ** -0.5)).astype(_BF16)
    v_ref[...] = y[:, A_QK_DIM:A_QK_DIM + D_MODEL].astype(_BF16)
    o_ref[...] = y[:, A_QK_DIM + D_MODEL:]
    nt = (((1,), (1,)), ((), ()))
    kt_ref[...] = lax.dot_general(wkt_ref[...], x, nt, preferred_element_type=_F32).astype(_BF16)
    gt = lax.dot_general(wgt_ref[...], x, nt, preferred_element_type=_F32) + bg_ref[...]

    n_chunks = gt.shape[1] // A_CHUNK

    def by_chunk(rows):
        return jnp.concatenate([rows[:, c * A_CHUNK:(c + 1) * A_CHUNK] for c in range(n_chunks)], axis=0)

    def by_token(x):
        return jnp.concatenate([x[c * A_HEADS:(c + 1) * A_HEADS] for c in range(n_chunks)], axis=1)

    def log_sigmoid(z):
        return jnp.minimum(z, 0.0) - jnp.log1p(jnp.exp(-jnp.abs(z)))

    row_form, col_cm, col_b = [], [], []
    for d in range(2):
        li = by_chunk(gt[2 * d * A_HEADS:(2 * d + 1) * A_HEADS])
        lf = log_sigmoid(by_chunk(gt[(2 * d + 1) * A_HEADS:(2 * d + 2) * A_HEADS]))
        b = _lane_scan(lf, jnp.add, 0.0, d == 1)
        a = li - b
        cm = _lane_scan(a, jnp.maximum, -jnp.inf, d == 1)
        row_form += [by_token(a), by_token(b)]
        col_cm.append(cm)
        col_b.append(b)
    gr_ref[...] = jnp.concatenate(row_form, axis=0)
    pad = jnp.zeros((LANES - 2 * A_HEADS, LANES), _F32)
    for c in range(n_chunks):
        rows = slice(c * A_HEADS, (c + 1) * A_HEADS)
        toks = slice(c * A_CHUNK, (c + 1) * A_CHUNK)
        gc_ref[toks, :LANES] = jnp.concatenate([col_cm[0][rows], col_cm[1][rows], pad], axis=0).T
        gc_ref[toks, LANES:] = jnp.concatenate([col_b[0][rows], col_b[1][rows], pad], axis=0).T


def _a_proj(h, g, mod_l, wqvo, wkt, wgt, bgate, dims):
    rows = h.shape[0]
    tm = ROW_TILE
    n_gate = 4 * A_HEADS
    return pl.pallas_call(
        _a_proj_kernel,
        out_shape=(
            jax.ShapeDtypeStruct((rows, A_QK_DIM), _BF16),
            jax.ShapeDtypeStruct((A_QK_DIM, rows), _BF16),
            jax.ShapeDtypeStruct((rows, D_MODEL), _BF16),
            jax.ShapeDtypeStruct((rows, D_MODEL), _F32),
            jax.ShapeDtypeStruct((n_gate, rows), _F32),
            jax.ShapeDtypeStruct((rows, 2 * LANES), _F32),
        ),
        grid=(rows // tm,),
        in_specs=[
            _row_spec(tm, D_MODEL),
            _resident((1, D_MODEL)),
            _mod_spec(dims.n_ctx_rows, dims.seq, tm),
            _resident(wqvo.shape),
            _resident(wkt.shape),
            _resident(wgt.shape),
            _resident((n_gate, 1)),
        ],
        out_specs=(
            _row_spec(tm, A_QK_DIM),
            pl.BlockSpec((A_QK_DIM, tm), lambda i: (0, i)),
            _row_spec(tm, D_MODEL),
            _row_spec(tm, D_MODEL),
            pl.BlockSpec((n_gate, tm), lambda i: (0, i)),
            _row_spec(tm, 2 * LANES),
        ),
        compiler_params=_cparams("arbitrary"),
        name="a_proj",
    )(h, g, mod_l, wqvo, wkt, wgt, bgate)


def _lane_scan(x, op, fill, reverse):
    lane = lax.broadcasted_iota(jnp.int32, x.shape, 1)
    k = 1
    while k < LANES:
        if reverse:
            shifted = jnp.where(lane < LANES - k, pltpu.roll(x, LANES - k, axis=1), fill)
        else:
            shifted = jnp.where(lane >= k, pltpu.roll(x, k, axis=1), fill)
        x = op(x, shifted)
        k *= 2
    return x


def _a_scan_kernel(qf_ref, ktf_ref, vf_ref, grf_ref, gcf_ref, qb_ref, ktb_ref, vb_ref, grb_ref, gcb_ref,
                   hf_ref, hb_ref, cf_ref, cb_ref, mrf_ref, mrb_ref, mc_ref):
    @pl.when(pl.program_id(1) == 0)
    def _():
        cf_ref[...] = jnp.zeros(cf_ref.shape, _F32)
        cb_ref[...] = jnp.zeros(cb_ref.shape, _F32)
        mrf_ref[...] = jnp.full(mrf_ref.shape, -jnp.inf, _F32)
        mrb_ref[...] = jnp.full(mrb_ref.shape, -jnp.inf, _F32)
        mc_ref[...] = jnp.full(mc_ref.shape, -jnp.inf, _F32)

    t_idx = lax.broadcasted_iota(jnp.int32, (A_CHUNK, A_CHUNK), 0)
    s_idx = lax.broadcasted_iota(jnp.int32, (A_CHUNK, A_CHUNK), 1)
    ones_blk = jnp.ones((A_CHUNK, A_DV), _BF16)
    m_col = mc_ref[0:1, :]
    m_col_new = []

    dirs = []
    for d, (gr_ref, gc_ref, mr_ref) in enumerate(((grf_ref, gcf_ref, mrf_ref), (grb_ref, gcb_ref, mrb_ref))):
        last = A_CHUNK - 1 if d == 0 else 0
        a = gr_ref[2 * d * A_HEADS:(2 * d + 1) * A_HEADS, :]
        b = gr_ref[(2 * d + 1) * A_HEADS:(2 * d + 2) * A_HEADS, :]
        m_row = mr_ref[:, 0:1]
        gg_end = jnp.maximum(m_row, jnp.max(a, axis=1, keepdims=True))
        ws = jnp.exp(a - gg_end)
        decay = jnp.exp(m_row - gg_end)
        mr_ref[...] = jnp.broadcast_to(b[:, last:last + 1] + gg_end, mr_ref.shape)
        gg_c = jnp.maximum(m_col, gc_ref[:, :LANES])
        b_plus_gg = gc_ref[:, LANES:] + gg_c
        clamp_c = jnp.exp(-b_plus_gg)
        m_col_new.append(b_plus_gg[last:last + 1, :])
        dirs.append((a, m_row, ws, decay, gg_c, clamp_c))
    lane = lax.broadcasted_iota(jnp.int32, (1, LANES), 1)
    mc_ref[...] = jnp.broadcast_to(jnp.where(lane < A_HEADS, m_col_new[0], m_col_new[1]), mc_ref.shape)

    for hd in range(A_HEADS):
        for d, (q_ref, kt_ref, v_ref, h_ref, c_ref) in enumerate((
                (qf_ref, ktf_ref, vf_ref, hf_ref, cf_ref), (qb_ref, ktb_ref, vb_ref, hb_ref, cb_ref))):
            a, m_row, ws, decay, gg_c, clamp_c = dirs[d]
            mask = (s_idx <= t_idx) if d == 0 else (s_idx >= t_idx)
            col = d * A_HEADS + hd
            gg = jnp.broadcast_to(gg_c[:, col:col + 1], (A_CHUNK, A_CHUNK))
            clamp = jnp.broadcast_to(clamp_c[:, col:col + 1], (A_CHUNK, A_DV))
            q = q_ref[:, hd * A_DK:(hd + 1) * A_DK]
            kt = kt_ref[hd * A_DK:(hd + 1) * A_DK, :]
            v = v_ref[:, hd * A_DV:(hd + 1) * A_DV]
            v_ext = jnp.concatenate([v, ones_blk], axis=1)
            c_old = c_ref[hd]
            s = jnp.dot(q, kt, preferred_element_type=_F32)
            e_keys = jnp.exp(jnp.where(mask, a[hd:hd + 1, :] - gg, -jnp.inf))
            e_state = jnp.exp(m_row[hd:hd + 1, :] - gg[:, :A_DK])
            p = jnp.concatenate([s * e_keys, q.astype(_F32) * e_state], axis=1).astype(_BF16)
            rhs = jnp.concatenate([v_ext, c_old.astype(_BF16)], axis=0)
            num = jnp.dot(p, rhs, preferred_element_type=_F32)
            den = jnp.maximum(jnp.abs(num[:, A_DV:]), clamp)
            h_ref[:, hd * A_DV:(hd + 1) * A_DV] = num[:, :A_DV] / den
            kw = (kt.astype(_F32) * ws[hd:hd + 1, :]).astype(_BF16)
            c_ref[hd] = decay[hd:hd + 1, :] * c_old + jnp.dot(kw, v_ext, preferred_element_type=_F32)


def _a_scan(q, kt, v, gr, gc, dims):
    rows = q.shape[0]
    ncc = dims.ctx_len // A_CHUNK
    ncl = dims.seq // A_CHUNK
    lat0 = dims.batch * ncc

    def fwd_blk(b, j):
        return jnp.where(j < ncc, b * ncc + j, lat0 + b * ncl + (j - ncc))

    def bwd_blk(b, j):
        return jnp.where(j < ncc, b * ncc + (ncc - 1 - j), lat0 + b * ncl + (ncl - 1 - (j - ncc)))

    def specs(blk):
        return [
            pl.BlockSpec((A_CHUNK, A_QK_DIM), lambda b, j: (blk(b, j), 0)),
            pl.BlockSpec((A_QK_DIM, A_CHUNK), lambda b, j: (0, blk(b, j))),
            pl.BlockSpec((A_CHUNK, D_MODEL), lambda b, j: (blk(b, j), 0)),
            pl.BlockSpec((4 * A_HEADS, A_CHUNK), lambda b, j: (0, blk(b, j))),
            pl.BlockSpec((A_CHUNK, 2 * LANES), lambda b, j: (blk(b, j), 0)),
        ]

    state = pltpu.VMEM((A_HEADS, A_DK, 2 * A_DV), _F32)
    stab = pltpu.VMEM((A_HEADS, LANES), _F32)
    return pl.pallas_call(
        _a_scan_kernel,
        out_shape=(jax.ShapeDtypeStruct((rows, D_MODEL), _F32),) * 2,
        grid=(dims.batch, ncc + ncl),
        in_specs=specs(fwd_blk) + specs(bwd_blk),
        out_specs=(
            pl.BlockSpec((A_CHUNK, D_MODEL), lambda b, j: (fwd_blk(b, j), 0)),
            pl.BlockSpec((A_CHUNK, D_MODEL), lambda b, j: (bwd_blk(b, j), 0)),
        ),
        scratch_shapes=[state, state, stab, stab, stab],
        compiler_params=_cparams("arbitrary", "arbitrary"),
        name="a_scan",
    )(q, kt, v, gr, gc, q, kt, v, gr, gc)


def _a_out_kernel(h_ref, mod_ref, hf_ref, hb_ref, o_ref, hg_ref, w_ref, out_ref):
    hs = hf_ref[...] + hb_ref[...]
    parts = []
    for hd in range(A_HEADS):
        x = hs[:, hd * A_DV:(hd + 1) * A_DV]
        parts.append(x * lax.rsqrt(jnp.mean(x * x, axis=-1, keepdims=True) + NORM_EPS))
    y = jnp.concatenate(parts, axis=1) * hg_ref[...]
    z = (jax.nn.sigmoid(o_ref[...]) * y).astype(_BF16)
    out_ref[...] = _residual_out(h_ref[...], mod_ref, z, w_ref)


def _a_out(h, mod_l, hf, hb, o, head_g, w_out, dims):
    rows = h.shape[0]
    tm = ROW_TILE
    return pl.pallas_call(
        _a_out_kernel,
        out_shape=jax.ShapeDtypeStruct(h.shape, _F32),
        grid=(rows // tm,),
        in_specs=[
            _row_spec(tm, D_MODEL),
            _mod_spec(dims.n_ctx_rows, dims.seq, tm),
            _row_spec(tm, D_MODEL),
            _row_spec(tm, D_MODEL),
            _row_spec(tm, D_MODEL),
            _resident((1, D_MODEL)),
            _resident(w_out.shape),
        ],
        out_specs=_row_spec(tm, D_MODEL),
        compiler_params=_cparams("arbitrary"),
        name="a_out",
    )(h, mod_l, hf, hb, o, head_g, w_out)


ROPE_HALF = B_HEAD_DIM // 4


def _swap_halves_lanes(x):
    lane = lax.broadcasted_iota(jnp.int32, x.shape, 1)
    fwd = pltpu.roll(x, LANES - ROPE_HALF, axis=1)
    back = pltpu.roll(x, ROPE_HALF, axis=1)
    return jnp.where(lane % (2 * ROPE_HALF) < ROPE_HALF, fwd, back)


def _swap_halves_rows(x):
    parts = []
    for r0 in range(0, x.shape[0], 2 * ROPE_HALF):
        parts += [x[r0 + ROPE_HALF:r0 + 2 * ROPE_HALF], x[r0:r0 + ROPE_HALF]]
    return jnp.concatenate(parts, axis=0)


def _b_proj_kernel(h_ref, g_ref, mod_ref, wqvt_ref, wk_ref, cos_ref, sin_ref, cost_ref, sint_ref,
                   qt_ref, k_ref, vt_ref):
    x = _norm_mod(h_ref[...], g_ref[...], _mod_part(mod_ref, 0), _mod_part(mod_ref, 1)).astype(_BF16)
    nt = (((1,), (1,)), ((), ()))
    yt = lax.dot_general(wqvt_ref[...], x, nt, preferred_element_type=_F32)
    cost, sint = cost_ref[...], sint_ref[...]
    for hd in range(B_Q_HEADS):
        rows = slice(hd * B_HEAD_DIM, (hd + 1) * B_HEAD_DIM)
        qt = yt[rows, :]
        qt = (qt * cost + _swap_halves_rows(qt) * sint) * (B_HEAD_DIM ** -0.5)
        qt_ref[rows, :] = qt.astype(_BF16)
    vt_ref[...] = yt[D_MODEL:, :].astype(_BF16)
    cos, sin = cos_ref[...], sin_ref[...]
    k = jnp.dot(x, wk_ref[...], preferred_element_type=_F32)
    for c0 in range(0, B_KV_DIM, LANES):
        kc = k[:, c0:c0 + LANES]
        k_ref[:, c0:c0 + LANES] = (kc * cos + _swap_halves_lanes(kc) * sin).astype(_BF16)


def _b_proj(h, g, mod_l, wqvt, wk, rope, dims):
    rows = h.shape[0]
    tm = ROW_TILE
    cos, sin, cost, sint = rope
    return pl.pallas_call(
        _b_proj_kernel,
        out_shape=(
            jax.ShapeDtypeStruct((D_MODEL, rows), _BF16),
            jax.ShapeDtypeStruct((rows, B_KV_DIM), _BF16),
            jax.ShapeDtypeStruct((B_KV_DIM, rows), _BF16),
        ),
        grid=(rows // tm,),
        in_specs=[
            _row_spec(tm, D_MODEL),
            _resident((1, D_MODEL)),
            _mod_spec(dims.n_ctx_rows, dims.seq, tm),
            _resident(wqvt.shape),
            _resident(wk.shape),
            _row_spec(tm, LANES),
            _row_spec(tm, LANES),
            pl.BlockSpec((B_HEAD_DIM, tm), lambda i: (0, i)),
            pl.BlockSpec((B_HEAD_DIM, tm), lambda i: (0, i)),
        ],
        out_specs=(
            pl.BlockSpec((D_MODEL, tm), lambda i: (0, i)),
            _row_spec(tm, B_KV_DIM),
            pl.BlockSpec((B_KV_DIM, tm), lambda i: (0, i)),
        ),
        compiler_params=_cparams("arbitrary"),
        name="b_proj",
    )(h, g, mod_l, wqvt, wk, cos, sin, cost, sint)


B_ONES_ROWS = 16


def _b_attn_kernel(sink_ref, qt_ref, kl_ref, kc_ref, kr_ref, kx_ref, vtl_ref, vtc_ref, vtr_ref, vtx_ref,
                   o_ref, s_ref, *, n_ctx_blocks, blocks_per_seq, ctx_len):
    i = pl.program_id(0)
    is_lat = i >= n_ctx_blocks
    n = (i - n_ctx_blocks) % blocks_per_seq
    neg = -jnp.inf
    has_left = jnp.logical_and(is_lat, n >= 1)
    has_right = jnp.logical_and(is_lat, n <= blocks_per_seq - 2)
    key = lax.broadcasted_iota(jnp.int32, (B_BLOCK, B_BLOCK), 0)
    qry = lax.broadcasted_iota(jnp.int32, (B_BLOCK, B_BLOCK), 1)
    bias_l = jnp.where(jnp.logical_and(key >= qry, has_left), 0.0, neg)
    bias_r = jnp.where(jnp.logical_and(key <= qry, has_right), 0.0, neg)
    bias_l = jnp.concatenate([bias_l] * B_GROUP, axis=1)
    bias_r = jnp.concatenate([bias_r] * B_GROUP, axis=1)
    bias_c = jnp.where(is_lat, 0.0, neg)

    n_q = B_GROUP * B_BLOCK
    head_of_lane = lax.broadcasted_iota(jnp.int32, (1, n_q), 1) // B_BLOCK
    ones_rows = jnp.ones((B_ONES_ROWS, B_BLOCK), _BF16)
    n_ctx_tiles = ctx_len // B_BLOCK

    for g in range(B_KV_HEADS):
        qt = jnp.concatenate(
            [qt_ref[(g * B_GROUP + j) * B_HEAD_DIM:(g * B_GROUP + j + 1) * B_HEAD_DIM, :]
             for j in range(B_GROUP)], axis=1)
        sink = jnp.zeros((1, n_q), _F32)
        for j in range(B_GROUP):
            sink = jnp.where(head_of_lane == j, sink_ref[g * B_GROUP + j], sink)
        ks = slice(g * B_HEAD_DIM, (g + 1) * B_HEAD_DIM)
        k_tiles = [(kl_ref[:, ks], bias_l), (kc_ref[:, ks], bias_c), (kr_ref[:, ks], bias_r)]
        vt_tiles = [vtl_ref[ks, :], vtc_ref[ks, :], vtr_ref[ks, :]]
        for t in range(n_ctx_tiles):
            k_tiles.append((kx_ref[t * B_BLOCK:(t + 1) * B_BLOCK, ks], None))
            vt_tiles.append(vtx_ref[ks, t * B_BLOCK:(t + 1) * B_BLOCK])
        m_tile = None
        for t, (k, bias) in enumerate(k_tiles):
            s = jnp.dot(k, qt, preferred_element_type=_F32)
            if bias is not None:
                s = s + bias
            s_ref[t] = s
            m_tile = s if m_tile is None else jnp.maximum(m_tile, s)
        m = jnp.maximum(sink, jnp.max(m_tile, axis=0, keepdims=True))
        acc = jnp.zeros((B_HEAD_DIM + B_ONES_ROWS, n_q), _F32)
        for t, vt in enumerate(vt_tiles):
            p = jnp.exp(s_ref[t] - m).astype(_BF16)
            vt_ext = jnp.concatenate([vt, ones_rows], axis=0)
            acc = acc + jnp.dot(vt_ext, p, preferred_element_type=_F32)
        denom = jnp.exp(sink - m) + acc[B_HEAD_DIM:B_HEAD_DIM + 1, :]
        out_t = acc[:B_HEAD_DIM, :] * (1.0 / denom)
        for pair in range(B_GROUP // 2):
            two = jnp.concatenate([out_t[:, (2 * pair) * B_BLOCK:(2 * pair + 1) * B_BLOCK],
                                   out_t[:, (2 * pair + 1) * B_BLOCK:(2 * pair + 2) * B_BLOCK]], axis=0)
            c0 = (g * B_GROUP + 2 * pair) * B_HEAD_DIM
            o_ref[:, c0:c0 + 2 * B_HEAD_DIM] = two.T.astype(_BF16)


def _b_attn(sinks, qt, k, vt, dims):
    rows = k.shape[0]
    ncc = dims.ctx_len // B_BLOCK
    ncl = dims.seq // B_BLOCK
    lat0 = dims.batch * ncc
    nblk = rows // B_BLOCK

    def batch_of(i):
        return jnp.where(i < lat0, i // ncc, (i - lat0) // ncl)

    def left(i):
        return jnp.maximum(i - 1, 0)

    def right(i):
        return jnp.minimum(i + 1, nblk - 1)

    def k_spec(f):
        return pl.BlockSpec((B_BLOCK, B_KV_DIM), lambda i: (f(i), 0))

    def vt_spec(f):
        return pl.BlockSpec((B_KV_DIM, B_BLOCK), lambda i: (0, f(i)))

    n_tiles = 3 + ncc
    return pl.pallas_call(
        functools.partial(_b_attn_kernel, n_ctx_blocks=lat0, blocks_per_seq=ncl, ctx_len=dims.ctx_len),
        out_shape=jax.ShapeDtypeStruct((rows, D_MODEL), _BF16),
        grid=(nblk,),
        in_specs=[
            pl.BlockSpec(memory_space=pltpu.SMEM),
            pl.BlockSpec((D_MODEL, B_BLOCK), lambda i: (0, i)),
            k_spec(left), k_spec(lambda i: i), k_spec(right),
            pl.BlockSpec((dims.ctx_len, B_KV_DIM), lambda i: (batch_of(i), 0)),
            vt_spec(left), vt_spec(lambda i: i), vt_spec(right),
            pl.BlockSpec((B_KV_DIM, dims.ctx_len), lambda i: (0, batch_of(i))),
        ],
        out_specs=pl.BlockSpec((B_BLOCK, D_MODEL), lambda i: (i, 0)),
        scratch_shapes=[pltpu.VMEM((n_tiles, B_BLOCK, B_GROUP * B_BLOCK), _F32)],
        compiler_params=_cparams("arbitrary"),
        name="b_attn",
    )(sinks, qt, k, k, k, k, vt, vt, vt, vt)


def _b_out_kernel(h_ref, mod_ref, y_ref, w_ref, out_ref):
    out_ref[...] = _residual_out(h_ref[...], mod_ref, y_ref[...], w_ref)


def _b_out(h, mod_l, y, w_out, dims):
    rows = h.shape[0]
    tm = ROW_TILE
    return pl.pallas_call(
        _b_out_kernel,
        out_shape=jax.ShapeDtypeStruct(h.shape, _F32),
        grid=(rows // tm,),
        in_specs=[
            _row_spec(tm, D_MODEL),
            _mod_spec(dims.n_ctx_rows, dims.seq, tm),
            _row_spec(tm, D_MODEL),
            _resident(w_out.shape),
        ],
        out_specs=_row_spec(tm, D_MODEL),
        compiler_params=_cparams("arbitrary"),
        name="b_out",
    )(h, mod_l, y, w_out)


def _c_proj_kernel(h_ref, g_ref, mod_ref, w_ref, bg_ref, u_ref):
    x = _norm_mod(h_ref[...], g_ref[...], _mod_part(mod_ref, 0), _mod_part(mod_ref, 1)).astype(_BF16)
    y = jnp.dot(x, w_ref[...], preferred_element_type=_F32)
    bg_ref[...] = y[:, :D_MODEL]
    u_ref[...] = y[:, D_MODEL:2 * D_MODEL] * y[:, 2 * D_MODEL:]


def _c_proj(h, g, mod_l, w_in, dims):
    rows = h.shape[0]
    tm = ROW_TILE
    return pl.pallas_call(
        _c_proj_kernel,
        out_shape=(jax.ShapeDtypeStruct((rows, D_MODEL), _F32),) * 2,
        grid=(rows // tm,),
        in_specs=[
            _row_spec(tm, D_MODEL),
            _resident((1, D_MODEL)),
            _mod_spec(dims.n_ctx_rows, dims.seq, tm),
            _resident(w_in.shape),
        ],
        out_specs=(_row_spec(tm, D_MODEL),) * 2,
        compiler_params=_cparams("arbitrary"),
        name="c_proj",
    )(h, g, mod_l, w_in)


def _c_out_kernel(h_ref, mod_ref, bg_ref, u_ref, up_ref, un_ref, cw_ref, cb_ref, w_ref, out_ref,
                  *, n_ctx_rows, ctx_len, seq):
    tm = u_ref.shape[0]
    u = u_ref[...]
    row = lax.broadcasted_iota(jnp.int32, (tm, 1), 0)
    g_row = pl.program_id(0) * tm + row
    pos = jnp.where(g_row < n_ctx_rows, g_row % ctx_len, (g_row - n_ctx_rows) % seq)
    length = jnp.where(g_row < n_ctx_rows, ctx_len, seq)
    prev = jnp.where(row == 0, up_ref[SUBLANES - 1:SUBLANES, :], pltpu.roll(u, 1, axis=0))
    nxt = jnp.where(row == tm - 1, un_ref[0:1, :], pltpu.roll(u, tm - 1, axis=0))
    prev = jnp.where(pos == 0, 0.0, prev)
    nxt = jnp.where(pos == length - 1, 0.0, nxt)
    conv = prev * cw_ref[0:1, :] + u * cw_ref[1:2, :] + nxt * cw_ref[2:3, :] + cb_ref[...]
    y = (bg_ref[...] * conv).astype(_BF16)
    out_ref[...] = _residual_out(h_ref[...], mod_ref, y, w_ref)


def _c_out(h, mod_l, bg, u, conv_w, conv_b, w_out, dims):
    rows = h.shape[0]
    tm = ROW_TILE
    per = tm // SUBLANES
    last = rows // SUBLANES - 1
    return pl.pallas_call(
        functools.partial(_c_out_kernel, n_ctx_rows=dims.n_ctx_rows, ctx_len=dims.ctx_len, seq=dims.seq),
        out_shape=jax.ShapeDtypeStruct(h.shape, _F32),
        grid=(rows // tm,),
        in_specs=[
            _row_spec(tm, D_MODEL),
            _mod_spec(dims.n_ctx_rows, dims.seq, tm),
            _row_spec(tm, D_MODEL),
            _row_spec(tm, D_MODEL),
            pl.BlockSpec((SUBLANES, D_MODEL), lambda i: (jnp.maximum(i * per - 1, 0), 0)),
            pl.BlockSpec((SUBLANES, D_MODEL), lambda i: (jnp.minimum((i + 1) * per, last), 0)),
            _resident(conv_w.shape),
            _resident((1, D_MODEL)),
            _resident(w_out.shape),
        ],
        out_specs=_row_spec(tm, D_MODEL),
        compiler_params=_cparams("arbitrary"),
        name="c_out",
    )(h, mod_l, bg, u, u, u, conv_w, conv_b, w_out)


class _Dims:
    def __init__(self, batch, seq, ctx_len):
        self.batch = batch
        self.seq = seq
        self.ctx_len = ctx_len
        self.n_ctx_rows = batch * ctx_len


def _rope_tables(dims):
    n_freq = B_HEAD_DIM // 4
    t = jnp.arange(dims.seq)
    inv_freq = ROPE_BASE ** (-jnp.arange(n_freq, dtype=_F32) / n_freq)
    ang_row = (t // GRID_W).astype(_F32)[:, None] * inv_freq
    ang_col = (t % GRID_W).astype(_F32)[:, None] * inv_freq
    ang = jnp.concatenate([ang_row, ang_row, ang_col, ang_col], axis=1)
    sign = jnp.tile(jnp.concatenate([-jnp.ones(n_freq, _F32), jnp.ones(n_freq, _F32)]), 2)
    cos = jnp.tile(jnp.cos(ang), (dims.batch, 1))
    sin = jnp.tile(jnp.sin(ang) * sign, (dims.batch, 1))
    cos = jnp.concatenate([jnp.ones((dims.n_ctx_rows, B_HEAD_DIM), _F32), cos], axis=0)
    sin = jnp.concatenate([jnp.zeros((dims.n_ctx_rows, B_HEAD_DIM), _F32), sin], axis=0)
    return jnp.tile(cos, (1, 2)), jnp.tile(sin, (1, 2)), cos.T, sin.T


def kernel(x, c, ctx, c_ctx, ada_w, ada_b, norm_g, final_g, mlp_w1, mlp_w2,
           a_w_in, a_w_gate, a_b_gate, a_head_g, a_w_out,
           b_w_qkv, b_sinks, b_w_out, c_w_in, c_conv_w, c_conv_b, c_w_out):
    batch, seq, d = x.shape
    ctx_len = ctx.shape[1]
    depth = ada_w.shape[0]
    dims = _Dims(batch, seq, ctx_len)
    assert d == D_MODEL and seq % ROW_TILE == 0 and dims.n_ctx_rows % ROW_TILE == 0
    assert ctx_len % A_CHUNK == 0 and 1 + batch <= MOD_ROWS

    h = jnp.concatenate([ctx.reshape(-1, d), x.reshape(-1, d)], axis=0)
    cs = jnp.concatenate([c_ctx[None], c, jnp.zeros((MOD_ROWS - 1 - batch, d), _F32)], axis=0)
    mod = _ada_table(cs, ada_w, ada_b).reshape(depth, MOD_ROWS, 1, 6 * d)
    rope = _rope_tables(dims)
    fg = final_g.reshape(1, d)

    for l in range(depth):
        kind, j = l % N_MIXERS, l // N_MIXERS
        mod_l = mod[l]
        g0 = norm_g[l, 0].reshape(1, d)
        g1 = norm_g[l, 1].reshape(1, d)
        if kind == 0:
            w_in = a_w_in[j]
            wqvo = jnp.concatenate([w_in[:, :A_QK_DIM], w_in[:, 2 * A_QK_DIM:]], axis=1).astype(_BF16)
            wkt = w_in[:, A_QK_DIM:2 * A_QK_DIM].T.astype(_BF16)
            wgt = a_w_gate[j].T.astype(_BF16)
            q, kt, v, o, gr, gc = _a_proj(h, g0, mod_l, wqvo, wkt, wgt, a_b_gate[j].reshape(-1, 1), dims)
            hf, hb = _a_scan(q, kt, v, gr, gc, dims)
            h = _a_out(h, mod_l, hf, hb, o, a_head_g[j].reshape(1, d), a_w_out[j].astype(_BF16), dims)
        elif kind == 1:
            w = b_w_qkv[j]
            wqvt = jnp.concatenate([w[:, :D_MODEL], w[:, D_MODEL + B_KV_DIM:]], axis=1).T.astype(_BF16)
            wk = w[:, D_MODEL:D_MODEL + B_KV_DIM].astype(_BF16)
            qt, k, vt = _b_proj(h, g0, mod_l, wqvt, wk, rope, dims)
            y = _b_attn(b_sinks[j], qt, k, vt, dims)
            h = _b_out(h, mod_l, y, b_w_out[j].astype(_BF16), dims)
        else:
            bg, u = _c_proj(h, g0, mod_l, c_w_in[j].astype(_BF16), dims)
            h = _c_out(h, mod_l, bg, u, c_conv_w[j], c_conv_b[j].reshape(1, d), c_w_out[j].astype(_BF16), dims)
        h = _mlp(h, g1, mod_l, mlp_w1[l].astype(_BF16), mlp_w2[l].astype(_BF16), fg, dims,
                 final_norm=(l == depth - 1))
    return h[dims.n_ctx_rows:].reshape(batch, seq, d)
```

```python
import functools

import jax
import jax.numpy as jnp
from jax import lax
from jax.experimental import pallas as pl
from jax.experimental.pallas import tpu as pltpu

D_MODEL = 1024
GRID_W = 64
N_MIXERS = 3
A_HEADS = 8
A_QK_DIM = D_MODEL // 2
A_DK = A_QK_DIM // A_HEADS
A_DV = D_MODEL // A_HEADS
A_CHUNK = 128
B_Q_HEADS = 16
B_KV_HEADS = 4
B_HEAD_DIM = D_MODEL // B_Q_HEADS
B_GROUP = B_Q_HEADS // B_KV_HEADS
B_KV_DIM = B_KV_HEADS * B_HEAD_DIM
B_BLOCK = 128
ROPE_BASE = 10000.0
MLP_HIDDEN = 4 * D_MODEL
NORM_EPS = 1e-6

LANES = 128
SUBLANES = 8
VMEM_LIMIT_BYTES = 56 * 1024 * 1024

ROW_TILE = 512
MLP_HIDDEN_CHUNK = 1024
MOD_ROWS = 8
ADA_COL_TILE = 1536

_BF16 = jnp.bfloat16
_F32 = jnp.float32


def _cparams(*sem):
    return pltpu.CompilerParams(dimension_semantics=sem, vmem_limit_bytes=VMEM_LIMIT_BYTES)


def _resident(shape):
    nd = len(shape)
    return pl.BlockSpec(shape, lambda *_: (0,) * nd, pipeline_mode=pl.Buffered(1))


def _norm_mod(h, g, shift, scale):
    ms = jnp.mean(h * h, axis=-1, keepdims=True)
    y = h * lax.rsqrt(ms + NORM_EPS) * g
    return y * (1.0 + scale) + shift


def _mod_part(mod_ref, k):
    return mod_ref[:, k * D_MODEL:(k + 1) * D_MODEL]


def _mod_spec(n_ctx_rows, seq, tm, tile0=0):
    def idx(i):
        r0 = (i + tile0) * tm
        grp = jnp.where(r0 < n_ctx_rows, 0, 1 + (r0 - n_ctx_rows) // seq)
        return (grp, 0, 0)
    return pl.BlockSpec((None, 1, 6 * D_MODEL), idx)


def _row_spec(tm, width, tile0=0):
    return pl.BlockSpec((tm, width), lambda i: (i + tile0, 0))


def _h_specs(tm, split):
    if not split:
        return [_row_spec(tm, D_MODEL)]
    return [pl.BlockSpec((tm, D_MODEL), lambda i: (jnp.maximum(i - 1, 0), 0)),
            pl.BlockSpec((tm, D_MODEL), lambda i: (0, 0))]


def _load_h(h_refs):
    if len(h_refs) == 1:
        return h_refs[0][...]
    lat_ref, ctx_ref = h_refs
    return jnp.where(pl.program_id(0) == 0, ctx_ref[...], lat_ref[...])


def _ada_kernel(cs_ref, w_ref, b_ref, o_ref):
    cs = cs_ref[...]
    s = (cs * jax.nn.sigmoid(cs)).astype(_BF16)
    o_ref[...] = jnp.dot(s, w_ref[...].astype(_BF16), preferred_element_type=_F32) + b_ref[...]


def _ada_table(cs, ada_w, ada_b):
    depth = ada_w.shape[0]
    n = ada_w.shape[2]
    return pl.pallas_call(
        _ada_kernel,
        out_shape=jax.ShapeDtypeStruct((depth, MOD_ROWS, n), _F32),
        grid=(depth, n // ADA_COL_TILE),
        in_specs=[
            pl.BlockSpec((MOD_ROWS, D_MODEL), lambda l, j: (0, 0)),
            pl.BlockSpec((None, D_MODEL, ADA_COL_TILE), lambda l, j: (l, 0, j)),
            pl.BlockSpec((None, 1, ADA_COL_TILE), lambda l, j: (l, 0, j)),
        ],
        out_specs=pl.BlockSpec((None, MOD_ROWS, ADA_COL_TILE), lambda l, j: (l, 0, j)),
        compiler_params=_cparams("arbitrary", "arbitrary"),
        name="ada_table",
    )(cs, ada_w, ada_b.reshape(depth, 1, n))


def _a_proj_kernel(*refs, n_h):
    h_refs, (g_ref, mod_ref, wqvo_ref, wkt_ref, wgt_ref, bg_ref,
             q_ref, kt_ref, v_ref, o_ref, gr_ref, gc_ref) = refs[:n_h], refs[n_h:]
    x = _norm_mod(_load_h(h_refs), g_ref[...], _mod_part(mod_ref, 0), _mod_part(mod_ref, 1)).astype(_BF16)
    nt = (((1,), (1,)), ((), ()))
    gt = lax.dot_general(wgt_ref[...], x, nt, preferred_element_type=_F32) + bg_ref[...]

    n_chunks = gt.shape[1] // A_CHUNK

    def by_chunk(rows):
        return jnp.concatenate([rows[:, c * A_CHUNK:(c + 1) * A_CHUNK] for c in range(n_chunks)], axis=0)

    def by_token(x):
        return jnp.concatenate([x[c * A_HEADS:(c + 1) * A_HEADS] for c in range(n_chunks)], axis=1)

    def log_sigmoid(z):
        return jnp.minimum(z, 0.0) - jnp.log1p(jnp.exp(-jnp.abs(z)))

    row_form, col_cm, col_b = [], [], []
    for d in range(2):
        li = by_chunk(gt[2 * d * A_HEADS:(2 * d + 1) * A_HEADS])
        lf = log_sigmoid(by_chunk(gt[(2 * d + 1) * A_HEADS:(2 * d + 2) * A_HEADS]))
        b = _lane_scan(lf, jnp.add, 0.0, d == 1)
        a = li - b
        cm = _lane_scan(a, jnp.maximum, -jnp.inf, d == 1)
        row_form += [by_token(a), by_token(b)]
        col_cm.append(cm)
        col_b.append(b)
    gr_ref[...] = jnp.concatenate(row_form, axis=0)
    pad = jnp.zeros((LANES - 2 * A_HEADS, LANES), _F32)
    for c in range(n_chunks):
        rows = slice(c * A_HEADS, (c + 1) * A_HEADS)
        toks = slice(c * A_CHUNK, (c + 1) * A_CHUNK)
        gc_ref[toks, :LANES] = jnp.concatenate([col_cm[0][rows], col_cm[1][rows], pad], axis=0).T
        gc_ref[toks, LANES:] = jnp.concatenate([col_b[0][rows], col_b[1][rows], pad], axis=0).T

    y = jnp.dot(x, wqvo_ref[...], preferred_element_type=_F32)
    q_ref[...] = (y[:, :A_QK_DIM] * (A_DK ** -0.5)).astype(_BF16)
    v_ref[...] = y[:, A_QK_DIM:A_QK_DIM + D_MODEL].astype(_BF16)
    o_ref[...] = y[:, A_QK_DIM + D_MODEL:]
    kt_ref[...] = lax.dot_general(wkt_ref[...], x, nt, preferred_element_type=_F32).astype(_BF16)


def _a_proj(h_parts, g, mod_l, wqvo, wkt, wgt, bgate, dims):
    rows = dims.rows
    tm = ROW_TILE
    n_gate = 4 * A_HEADS
    return pl.pallas_call(
        functools.partial(_a_proj_kernel, n_h=len(h_parts)),
        out_shape=(
            jax.ShapeDtypeStruct((rows, A_QK_DIM), _BF16),
            jax.ShapeDtypeStruct((A_QK_DIM, rows), _BF16),
            jax.ShapeDtypeStruct((rows, D_MODEL), _BF16),
            jax.ShapeDtypeStruct((rows, D_MODEL), _F32),
            jax.ShapeDtypeStruct((n_gate, rows), _F32),
            jax.ShapeDtypeStruct((rows, 2 * LANES), _F32),
        ),
        grid=(rows // tm,),
        in_specs=_h_specs(tm, len(h_parts) == 2) + [
            _resident((1, D_MODEL)),
            _mod_spec(dims.n_ctx_rows, dims.seq, tm),
            _resident(wqvo.shape),
            _resident(wkt.shape),
            _resident(wgt.shape),
            _resident((n_gate, 1)),
        ],
        out_specs=(
            _row_spec(tm, A_QK_DIM),
            pl.BlockSpec((A_QK_DIM, tm), lambda i: (0, i)),
            _row_spec(tm, D_MODEL),
            _row_spec(tm, D_MODEL),
            pl.BlockSpec((n_gate, tm), lambda i: (0, i)),
            _row_spec(tm, 2 * LANES),
        ),
        compiler_params=_cparams("arbitrary"),
        name="a_proj",
    )(*h_parts, g, mod_l, wqvo, wkt, wgt, bgate)


def _lane_scan(x, op, fill, reverse):
    lane = lax.broadcasted_iota(jnp.int32, x.shape, 1)
    k = 1
    while k < LANES:
        if reverse:
            shifted = jnp.where(lane < LANES - k, pltpu.roll(x, LANES - k, axis=1), fill)
        else:
            shifted = jnp.where(lane >= k, pltpu.roll(x, k, axis=1), fill)
        x = op(x, shifted)
        k *= 2
    return x


def _a_scan_kernel(qf_ref, ktf_ref, vf_ref, grf_ref, gcf_ref, qb_ref, ktb_ref, vb_ref, grb_ref, gcb_ref,
                   hf_ref, hb_ref, cf_ref, cb_ref, mrf_ref, mrb_ref, mc_ref):
    @pl.when(pl.program_id(1) == 0)
    def _():
        cf_ref[...] = jnp.zeros(cf_ref.shape, _F32)
        cb_ref[...] = jnp.zeros(cb_ref.shape, _F32)
        mrf_ref[...] = jnp.full(mrf_ref.shape, -jnp.inf, _F32)
        mrb_ref[...] = jnp.full(mrb_ref.shape, -jnp.inf, _F32)
        mc_ref[...] = jnp.full(mc_ref.shape, -jnp.inf, _F32)

    t_idx = lax.broadcasted_iota(jnp.int32, (A_CHUNK, A_CHUNK), 0)
    s_idx = lax.broadcasted_iota(jnp.int32, (A_CHUNK, A_CHUNK), 1)
    ones_blk = jnp.ones((A_CHUNK, A_DV), _BF16)
    m_col = mc_ref[0:1, :]
    m_col_new = []

    dirs = []
    for d, (gr_ref, gc_ref, mr_ref) in enumerate(((grf_ref, gcf_ref, mrf_ref), (grb_ref, gcb_ref, mrb_ref))):
        last = A_CHUNK - 1 if d == 0 else 0
        a = gr_ref[2 * d * A_HEADS:(2 * d + 1) * A_HEADS, :]
        b = gr_ref[(2 * d + 1) * A_HEADS:(2 * d + 2) * A_HEADS, :]
        m_row = mr_ref[:, 0:1]
        gg_end = jnp.maximum(m_row, jnp.max(a, axis=1, keepdims=True))
        ws = jnp.exp(a - gg_end)
        decay = jnp.exp(m_row - gg_end)
        mr_ref[...] = jnp.broadcast_to(b[:, last:last + 1] + gg_end, mr_ref.shape)
        gg_c = jnp.maximum(m_col, gc_ref[:, :LANES])
        b_plus_gg = gc_ref[:, LANES:] + gg_c
        clamp_c = jnp.exp(-b_plus_gg)
        m_col_new.append(b_plus_gg[last:last + 1, :])
        dirs.append((a, m_row, ws, decay, gg_c, clamp_c))
    lane = lax.broadcasted_iota(jnp.int32, (1, LANES), 1)
    mc_ref[...] = jnp.broadcast_to(jnp.where(lane < A_HEADS, m_col_new[0], m_col_new[1]), mc_ref.shape)

    for hd in range(A_HEADS):
        for d, (q_ref, kt_ref, v_ref, h_ref, c_ref) in enumerate((
                (qf_ref, ktf_ref, vf_ref, hf_ref, cf_ref), (qb_ref, ktb_ref, vb_ref, hb_ref, cb_ref))):
            a, m_row, ws, decay, gg_c, clamp_c = dirs[d]
            mask = (s_idx <= t_idx) if d == 0 else (s_idx >= t_idx)
            col = d * A_HEADS + hd
            gg = jnp.broadcast_to(gg_c[:, col:col + 1], (A_CHUNK, A_CHUNK))
            clamp = jnp.broadcast_to(clamp_c[:, col:col + 1], (A_CHUNK, A_DV))
            q = q_ref[:, hd * A_DK:(hd + 1) * A_DK]
            kt = kt_ref[hd * A_DK:(hd + 1) * A_DK, :]
            v = v_ref[:, hd * A_DV:(hd + 1) * A_DV]
            v_ext = jnp.concatenate([v, ones_blk], axis=1)
            c_old = c_ref[hd]
            s = jnp.dot(q, kt, preferred_element_type=_F32)
            e_keys = jnp.exp(jnp.where(mask, a[hd:hd + 1, :] - gg, -jnp.inf))
            e_state = jnp.exp(m_row[hd:hd + 1, :] - gg[:, :A_DK])
            p = jnp.concatenate([s * e_keys, q.astype(_F32) * e_state], axis=1).astype(_BF16)
            rhs = jnp.concatenate([v_ext, c_old.astype(_BF16)], axis=0)
            num = jnp.dot(p, rhs, preferred_element_type=_F32)
            den = jnp.maximum(jnp.abs(num[:, A_DV:]), clamp)
            h_ref[:, hd * A_DV:(hd + 1) * A_DV] = num[:, :A_DV] / den
            kw = (kt.astype(_F32) * ws[hd:hd + 1, :]).astype(_BF16)
            c_ref[hd] = decay[hd:hd + 1, :] * c_old + jnp.dot(kw, v_ext, preferred_element_type=_F32)


def _a_scan(q, kt, v, gr, gc, dims):
    rows = q.shape[0]
    ncc = dims.ctx_len // A_CHUNK
    ncl = dims.seq // A_CHUNK
    lat0 = dims.batch * ncc

    def fwd_blk(b, j):
        return jnp.where(j < ncc, b * ncc + j, lat0 + b * ncl + (j - ncc))

    def bwd_blk(b, j):
        return jnp.where(j < ncc, b * ncc + (ncc - 1 - j), lat0 + b * ncl + (ncl - 1 - (j - ncc)))

    def specs(blk):
        return [
            pl.BlockSpec((A_CHUNK, A_QK_DIM), lambda b, j: (blk(b, j), 0)),
            pl.BlockSpec((A_QK_DIM, A_CHUNK), lambda b, j: (0, blk(b, j))),
            pl.BlockSpec((A_CHUNK, D_MODEL), lambda b, j: (blk(b, j), 0)),
            pl.BlockSpec((4 * A_HEADS, A_CHUNK), lambda b, j: (0, blk(b, j))),
            pl.BlockSpec((A_CHUNK, 2 * LANES), lambda b, j: (blk(b, j), 0)),
        ]

    state = pltpu.VMEM((A_HEADS, A_DK, 2 * A_DV), _F32)
    stab = pltpu.VMEM((A_HEADS, LANES), _F32)
    return pl.pallas_call(
        _a_scan_kernel,
        out_shape=(jax.ShapeDtypeStruct((rows, D_MODEL), _F32),) * 2,
        grid=(dims.batch, ncc + ncl),
        in_specs=specs(fwd_blk) + specs(bwd_blk),
        out_specs=(
            pl.BlockSpec((A_CHUNK, D_MODEL), lambda b, j: (fwd_blk(b, j), 0)),
            pl.BlockSpec((A_CHUNK, D_MODEL), lambda b, j: (bwd_blk(b, j), 0)),
        ),
        scratch_shapes=[state, state, stab, stab, stab],
        compiler_params=_cparams("arbitrary", "arbitrary"),
        name="a_scan",
    )(q, kt, v, gr, gc, q, kt, v, gr, gc)


def _a_mix(hf_ref, hb_ref, o_ref, hg_ref):
    hs = hf_ref[...] + hb_ref[...]
    parts = []
    for hd in range(A_HEADS):
        x = hs[:, hd * A_DV:(hd + 1) * A_DV]
        parts.append(x * lax.rsqrt(jnp.mean(x * x, axis=-1, keepdims=True) + NORM_EPS))
    y = jnp.concatenate(parts, axis=1) * hg_ref[...]
    return (jax.nn.sigmoid(o_ref[...]) * y).astype(_BF16)


ROPE_HALF = B_HEAD_DIM // 4


def _swap_halves_lanes(x):
    lane = lax.broadcasted_iota(jnp.int32, x.shape, 1)
    fwd = pltpu.roll(x, LANES - ROPE_HALF, axis=1)
    back = pltpu.roll(x, ROPE_HALF, axis=1)
    return jnp.where(lane % (2 * ROPE_HALF) < ROPE_HALF, fwd, back)


def _swap_halves_rows(x):
    parts = []
    for r0 in range(0, x.shape[0], 2 * ROPE_HALF):
        parts += [x[r0 + ROPE_HALF:r0 + 2 * ROPE_HALF], x[r0:r0 + ROPE_HALF]]
    return jnp.concatenate(parts, axis=0)


def _b_proj_kernel(h_ref, g_ref, mod_ref, wqvt_ref, wk_ref, cos_ref, sin_ref, cost_ref, sint_ref,
                   qt_ref, k_ref, vt_ref):
    x = _norm_mod(h_ref[...], g_ref[...], _mod_part(mod_ref, 0), _mod_part(mod_ref, 1)).astype(_BF16)
    nt = (((1,), (1,)), ((), ()))
    yt = lax.dot_general(wqvt_ref[...], x, nt, preferred_element_type=_F32)
    cost, sint = cost_ref[...], sint_ref[...]
    for hd in range(B_Q_HEADS):
        rows = slice(hd * B_HEAD_DIM, (hd + 1) * B_HEAD_DIM)
        qt = yt[rows, :]
        qt = (qt * cost + _swap_halves_rows(qt) * sint) * (B_HEAD_DIM ** -0.5)
        qt_ref[rows, :] = qt.astype(_BF16)
    vt_ref[...] = yt[D_MODEL:, :].astype(_BF16)
    cos, sin = cos_ref[...], sin_ref[...]
    k = jnp.dot(x, wk_ref[...], preferred_element_type=_F32)
    for c0 in range(0, B_KV_DIM, LANES):
        kc = k[:, c0:c0 + LANES]
        k_ref[:, c0:c0 + LANES] = (kc * cos + _swap_halves_lanes(kc) * sin).astype(_BF16)


def _b_proj(h, g, mod_l, wqvt, wk, rope, dims):
    rows = h.shape[0]
    tm = ROW_TILE
    cos, sin, cost, sint = rope
    return pl.pallas_call(
        _b_proj_kernel,
        out_shape=(
            jax.ShapeDtypeStruct((D_MODEL, rows), _BF16),
            jax.ShapeDtypeStruct((rows, B_KV_DIM), _BF16),
            jax.ShapeDtypeStruct((B_KV_DIM, rows), _BF16),
        ),
        grid=(rows // tm,),
        in_specs=[
            _row_spec(tm, D_MODEL),
            _resident((1, D_MODEL)),
            _mod_spec(dims.n_ctx_rows, dims.seq, tm),
            _resident(wqvt.shape),
            _resident(wk.shape),
            _row_spec(tm, LANES),
            _row_spec(tm, LANES),
            pl.BlockSpec((B_HEAD_DIM, tm), lambda i: (0, i)),
            pl.BlockSpec((B_HEAD_DIM, tm), lambda i: (0, i)),
        ],
        out_specs=(
            pl.BlockSpec((D_MODEL, tm), lambda i: (0, i)),
            _row_spec(tm, B_KV_DIM),
            pl.BlockSpec((B_KV_DIM, tm), lambda i: (0, i)),
        ),
        compiler_params=_cparams("arbitrary"),
        name="b_proj",
    )(h, g, mod_l, wqvt, wk, cos, sin, cost, sint)


B_ONES_ROWS = 16


def _b_attn_kernel(sink_ref, qt_ref, kl_ref, kc_ref, kr_ref, kx_ref, vtl_ref, vtc_ref, vtr_ref, vtx_ref,
                   o_ref, s_ref, *, n_ctx_blocks, blocks_per_seq, ctx_len):
    i = pl.program_id(0)
    is_lat = i >= n_ctx_blocks
    n = (i - n_ctx_blocks) % blocks_per_seq
    neg = -jnp.inf
    has_left = jnp.logical_and(is_lat, n >= 1)
    has_right = jnp.logical_and(is_lat, n <= blocks_per_seq - 2)
    key = lax.broadcasted_iota(jnp.int32, (B_BLOCK, B_BLOCK), 0)
    qry = lax.broadcasted_iota(jnp.int32, (B_BLOCK, B_BLOCK), 1)
    bias_l = jnp.where(jnp.logical_and(key >= qry, has_left), 0.0, neg)
    bias_r = jnp.where(jnp.logical_and(key <= qry, has_right), 0.0, neg)
    bias_l = jnp.concatenate([bias_l] * B_GROUP, axis=1)
    bias_r = jnp.concatenate([bias_r] * B_GROUP, axis=1)
    bias_c = jnp.where(is_lat, 0.0, neg)

    n_q = B_GROUP * B_BLOCK
    head_of_lane = lax.broadcasted_iota(jnp.int32, (1, n_q), 1) // B_BLOCK
    ones_rows = jnp.ones((B_ONES_ROWS, B_BLOCK), _BF16)
    n_ctx_tiles = ctx_len // B_BLOCK

    for g in range(B_KV_HEADS):
        qt = jnp.concatenate(
            [qt_ref[(g * B_GROUP + j) * B_HEAD_DIM:(g * B_GROUP + j + 1) * B_HEAD_DIM, :]
             for j in range(B_GROUP)], axis=1)
        sink = jnp.zeros((1, n_q), _F32)
        for j in range(B_GROUP):
            sink = jnp.where(head_of_lane == j, sink_ref[g * B_GROUP + j], sink)
        ks = slice(g * B_HEAD_DIM, (g + 1) * B_HEAD_DIM)
        k_tiles = [(kl_ref[:, ks], bias_l), (kc_ref[:, ks], bias_c), (kr_ref[:, ks], bias_r)]
        vt_tiles = [vtl_ref[ks, :], vtc_ref[ks, :], vtr_ref[ks, :]]
        for t in range(n_ctx_tiles):
            k_tiles.append((kx_ref[t * B_BLOCK:(t + 1) * B_BLOCK, ks], None))
            vt_tiles.append(vtx_ref[ks, t * B_BLOCK:(t + 1) * B_BLOCK])
        m_tile = None
        for t, (k, bias) in enumerate(k_tiles):
            s = jnp.dot(k, qt, preferred_element_type=_F32)
            if bias is not None:
                s = s + bias
            s_ref[t] = s
            m_tile = s if m_tile is None else jnp.maximum(m_tile, s)
        m = jnp.maximum(sink, jnp.max(m_tile, axis=0, keepdims=True))
        acc = jnp.zeros((B_HEAD_DIM + B_ONES_ROWS, n_q), _F32)
        for t, vt in enumerate(vt_tiles):
            p = jnp.exp(s_ref[t] - m).astype(_BF16)
            vt_ext = jnp.concatenate([vt, ones_rows], axis=0)
            acc = acc + jnp.dot(vt_ext, p, preferred_element_type=_F32)
        denom = jnp.exp(sink - m) + acc[B_HEAD_DIM:B_HEAD_DIM + 1, :]
        out_t = acc[:B_HEAD_DIM, :] * (1.0 / denom)
        for pair in range(B_GROUP // 2):
            two = jnp.concatenate([out_t[:, (2 * pair) * B_BLOCK:(2 * pair + 1) * B_BLOCK],
                                   out_t[:, (2 * pair + 1) * B_BLOCK:(2 * pair + 2) * B_BLOCK]], axis=0)
            c0 = (g * B_GROUP + 2 * pair) * B_HEAD_DIM
            o_ref[:, c0:c0 + 2 * B_HEAD_DIM] = two.T.astype(_BF16)


def _b_attn(sinks, qt, k, vt, dims):
    rows = k.shape[0]
    ncc = dims.ctx_len // B_BLOCK
    ncl = dims.seq // B_BLOCK
    lat0 = dims.batch * ncc
    nblk = rows // B_BLOCK

    def batch_of(i):
        return jnp.where(i < lat0, i // ncc, (i - lat0) // ncl)

    def left(i):
        return jnp.maximum(i - 1, 0)

    def right(i):
        return jnp.minimum(i + 1, nblk - 1)

    def k_spec(f):
        return pl.BlockSpec((B_BLOCK, B_KV_DIM), lambda i: (f(i), 0))

    def vt_spec(f):
        return pl.BlockSpec((B_KV_DIM, B_BLOCK), lambda i: (0, f(i)))

    n_tiles = 3 + ncc
    return pl.pallas_call(
        functools.partial(_b_attn_kernel, n_ctx_blocks=lat0, blocks_per_seq=ncl, ctx_len=dims.ctx_len),
        out_shape=jax.ShapeDtypeStruct((rows, D_MODEL), _BF16),
        grid=(nblk,),
        in_specs=[
            pl.BlockSpec(memory_space=pltpu.SMEM),
            pl.BlockSpec((D_MODEL, B_BLOCK), lambda i: (0, i)),
            k_spec(left), k_spec(lambda i: i), k_spec(right),
            pl.BlockSpec((dims.ctx_len, B_KV_DIM), lambda i: (batch_of(i), 0)),
            vt_spec(left), vt_spec(lambda i: i), vt_spec(right),
            pl.BlockSpec((B_KV_DIM, dims.ctx_len), lambda i: (0, batch_of(i))),
        ],
        out_specs=pl.BlockSpec((B_BLOCK, D_MODEL), lambda i: (i, 0)),
        scratch_shapes=[pltpu.VMEM((n_tiles, B_BLOCK, B_GROUP * B_BLOCK), _F32)],
        compiler_params=_cparams("arbitrary"),
        name="b_attn",
    )(sinks, qt, k, k, k, k, vt, vt, vt, vt)


def _c_proj_kernel(h_ref, g_ref, mod_ref, w_ref, bg_ref, u_ref):
    x = _norm_mod(h_ref[...], g_ref[...], _mod_part(mod_ref, 0), _mod_part(mod_ref, 1)).astype(_BF16)
    y = jnp.dot(x, w_ref[...], preferred_element_type=_F32)
    bg_ref[...] = y[:, :D_MODEL]
    u_ref[...] = y[:, D_MODEL:2 * D_MODEL] * y[:, 2 * D_MODEL:]


def _c_proj(h, g, mod_l, w_in, dims):
    rows = h.shape[0]
    tm = ROW_TILE
    return pl.pallas_call(
        _c_proj_kernel,
        out_shape=(jax.ShapeDtypeStruct((rows, D_MODEL), _F32),) * 2,
        grid=(rows // tm,),
        in_specs=[
            _row_spec(tm, D_MODEL),
            _resident((1, D_MODEL)),
            _mod_spec(dims.n_ctx_rows, dims.seq, tm),
            _resident(w_in.shape),
        ],
        out_specs=(_row_spec(tm, D_MODEL),) * 2,
        compiler_params=_cparams("arbitrary"),
        name="c_proj",
    )(h, g, mod_l, w_in)


def _c_mix(bg_ref, u_ref, up_ref, un_ref, cw_ref, cb_ref, tile, dims):
    tm = u_ref.shape[0]
    u = u_ref[...]
    row = lax.broadcasted_iota(jnp.int32, (tm, 1), 0)
    g_row = tile * tm + row
    in_ctx = g_row < dims.n_ctx_rows
    pos = jnp.where(in_ctx, g_row % dims.ctx_len, (g_row - dims.n_ctx_rows) % dims.seq)
    length = jnp.where(in_ctx, dims.ctx_len, dims.seq)
    prev = jnp.where(row == 0, up_ref[SUBLANES - 1:SUBLANES, :], pltpu.roll(u, 1, axis=0))
    nxt = jnp.where(row == tm - 1, un_ref[0:1, :], pltpu.roll(u, tm - 1, axis=0))
    prev = jnp.where(pos == 0, 0.0, prev)
    nxt = jnp.where(pos == length - 1, 0.0, nxt)
    conv = prev * cw_ref[0:1, :] + u * cw_ref[1:2, :] + nxt * cw_ref[2:3, :] + cb_ref[...]
    return (bg_ref[...] * conv).astype(_BF16)


_N_MIX_REFS = (4, 1, 6)


def _post_kernel(*refs, kind, n_h, tile0, final_norm, dims):
    n_mix = _N_MIX_REFS[kind]
    h_refs, mod_ref = refs[:n_h], refs[n_h]
    mix_refs = refs[n_h + 1:n_h + 1 + n_mix]
    wo_ref, g_ref, w1_ref, w2_ref, fg_ref, out_ref = refs[n_h + 1 + n_mix:]
    if kind == 0:
        y = _a_mix(*mix_refs)
    elif kind == 1:
        y = mix_refs[0][...]
    else:
        y = _c_mix(*mix_refs, pl.program_id(0) + tile0, dims)
    h = _load_h(h_refs)
    h = h + _mod_part(mod_ref, 2) * jnp.dot(y, wo_ref[...], preferred_element_type=_F32)
    x = _norm_mod(h, g_ref[...], _mod_part(mod_ref, 3), _mod_part(mod_ref, 4)).astype(_BF16)
    acc = jnp.zeros(h.shape, _F32)
    for c in range(MLP_HIDDEN // MLP_HIDDEN_CHUNK):
        cols = slice(c * MLP_HIDDEN_CHUNK, (c + 1) * MLP_HIDDEN_CHUNK)
        u = jnp.dot(x, w1_ref[:, cols], preferred_element_type=_F32)
        u = jnp.square(jnp.maximum(u, 0.0)).astype(_BF16)
        acc = acc + jnp.dot(u, w2_ref[cols, :], preferred_element_type=_F32)
    out = h + _mod_part(mod_ref, 5) * acc
    if final_norm:
        ms = jnp.mean(out * out, axis=-1, keepdims=True)
        out = out * lax.rsqrt(ms + NORM_EPS) * fg_ref[...]
    out_ref[...] = out


def _post(kind, h_parts, mod_l, mix, w_out, g, w1, w2, final_g, dims, tile0=0, final_norm=False):
    tm = ROW_TILE
    n_tiles = dims.rows // tm - tile0
    row = functools.partial(_row_spec, tm, D_MODEL, tile0)
    if kind == 0:
        mix_specs = [row(), row(), row(), _resident((1, D_MODEL))]
    elif kind == 1:
        mix_specs = [row()]
    else:
        bg, u, conv_w, conv_b = mix
        per = tm // SUBLANES
        last = dims.rows // SUBLANES - 1
        mix = (bg, u, u, u, conv_w, conv_b)
        mix_specs = [
            row(), row(),
            pl.BlockSpec((SUBLANES, D_MODEL), lambda i: (jnp.maximum((i + tile0) * per - 1, 0), 0)),
            pl.BlockSpec((SUBLANES, D_MODEL), lambda i: (jnp.minimum((i + tile0 + 1) * per, last), 0)),
            _resident(conv_w.shape), _resident((1, D_MODEL)),
        ]
    assert len(h_parts) == 1 or tile0 == 0
    h_specs = _h_specs(tm, True) if len(h_parts) == 2 else [row()]
    return pl.pallas_call(
        functools.partial(_post_kernel, kind=kind, n_h=len(h_parts), tile0=tile0,
                          final_norm=final_norm, dims=dims),
        out_shape=jax.ShapeDtypeStruct((n_tiles * tm, D_MODEL), _F32),
        grid=(n_tiles,),
        in_specs=h_specs + [_mod_spec(dims.n_ctx_rows, dims.seq, tm, tile0)] + mix_specs + [
            _resident(w_out.shape),
            _resident((1, D_MODEL)),
            _resident(w1.shape),
            _resident(w2.shape),
            _resident((1, D_MODEL)),
        ],
        out_specs=_row_spec(tm, D_MODEL),
        compiler_params=_cparams("arbitrary"),
        name="post",
    )(*h_parts, mod_l, *mix, w_out, g, w1, w2, final_g)


class _Dims:
    def __init__(self, batch, seq, ctx_len):
        self.batch = batch
        self.seq = seq
        self.ctx_len = ctx_len
        self.n_ctx_rows = batch * ctx_len
        self.rows = self.n_ctx_rows + batch * seq


def _rope_tables(dims):
    n_freq = B_HEAD_DIM // 4
    t = jnp.arange(dims.seq)
    inv_freq = ROPE_BASE ** (-jnp.arange(n_freq, dtype=_F32) / n_freq)
    ang_row = (t // GRID_W).astype(_F32)[:, None] * inv_freq
    ang_col = (t % GRID_W).astype(_F32)[:, None] * inv_freq
    ang = jnp.concatenate([ang_row, ang_row, ang_col, ang_col], axis=1)
    sign = jnp.tile(jnp.concatenate([-jnp.ones(n_freq, _F32), jnp.ones(n_freq, _F32)]), 2)
    cos = jnp.tile(jnp.cos(ang), (dims.batch, 1))
    sin = jnp.tile(jnp.sin(ang) * sign, (dims.batch, 1))
    cos = jnp.concatenate([jnp.ones((dims.n_ctx_rows, B_HEAD_DIM), _F32), cos], axis=0)
    sin = jnp.concatenate([jnp.zeros((dims.n_ctx_rows, B_HEAD_DIM), _F32), sin], axis=0)
    return jnp.tile(cos, (1, 2)), jnp.tile(sin, (1, 2)), cos.T, sin.T


def kernel(x, c, ctx, c_ctx, ada_w, ada_b, norm_g, final_g, mlp_w1, mlp_w2,
           a_w_in, a_w_gate, a_b_gate, a_head_g, a_w_out,
           b_w_qkv, b_sinks, b_w_out, c_w_in, c_conv_w, c_conv_b, c_w_out):
    batch, seq, d = x.shape
    ctx_len = ctx.shape[1]
    depth = ada_w.shape[0]
    dims = _Dims(batch, seq, ctx_len)
    assert d == D_MODEL and seq % ROW_TILE == 0 and dims.n_ctx_rows % ROW_TILE == 0
    assert ctx_len % A_CHUNK == 0 and 1 + batch <= MOD_ROWS

    assert dims.n_ctx_rows == ROW_TILE
    h_parts = (x.reshape(-1, d), ctx.reshape(-1, d))
    cs = jnp.concatenate([c_ctx[None], c, jnp.zeros((MOD_ROWS - 1 - batch, d), _F32)], axis=0)
    mod = _ada_table(cs, ada_w, ada_b).reshape(depth, MOD_ROWS, 1, 6 * d)
    rope = _rope_tables(dims)
    fg = final_g.reshape(1, d)

    for l in range(depth):
        kind, j = l % N_MIXERS, l // N_MIXERS
        mod_l = mod[l]
        g0 = norm_g[l, 0].reshape(1, d)
        g1 = norm_g[l, 1].reshape(1, d)
        last_layer = l == depth - 1
        if kind == 0:
            w_in = a_w_in[j]
            wqvo = jnp.concatenate([w_in[:, :A_QK_DIM], w_in[:, 2 * A_QK_DIM:]], axis=1).astype(_BF16)
            wkt = w_in[:, A_QK_DIM:2 * A_QK_DIM].T.astype(_BF16)
            wgt = a_w_gate[j].T.astype(_BF16)
            q, kt, v, o, gr, gc = _a_proj(h_parts, g0, mod_l, wqvo, wkt, wgt, a_b_gate[j].reshape(-1, 1), dims)
            hf, hb = _a_scan(q, kt, v, gr, gc, dims)
            mix = (hf, hb, o, a_head_g[j].reshape(1, d))
            w_out = a_w_out[j]
        elif kind == 1:
            (h,) = h_parts
            w = b_w_qkv[j]
            wqvt = jnp.concatenate([w[:, :D_MODEL], w[:, D_MODEL + B_KV_DIM:]], axis=1).T.astype(_BF16)
            wk = w[:, D_MODEL:D_MODEL + B_KV_DIM].astype(_BF16)
            qt, k, vt = _b_proj(h, g0, mod_l, wqvt, wk, rope, dims)
            mix = (_b_attn(b_sinks[j], qt, k, vt, dims),)
            w_out = b_w_out[j]
        else:
            (h,) = h_parts
            bg, u = _c_proj(h, g0, mod_l, c_w_in[j].astype(_BF16), dims)
            mix = (bg, u, c_conv_w[j], c_conv_b[j].reshape(1, d))
            w_out = c_w_out[j]
        tile0 = dims.n_ctx_rows // ROW_TILE if last_layer else 0
        h_parts = (_post(kind, h_parts, mod_l, mix, w_out.astype(_BF16), g1,
                         mlp_w1[l].astype(_BF16), mlp_w2[l].astype(_BF16), fg, dims,
                         tile0=tile0, final_norm=last_layer),)
    return h_parts[0].reshape(batch, seq, d)
```

```python
import functools

import jax
import jax.numpy as jnp
from jax import lax
from jax.experimental import pallas as pl
from jax.experimental.pallas import tpu as pltpu

D_MODEL = 1024
GRID_W = 64
N_MIXERS = 3
A_HEADS = 8
A_QK_DIM = D_MODEL // 2
A_DK = A_QK_DIM // A_HEADS
A_DV = D_MODEL // A_HEADS
A_CHUNK = 128
A_STEP_CHUNKS = 2
B_Q_HEADS = 16
B_KV_HEADS = 4
B_HEAD_DIM = D_MODEL // B_Q_HEADS
B_GROUP = B_Q_HEADS // B_KV_HEADS
B_KV_DIM = B_KV_HEADS * B_HEAD_DIM
B_BLOCK = 128
ROPE_BASE = 10000.0
MLP_HIDDEN = 4 * D_MODEL
NORM_EPS = 1e-6
LOG2E = 1.4426950408889634

LANES = 128
SUBLANES = 8
VMEM_LIMIT_BYTES = 56 * 1024 * 1024

ROW_TILE = 512
MLP_HIDDEN_CHUNK = 1024
MOD_ROWS = 8
ADA_COL_TILE = 1536

_BF16 = jnp.bfloat16
_F32 = jnp.float32


def _cparams(*sem):
    return pltpu.CompilerParams(dimension_semantics=sem, vmem_limit_bytes=VMEM_LIMIT_BYTES)


def _resident(shape):
    nd = len(shape)
    return pl.BlockSpec(shape, lambda *_: (0,) * nd, pipeline_mode=pl.Buffered(1))


def _resident_layer(shape, layer):
    nd = len(shape) - 1
    return pl.BlockSpec((None,) + tuple(shape[1:]), lambda *_: (layer,) + (0,) * nd,
                        pipeline_mode=pl.Buffered(1))


def _norm_mod(h, g, shift, scale):
    ms = jnp.mean(h * h, axis=-1, keepdims=True)
    y = h * lax.rsqrt(ms + NORM_EPS) * g
    return y * (1.0 + scale) + shift


def _mod_part(mod_ref, k):
    return mod_ref[:, k * D_MODEL:(k + 1) * D_MODEL]


def _mod_spec(n_ctx_rows, seq, tm, tile0=0):
    def idx(i):
        r0 = (i + tile0) * tm
        grp = jnp.where(r0 < n_ctx_rows, 0, 1 + (r0 - n_ctx_rows) // seq)
        return (grp, 0, 0)
    return pl.BlockSpec((None, 1, 6 * D_MODEL), idx)


def _row_spec(tm, width, tile0=0):
    return pl.BlockSpec((tm, width), lambda i: (i + tile0, 0))


def _h_specs(tm, split):
    if not split:
        return [_row_spec(tm, D_MODEL)]
    return [pl.BlockSpec((tm, D_MODEL), lambda i: (jnp.maximum(i - 1, 0), 0)),
            pl.BlockSpec((tm, D_MODEL), lambda i: (0, 0))]


def _load_h(h_refs):
    if len(h_refs) == 1:
        return h_refs[0][...]
    lat_ref, ctx_ref = h_refs
    return jnp.where(pl.program_id(0) == 0, ctx_ref[...], lat_ref[...])


def _ada_kernel(cs_ref, w_ref, b_ref, o_ref):
    cs = cs_ref[...]
    s = (cs * jax.nn.sigmoid(cs)).astype(_BF16)
    o_ref[...] = jnp.dot(s, w_ref[...].astype(_BF16), preferred_element_type=_F32) + b_ref[...]


def _ada_table(cs, ada_w, ada_b):
    depth = ada_w.shape[0]
    n = ada_w.shape[2]
    return pl.pallas_call(
        _ada_kernel,
        out_shape=jax.ShapeDtypeStruct((depth, MOD_ROWS, n), _F32),
        grid=(depth, n // ADA_COL_TILE),
        in_specs=[
            pl.BlockSpec((MOD_ROWS, D_MODEL), lambda l, j: (0, 0)),
            pl.BlockSpec((None, D_MODEL, ADA_COL_TILE), lambda l, j: (l, 0, j)),
            pl.BlockSpec((None, 1, ADA_COL_TILE), lambda l, j: (l, 0, j)),
        ],
        out_specs=pl.BlockSpec((None, MOD_ROWS, ADA_COL_TILE), lambda l, j: (l, 0, j)),
        compiler_params=_cparams("arbitrary", "arbitrary"),
        name="ada_table",
    )(cs, ada_w, ada_b.reshape(depth, 1, n))


def _a_proj_kernel(*refs, n_h):
    h_refs, (g_ref, mod_ref, wqvo_ref, wkt_ref, wgt_ref, bg_ref,
             q_ref, kt_ref, v_ref, o_ref, gr_ref, gc_ref) = refs[:n_h], refs[n_h:]
    x = _norm_mod(_load_h(h_refs), g_ref[...], _mod_part(mod_ref, 0), _mod_part(mod_ref, 1)).astype(_BF16)
    nt = (((1,), (1,)), ((), ()))
    gt = lax.dot_general(wgt_ref[...], x, nt, preferred_element_type=_F32) + bg_ref[...]

    n_chunks = gt.shape[1] // A_CHUNK

    def by_chunk(rows):
        return jnp.concatenate([rows[:, c * A_CHUNK:(c + 1) * A_CHUNK] for c in range(n_chunks)], axis=0)

    def by_token(x):
        return jnp.concatenate([x[c * A_HEADS:(c + 1) * A_HEADS] for c in range(n_chunks)], axis=1)

    def log_sigmoid(z):
        return jnp.minimum(z, 0.0) - jnp.log1p(jnp.exp(-jnp.abs(z)))

    row_form, col_cm, col_b = [], [], []
    for d in range(2):
        li = by_chunk(gt[2 * d * A_HEADS:(2 * d + 1) * A_HEADS]) * LOG2E
        lf = log_sigmoid(by_chunk(gt[(2 * d + 1) * A_HEADS:(2 * d + 2) * A_HEADS])) * LOG2E
        b = _lane_scan(lf, jnp.add, 0.0, d == 1)
        a = li - b
        cm = _lane_scan(a, jnp.maximum, -jnp.inf, d == 1)
        row_form += [by_token(a), by_token(b)]
        col_cm.append(cm)
        col_b.append(b)
    gr_ref[...] = jnp.concatenate(row_form, axis=0)
    pad = jnp.zeros((LANES - 2 * A_HEADS, LANES), _F32)
    for c in range(n_chunks):
        rows = slice(c * A_HEADS, (c + 1) * A_HEADS)
        toks = slice(c * A_CHUNK, (c + 1) * A_CHUNK)
        gc_ref[toks, :LANES] = jnp.concatenate([col_cm[0][rows], col_cm[1][rows], pad], axis=0).T
        gc_ref[toks, LANES:] = jnp.concatenate([col_b[0][rows], col_b[1][rows], pad], axis=0).T

    y = jnp.dot(x, wqvo_ref[...], preferred_element_type=_F32)
    q_ref[...] = (y[:, :A_QK_DIM] * (A_DK ** -0.5)).astype(_BF16)
    v_ref[...] = y[:, A_QK_DIM:A_QK_DIM + D_MODEL].astype(_BF16)
    o_ref[...] = y[:, A_QK_DIM + D_MODEL:]
    kt_ref[...] = lax.dot_general(wkt_ref[...], x, nt, preferred_element_type=_F32).astype(_BF16)


def _a_proj(h_parts, g, mod_l, wqvo, wkt, wgt, bgate, dims):
    rows = dims.rows
    tm = ROW_TILE
    n_gate = 4 * A_HEADS
    return pl.pallas_call(
        functools.partial(_a_proj_kernel, n_h=len(h_parts)),
        out_shape=(
            jax.ShapeDtypeStruct((rows, A_QK_DIM), _BF16),
            jax.ShapeDtypeStruct((A_QK_DIM, rows), _BF16),
            jax.ShapeDtypeStruct((rows, D_MODEL), _BF16),
            jax.ShapeDtypeStruct((rows, D_MODEL), _F32),
            jax.ShapeDtypeStruct((n_gate, rows), _F32),
            jax.ShapeDtypeStruct((rows, 2 * LANES), _F32),
        ),
        grid=(rows // tm,),
        in_specs=_h_specs(tm, len(h_parts) == 2) + [
            _resident((1, D_MODEL)),
            _mod_spec(dims.n_ctx_rows, dims.seq, tm),
            _resident(wqvo.shape),
            _resident(wkt.shape),
            _resident(wgt.shape),
            _resident((n_gate, 1)),
        ],
        out_specs=(
            _row_spec(tm, A_QK_DIM),
            pl.BlockSpec((A_QK_DIM, tm), lambda i: (0, i)),
            _row_spec(tm, D_MODEL),
            _row_spec(tm, D_MODEL),
            pl.BlockSpec((n_gate, tm), lambda i: (0, i)),
            _row_spec(tm, 2 * LANES),
        ),
        compiler_params=_cparams("arbitrary"),
        name="a_proj",
    )(*h_parts, g, mod_l, wqvo, wkt, wgt, bgate)


def _lane_scan(x, op, fill, reverse):
    lane = lax.broadcasted_iota(jnp.int32, x.shape, 1)
    k = 1
    while k < LANES:
        if reverse:
            shifted = jnp.where(lane < LANES - k, pltpu.roll(x, LANES - k, axis=1), fill)
        else:
            shifted = jnp.where(lane >= k, pltpu.roll(x, k, axis=1), fill)
        x = op(x, shifted)
        k *= 2
    return x


def _a_scan_kernel(qf_ref, ktf_ref, vf_ref, grf_ref, gcf_ref, qb_ref, ktb_ref, vb_ref, grb_ref, gcb_ref,
                   hf_ref, hb_ref, cf_ref, cb_ref, mrf_ref, mrb_ref, mc_ref):
    @pl.when(pl.program_id(1) == 0)
    def _():
        cf_ref[...] = jnp.zeros(cf_ref.shape, _F32)
        cb_ref[...] = jnp.zeros(cb_ref.shape, _F32)
        mrf_ref[...] = jnp.full(mrf_ref.shape, -jnp.inf, _F32)
        mrb_ref[...] = jnp.full(mrb_ref.shape, -jnp.inf, _F32)
        mc_ref[...] = jnp.full(mc_ref.shape, -jnp.inf, _F32)

    t_idx = lax.broadcasted_iota(jnp.int32, (A_CHUNK, A_CHUNK), 0)
    s_idx = lax.broadcasted_iota(jnp.int32, (A_CHUNK, A_CHUNK), 1)
    ones_blk = jnp.ones((A_CHUNK, A_DV), _BF16)
    lane = lax.broadcasted_iota(jnp.int32, (1, LANES), 1)
    for sub in range(A_STEP_CHUNKS):
        toks = (slice(sub * A_CHUNK, (sub + 1) * A_CHUNK),
                slice((A_STEP_CHUNKS - 1 - sub) * A_CHUNK, (A_STEP_CHUNKS - sub) * A_CHUNK))
        _a_chunk_pair(toks, (qf_ref, qb_ref), (ktf_ref, ktb_ref), (vf_ref, vb_ref), (grf_ref, grb_ref),
                      (gcf_ref, gcb_ref), (hf_ref, hb_ref), (cf_ref, cb_ref), (mrf_ref, mrb_ref), mc_ref,
                      t_idx, s_idx, ones_blk, lane)


def _a_chunk_pair(toks, q_refs, kt_refs, v_refs, gr_refs, gc_refs, h_refs, c_refs, mr_refs, mc_ref,
                  t_idx, s_idx, ones_blk, lane):
    m_col = mc_ref[0:1, :]
    m_col_new = []
    dirs = []
    for d in range(2):
        gr_ref, gc_ref, mr_ref, tok = gr_refs[d], gc_refs[d], mr_refs[d], toks[d]
        last = A_CHUNK - 1 if d == 0 else 0
        a = gr_ref[2 * d * A_HEADS:(2 * d + 1) * A_HEADS, tok]
        b = gr_ref[(2 * d + 1) * A_HEADS:(2 * d + 2) * A_HEADS, tok]
        m_row = mr_ref[:, 0:1]
        gg_end = jnp.maximum(m_row, jnp.max(a, axis=1, keepdims=True))
        ws = jnp.exp2(a - gg_end)
        decay = jnp.exp2(m_row - gg_end)
        mr_ref[...] = jnp.broadcast_to(b[:, last:last + 1] + gg_end, mr_ref.shape)
        gg_c = jnp.maximum(m_col, gc_ref[tok, :LANES])
        b_plus_gg = gc_ref[tok, LANES:] + gg_c
        clamp_c = jnp.exp2(-b_plus_gg)
        m_col_new.append(b_plus_gg[last:last + 1, :])
        dirs.append((a, m_row, ws, decay, gg_c, clamp_c))
    mc_ref[...] = jnp.broadcast_to(jnp.where(lane < A_HEADS, m_col_new[0], m_col_new[1]), mc_ref.shape)

    for hd in range(A_HEADS):
        for d in range(2):
            q_ref, kt_ref, v_ref, h_ref, c_ref, tok = q_refs[d], kt_refs[d], v_refs[d], h_refs[d], c_refs[d], toks[d]
            a, m_row, ws, decay, gg_c, clamp_c = dirs[d]
            mask = (s_idx <= t_idx) if d == 0 else (s_idx >= t_idx)
            col = d * A_HEADS + hd
            gg = jnp.broadcast_to(gg_c[:, col:col + 1], (A_CHUNK, A_CHUNK))
            clamp = jnp.broadcast_to(clamp_c[:, col:col + 1], (A_CHUNK, A_DV))
            q = q_ref[tok, hd * A_DK:(hd + 1) * A_DK]
            kt = kt_ref[hd * A_DK:(hd + 1) * A_DK, tok]
            v = v_ref[tok, hd * A_DV:(hd + 1) * A_DV]
            v_ext = jnp.concatenate([v, ones_blk], axis=1)
            c_old = c_ref[hd]
            s = jnp.dot(q, kt, preferred_element_type=_F32)
            e_keys = jnp.exp2(jnp.where(mask, a[hd:hd + 1, :] - gg, -jnp.inf))
            e_state = jnp.exp2(m_row[hd:hd + 1, :] - gg[:, :A_DK])
            p = jnp.concatenate([s * e_keys, q.astype(_F32) * e_state], axis=1).astype(_BF16)
            rhs = jnp.concatenate([v_ext, c_old.astype(_BF16)], axis=0)
            num = jnp.dot(p, rhs, preferred_element_type=_F32)
            den = jnp.maximum(jnp.abs(num[:, A_DV:]), clamp)
            h_ref[tok, hd * A_DV:(hd + 1) * A_DV] = num[:, :A_DV] / den
            kw = (kt.astype(_F32) * ws[hd:hd + 1, :]).astype(_BF16)
            c_ref[hd] = decay[hd:hd + 1, :] * c_old + jnp.dot(kw, v_ext, preferred_element_type=_F32)


def _a_scan(q, kt, v, gr, gc, dims):
    rows = q.shape[0]
    blk_rows = A_STEP_CHUNKS * A_CHUNK
    ncc = dims.ctx_len // blk_rows
    ncl = dims.seq // blk_rows
    lat0 = dims.batch * ncc

    def fwd_blk(b, j):
        return jnp.where(j < ncc, b * ncc + j, lat0 + b * ncl + (j - ncc))

    def bwd_blk(b, j):
        return jnp.where(j < ncc, b * ncc + (ncc - 1 - j), lat0 + b * ncl + (ncl - 1 - (j - ncc)))

    def specs(blk):
        return [
            pl.BlockSpec((blk_rows, A_QK_DIM), lambda b, j: (blk(b, j), 0)),
            pl.BlockSpec((A_QK_DIM, blk_rows), lambda b, j: (0, blk(b, j))),
            pl.BlockSpec((blk_rows, D_MODEL), lambda b, j: (blk(b, j), 0)),
            pl.BlockSpec((4 * A_HEADS, blk_rows), lambda b, j: (0, blk(b, j))),
            pl.BlockSpec((blk_rows, 2 * LANES), lambda b, j: (blk(b, j), 0)),
        ]

    state = pltpu.VMEM((A_HEADS, A_DK, 2 * A_DV), _F32)
    stab = pltpu.VMEM((A_HEADS, LANES), _F32)
    return pl.pallas_call(
        _a_scan_kernel,
        out_shape=(jax.ShapeDtypeStruct((rows, D_MODEL), _F32),) * 2,
        grid=(dims.batch, ncc + ncl),
        in_specs=specs(fwd_blk) + specs(bwd_blk),
        out_specs=(
            pl.BlockSpec((blk_rows, D_MODEL), lambda b, j: (fwd_blk(b, j), 0)),
            pl.BlockSpec((blk_rows, D_MODEL), lambda b, j: (bwd_blk(b, j), 0)),
        ),
        scratch_shapes=[state, state, stab, stab, stab],
        compiler_params=_cparams("arbitrary", "arbitrary"),
        name="a_scan",
    )(q, kt, v, gr, gc, q, kt, v, gr, gc)


def _a_mix(hf_ref, hb_ref, o_ref, hg_ref):
    hs = hf_ref[...] + hb_ref[...]
    parts = []
    for hd in range(A_HEADS):
        x = hs[:, hd * A_DV:(hd + 1) * A_DV]
        parts.append(x * lax.rsqrt(jnp.mean(x * x, axis=-1, keepdims=True) + NORM_EPS))
    y = jnp.concatenate(parts, axis=1) * hg_ref[...]
    return (jax.nn.sigmoid(o_ref[...]) * y).astype(_BF16)


ROPE_HALF = B_HEAD_DIM // 4


def _swap_halves_lanes(x):
    lane = lax.broadcasted_iota(jnp.int32, x.shape, 1)
    fwd = pltpu.roll(x, LANES - ROPE_HALF, axis=1)
    back = pltpu.roll(x, ROPE_HALF, axis=1)
    return jnp.where(lane % (2 * ROPE_HALF) < ROPE_HALF, fwd, back)


def _swap_halves_rows(x):
    parts = []
    for r0 in range(0, x.shape[0], 2 * ROPE_HALF):
        parts += [x[r0 + ROPE_HALF:r0 + 2 * ROPE_HALF], x[r0:r0 + ROPE_HALF]]
    return jnp.concatenate(parts, axis=0)


def _b_proj_kernel(h_ref, g_ref, mod_ref, wqvt_ref, wk_ref, cos_ref, sin_ref, cost_ref, sint_ref,
                   qt_ref, k_ref, vt_ref):
    x = _norm_mod(h_ref[...], g_ref[...], _mod_part(mod_ref, 0), _mod_part(mod_ref, 1)).astype(_BF16)
    nt = (((1,), (1,)), ((), ()))
    yt = lax.dot_general(wqvt_ref[...], x, nt, preferred_element_type=_F32)
    cost, sint = cost_ref[...], sint_ref[...]
    for hd in range(B_Q_HEADS):
        rows = slice(hd * B_HEAD_DIM, (hd + 1) * B_HEAD_DIM)
        qt = yt[rows, :]
        qt = (qt * cost + _swap_halves_rows(qt) * sint) * (B_HEAD_DIM ** -0.5 * LOG2E)
        qt_ref[rows, :] = qt.astype(_BF16)
    vt_ref[...] = yt[D_MODEL:, :].astype(_BF16)
    cos, sin = cos_ref[...], sin_ref[...]
    k = jnp.dot(x, wk_ref[...], preferred_element_type=_F32)
    for c0 in range(0, B_KV_DIM, LANES):
        kc = k[:, c0:c0 + LANES]
        k_ref[:, c0:c0 + LANES] = (kc * cos + _swap_halves_lanes(kc) * sin).astype(_BF16)


def _b_proj(h, g, mod_l, wqvt, wk, rope, dims):
    rows = h.shape[0]
    tm = ROW_TILE
    cos, sin, cost, sint = rope
    return pl.pallas_call(
        _b_proj_kernel,
        out_shape=(
            jax.ShapeDtypeStruct((D_MODEL, rows), _BF16),
            jax.ShapeDtypeStruct((rows, B_KV_DIM), _BF16),
            jax.ShapeDtypeStruct((B_KV_DIM, rows), _BF16),
        ),
        grid=(rows // tm,),
        in_specs=[
            _row_spec(tm, D_MODEL),
            _resident((1, D_MODEL)),
            _mod_spec(dims.n_ctx_rows, dims.seq, tm),
            _resident(wqvt.shape),
            _resident(wk.shape),
            _row_spec(tm, LANES),
            _row_spec(tm, LANES),
            pl.BlockSpec((B_HEAD_DIM, tm), lambda i: (0, i)),
            pl.BlockSpec((B_HEAD_DIM, tm), lambda i: (0, i)),
        ],
        out_specs=(
            pl.BlockSpec((D_MODEL, tm), lambda i: (0, i)),
            _row_spec(tm, B_KV_DIM),
            pl.BlockSpec((B_KV_DIM, tm), lambda i: (0, i)),
        ),
        compiler_params=_cparams("arbitrary"),
        name="b_proj",
    )(h, g, mod_l, wqvt, wk, cos, sin, cost, sint)


B_ONES_ROWS = 16


def _b_attn_kernel(sink_ref, win_l_ref, win_r_ref, qt_ref, kl_ref, kc_ref, kr_ref, kx_ref,
                   vtl_ref, vtc_ref, vtr_ref, vtx_ref, o_ref, s_ref, *, n_ctx_blocks, blocks_per_seq, ctx_len):
    i = pl.program_id(0)
    is_lat = i >= n_ctx_blocks
    n = (i - n_ctx_blocks) % blocks_per_seq
    neg = -jnp.inf
    has_left = jnp.logical_and(is_lat, n >= 1)
    has_right = jnp.logical_and(is_lat, n <= blocks_per_seq - 2)
    bias_l = jnp.where(has_left, win_l_ref[...], neg)
    bias_r = jnp.where(has_right, win_r_ref[...], neg)
    bias_c = jnp.where(is_lat, 0.0, neg)

    n_q = B_GROUP * B_BLOCK
    head_of_lane = lax.broadcasted_iota(jnp.int32, (1, n_q), 1) // B_BLOCK
    ones_rows = jnp.ones((B_ONES_ROWS, B_BLOCK), _BF16)
    n_ctx_tiles = ctx_len // B_BLOCK

    for g in range(B_KV_HEADS):
        qt = jnp.concatenate(
            [qt_ref[(g * B_GROUP + j) * B_HEAD_DIM:(g * B_GROUP + j + 1) * B_HEAD_DIM, :]
             for j in range(B_GROUP)], axis=1)
        sink = jnp.zeros((1, n_q), _F32)
        for j in range(B_GROUP):
            sink = jnp.where(head_of_lane == j, sink_ref[g * B_GROUP + j] * LOG2E, sink)
        ks = slice(g * B_HEAD_DIM, (g + 1) * B_HEAD_DIM)
        k_tiles = [(kl_ref[:, ks], bias_l), (kc_ref[:, ks], bias_c), (kr_ref[:, ks], bias_r)]
        vt_tiles = [vtl_ref[ks, :], vtc_ref[ks, :], vtr_ref[ks, :]]
        for t in range(n_ctx_tiles):
            k_tiles.append((kx_ref[t * B_BLOCK:(t + 1) * B_BLOCK, ks], None))
            vt_tiles.append(vtx_ref[ks, t * B_BLOCK:(t + 1) * B_BLOCK])
        m_tile = None
        for t, (k, bias) in enumerate(k_tiles):
            s = jnp.dot(k, qt, preferred_element_type=_F32)
            if bias is not None:
                s = s + bias
            s_ref[g, t] = s
            m_tile = s if m_tile is None else jnp.maximum(m_tile, s)
        m = jnp.maximum(sink, jnp.max(m_tile, axis=0, keepdims=True))
        acc = jnp.zeros((B_HEAD_DIM + B_ONES_ROWS, n_q), _F32)
        for t, vt in enumerate(vt_tiles):
            p = jnp.exp2(s_ref[g, t] - m).astype(_BF16)
            vt_ext = jnp.concatenate([vt, ones_rows], axis=0)
            acc = acc + jnp.dot(vt_ext, p, preferred_element_type=_F32)
        denom = jnp.exp2(sink - m) + acc[B_HEAD_DIM:B_HEAD_DIM + 1, :]
        out_t = acc[:B_HEAD_DIM, :] * (1.0 / denom)
        for pair in range(B_GROUP // 2):
            two = jnp.concatenate([out_t[:, (2 * pair) * B_BLOCK:(2 * pair + 1) * B_BLOCK],
                                   out_t[:, (2 * pair + 1) * B_BLOCK:(2 * pair + 2) * B_BLOCK]], axis=0)
            c0 = (g * B_GROUP + 2 * pair) * B_HEAD_DIM
            o_ref[:, c0:c0 + 2 * B_HEAD_DIM] = two.T.astype(_BF16)


def _window_bias():
    key = jnp.arange(B_BLOCK)[:, None]
    qry = jnp.arange(B_BLOCK)[None, :]
    prev_blk = jnp.where(key >= qry, 0.0, -jnp.inf).astype(_F32)
    next_blk = jnp.where(key <= qry, 0.0, -jnp.inf).astype(_F32)
    return jnp.tile(prev_blk, (1, B_GROUP)), jnp.tile(next_blk, (1, B_GROUP))


def _b_attn(sinks, qt, k, vt, dims):
    rows = k.shape[0]
    ncc = dims.ctx_len // B_BLOCK
    ncl = dims.seq // B_BLOCK
    lat0 = dims.batch * ncc
    nblk = rows // B_BLOCK

    def batch_of(i):
        return jnp.where(i < lat0, i // ncc, (i - lat0) // ncl)

    def left(i):
        return jnp.maximum(i - 1, 0)

    def right(i):
        return jnp.minimum(i + 1, nblk - 1)

    def k_spec(f):
        return pl.BlockSpec((B_BLOCK, B_KV_DIM), lambda i: (f(i), 0))

    def vt_spec(f):
        return pl.BlockSpec((B_KV_DIM, B_BLOCK), lambda i: (0, f(i)))

    n_tiles = 3 + ncc
    return pl.pallas_call(
        functools.partial(_b_attn_kernel, n_ctx_blocks=lat0, blocks_per_seq=ncl, ctx_len=dims.ctx_len),
        out_shape=jax.ShapeDtypeStruct((rows, D_MODEL), _BF16),
        grid=(nblk,),
        in_specs=[
            pl.BlockSpec(memory_space=pltpu.SMEM),
            _resident((B_BLOCK, B_GROUP * B_BLOCK)),
            _resident((B_BLOCK, B_GROUP * B_BLOCK)),
            pl.BlockSpec((D_MODEL, B_BLOCK), lambda i: (0, i)),
            k_spec(left), k_spec(lambda i: i), k_spec(right),
            pl.BlockSpec((dims.ctx_len, B_KV_DIM), lambda i: (batch_of(i), 0)),
            vt_spec(left), vt_spec(lambda i: i), vt_spec(right),
            pl.BlockSpec((B_KV_DIM, dims.ctx_len), lambda i: (0, batch_of(i))),
        ],
        out_specs=pl.BlockSpec((B_BLOCK, D_MODEL), lambda i: (i, 0)),
        scratch_shapes=[pltpu.VMEM((B_KV_HEADS, n_tiles, B_BLOCK, B_GROUP * B_BLOCK), _F32)],
        compiler_params=_cparams("arbitrary"),
        name="b_attn",
    )(sinks, *_window_bias(), qt, k, k, k, k, vt, vt, vt, vt)


def _c_proj_kernel(h_ref, g_ref, mod_ref, w_ref, bg_ref, u_ref):
    x = _norm_mod(h_ref[...], g_ref[...], _mod_part(mod_ref, 0), _mod_part(mod_ref, 1)).astype(_BF16)
    y = jnp.dot(x, w_ref[...], preferred_element_type=_F32)
    bg_ref[...] = y[:, :D_MODEL]
    u_ref[...] = y[:, D_MODEL:2 * D_MODEL] * y[:, 2 * D_MODEL:]


def _c_proj(h, g, mod_l, w_in, dims):
    rows = h.shape[0]
    tm = ROW_TILE
    return pl.pallas_call(
        _c_proj_kernel,
        out_shape=(jax.ShapeDtypeStruct((rows, D_MODEL), _F32),) * 2,
        grid=(rows // tm,),
        in_specs=[
            _row_spec(tm, D_MODEL),
            _resident((1, D_MODEL)),
            _mod_spec(dims.n_ctx_rows, dims.seq, tm),
            _resident(w_in.shape),
        ],
        out_specs=(_row_spec(tm, D_MODEL),) * 2,
        compiler_params=_cparams("arbitrary"),
        name="c_proj",
    )(h, g, mod_l, w_in)


def _c_mix(bg_ref, u_ref, up_ref, un_ref, cw_ref, cb_ref, tile, dims):
    tm = u_ref.shape[0]
    u = u_ref[...]
    row = lax.broadcasted_iota(jnp.int32, (tm, 1), 0)
    g_row = tile * tm + row
    in_ctx = g_row < dims.n_ctx_rows
    pos = jnp.where(in_ctx, g_row % dims.ctx_len, (g_row - dims.n_ctx_rows) % dims.seq)
    length = jnp.where(in_ctx, dims.ctx_len, dims.seq)
    prev = jnp.where(row == 0, up_ref[SUBLANES - 1:SUBLANES, :], pltpu.roll(u, 1, axis=0))
    nxt = jnp.where(row == tm - 1, un_ref[0:1, :], pltpu.roll(u, tm - 1, axis=0))
    prev = jnp.where(pos == 0, 0.0, prev)
    nxt = jnp.where(pos == length - 1, 0.0, nxt)
    conv = prev * cw_ref[0:1, :] + u * cw_ref[1:2, :] + nxt * cw_ref[2:3, :] + cb_ref[...]
    return (bg_ref[...] * conv).astype(_BF16)


_N_MIX_REFS = (4, 1, 6)


def _post_kernel(*refs, kind, n_h, tile0, final_norm, dims):
    n_mix = _N_MIX_REFS[kind]
    h_refs, mod_ref = refs[:n_h], refs[n_h]
    mix_refs = refs[n_h + 1:n_h + 1 + n_mix]
    wo_ref, g_ref, w1_ref, w2_ref, fg_ref, out_ref = refs[n_h + 1 + n_mix:]
    if kind == 0:
        y = _a_mix(*mix_refs)
    elif kind == 1:
        y = mix_refs[0][...]
    else:
        y = _c_mix(*mix_refs, pl.program_id(0) + tile0, dims)
    h = _load_h(h_refs)
    h = h + _mod_part(mod_ref, 2) * jnp.dot(y, wo_ref[...], preferred_element_type=_F32)
    x = _norm_mod(h, g_ref[...], _mod_part(mod_ref, 3), _mod_part(mod_ref, 4)).astype(_BF16)
    acc = jnp.zeros(h.shape, _F32)
    for c in range(MLP_HIDDEN // MLP_HIDDEN_CHUNK):
        cols = slice(c * MLP_HIDDEN_CHUNK, (c + 1) * MLP_HIDDEN_CHUNK)
        u = jnp.dot(x, w1_ref[:, cols], preferred_element_type=_F32)
        u = jnp.square(jnp.maximum(u, 0.0)).astype(_BF16)
        acc = acc + jnp.dot(u, w2_ref[cols, :], preferred_element_type=_F32)
    out = h + _mod_part(mod_ref, 5) * acc
    if final_norm:
        ms = jnp.mean(out * out, axis=-1, keepdims=True)
        out = out * lax.rsqrt(ms + NORM_EPS) * fg_ref[...]
    out_ref[...] = out


def _post(kind, h_parts, mod_l, mix, w_out, g, w1, w2, layer, final_g, dims, tile0=0, final_norm=False):
    tm = ROW_TILE
    n_tiles = dims.rows // tm - tile0
    row = functools.partial(_row_spec, tm, D_MODEL, tile0)
    if kind == 0:
        mix_specs = [row(), row(), row(), _resident((1, D_MODEL))]
    elif kind == 1:
        mix_specs = [row()]
    else:
        bg, u, conv_w, conv_b = mix
        per = tm // SUBLANES
        last = dims.rows // SUBLANES - 1
        mix = (bg, u, u, u, conv_w, conv_b)
        mix_specs = [
            row(), row(),
            pl.BlockSpec((SUBLANES, D_MODEL), lambda i: (jnp.maximum((i + tile0) * per - 1, 0), 0)),
            pl.BlockSpec((SUBLANES, D_MODEL), lambda i: (jnp.minimum((i + tile0 + 1) * per, last), 0)),
            _resident(conv_w.shape), _resident((1, D_MODEL)),
        ]
    assert len(h_parts) == 1 or tile0 == 0
    h_specs = _h_specs(tm, True) if len(h_parts) == 2 else [row()]
    return pl.pallas_call(
        functools.partial(_post_kernel, kind=kind, n_h=len(h_parts), tile0=tile0,
                          final_norm=final_norm, dims=dims),
        out_shape=jax.ShapeDtypeStruct((n_tiles * tm, D_MODEL), _F32),
        grid=(n_tiles,),
        in_specs=h_specs + [_mod_spec(dims.n_ctx_rows, dims.seq, tm, tile0)] + mix_specs + [
            _resident(w_out.shape),
            _resident((1, D_MODEL)),
            _resident_layer(w1.shape, layer),
            _resident_layer(w2.shape, layer),
            _resident((1, D_MODEL)),
        ],
        out_specs=_row_spec(tm, D_MODEL),
        compiler_params=_cparams("arbitrary"),
        name="post",
    )(*h_parts, mod_l, *mix, w_out, g, w1, w2, final_g)


class _Dims:
    def __init__(self, batch, seq, ctx_len):
        self.batch = batch
        self.seq = seq
        self.ctx_len = ctx_len
        self.n_ctx_rows = batch * ctx_len
        self.rows = self.n_ctx_rows + batch * seq


def _rope_tables(dims):
    n_freq = B_HEAD_DIM // 4
    t = jnp.arange(dims.seq)
    inv_freq = ROPE_BASE ** (-jnp.arange(n_freq, dtype=_F32) / n_freq)
    ang_row = (t // GRID_W).astype(_F32)[:, None] * inv_freq
    ang_col = (t % GRID_W).astype(_F32)[:, None] * inv_freq
    ang = jnp.concatenate([ang_row, ang_row, ang_col, ang_col], axis=1)
    sign = jnp.tile(jnp.concatenate([-jnp.ones(n_freq, _F32), jnp.ones(n_freq, _F32)]), 2)
    cos = jnp.tile(jnp.cos(ang), (dims.batch, 1))
    sin = jnp.tile(jnp.sin(ang) * sign, (dims.batch, 1))
    cos = jnp.concatenate([jnp.ones((dims.n_ctx_rows, B_HEAD_DIM), _F32), cos], axis=0)
    sin = jnp.concatenate([jnp.zeros((dims.n_ctx_rows, B_HEAD_DIM), _F32), sin], axis=0)
    return jnp.tile(cos, (1, 2)), jnp.tile(sin, (1, 2)), cos.T, sin.T


def kernel(x, c, ctx, c_ctx, ada_w, ada_b, norm_g, final_g, mlp_w1, mlp_w2,
           a_w_in, a_w_gate, a_b_gate, a_head_g, a_w_out,
           b_w_qkv, b_sinks, b_w_out, c_w_in, c_conv_w, c_conv_b, c_w_out):
    batch, seq, d = x.shape
    ctx_len = ctx.shape[1]
    depth = ada_w.shape[0]
    dims = _Dims(batch, seq, ctx_len)
    assert d == D_MODEL and seq % ROW_TILE == 0 and dims.n_ctx_rows % ROW_TILE == 0
    assert ctx_len % (A_STEP_CHUNKS * A_CHUNK) == 0 and 1 + batch <= MOD_ROWS

    assert dims.n_ctx_rows == ROW_TILE
    h_parts = (x.reshape(-1, d), ctx.reshape(-1, d))
    cs = jnp.concatenate([c_ctx[None], c, jnp.zeros((MOD_ROWS - 1 - batch, d), _F32)], axis=0)
    mod = _ada_table(cs, ada_w, ada_b).reshape(depth, MOD_ROWS, 1, 6 * d)
    rope = _rope_tables(dims)
    fg = final_g.reshape(1, d)
    w1_all = mlp_w1.astype(_BF16)
    w2_all = mlp_w2.astype(_BF16)

    for l in range(depth):
        kind, j = l % N_MIXERS, l // N_MIXERS
        mod_l = mod[l]
        g0 = norm_g[l, 0].reshape(1, d)
        g1 = norm_g[l, 1].reshape(1, d)
        last_layer = l == depth - 1
        if kind == 0:
            w_in = a_w_in[j]
            wqvo = jnp.concatenate([w_in[:, :A_QK_DIM], w_in[:, 2 * A_QK_DIM:]], axis=1).astype(_BF16)
            wkt = w_in[:, A_QK_DIM:2 * A_QK_DIM].T.astype(_BF16)
            wgt = a_w_gate[j].T.astype(_BF16)
            q, kt, v, o, gr, gc = _a_proj(h_parts, g0, mod_l, wqvo, wkt, wgt, a_b_gate[j].reshape(-1, 1), dims)
            hf, hb = _a_scan(q, kt, v, gr, gc, dims)
            mix = (hf, hb, o, a_head_g[j].reshape(1, d))
            w_out = a_w_out[j]
        elif kind == 1:
            (h,) = h_parts
            w = b_w_qkv[j]
            wqvt = jnp.concatenate([w[:, :D_MODEL], w[:, D_MODEL + B_KV_DIM:]], axis=1).T.astype(_BF16)
            wk = w[:, D_MODEL:D_MODEL + B_KV_DIM].astype(_BF16)
            qt, k, vt = _b_proj(h, g0, mod_l, wqvt, wk, rope, dims)
            mix = (_b_attn(b_sinks[j], qt, k, vt, dims),)
            w_out = b_w_out[j]
        else:
            (h,) = h_parts
            bg, u = _c_proj(h, g0, mod_l, c_w_in[j].astype(_BF16), dims)
            mix = (bg, u, c_conv_w[j], c_conv_b[j].reshape(1, d))
            w_out = c_w_out[j]
        tile0 = dims.n_ctx_rows // ROW_TILE if last_layer else 0
        h_parts = (_post(kind, h_parts, mod_l, mix, w_out.astype(_BF16), g1,
                         w1_all, w2_all, l, fg, dims,
                         tile0=tile0, final_norm=last_layer),)
    return h_parts[0].reshape(batch, seq, d)
```

```python
import functools

import jax
import jax.numpy as jnp
from jax import lax
from jax.experimental import pallas as pl
from jax.experimental.pallas import tpu as pltpu

D_MODEL = 1024
GRID_W = 64
N_MIXERS = 3
A_HEADS = 8
A_QK_DIM = D_MODEL // 2
A_DK = A_QK_DIM // A_HEADS
A_DV = D_MODEL // A_HEADS
A_CHUNK = 128
A_STEP_CHUNKS = 2
B_Q_HEADS = 16
B_KV_HEADS = 4
B_HEAD_DIM = D_MODEL // B_Q_HEADS
B_GROUP = B_Q_HEADS // B_KV_HEADS
B_KV_DIM = B_KV_HEADS * B_HEAD_DIM
B_BLOCK = 128
ROPE_BASE = 10000.0
MLP_HIDDEN = 4 * D_MODEL
NORM_EPS = 1e-6
LOG2E = 1.4426950408889634

LANES = 128
SUBLANES = 8
VMEM_LIMIT_BYTES = 56 * 1024 * 1024

ROW_TILE = 512
MLP_HIDDEN_CHUNK = 1024
MOD_ROWS = 8
ADA_COL_TILE = 1536

_BF16 = jnp.bfloat16
_F32 = jnp.float32


def _cparams(*sem):
    return pltpu.CompilerParams(dimension_semantics=sem, vmem_limit_bytes=VMEM_LIMIT_BYTES)


def _resident(shape):
    nd = len(shape)
    return pl.BlockSpec(shape, lambda *_: (0,) * nd, pipeline_mode=pl.Buffered(1))


def _resident_layer(shape, layer):
    nd = len(shape) - 1
    return pl.BlockSpec((None,) + tuple(shape[1:]), lambda *_: (layer,) + (0,) * nd,
                        pipeline_mode=pl.Buffered(1))


def _norm_mod(h, g, shift, scale):
    ms = jnp.mean(h * h, axis=-1, keepdims=True)
    y = h * lax.rsqrt(ms + NORM_EPS) * g
    return y * (1.0 + scale) + shift


def _mod_part(mod_ref, k):
    return mod_ref[:, k * D_MODEL:(k + 1) * D_MODEL]


def _mod_spec(n_ctx_rows, seq, tm, tile0=0):
    def idx(i):
        r0 = (i + tile0) * tm
        grp = jnp.where(r0 < n_ctx_rows, 0, 1 + (r0 - n_ctx_rows) // seq)
        return (grp, 0, 0)
    return pl.BlockSpec((None, 1, 6 * D_MODEL), idx)


def _row_spec(tm, width, tile0=0):
    return pl.BlockSpec((tm, width), lambda i: (i + tile0, 0))


def _h_specs(tm, split):
    if not split:
        return [_row_spec(tm, D_MODEL)]
    return [pl.BlockSpec((tm, D_MODEL), lambda i: (jnp.maximum(i - 1, 0), 0)),
            pl.BlockSpec((tm, D_MODEL), lambda i: (0, 0))]


def _load_h(h_refs):
    if len(h_refs) == 1:
        return h_refs[0][...]
    lat_ref, ctx_ref = h_refs
    return jnp.where(pl.program_id(0) == 0, ctx_ref[...], lat_ref[...])


def _ada_kernel(cs_ref, w_ref, b_ref, o_ref):
    cs = cs_ref[...]
    s = (cs * jax.nn.sigmoid(cs)).astype(_BF16)
    o_ref[...] = jnp.dot(s, w_ref[...].astype(_BF16), preferred_element_type=_F32) + b_ref[...]


def _ada_table(cs, ada_w, ada_b):
    depth = ada_w.shape[0]
    n = ada_w.shape[2]
    return pl.pallas_call(
        _ada_kernel,
        out_shape=jax.ShapeDtypeStruct((depth, MOD_ROWS, n), _F32),
        grid=(depth, n // ADA_COL_TILE),
        in_specs=[
            pl.BlockSpec((MOD_ROWS, D_MODEL), lambda l, j: (0, 0)),
            pl.BlockSpec((None, D_MODEL, ADA_COL_TILE), lambda l, j: (l, 0, j)),
            pl.BlockSpec((None, 1, ADA_COL_TILE), lambda l, j: (l, 0, j)),
        ],
        out_specs=pl.BlockSpec((None, MOD_ROWS, ADA_COL_TILE), lambda l, j: (l, 0, j)),
        compiler_params=_cparams("arbitrary", "arbitrary"),
        name="ada_table",
    )(cs, ada_w, ada_b.reshape(depth, 1, n))


def _a_proj_kernel(*refs, n_h):
    h_refs, (g_ref, mod_ref, wqvo_ref, wkt_ref, wgt_ref, bg_ref,
             q_ref, kt_ref, v_ref, o_ref, gr_ref, gc_ref) = refs[:n_h], refs[n_h:]
    x = _norm_mod(_load_h(h_refs), g_ref[...], _mod_part(mod_ref, 0), _mod_part(mod_ref, 1)).astype(_BF16)
    nt = (((1,), (1,)), ((), ()))
    gt = lax.dot_general(wgt_ref[...], x, nt, preferred_element_type=_F32) + bg_ref[...]

    n_chunks = gt.shape[1] // A_CHUNK

    def by_chunk(rows):
        return jnp.concatenate([rows[:, c * A_CHUNK:(c + 1) * A_CHUNK] for c in range(n_chunks)], axis=0)

    def by_token(x):
        return jnp.concatenate([x[c * A_HEADS:(c + 1) * A_HEADS] for c in range(n_chunks)], axis=1)

    def log_sigmoid(z):
        return jnp.minimum(z, 0.0) - jnp.log1p(jnp.exp(-jnp.abs(z)))

    row_form, col_cm, col_b = [], [], []
    for d in range(2):
        li = by_chunk(gt[2 * d * A_HEADS:(2 * d + 1) * A_HEADS]) * LOG2E
        lf = log_sigmoid(by_chunk(gt[(2 * d + 1) * A_HEADS:(2 * d + 2) * A_HEADS])) * LOG2E
        b = _lane_scan(lf, jnp.add, 0.0, d == 1)
        a = li - b
        cm = _lane_scan(a, jnp.maximum, -jnp.inf, d == 1)
        row_form += [by_token(a), by_token(b)]
        col_cm.append(cm)
        col_b.append(b)
    gr_ref[...] = jnp.concatenate(row_form, axis=0)
    pad = jnp.zeros((LANES - 2 * A_HEADS, LANES), _F32)
    for c in range(n_chunks):
        rows = slice(c * A_HEADS, (c + 1) * A_HEADS)
        toks = slice(c * A_CHUNK, (c + 1) * A_CHUNK)
        gc_ref[toks, :LANES] = jnp.concatenate([col_cm[0][rows], col_cm[1][rows], pad], axis=0).T
        gc_ref[toks, LANES:] = jnp.concatenate([col_b[0][rows], col_b[1][rows], pad], axis=0).T

    y = jnp.dot(x, wqvo_ref[...], preferred_element_type=_F32)
    q_ref[...] = (y[:, :A_QK_DIM] * (A_DK ** -0.5)).astype(_BF16)
    v_ref[...] = y[:, A_QK_DIM:A_QK_DIM + D_MODEL].astype(_BF16)
    o_ref[...] = y[:, A_QK_DIM + D_MODEL:]
    kt_ref[...] = lax.dot_general(wkt_ref[...], x, nt, preferred_element_type=_F32).astype(_BF16)


def _a_proj(h_parts, g, mod_l, wqvo, wkt, wgt, bgate, dims):
    rows = dims.rows
    tm = ROW_TILE
    n_gate = 4 * A_HEADS
    return pl.pallas_call(
        functools.partial(_a_proj_kernel, n_h=len(h_parts)),
        out_shape=(
            jax.ShapeDtypeStruct((rows, A_QK_DIM), _BF16),
            jax.ShapeDtypeStruct((A_QK_DIM, rows), _BF16),
            jax.ShapeDtypeStruct((rows, D_MODEL), _BF16),
            jax.ShapeDtypeStruct((rows, D_MODEL), _F32),
            jax.ShapeDtypeStruct((n_gate, rows), _F32),
            jax.ShapeDtypeStruct((rows, 2 * LANES), _F32),
        ),
        grid=(rows // tm,),
        in_specs=_h_specs(tm, len(h_parts) == 2) + [
            _resident((1, D_MODEL)),
            _mod_spec(dims.n_ctx_rows, dims.seq, tm),
            _resident(wqvo.shape),
            _resident(wkt.shape),
            _resident(wgt.shape),
            _resident((n_gate, 1)),
        ],
        out_specs=(
            _row_spec(tm, A_QK_DIM),
            pl.BlockSpec((A_QK_DIM, tm), lambda i: (0, i)),
            _row_spec(tm, D_MODEL),
            _row_spec(tm, D_MODEL),
            pl.BlockSpec((n_gate, tm), lambda i: (0, i)),
            _row_spec(tm, 2 * LANES),
        ),
        compiler_params=_cparams("arbitrary"),
        name="a_proj",
    )(*h_parts, g, mod_l, wqvo, wkt, wgt, bgate)


def _lane_scan(x, op, fill, reverse):
    lane = lax.broadcasted_iota(jnp.int32, x.shape, 1)
    k = 1
    while k < LANES:
        if reverse:
            shifted = jnp.where(lane < LANES - k, pltpu.roll(x, LANES - k, axis=1), fill)
        else:
            shifted = jnp.where(lane >= k, pltpu.roll(x, k, axis=1), fill)
        x = op(x, shifted)
        k *= 2
    return x


def _a_scan_kernel(qf_ref, ktf_ref, vf_ref, grf_ref, gcf_ref, qb_ref, ktb_ref, vb_ref, grb_ref, gcb_ref,
                   hf_ref, hb_ref, cf_ref, cb_ref, mrf_ref, mrb_ref, mc_ref):
    @pl.when(pl.program_id(1) == 0)
    def _():
        cf_ref[...] = jnp.zeros(cf_ref.shape, _F32)
        cb_ref[...] = jnp.zeros(cb_ref.shape, _F32)
        mrf_ref[...] = jnp.full(mrf_ref.shape, -jnp.inf, _F32)
        mrb_ref[...] = jnp.full(mrb_ref.shape, -jnp.inf, _F32)
        mc_ref[...] = jnp.full(mc_ref.shape, -jnp.inf, _F32)

    t_idx = lax.broadcasted_iota(jnp.int32, (A_CHUNK, A_CHUNK), 0)
    s_idx = lax.broadcasted_iota(jnp.int32, (A_CHUNK, A_CHUNK), 1)
    ones_blk = jnp.ones((A_CHUNK, A_DV), _BF16)
    lane = lax.broadcasted_iota(jnp.int32, (1, LANES), 1)
    for sub in range(A_STEP_CHUNKS):
        toks = (slice(sub * A_CHUNK, (sub + 1) * A_CHUNK),
                slice((A_STEP_CHUNKS - 1 - sub) * A_CHUNK, (A_STEP_CHUNKS - sub) * A_CHUNK))
        _a_chunk_pair(toks, (qf_ref, qb_ref), (ktf_ref, ktb_ref), (vf_ref, vb_ref), (grf_ref, grb_ref),
                      (gcf_ref, gcb_ref), (hf_ref, hb_ref), (cf_ref, cb_ref), (mrf_ref, mrb_ref), mc_ref,
                      t_idx, s_idx, ones_blk, lane)


def _a_chunk_pair(toks, q_refs, kt_refs, v_refs, gr_refs, gc_refs, h_refs, c_refs, mr_refs, mc_ref,
                  t_idx, s_idx, ones_blk, lane):
    m_col = mc_ref[0:1, :]
    m_col_new = []
    dirs = []
    for d in range(2):
        gr_ref, gc_ref, mr_ref, tok = gr_refs[d], gc_refs[d], mr_refs[d], toks[d]
        last = A_CHUNK - 1 if d == 0 else 0
        a = gr_ref[2 * d * A_HEADS:(2 * d + 1) * A_HEADS, tok]
        b = gr_ref[(2 * d + 1) * A_HEADS:(2 * d + 2) * A_HEADS, tok]
        m_row = mr_ref[:, 0:1]
        gg_end = jnp.maximum(m_row, jnp.max(a, axis=1, keepdims=True))
        ws = jnp.exp2(a - gg_end)
        decay = jnp.exp2(m_row - gg_end)
        mr_ref[...] = jnp.broadcast_to(b[:, last:last + 1] + gg_end, mr_ref.shape)
        gg_c = jnp.maximum(m_col, gc_ref[tok, :LANES])
        b_plus_gg = gc_ref[tok, LANES:] + gg_c
        clamp_c = jnp.exp2(-b_plus_gg)
        m_col_new.append(b_plus_gg[last:last + 1, :])
        dirs.append((a, m_row, ws, decay, gg_c, clamp_c))
    mc_ref[...] = jnp.broadcast_to(jnp.where(lane < A_HEADS, m_col_new[0], m_col_new[1]), mc_ref.shape)

    for hd in range(A_HEADS):
        for d in range(2):
            q_ref, kt_ref, v_ref, h_ref, c_ref, tok = q_refs[d], kt_refs[d], v_refs[d], h_refs[d], c_refs[d], toks[d]
            a, m_row, ws, decay, gg_c, clamp_c = dirs[d]
            mask = (s_idx <= t_idx) if d == 0 else (s_idx >= t_idx)
            col = d * A_HEADS + hd
            gg = jnp.broadcast_to(gg_c[:, col:col + 1], (A_CHUNK, A_CHUNK))
            clamp = jnp.broadcast_to(clamp_c[:, col:col + 1], (A_CHUNK, A_DV))
            q = q_ref[tok, hd * A_DK:(hd + 1) * A_DK]
            kt = kt_ref[hd * A_DK:(hd + 1) * A_DK, tok]
            v = v_ref[tok, hd * A_DV:(hd + 1) * A_DV]
            v_ext = jnp.concatenate([v, ones_blk], axis=1)
            c_old = c_ref[hd]
            s = jnp.dot(q, kt, preferred_element_type=_F32)
            e_keys = jnp.exp2(jnp.where(mask, a[hd:hd + 1, :] - gg, -jnp.inf))
            e_state = jnp.exp2(m_row[hd:hd + 1, :] - gg[:, :A_DK])
            p = jnp.concatenate([s * e_keys, q.astype(_F32) * e_state], axis=1).astype(_BF16)
            rhs = jnp.concatenate([v_ext, c_old.astype(_BF16)], axis=0)
            num = jnp.dot(p, rhs, preferred_element_type=_F32)
            den = jnp.maximum(jnp.abs(num[:, A_DV:]), clamp)
            h_ref[tok, hd * A_DV:(hd + 1) * A_DV] = num[:, :A_DV] / den
            kw = (kt.astype(_F32) * ws[hd:hd + 1, :]).astype(_BF16)
            c_ref[hd] = decay[hd:hd + 1, :] * c_old + jnp.dot(kw, v_ext, preferred_element_type=_F32)


def _a_scan(q, kt, v, gr, gc, dims):
    rows = q.shape[0]
    blk_rows = A_STEP_CHUNKS * A_CHUNK
    ncc = dims.ctx_len // blk_rows
    ncl = dims.seq // blk_rows
    lat0 = dims.batch * ncc

    def fwd_blk(b, j):
        return jnp.where(j < ncc, b * ncc + j, lat0 + b * ncl + (j - ncc))

    def bwd_blk(b, j):
        return jnp.where(j < ncc, b * ncc + (ncc - 1 - j), lat0 + b * ncl + (ncl - 1 - (j - ncc)))

    def specs(blk):
        return [
            pl.BlockSpec((blk_rows, A_QK_DIM), lambda b, j: (blk(b, j), 0)),
            pl.BlockSpec((A_QK_DIM, blk_rows), lambda b, j: (0, blk(b, j))),
            pl.BlockSpec((blk_rows, D_MODEL), lambda b, j: (blk(b, j), 0)),
            pl.BlockSpec((4 * A_HEADS, blk_rows), lambda b, j: (0, blk(b, j))),
            pl.BlockSpec((blk_rows, 2 * LANES), lambda b, j: (blk(b, j), 0)),
        ]

    state = pltpu.VMEM((A_HEADS, A_DK, 2 * A_DV), _F32)
    stab = pltpu.VMEM((A_HEADS, LANES), _F32)
    return pl.pallas_call(
        _a_scan_kernel,
        out_shape=(jax.ShapeDtypeStruct((rows, D_MODEL), _F32),) * 2,
        grid=(dims.batch, ncc + ncl),
        in_specs=specs(fwd_blk) + specs(bwd_blk),
        out_specs=(
            pl.BlockSpec((blk_rows, D_MODEL), lambda b, j: (fwd_blk(b, j), 0)),
            pl.BlockSpec((blk_rows, D_MODEL), lambda b, j: (bwd_blk(b, j), 0)),
        ),
        scratch_shapes=[state, state, stab, stab, stab],
        compiler_params=_cparams("arbitrary", "arbitrary"),
        name="a_scan",
    )(q, kt, v, gr, gc, q, kt, v, gr, gc)


def _a_mix(hf_ref, hb_ref, o_ref, hg_ref):
    hs = hf_ref[...] + hb_ref[...]
    parts = []
    for hd in range(A_HEADS):
        x = hs[:, hd * A_DV:(hd + 1) * A_DV]
        parts.append(x * lax.rsqrt(jnp.mean(x * x, axis=-1, keepdims=True) + NORM_EPS))
    y = jnp.concatenate(parts, axis=1) * hg_ref[...]
    return (jax.nn.sigmoid(o_ref[...]) * y).astype(_BF16)


ROPE_HALF = B_HEAD_DIM // 4


def _swap_halves_lanes(x):
    lane = lax.broadcasted_iota(jnp.int32, x.shape, 1)
    fwd = pltpu.roll(x, LANES - ROPE_HALF, axis=1)
    back = pltpu.roll(x, ROPE_HALF, axis=1)
    return jnp.where(lane % (2 * ROPE_HALF) < ROPE_HALF, fwd, back)


def _swap_halves_rows(x):
    parts = []
    for r0 in range(0, x.shape[0], 2 * ROPE_HALF):
        parts += [x[r0 + ROPE_HALF:r0 + 2 * ROPE_HALF], x[r0:r0 + ROPE_HALF]]
    return jnp.concatenate(parts, axis=0)


def _b_proj_kernel(h_ref, g_ref, mod_ref, wqvt_ref, wk_ref, cos_ref, sin_ref, cost_ref, sint_ref,
                   qt_ref, k_ref, vt_ref):
    x = _norm_mod(h_ref[...], g_ref[...], _mod_part(mod_ref, 0), _mod_part(mod_ref, 1)).astype(_BF16)
    nt = (((1,), (1,)), ((), ()))
    yt = lax.dot_general(wqvt_ref[...], x, nt, preferred_element_type=_F32)
    cost, sint = cost_ref[...], sint_ref[...]
    for hd in range(B_Q_HEADS):
        rows = slice(hd * B_HEAD_DIM, (hd + 1) * B_HEAD_DIM)
        qt = yt[rows, :]
        qt = (qt * cost + _swap_halves_rows(qt) * sint) * (B_HEAD_DIM ** -0.5 * LOG2E)
        qt_ref[rows, :] = qt.astype(_BF16)
    vt_ref[...] = yt[D_MODEL:, :].astype(_BF16)
    cos, sin = cos_ref[...], sin_ref[...]
    k = jnp.dot(x, wk_ref[...], preferred_element_type=_F32)
    for c0 in range(0, B_KV_DIM, LANES):
        kc = k[:, c0:c0 + LANES]
        k_ref[:, c0:c0 + LANES] = (kc * cos + _swap_halves_lanes(kc) * sin).astype(_BF16)


def _b_proj(h, g, mod_l, wqvt, wk, rope, dims):
    rows = h.shape[0]
    tm = ROW_TILE
    cos, sin, cost, sint = rope
    return pl.pallas_call(
        _b_proj_kernel,
        out_shape=(
            jax.ShapeDtypeStruct((D_MODEL, rows), _BF16),
            jax.ShapeDtypeStruct((rows, B_KV_DIM), _BF16),
            jax.ShapeDtypeStruct((B_KV_DIM, rows), _BF16),
        ),
        grid=(rows // tm,),
        in_specs=[
            _row_spec(tm, D_MODEL),
            _resident((1, D_MODEL)),
            _mod_spec(dims.n_ctx_rows, dims.seq, tm),
            _resident(wqvt.shape),
            _resident(wk.shape),
            _row_spec(tm, LANES),
            _row_spec(tm, LANES),
            pl.BlockSpec((B_HEAD_DIM, tm), lambda i: (0, i)),
            pl.BlockSpec((B_HEAD_DIM, tm), lambda i: (0, i)),
        ],
        out_specs=(
            pl.BlockSpec((D_MODEL, tm), lambda i: (0, i)),
            _row_spec(tm, B_KV_DIM),
            pl.BlockSpec((B_KV_DIM, tm), lambda i: (0, i)),
        ),
        compiler_params=_cparams("arbitrary"),
        name="b_proj",
    )(h, g, mod_l, wqvt, wk, cos, sin, cost, sint)


B_ONES_ROWS = 16


def _b_attn_kernel(sink_ref, win_l_ref, win_r_ref, qt_ref, kl_ref, kc_ref, kr_ref, kx_ref,
                   vtl_ref, vtc_ref, vtr_ref, vtx_ref, o_ref, s_ref, *, n_ctx_blocks, blocks_per_seq, ctx_len):
    i = pl.program_id(0)
    is_lat = i >= n_ctx_blocks
    n = (i - n_ctx_blocks) % blocks_per_seq
    neg = -jnp.inf
    has_left = jnp.logical_and(is_lat, n >= 1)
    has_right = jnp.logical_and(is_lat, n <= blocks_per_seq - 2)
    bias_l = jnp.where(has_left, win_l_ref[...], neg)
    bias_r = jnp.where(has_right, win_r_ref[...], neg)
    bias_c = jnp.where(is_lat, 0.0, neg)

    n_q = B_GROUP * B_BLOCK
    head_of_lane = lax.broadcasted_iota(jnp.int32, (1, n_q), 1) // B_BLOCK
    ones_rows = jnp.ones((B_ONES_ROWS, B_BLOCK), _BF16)
    n_ctx_tiles = ctx_len // B_BLOCK

    def scores(g):
        qt = jnp.concatenate(
            [qt_ref[(g * B_GROUP + j) * B_HEAD_DIM:(g * B_GROUP + j + 1) * B_HEAD_DIM, :]
             for j in range(B_GROUP)], axis=1)
        sink = jnp.zeros((1, n_q), _F32)
        for j in range(B_GROUP):
            sink = jnp.where(head_of_lane == j, sink_ref[g * B_GROUP + j] * LOG2E, sink)
        ks = slice(g * B_HEAD_DIM, (g + 1) * B_HEAD_DIM)
        k_tiles = [(kl_ref[:, ks], bias_l), (kc_ref[:, ks], bias_c), (kr_ref[:, ks], bias_r)]
        for t in range(n_ctx_tiles):
            k_tiles.append((kx_ref[t * B_BLOCK:(t + 1) * B_BLOCK, ks], None))
        m_tile = None
        for t, (k, bias) in enumerate(k_tiles):
            s = jnp.dot(k, qt, preferred_element_type=_F32)
            if bias is not None:
                s = s + bias
            s_ref[g, t] = s
            m_tile = s if m_tile is None else jnp.maximum(m_tile, s)
        return jnp.maximum(sink, jnp.max(m_tile, axis=0, keepdims=True)), sink

    def attend(g, m, sink):
        ks = slice(g * B_HEAD_DIM, (g + 1) * B_HEAD_DIM)
        vt_tiles = [vtl_ref[ks, :], vtc_ref[ks, :], vtr_ref[ks, :]]
        for t in range(n_ctx_tiles):
            vt_tiles.append(vtx_ref[ks, t * B_BLOCK:(t + 1) * B_BLOCK])
        acc = jnp.zeros((B_HEAD_DIM + B_ONES_ROWS, n_q), _F32)
        for t, vt in enumerate(vt_tiles):
            p = jnp.exp2(s_ref[g, t] - m).astype(_BF16)
            vt_ext = jnp.concatenate([vt, ones_rows], axis=0)
            acc = acc + jnp.dot(vt_ext, p, preferred_element_type=_F32)
        denom = jnp.exp2(sink - m) + acc[B_HEAD_DIM:B_HEAD_DIM + 1, :]
        out_t = acc[:B_HEAD_DIM, :] * (1.0 / denom)
        for pair in range(B_GROUP // 2):
            two = jnp.concatenate([out_t[:, (2 * pair) * B_BLOCK:(2 * pair + 1) * B_BLOCK],
                                   out_t[:, (2 * pair + 1) * B_BLOCK:(2 * pair + 2) * B_BLOCK]], axis=0)
            c0 = (g * B_GROUP + 2 * pair) * B_HEAD_DIM
            o_ref[:, c0:c0 + 2 * B_HEAD_DIM] = two.T.astype(_BF16)

    pending = [scores(0), scores(1)]
    for g in range(B_KV_HEADS):
        if g + 2 < B_KV_HEADS:
            pending.append(scores(g + 2))
        attend(g, *pending[g])


def _window_bias():
    key = jnp.arange(B_BLOCK)[:, None]
    qry = jnp.arange(B_BLOCK)[None, :]
    prev_blk = jnp.where(key >= qry, 0.0, -jnp.inf).astype(_F32)
    next_blk = jnp.where(key <= qry, 0.0, -jnp.inf).astype(_F32)
    return jnp.tile(prev_blk, (1, B_GROUP)), jnp.tile(next_blk, (1, B_GROUP))


def _b_attn(sinks, qt, k, vt, dims):
    rows = k.shape[0]
    ncc = dims.ctx_len // B_BLOCK
    ncl = dims.seq // B_BLOCK
    lat0 = dims.batch * ncc
    nblk = rows // B_BLOCK

    def batch_of(i):
        return jnp.where(i < lat0, i // ncc, (i - lat0) // ncl)

    def left(i):
        return jnp.maximum(i - 1, 0)

    def right(i):
        return jnp.minimum(i + 1, nblk - 1)

    def k_spec(f):
        return pl.BlockSpec((B_BLOCK, B_KV_DIM), lambda i: (f(i), 0))

    def vt_spec(f):
        return pl.BlockSpec((B_KV_DIM, B_BLOCK), lambda i: (0, f(i)))

    n_tiles = 3 + ncc
    return pl.pallas_call(
        functools.partial(_b_attn_kernel, n_ctx_blocks=lat0, blocks_per_seq=ncl, ctx_len=dims.ctx_len),
        out_shape=jax.ShapeDtypeStruct((rows, D_MODEL), _BF16),
        grid=(nblk,),
        in_specs=[
            pl.BlockSpec(memory_space=pltpu.SMEM),
            _resident((B_BLOCK, B_GROUP * B_BLOCK)),
            _resident((B_BLOCK, B_GROUP * B_BLOCK)),
            pl.BlockSpec((D_MODEL, B_BLOCK), lambda i: (0, i)),
            k_spec(left), k_spec(lambda i: i), k_spec(right),
            pl.BlockSpec((dims.ctx_len, B_KV_DIM), lambda i: (batch_of(i), 0)),
            vt_spec(left), vt_spec(lambda i: i), vt_spec(right),
            pl.BlockSpec((B_KV_DIM, dims.ctx_len), lambda i: (0, batch_of(i))),
        ],
        out_specs=pl.BlockSpec((B_BLOCK, D_MODEL), lambda i: (i, 0)),
        scratch_shapes=[pltpu.VMEM((B_KV_HEADS, n_tiles, B_BLOCK, B_GROUP * B_BLOCK), _F32)],
        compiler_params=_cparams("arbitrary"),
        name="b_attn",
    )(sinks, *_window_bias(), qt, k, k, k, k, vt, vt, vt, vt)


def _c_proj_kernel(h_ref, g_ref, mod_ref, w_ref, bg_ref, u_ref):
    x = _norm_mod(h_ref[...], g_ref[...], _mod_part(mod_ref, 0), _mod_part(mod_ref, 1)).astype(_BF16)
    y = jnp.dot(x, w_ref[...], preferred_element_type=_F32)
    bg_ref[...] = y[:, :D_MODEL]
    u_ref[...] = y[:, D_MODEL:2 * D_MODEL] * y[:, 2 * D_MODEL:]


def _c_proj(h, g, mod_l, w_in, dims):
    rows = h.shape[0]
    tm = ROW_TILE
    return pl.pallas_call(
        _c_proj_kernel,
        out_shape=(jax.ShapeDtypeStruct((rows, D_MODEL), _F32),) * 2,
        grid=(rows // tm,),
        in_specs=[
            _row_spec(tm, D_MODEL),
            _resident((1, D_MODEL)),
            _mod_spec(dims.n_ctx_rows, dims.seq, tm),
            _resident(w_in.shape),
        ],
        out_specs=(_row_spec(tm, D_MODEL),) * 2,
        compiler_params=_cparams("arbitrary"),
        name="c_proj",
    )(h, g, mod_l, w_in)


def _c_mix(bg_ref, u_ref, up_ref, un_ref, cw_ref, cb_ref, tile, dims):
    tm = u_ref.shape[0]
    u = u_ref[...]
    row = lax.broadcasted_iota(jnp.int32, (tm, 1), 0)
    g_row = tile * tm + row
    in_ctx = g_row < dims.n_ctx_rows
    pos = jnp.where(in_ctx, g_row % dims.ctx_len, (g_row - dims.n_ctx_rows) % dims.seq)
    length = jnp.where(in_ctx, dims.ctx_len, dims.seq)
    prev = jnp.where(row == 0, up_ref[SUBLANES - 1:SUBLANES, :], pltpu.roll(u, 1, axis=0))
    nxt = jnp.where(row == tm - 1, un_ref[0:1, :], pltpu.roll(u, tm - 1, axis=0))
    prev = jnp.where(pos == 0, 0.0, prev)
    nxt = jnp.where(pos == length - 1, 0.0, nxt)
    conv = prev * cw_ref[0:1, :] + u * cw_ref[1:2, :] + nxt * cw_ref[2:3, :] + cb_ref[...]
    return (bg_ref[...] * conv).astype(_BF16)


_N_MIX_REFS = (4, 1, 6)


def _post_kernel(*refs, kind, n_h, tile0, final_norm, dims):
    n_mix = _N_MIX_REFS[kind]
    h_refs, mod_ref = refs[:n_h], refs[n_h]
    mix_refs = refs[n_h + 1:n_h + 1 + n_mix]
    wo_ref, g_ref, w1_ref, w2_ref, fg_ref, out_ref = refs[n_h + 1 + n_mix:]
    if kind == 0:
        y = _a_mix(*mix_refs)
    elif kind == 1:
        y = mix_refs[0][...]
    else:
        y = _c_mix(*mix_refs, pl.program_id(0) + tile0, dims)
    h = _load_h(h_refs)
    h = h + _mod_part(mod_ref, 2) * jnp.dot(y, wo_ref[...], preferred_element_type=_F32)
    x = _norm_mod(h, g_ref[...], _mod_part(mod_ref, 3), _mod_part(mod_ref, 4)).astype(_BF16)
    acc = jnp.zeros(h.shape, _F32)
    for c in range(MLP_HIDDEN // MLP_HIDDEN_CHUNK):
        cols = slice(c * MLP_HIDDEN_CHUNK, (c + 1) * MLP_HIDDEN_CHUNK)
        u = jnp.dot(x, w1_ref[:, cols], preferred_element_type=_F32)
        u = jnp.square(jnp.maximum(u, 0.0)).astype(_BF16)
        acc = acc + jnp.dot(u, w2_ref[cols, :], preferred_element_type=_F32)
    out = h + _mod_part(mod_ref, 5) * acc
    if final_norm:
        ms = jnp.mean(out * out, axis=-1, keepdims=True)
        out = out * lax.rsqrt(ms + NORM_EPS) * fg_ref[...]
    out_ref[...] = out


def _post(kind, h_parts, mod_l, mix, w_out, g, w1, w2, layer, final_g, dims, tile0=0, final_norm=False):
    tm = ROW_TILE
    n_tiles = dims.rows // tm - tile0
    row = functools.partial(_row_spec, tm, D_MODEL, tile0)
    if kind == 0:
        mix_specs = [row(), row(), row(), _resident((1, D_MODEL))]
    elif kind == 1:
        mix_specs = [row()]
    else:
        bg, u, conv_w, conv_b = mix
        per = tm // SUBLANES
        last = dims.rows // SUBLANES - 1
        mix = (bg, u, u, u, conv_w, conv_b)
        mix_specs = [
            row(), row(),
            pl.BlockSpec((SUBLANES, D_MODEL), lambda i: (jnp.maximum((i + tile0) * per - 1, 0), 0)),
            pl.BlockSpec((SUBLANES, D_MODEL), lambda i: (jnp.minimum((i + tile0 + 1) * per, last), 0)),
            _resident(conv_w.shape), _resident((1, D_MODEL)),
        ]
    assert len(h_parts) == 1 or tile0 == 0
    h_specs = _h_specs(tm, True) if len(h_parts) == 2 else [row()]
    return pl.pallas_call(
        functools.partial(_post_kernel, kind=kind, n_h=len(h_parts), tile0=tile0,
                          final_norm=final_norm, dims=dims),
        out_shape=jax.ShapeDtypeStruct((n_tiles * tm, D_MODEL), _F32),
        grid=(n_tiles,),
        in_specs=h_specs + [_mod_spec(dims.n_ctx_rows, dims.seq, tm, tile0)] + mix_specs + [
            _resident(w_out.shape),
            _resident((1, D_MODEL)),
            _resident_layer(w1.shape, layer),
            _resident_layer(w2.shape, layer),
            _resident((1, D_MODEL)),
        ],
        out_specs=_row_spec(tm, D_MODEL),
        compiler_params=_cparams("arbitrary"),
        name="post",
    )(*h_parts, mod_l, *mix, w_out, g, w1, w2, final_g)


class _Dims:
    def __init__(self, batch, seq, ctx_len):
        self.batch = batch
        self.seq = seq
        self.ctx_len = ctx_len
        self.n_ctx_rows = batch * ctx_len
        self.rows = self.n_ctx_rows + batch * seq


def _rope_tables(dims):
    n_freq = B_HEAD_DIM // 4
    t = jnp.arange(dims.seq)
    inv_freq = ROPE_BASE ** (-jnp.arange(n_freq, dtype=_F32) / n_freq)
    ang_row = (t // GRID_W).astype(_F32)[:, None] * inv_freq
    ang_col = (t % GRID_W).astype(_F32)[:, None] * inv_freq
    ang = jnp.concatenate([ang_row, ang_row, ang_col, ang_col], axis=1)
    sign = jnp.tile(jnp.concatenate([-jnp.ones(n_freq, _F32), jnp.ones(n_freq, _F32)]), 2)
    cos = jnp.tile(jnp.cos(ang), (dims.batch, 1))
    sin = jnp.tile(jnp.sin(ang) * sign, (dims.batch, 1))
    cos = jnp.concatenate([jnp.ones((dims.n_ctx_rows, B_HEAD_DIM), _F32), cos], axis=0)
    sin = jnp.concatenate([jnp.zeros((dims.n_ctx_rows, B_HEAD_DIM), _F32), sin], axis=0)
    return jnp.tile(cos, (1, 2)), jnp.tile(sin, (1, 2)), cos.T, sin.T


def kernel(x, c, ctx, c_ctx, ada_w, ada_b, norm_g, final_g, mlp_w1, mlp_w2,
           a_w_in, a_w_gate, a_b_gate, a_head_g, a_w_out,
           b_w_qkv, b_sinks, b_w_out, c_w_in, c_conv_w, c_conv_b, c_w_out):
    batch, seq, d = x.shape
    ctx_len = ctx.shape[1]
    depth = ada_w.shape[0]
    dims = _Dims(batch, seq, ctx_len)
    assert d == D_MODEL and seq % ROW_TILE == 0 and dims.n_ctx_rows % ROW_TILE == 0
    assert ctx_len % (A_STEP_CHUNKS * A_CHUNK) == 0 and 1 + batch <= MOD_ROWS

    assert dims.n_ctx_rows == ROW_TILE
    h_parts = (x.reshape(-1, d), ctx.reshape(-1, d))
    cs = jnp.concatenate([c_ctx[None], c, jnp.zeros((MOD_ROWS - 1 - batch, d), _F32)], axis=0)
    mod = _ada_table(cs, ada_w, ada_b).reshape(depth, MOD_ROWS, 1, 6 * d)
    rope = _rope_tables(dims)
    fg = final_g.reshape(1, d)
    w1_all = mlp_w1.astype(_BF16)
    w2_all = mlp_w2.astype(_BF16)

    for l in range(depth):
        kind, j = l % N_MIXERS, l // N_MIXERS
        mod_l = mod[l]
        g0 = norm_g[l, 0].reshape(1, d)
        g1 = norm_g[l, 1].reshape(1, d)
        last_layer = l == depth - 1
        if kind == 0:
            w_in = a_w_in[j]
            wqvo = jnp.concatenate([w_in[:, :A_QK_DIM], w_in[:, 2 * A_QK_DIM:]], axis=1).astype(_BF16)
            wkt = w_in[:, A_QK_DIM:2 * A_QK_DIM].T.astype(_BF16)
            wgt = a_w_gate[j].T.astype(_BF16)
            q, kt, v, o, gr, gc = _a_proj(h_parts, g0, mod_l, wqvo, wkt, wgt, a_b_gate[j].reshape(-1, 1), dims)
            hf, hb = _a_scan(q, kt, v, gr, gc, dims)
            mix = (hf, hb, o, a_head_g[j].reshape(1, d))
            w_out = a_w_out[j]
        elif kind == 1:
            (h,) = h_parts
            w = b_w_qkv[j]
            wqvt = jnp.concatenate([w[:, :D_MODEL], w[:, D_MODEL + B_KV_DIM:]], axis=1).T.astype(_BF16)
            wk = w[:, D_MODEL:D_MODEL + B_KV_DIM].astype(_BF16)
            qt, k, vt = _b_proj(h, g0, mod_l, wqvt, wk, rope, dims)
            mix = (_b_attn(b_sinks[j], qt, k, vt, dims),)
            w_out = b_w_out[j]
        else:
            (h,) = h_parts
            bg, u = _c_proj(h, g0, mod_l, c_w_in[j].astype(_BF16), dims)
            mix = (bg, u, c_conv_w[j], c_conv_b[j].reshape(1, d))
            w_out = c_w_out[j]
        tile0 = dims.n_ctx_rows // ROW_TILE if last_layer else 0
        h_parts = (_post(kind, h_parts, mod_l, mix, w_out.astype(_BF16), g1,
                         w1_all, w2_all, l, fg, dims,
                         tile0=tile0, final_norm=last_layer),)
    return h_parts[0].reshape(batch, seq, d)
```

```python
import functools

import jax
import jax.numpy as jnp
import numpy as np
from jax import lax
from jax.experimental import pallas as pl
from jax.experimental.pallas import tpu as pltpu

D_MODEL = 1024
GRID_W = 64
N_MIXERS = 3
A_HEADS = 8
A_QK_DIM = D_MODEL // 2
A_DK = A_QK_DIM // A_HEADS
A_DV = D_MODEL // A_HEADS
A_CHUNK = 128
A_STEP_CHUNKS = 2
B_Q_HEADS = 16
B_KV_HEADS = 4
B_HEAD_DIM = D_MODEL // B_Q_HEADS
B_GROUP = B_Q_HEADS // B_KV_HEADS
B_KV_DIM = B_KV_HEADS * B_HEAD_DIM
B_BLOCK = 128
ROPE_BASE = 10000.0
MLP_HIDDEN = 4 * D_MODEL
NORM_EPS = 1e-6
LOG2E = 1.4426950408889634

LANES = 128
SUBLANES = 8
VMEM_LIMIT_BYTES = 56 * 1024 * 1024

ROW_TILE = 512
MLP_HIDDEN_CHUNK = 1024
MOD_ROWS = 8
ADA_COL_TILE = 1536

_BF16 = jnp.bfloat16
_F32 = jnp.float32


def _cparams(*sem):
    return pltpu.CompilerParams(dimension_semantics=sem, vmem_limit_bytes=VMEM_LIMIT_BYTES)


def _resident(shape):
    nd = len(shape)
    return pl.BlockSpec(shape, lambda *_: (0,) * nd, pipeline_mode=pl.Buffered(1))


STAGE_COLS = 512


def _weight_spec():
    return pl.BlockSpec(memory_space=pl.ANY)


def _stage_scratch():
    return [pltpu.VMEM((2, D_MODEL, STAGE_COLS), _F32), pltpu.SemaphoreType.DMA((2,))]


def _stage_weights(jobs, stage_ref, sem_ref):
    def copy(i):
        src = jobs[i][0]
        dst = stage_ref.at[i % 2, pl.ds(0, src.shape[0]), pl.ds(0, src.shape[1])]
        return pltpu.make_async_copy(src, dst, sem_ref.at[i % 2])

    copy(0).start()
    for i, (src, store) in enumerate(jobs):
        if i + 1 < len(jobs):
            copy(i + 1).start()
        copy(i).wait()
        store(stage_ref[i % 2, :src.shape[0], :src.shape[1]])


def _col_tiles(w_hbm, layer, col0, ncols, store):
    jobs = []
    for c in range(0, ncols, STAGE_COLS):
        width = min(STAGE_COLS, ncols - c)
        jobs.append((w_hbm.at[layer, :, pl.ds(col0 + c, width)], functools.partial(store, c, width)))
    return jobs


def _norm_mod(h, g, shift, scale):
    ms = jnp.mean(h * h, axis=-1, keepdims=True)
    y = h * lax.rsqrt(ms + NORM_EPS) * g
    return y * (1.0 + scale) + shift


def _mod_part(mod_ref, k):
    return mod_ref[:, k * D_MODEL:(k + 1) * D_MODEL]


def _mod_spec(n_ctx_rows, seq, tm, tile0=0):
    def idx(i):
        r0 = (i + tile0) * tm
        grp = jnp.where(r0 < n_ctx_rows, 0, 1 + (r0 - n_ctx_rows) // seq)
        return (grp, 0, 0)
    return pl.BlockSpec((None, 1, 6 * D_MODEL), idx)


def _row_spec(tm, width, tile0=0):
    return pl.BlockSpec((tm, width), lambda i: (i + tile0, 0))


def _h_specs(tm, split):
    if not split:
        return [_row_spec(tm, D_MODEL)]
    return [pl.BlockSpec((tm, D_MODEL), lambda i: (jnp.maximum(i - 1, 0), 0)),
            pl.BlockSpec((tm, D_MODEL), lambda i: (0, 0))]


def _load_h(h_refs):
    if len(h_refs) == 1:
        return h_refs[0][...]
    lat_ref, ctx_ref = h_refs
    return jnp.where(pl.program_id(0) == 0, ctx_ref[...], lat_ref[...])


def _ada_kernel(cs_ref, w_ref, b_ref, o_ref):
    cs = cs_ref[...]
    s = (cs * jax.nn.sigmoid(cs)).astype(_BF16)
    o_ref[...] = jnp.dot(s, w_ref[...].astype(_BF16), preferred_element_type=_F32) + b_ref[...]


def _ada_table(cs, ada_w, ada_b):
    depth = ada_w.shape[0]
    n = ada_w.shape[2]
    return pl.pallas_call(
        _ada_kernel,
        out_shape=jax.ShapeDtypeStruct((depth, MOD_ROWS, n), _F32),
        grid=(depth, n // ADA_COL_TILE),
        in_specs=[
            pl.BlockSpec((MOD_ROWS, D_MODEL), lambda l, j: (0, 0)),
            pl.BlockSpec((None, D_MODEL, ADA_COL_TILE), lambda l, j: (l, 0, j)),
            pl.BlockSpec((None, 1, ADA_COL_TILE), lambda l, j: (l, 0, j)),
        ],
        out_specs=pl.BlockSpec((None, MOD_ROWS, ADA_COL_TILE), lambda l, j: (l, 0, j)),
        compiler_params=_cparams("arbitrary", "arbitrary"),
        name="ada_table",
    )(cs, ada_w, ada_b.reshape(depth, 1, n))


def _a_proj_kernel(*refs, n_h, layer):
    h_refs, (g_ref, mod_ref, win_hbm, wgt_ref, bg_ref,
             q_ref, kt_ref, v_ref, o_ref, gr_ref, gc_ref,
             wqvo_ref, wkt_ref, stage_ref, sem_ref) = refs[:n_h], refs[n_h:]

    @pl.when(pl.program_id(0) == 0)
    def _():
        def store_qvo(dst0, c, width, tile):
            wqvo_ref[:, dst0 + c:dst0 + c + width] = tile.astype(_BF16)

        def store_kt(c, width, tile):
            wkt_ref[c:c + width, :] = tile.T.astype(_BF16)

        _stage_weights(
            _col_tiles(win_hbm, layer, 0, A_QK_DIM, functools.partial(store_qvo, 0))
            + _col_tiles(win_hbm, layer, A_QK_DIM, A_QK_DIM, store_kt)
            + _col_tiles(win_hbm, layer, 2 * A_QK_DIM, 2 * D_MODEL, functools.partial(store_qvo, A_QK_DIM)),
            stage_ref, sem_ref)

    x = _norm_mod(_load_h(h_refs), g_ref[...], _mod_part(mod_ref, 0), _mod_part(mod_ref, 1)).astype(_BF16)
    nt = (((1,), (1,)), ((), ()))
    gt = lax.dot_general(wgt_ref[...], x, nt, preferred_element_type=_F32) + bg_ref[...]

    n_chunks = gt.shape[1] // A_CHUNK

    def by_chunk(rows):
        return jnp.concatenate([rows[:, c * A_CHUNK:(c + 1) * A_CHUNK] for c in range(n_chunks)], axis=0)

    def by_token(x):
        return jnp.concatenate([x[c * A_HEADS:(c + 1) * A_HEADS] for c in range(n_chunks)], axis=1)

    def log_sigmoid(z):
        return jnp.minimum(z, 0.0) - jnp.log1p(jnp.exp(-jnp.abs(z)))

    row_form, col_cm, col_b = [], [], []
    for d in range(2):
        li = by_chunk(gt[2 * d * A_HEADS:(2 * d + 1) * A_HEADS]) * LOG2E
        lf = log_sigmoid(by_chunk(gt[(2 * d + 1) * A_HEADS:(2 * d + 2) * A_HEADS])) * LOG2E
        b = _lane_scan(lf, jnp.add, 0.0, d == 1)
        a = li - b
        cm = _lane_scan(a, jnp.maximum, -jnp.inf, d == 1)
        row_form += [by_token(a), by_token(b)]
        col_cm.append(cm)
        col_b.append(b)
    gr_ref[...] = jnp.concatenate(row_form, axis=0)
    pad = jnp.zeros((LANES - 2 * A_HEADS, LANES), _F32)
    for c in range(n_chunks):
        rows = slice(c * A_HEADS, (c + 1) * A_HEADS)
        toks = slice(c * A_CHUNK, (c + 1) * A_CHUNK)
        gc_ref[toks, :LANES] = jnp.concatenate([col_cm[0][rows], col_cm[1][rows], pad], axis=0).T
        gc_ref[toks, LANES:] = jnp.concatenate([col_b[0][rows], col_b[1][rows], pad], axis=0).T

    y = jnp.dot(x, wqvo_ref[...], preferred_element_type=_F32)
    q_ref[...] = (y[:, :A_QK_DIM] * (A_DK ** -0.5)).astype(_BF16)
    v_ref[...] = y[:, A_QK_DIM:A_QK_DIM + D_MODEL].astype(_BF16)
    o_ref[...] = y[:, A_QK_DIM + D_MODEL:]
    kt_ref[...] = lax.dot_general(wkt_ref[...], x, nt, preferred_element_type=_F32).astype(_BF16)


def _a_proj(h_parts, g, mod_l, w_in, layer, wgt, bgate, dims):
    rows = dims.rows
    tm = ROW_TILE
    n_gate = 4 * A_HEADS
    return pl.pallas_call(
        functools.partial(_a_proj_kernel, n_h=len(h_parts), layer=layer),
        out_shape=(
            jax.ShapeDtypeStruct((rows, A_QK_DIM), _BF16),
            jax.ShapeDtypeStruct((A_QK_DIM, rows), _BF16),
            jax.ShapeDtypeStruct((rows, D_MODEL), _BF16),
            jax.ShapeDtypeStruct((rows, D_MODEL), _F32),
            jax.ShapeDtypeStruct((n_gate, rows), _F32),
            jax.ShapeDtypeStruct((rows, 2 * LANES), _F32),
        ),
        grid=(rows // tm,),
        in_specs=_h_specs(tm, len(h_parts) == 2) + [
            _resident((1, D_MODEL)),
            _mod_spec(dims.n_ctx_rows, dims.seq, tm),
            _weight_spec(),
            _resident(wgt.shape),
            _resident((n_gate, 1)),
        ],
        out_specs=(
            _row_spec(tm, A_QK_DIM),
            pl.BlockSpec((A_QK_DIM, tm), lambda i: (0, i)),
            _row_spec(tm, D_MODEL),
            _row_spec(tm, D_MODEL),
            pl.BlockSpec((n_gate, tm), lambda i: (0, i)),
            _row_spec(tm, 2 * LANES),
        ),
        scratch_shapes=[pltpu.VMEM((D_MODEL, A_QK_DIM + 2 * D_MODEL), _BF16),
                        pltpu.VMEM((A_QK_DIM, D_MODEL), _BF16)] + _stage_scratch(),
        compiler_params=_cparams("arbitrary"),
        name="a_proj",
    )(*h_parts, g, mod_l, w_in, wgt, bgate)


def _lane_scan(x, op, fill, reverse):
    lane = lax.broadcasted_iota(jnp.int32, x.shape, 1)
    k = 1
    while k < LANES:
        if reverse:
            shifted = jnp.where(lane < LANES - k, pltpu.roll(x, LANES - k, axis=1), fill)
        else:
            shifted = jnp.where(lane >= k, pltpu.roll(x, k, axis=1), fill)
        x = op(x, shifted)
        k *= 2
    return x


def _a_scan_kernel(qf_ref, ktf_ref, vf_ref, grf_ref, gcf_ref, qb_ref, ktb_ref, vb_ref, grb_ref, gcb_ref,
                   hf_ref, hb_ref, cf_ref, cb_ref, mrf_ref, mrb_ref, mc_ref):
    @pl.when(pl.program_id(1) == 0)
    def _():
        cf_ref[...] = jnp.zeros(cf_ref.shape, _F32)
        cb_ref[...] = jnp.zeros(cb_ref.shape, _F32)
        mrf_ref[...] = jnp.full(mrf_ref.shape, -jnp.inf, _F32)
        mrb_ref[...] = jnp.full(mrb_ref.shape, -jnp.inf, _F32)
        mc_ref[...] = jnp.full(mc_ref.shape, -jnp.inf, _F32)

    t_idx = lax.broadcasted_iota(jnp.int32, (A_CHUNK, A_CHUNK), 0)
    s_idx = lax.broadcasted_iota(jnp.int32, (A_CHUNK, A_CHUNK), 1)
    ones_blk = jnp.ones((A_CHUNK, A_DV), _BF16)
    lane = lax.broadcasted_iota(jnp.int32, (1, LANES), 1)
    for sub in range(A_STEP_CHUNKS):
        toks = (slice(sub * A_CHUNK, (sub + 1) * A_CHUNK),
                slice((A_STEP_CHUNKS - 1 - sub) * A_CHUNK, (A_STEP_CHUNKS - sub) * A_CHUNK))
        _a_chunk_pair(toks, (qf_ref, qb_ref), (ktf_ref, ktb_ref), (vf_ref, vb_ref), (grf_ref, grb_ref),
                      (gcf_ref, gcb_ref), (hf_ref, hb_ref), (cf_ref, cb_ref), (mrf_ref, mrb_ref), mc_ref,
                      t_idx, s_idx, ones_blk, lane)


def _a_chunk_pair(toks, q_refs, kt_refs, v_refs, gr_refs, gc_refs, h_refs, c_refs, mr_refs, mc_ref,
                  t_idx, s_idx, ones_blk, lane):
    m_col = mc_ref[0:1, :]
    m_col_new = []
    dirs = []
    for d in range(2):
        gr_ref, gc_ref, mr_ref, tok = gr_refs[d], gc_refs[d], mr_refs[d], toks[d]
        last = A_CHUNK - 1 if d == 0 else 0
        a = gr_ref[2 * d * A_HEADS:(2 * d + 1) * A_HEADS, tok]
        b = gr_ref[(2 * d + 1) * A_HEADS:(2 * d + 2) * A_HEADS, tok]
        m_row = mr_ref[:, 0:1]
        gg_end = jnp.maximum(m_row, jnp.max(a, axis=1, keepdims=True))
        ws = jnp.exp2(a - gg_end)
        decay = jnp.exp2(m_row - gg_end)
        mr_ref[...] = jnp.broadcast_to(b[:, last:last + 1] + gg_end, mr_ref.shape)
        gg_c = jnp.maximum(m_col, gc_ref[tok, :LANES])
        b_plus_gg = gc_ref[tok, LANES:] + gg_c
        clamp_c = jnp.exp2(-b_plus_gg)
        m_col_new.append(b_plus_gg[last:last + 1, :])
        dirs.append((a, m_row, ws, decay, gg_c, clamp_c))
    mc_ref[...] = jnp.broadcast_to(jnp.where(lane < A_HEADS, m_col_new[0], m_col_new[1]), mc_ref.shape)

    for hd in range(A_HEADS):
        for d in range(2):
            q_ref, kt_ref, v_ref, h_ref, c_ref, tok = q_refs[d], kt_refs[d], v_refs[d], h_refs[d], c_refs[d], toks[d]
            a, m_row, ws, decay, gg_c, clamp_c = dirs[d]
            mask = (s_idx <= t_idx) if d == 0 else (s_idx >= t_idx)
            col = d * A_HEADS + hd
            gg = jnp.broadcast_to(gg_c[:, col:col + 1], (A_CHUNK, A_CHUNK))
            clamp = jnp.broadcast_to(clamp_c[:, col:col + 1], (A_CHUNK, A_DV))
            q = q_ref[tok, hd * A_DK:(hd + 1) * A_DK]
            kt = kt_ref[hd * A_DK:(hd + 1) * A_DK, tok]
            v = v_ref[tok, hd * A_DV:(hd + 1) * A_DV]
            v_ext = jnp.concatenate([v, ones_blk], axis=1)
            c_old = c_ref[hd]
            s = jnp.dot(q, kt, preferred_element_type=_F32)
            e_keys = jnp.exp2(jnp.where(mask, a[hd:hd + 1, :] - gg, -jnp.inf))
            e_state = jnp.exp2(m_row[hd:hd + 1, :] - gg[:, :A_DK])
            p = jnp.concatenate([s * e_keys, q.astype(_F32) * e_state], axis=1).astype(_BF16)
            rhs = jnp.concatenate([v_ext, c_old.astype(_BF16)], axis=0)
            num = jnp.dot(p, rhs, preferred_element_type=_F32)
            den = jnp.maximum(jnp.abs(num[:, A_DV:]), clamp)
            h_ref[tok, hd * A_DV:(hd + 1) * A_DV] = num[:, :A_DV] / den
            kw = (kt.astype(_F32) * ws[hd:hd + 1, :]).astype(_BF16)
            c_ref[hd] = decay[hd:hd + 1, :] * c_old + jnp.dot(kw, v_ext, preferred_element_type=_F32)


def _a_scan(q, kt, v, gr, gc, dims):
    rows = q.shape[0]
    blk_rows = A_STEP_CHUNKS * A_CHUNK
    ncc = dims.ctx_len // blk_rows
    ncl = dims.seq // blk_rows
    lat0 = dims.batch * ncc

    def fwd_blk(b, j):
        return jnp.where(j < ncc, b * ncc + j, lat0 + b * ncl + (j - ncc))

    def bwd_blk(b, j):
        return jnp.where(j < ncc, b * ncc + (ncc - 1 - j), lat0 + b * ncl + (ncl - 1 - (j - ncc)))

    def specs(blk):
        return [
            pl.BlockSpec((blk_rows, A_QK_DIM), lambda b, j: (blk(b, j), 0)),
            pl.BlockSpec((A_QK_DIM, blk_rows), lambda b, j: (0, blk(b, j))),
            pl.BlockSpec((blk_rows, D_MODEL), lambda b, j: (blk(b, j), 0)),
            pl.BlockSpec((4 * A_HEADS, blk_rows), lambda b, j: (0, blk(b, j))),
            pl.BlockSpec((blk_rows, 2 * LANES), lambda b, j: (blk(b, j), 0)),
        ]

    state = pltpu.VMEM((A_HEADS, A_DK, 2 * A_DV), _F32)
    stab = pltpu.VMEM((A_HEADS, LANES), _F32)
    return pl.pallas_call(
        _a_scan_kernel,
        out_shape=(jax.ShapeDtypeStruct((rows, D_MODEL), _F32),) * 2,
        grid=(dims.batch, ncc + ncl),
        in_specs=specs(fwd_blk) + specs(bwd_blk),
        out_specs=(
            pl.BlockSpec((blk_rows, D_MODEL), lambda b, j: (fwd_blk(b, j), 0)),
            pl.BlockSpec((blk_rows, D_MODEL), lambda b, j: (bwd_blk(b, j), 0)),
        ),
        scratch_shapes=[state, state, stab, stab, stab],
        compiler_params=_cparams("arbitrary", "arbitrary"),
        name="a_scan",
    )(q, kt, v, gr, gc, q, kt, v, gr, gc)


def _a_mix(hf_ref, hb_ref, o_ref, hg_ref):
    hs = hf_ref[...] + hb_ref[...]
    parts = []
    for hd in range(A_HEADS):
        x = hs[:, hd * A_DV:(hd + 1) * A_DV]
        parts.append(x * lax.rsqrt(jnp.mean(x * x, axis=-1, keepdims=True) + NORM_EPS))
    y = jnp.concatenate(parts, axis=1) * hg_ref[...]
    return (jax.nn.sigmoid(o_ref[...]) * y).astype(_BF16)


ROPE_HALF = B_HEAD_DIM // 4


def _swap_halves_lanes(x):
    lane = lax.broadcasted_iota(jnp.int32, x.shape, 1)
    fwd = pltpu.roll(x, LANES - ROPE_HALF, axis=1)
    back = pltpu.roll(x, ROPE_HALF, axis=1)
    return jnp.where(lane % (2 * ROPE_HALF) < ROPE_HALF, fwd, back)


def _swap_halves_rows(x):
    parts = []
    for r0 in range(0, x.shape[0], 2 * ROPE_HALF):
        parts += [x[r0 + ROPE_HALF:r0 + 2 * ROPE_HALF], x[r0:r0 + ROPE_HALF]]
    return jnp.concatenate(parts, axis=0)


def _b_proj_kernel(h_ref, g_ref, mod_ref, wqkv_hbm, cos_ref, sin_ref, cost_ref, sint_ref,
                   qt_ref, k_ref, vt_ref, wqvt_ref, wk_ref, stage_ref, sem_ref, *, layer):
    @pl.when(pl.program_id(0) == 0)
    def _():
        def store_t(dst0, c, width, tile):
            wqvt_ref[dst0 + c:dst0 + c + width, :] = tile.T.astype(_BF16)

        def store_k(c, width, tile):
            wk_ref[:, c:c + width] = tile.astype(_BF16)

        _stage_weights(
            _col_tiles(wqkv_hbm, layer, 0, D_MODEL, functools.partial(store_t, 0))
            + _col_tiles(wqkv_hbm, layer, D_MODEL, B_KV_DIM, store_k)
            + _col_tiles(wqkv_hbm, layer, D_MODEL + B_KV_DIM, B_KV_DIM, functools.partial(store_t, D_MODEL)),
            stage_ref, sem_ref)

    x = _norm_mod(h_ref[...], g_ref[...], _mod_part(mod_ref, 0), _mod_part(mod_ref, 1)).astype(_BF16)
    nt = (((1,), (1,)), ((), ()))
    yt = lax.dot_general(wqvt_ref[...], x, nt, preferred_element_type=_F32)
    cost, sint = cost_ref[...], sint_ref[...]
    for hd in range(B_Q_HEADS):
        rows = slice(hd * B_HEAD_DIM, (hd + 1) * B_HEAD_DIM)
        qt = yt[rows, :]
        qt = (qt * cost + _swap_halves_rows(qt) * sint) * (B_HEAD_DIM ** -0.5 * LOG2E)
        qt_ref[rows, :] = qt.astype(_BF16)
    vt_ref[...] = yt[D_MODEL:, :].astype(_BF16)
    cos, sin = cos_ref[...], sin_ref[...]
    k = jnp.dot(x, wk_ref[...], preferred_element_type=_F32)
    for c0 in range(0, B_KV_DIM, LANES):
        kc = k[:, c0:c0 + LANES]
        k_ref[:, c0:c0 + LANES] = (kc * cos + _swap_halves_lanes(kc) * sin).astype(_BF16)


def _b_proj(h, g, mod_l, w_qkv, layer, rope, dims):
    rows = h.shape[0]
    tm = ROW_TILE
    cos, sin, cost, sint = rope
    return pl.pallas_call(
        functools.partial(_b_proj_kernel, layer=layer),
        out_shape=(
            jax.ShapeDtypeStruct((D_MODEL, rows), _BF16),
            jax.ShapeDtypeStruct((rows, B_KV_DIM), _BF16),
            jax.ShapeDtypeStruct((B_KV_DIM, rows), _BF16),
        ),
        grid=(rows // tm,),
        in_specs=[
            _row_spec(tm, D_MODEL),
            _resident((1, D_MODEL)),
            _mod_spec(dims.n_ctx_rows, dims.seq, tm),
            _weight_spec(),
            _row_spec(tm, LANES),
            _row_spec(tm, LANES),
            pl.BlockSpec((B_HEAD_DIM, tm), lambda i: (0, i)),
            pl.BlockSpec((B_HEAD_DIM, tm), lambda i: (0, i)),
        ],
        out_specs=(
            pl.BlockSpec((D_MODEL, tm), lambda i: (0, i)),
            _row_spec(tm, B_KV_DIM),
            pl.BlockSpec((B_KV_DIM, tm), lambda i: (0, i)),
        ),
        scratch_shapes=[pltpu.VMEM((D_MODEL + B_KV_DIM, D_MODEL), _BF16),
                        pltpu.VMEM((D_MODEL, B_KV_DIM), _BF16)] + _stage_scratch(),
        compiler_params=_cparams("arbitrary"),
        name="b_proj",
    )(h, g, mod_l, w_qkv, cos, sin, cost, sint)


B_ONES_ROWS = 16


def _b_attn_kernel(sink_ref, win_l_ref, win_r_ref, qt_ref, kl_ref, kc_ref, kr_ref, kx_ref,
                   vtl_ref, vtc_ref, vtr_ref, vtx_ref, o_ref, s_ref, *, n_ctx_blocks, blocks_per_seq, ctx_len):
    i = pl.program_id(0)
    is_lat = i >= n_ctx_blocks
    n = (i - n_ctx_blocks) % blocks_per_seq
    neg = -jnp.inf
    has_left = jnp.logical_and(is_lat, n >= 1)
    has_right = jnp.logical_and(is_lat, n <= blocks_per_seq - 2)
    bias_l = jnp.where(has_left, win_l_ref[...], neg)
    bias_r = jnp.where(has_right, win_r_ref[...], neg)
    bias_c = jnp.where(is_lat, 0.0, neg)

    n_q = B_GROUP * B_BLOCK
    head_of_lane = lax.broadcasted_iota(jnp.int32, (1, n_q), 1) // B_BLOCK
    ones_rows = jnp.ones((B_ONES_ROWS, B_BLOCK), _BF16)
    n_ctx_tiles = ctx_len // B_BLOCK

    def scores(g):
        qt = jnp.concatenate(
            [qt_ref[(g * B_GROUP + j) * B_HEAD_DIM:(g * B_GROUP + j + 1) * B_HEAD_DIM, :]
             for j in range(B_GROUP)], axis=1)
        sink = jnp.zeros((1, n_q), _F32)
        for j in range(B_GROUP):
            sink = jnp.where(head_of_lane == j, sink_ref[g * B_GROUP + j] * LOG2E, sink)
        ks = slice(g * B_HEAD_DIM, (g + 1) * B_HEAD_DIM)
        k_tiles = [(kl_ref[:, ks], bias_l), (kc_ref[:, ks], bias_c), (kr_ref[:, ks], bias_r)]
        for t in range(n_ctx_tiles):
            k_tiles.append((kx_ref[t * B_BLOCK:(t + 1) * B_BLOCK, ks], None))
        m_tile = None
        for t, (k, bias) in enumerate(k_tiles):
            s = jnp.dot(k, qt, preferred_element_type=_F32)
            if bias is not None:
                s = s + bias
            s_ref[g, t] = s
            m_tile = s if m_tile is None else jnp.maximum(m_tile, s)
        return jnp.maximum(sink, jnp.max(m_tile, axis=0, keepdims=True)), sink

    def attend(g, m, sink):
        ks = slice(g * B_HEAD_DIM, (g + 1) * B_HEAD_DIM)
        vt_tiles = [vtl_ref[ks, :], vtc_ref[ks, :], vtr_ref[ks, :]]
        for t in range(n_ctx_tiles):
            vt_tiles.append(vtx_ref[ks, t * B_BLOCK:(t + 1) * B_BLOCK])
        acc = jnp.zeros((B_HEAD_DIM + B_ONES_ROWS, n_q), _F32)
        for t, vt in enumerate(vt_tiles):
            p = jnp.exp2(s_ref[g, t] - m).astype(_BF16)
            vt_ext = jnp.concatenate([vt, ones_rows], axis=0)
            acc = acc + jnp.dot(vt_ext, p, preferred_element_type=_F32)
        denom = jnp.exp2(sink - m) + acc[B_HEAD_DIM:B_HEAD_DIM + 1, :]
        out_t = acc[:B_HEAD_DIM, :] * (1.0 / denom)
        for pair in range(B_GROUP // 2):
            two = jnp.concatenate([out_t[:, (2 * pair) * B_BLOCK:(2 * pair + 1) * B_BLOCK],
                                   out_t[:, (2 * pair + 1) * B_BLOCK:(2 * pair + 2) * B_BLOCK]], axis=0)
            c0 = (g * B_GROUP + 2 * pair) * B_HEAD_DIM
            o_ref[:, c0:c0 + 2 * B_HEAD_DIM] = two.T.astype(_BF16)

    pending = [scores(0), scores(1)]
    for g in range(B_KV_HEADS):
        if g + 2 < B_KV_HEADS:
            pending.append(scores(g + 2))
        attend(g, *pending[g])


def _window_bias():
    key = np.arange(B_BLOCK)[:, None]
    qry = np.arange(B_BLOCK)[None, :]
    prev_blk = np.where(key >= qry, 0.0, -np.inf).astype(np.float32)
    next_blk = np.where(key <= qry, 0.0, -np.inf).astype(np.float32)
    return jnp.asarray(np.tile(prev_blk, (1, B_GROUP))), jnp.asarray(np.tile(next_blk, (1, B_GROUP)))


def _b_attn(sinks, qt, k, vt, dims):
    rows = k.shape[0]
    ncc = dims.ctx_len // B_BLOCK
    ncl = dims.seq // B_BLOCK
    lat0 = dims.batch * ncc
    nblk = rows // B_BLOCK

    def batch_of(i):
        return jnp.where(i < lat0, i // ncc, (i - lat0) // ncl)

    def left(i):
        return jnp.maximum(i - 1, 0)

    def right(i):
        return jnp.minimum(i + 1, nblk - 1)

    def k_spec(f):
        return pl.BlockSpec((B_BLOCK, B_KV_DIM), lambda i: (f(i), 0))

    def vt_spec(f):
        return pl.BlockSpec((B_KV_DIM, B_BLOCK), lambda i: (0, f(i)))

    n_tiles = 3 + ncc
    return pl.pallas_call(
        functools.partial(_b_attn_kernel, n_ctx_blocks=lat0, blocks_per_seq=ncl, ctx_len=dims.ctx_len),
        out_shape=jax.ShapeDtypeStruct((rows, D_MODEL), _BF16),
        grid=(nblk,),
        in_specs=[
            pl.BlockSpec(memory_space=pltpu.SMEM),
            _resident((B_BLOCK, B_GROUP * B_BLOCK)),
            _resident((B_BLOCK, B_GROUP * B_BLOCK)),
            pl.BlockSpec((D_MODEL, B_BLOCK), lambda i: (0, i)),
            k_spec(left), k_spec(lambda i: i), k_spec(right),
            pl.BlockSpec((dims.ctx_len, B_KV_DIM), lambda i: (batch_of(i), 0)),
            vt_spec(left), vt_spec(lambda i: i), vt_spec(right),
            pl.BlockSpec((B_KV_DIM, dims.ctx_len), lambda i: (0, batch_of(i))),
        ],
        out_specs=pl.BlockSpec((B_BLOCK, D_MODEL), lambda i: (i, 0)),
        scratch_shapes=[pltpu.VMEM((B_KV_HEADS, n_tiles, B_BLOCK, B_GROUP * B_BLOCK), _F32)],
        compiler_params=_cparams("arbitrary"),
        name="b_attn",
    )(sinks, *_window_bias(), qt, k, k, k, k, vt, vt, vt, vt)


def _c_proj_kernel(h_ref, g_ref, mod_ref, win_hbm, bg_ref, u_ref, w_ref, stage_ref, sem_ref, *, layer):
    @pl.when(pl.program_id(0) == 0)
    def _():
        def store(c, width, tile):
            w_ref[:, c:c + width] = tile.astype(_BF16)

        _stage_weights(_col_tiles(win_hbm, layer, 0, 3 * D_MODEL, store), stage_ref, sem_ref)

    x = _norm_mod(h_ref[...], g_ref[...], _mod_part(mod_ref, 0), _mod_part(mod_ref, 1)).astype(_BF16)
    y = jnp.dot(x, w_ref[...], preferred_element_type=_F32)
    bg_ref[...] = y[:, :D_MODEL]
    u_ref[...] = y[:, D_MODEL:2 * D_MODEL] * y[:, 2 * D_MODEL:]


def _c_proj(h, g, mod_l, w_in, layer, dims):
    rows = h.shape[0]
    tm = ROW_TILE
    return pl.pallas_call(
        functools.partial(_c_proj_kernel, layer=layer),
        out_shape=(jax.ShapeDtypeStruct((rows, D_MODEL), _F32),) * 2,
        grid=(rows // tm,),
        in_specs=[
            _row_spec(tm, D_MODEL),
            _resident((1, D_MODEL)),
            _mod_spec(dims.n_ctx_rows, dims.seq, tm),
            _weight_spec(),
        ],
        out_specs=(_row_spec(tm, D_MODEL),) * 2,
        scratch_shapes=[pltpu.VMEM((D_MODEL, 3 * D_MODEL), _BF16)] + _stage_scratch(),
        compiler_params=_cparams("arbitrary"),
        name="c_proj",
    )(h, g, mod_l, w_in)


def _c_mix(bg_ref, u_ref, up_ref, un_ref, cw_ref, cb_ref, tile, dims):
    tm = u_ref.shape[0]
    u = u_ref[...]
    row = lax.broadcasted_iota(jnp.int32, (tm, 1), 0)
    g_row = tile * tm + row
    in_ctx = g_row < dims.n_ctx_rows
    pos = jnp.where(in_ctx, g_row % dims.ctx_len, (g_row - dims.n_ctx_rows) % dims.seq)
    length = jnp.where(in_ctx, dims.ctx_len, dims.seq)
    prev = jnp.where(row == 0, up_ref[SUBLANES - 1:SUBLANES, :], pltpu.roll(u, 1, axis=0))
    nxt = jnp.where(row == tm - 1, un_ref[0:1, :], pltpu.roll(u, tm - 1, axis=0))
    prev = jnp.where(pos == 0, 0.0, prev)
    nxt = jnp.where(pos == length - 1, 0.0, nxt)
    conv = prev * cw_ref[0:1, :] + u * cw_ref[1:2, :] + nxt * cw_ref[2:3, :] + cb_ref[...]
    return (bg_ref[...] * conv).astype(_BF16)


_N_MIX_REFS = (4, 1, 6)


def _post_kernel(*refs, kind, n_h, tile0, final_norm, dims, out_layer, layer):
    n_mix = _N_MIX_REFS[kind]
    h_refs, mod_ref = refs[:n_h], refs[n_h]
    mix_refs = refs[n_h + 1:n_h + 1 + n_mix]
    (wo_hbm, g_ref, w1_hbm, w2_hbm, fg_ref, out_ref,
     wo_ref, w1_ref, w2_ref, stage_ref, sem_ref) = refs[n_h + 1 + n_mix:]

    @pl.when(pl.program_id(0) == 0)
    def _():
        def store_to(dst_ref, row0, c, width, tile):
            dst_ref[row0:row0 + tile.shape[0], c:c + width] = tile.astype(_BF16)

        jobs = (_col_tiles(wo_hbm, out_layer, 0, D_MODEL, functools.partial(store_to, wo_ref, 0))
                + _col_tiles(w1_hbm, layer, 0, MLP_HIDDEN, functools.partial(store_to, w1_ref, 0)))
        for r0 in range(0, MLP_HIDDEN, D_MODEL):
            for c in range(0, D_MODEL, STAGE_COLS):
                jobs.append((w2_hbm.at[layer, pl.ds(r0, D_MODEL), pl.ds(c, STAGE_COLS)],
                             functools.partial(store_to, w2_ref, r0, c, STAGE_COLS)))
        _stage_weights(jobs, stage_ref, sem_ref)

    if kind == 0:
        y = _a_mix(*mix_refs)
    elif kind == 1:
        y = mix_refs[0][...]
    else:
        y = _c_mix(*mix_refs, pl.program_id(0) + tile0, dims)
    h = _load_h(h_refs)
    h = h + _mod_part(mod_ref, 2) * jnp.dot(y, wo_ref[...], preferred_element_type=_F32)
    x = _norm_mod(h, g_ref[...], _mod_part(mod_ref, 3), _mod_part(mod_ref, 4)).astype(_BF16)
    acc = jnp.zeros(h.shape, _F32)
    for c in range(MLP_HIDDEN // MLP_HIDDEN_CHUNK):
        cols = slice(c * MLP_HIDDEN_CHUNK, (c + 1) * MLP_HIDDEN_CHUNK)
        u = jnp.dot(x, w1_ref[:, cols], preferred_element_type=_F32)
        u = jnp.square(jnp.maximum(u, 0.0)).astype(_BF16)
        acc = acc + jnp.dot(u, w2_ref[cols, :], preferred_element_type=_F32)
    out = h + _mod_part(mod_ref, 5) * acc
    if final_norm:
        ms = jnp.mean(out * out, axis=-1, keepdims=True)
        out = out * lax.rsqrt(ms + NORM_EPS) * fg_ref[...]
    out_ref[...] = out


def _post(kind, h_parts, mod_l, mix, w_out, out_layer, g, w1, w2, layer, final_g, dims, tile0=0,
          final_norm=False):
    tm = ROW_TILE
    n_tiles = dims.rows // tm - tile0
    row = functools.partial(_row_spec, tm, D_MODEL, tile0)
    if kind == 0:
        mix_specs = [row(), row(), row(), _resident((1, D_MODEL))]
    elif kind == 1:
        mix_specs = [row()]
    else:
        bg, u, conv_w, conv_b = mix
        per = tm // SUBLANES
        last = dims.rows // SUBLANES - 1
        mix = (bg, u, u, u, conv_w, conv_b)
        mix_specs = [
            row(), row(),
            pl.BlockSpec((SUBLANES, D_MODEL), lambda i: (jnp.maximum((i + tile0) * per - 1, 0), 0)),
            pl.BlockSpec((SUBLANES, D_MODEL), lambda i: (jnp.minimum((i + tile0 + 1) * per, last), 0)),
            _resident(conv_w.shape), _resident((1, D_MODEL)),
        ]
    assert len(h_parts) == 1 or tile0 == 0
    h_specs = _h_specs(tm, True) if len(h_parts) == 2 else [row()]
    return pl.pallas_call(
        functools.partial(_post_kernel, kind=kind, n_h=len(h_parts), tile0=tile0,
                          final_norm=final_norm, dims=dims, out_layer=out_layer, layer=layer),
        out_shape=jax.ShapeDtypeStruct((n_tiles * tm, D_MODEL), _F32),
        grid=(n_tiles,),
        in_specs=h_specs + [_mod_spec(dims.n_ctx_rows, dims.seq, tm, tile0)] + mix_specs + [
            _weight_spec(),
            _resident((1, D_MODEL)),
            _weight_spec(),
            _weight_spec(),
            _resident((1, D_MODEL)),
        ],
        out_specs=_row_spec(tm, D_MODEL),
        scratch_shapes=[pltpu.VMEM((D_MODEL, D_MODEL), _BF16), pltpu.VMEM((D_MODEL, MLP_HIDDEN), _BF16),
                        pltpu.VMEM((MLP_HIDDEN, D_MODEL), _BF16)] + _stage_scratch(),
        compiler_params=_cparams("arbitrary"),
        name="post",
    )(*h_parts, mod_l, *mix, w_out, g, w1, w2, final_g)


class _Dims:
    def __init__(self, batch, seq, ctx_len):
        self.batch = batch
        self.seq = seq
        self.ctx_len = ctx_len
        self.n_ctx_rows = batch * ctx_len
        self.rows = self.n_ctx_rows + batch * seq


def _rope_tables(dims):
    n_freq = B_HEAD_DIM // 4
    t = np.arange(dims.seq)
    inv_freq = ROPE_BASE ** (-np.arange(n_freq, dtype=np.float64) / n_freq)
    ang_row = (t // GRID_W)[:, None] * inv_freq
    ang_col = (t % GRID_W)[:, None] * inv_freq
    ang = np.concatenate([ang_row, ang_row, ang_col, ang_col], axis=1)
    sign = np.tile(np.concatenate([-np.ones(n_freq), np.ones(n_freq)]), 2)
    cos = np.tile(np.cos(ang), (dims.batch, 1))
    sin = np.tile(np.sin(ang) * sign, (dims.batch, 1))
    cos = np.concatenate([np.ones((dims.n_ctx_rows, B_HEAD_DIM)), cos], axis=0).astype(np.float32)
    sin = np.concatenate([np.zeros((dims.n_ctx_rows, B_HEAD_DIM)), sin], axis=0).astype(np.float32)
    tables = (np.tile(cos, (1, 2)), np.tile(sin, (1, 2)), np.ascontiguousarray(cos.T), np.ascontiguousarray(sin.T))
    return tuple(jnp.asarray(tab) for tab in tables)


def kernel(x, c, ctx, c_ctx, ada_w, ada_b, norm_g, final_g, mlp_w1, mlp_w2,
           a_w_in, a_w_gate, a_b_gate, a_head_g, a_w_out,
           b_w_qkv, b_sinks, b_w_out, c_w_in, c_conv_w, c_conv_b, c_w_out):
    batch, seq, d = x.shape
    ctx_len = ctx.shape[1]
    depth = ada_w.shape[0]
    dims = _Dims(batch, seq, ctx_len)
    assert d == D_MODEL and seq % ROW_TILE == 0 and dims.n_ctx_rows % ROW_TILE == 0
    assert ctx_len % (A_STEP_CHUNKS * A_CHUNK) == 0 and 1 + batch <= MOD_ROWS

    assert dims.n_ctx_rows == ROW_TILE
    h_parts = (x.reshape(-1, d), ctx.reshape(-1, d))
    cs = jnp.concatenate([c_ctx[None], c, jnp.zeros((MOD_ROWS - 1 - batch, d), _F32)], axis=0)
    mod = _ada_table(cs, ada_w, ada_b).reshape(depth, MOD_ROWS, 1, 6 * d)
    rope = _rope_tables(dims)
    fg = final_g.reshape(1, d)

    for l in range(depth):
        kind, j = l % N_MIXERS, l // N_MIXERS
        mod_l = mod[l]
        g0 = norm_g[l, 0].reshape(1, d)
        g1 = norm_g[l, 1].reshape(1, d)
        last_layer = l == depth - 1
        if kind == 0:
            wgt = a_w_gate[j].T.astype(_BF16)
            q, kt, v, o, gr, gc = _a_proj(h_parts, g0, mod_l, a_w_in, j, wgt, a_b_gate[j].reshape(-1, 1), dims)
            hf, hb = _a_scan(q, kt, v, gr, gc, dims)
            mix = (hf, hb, o, a_head_g[j].reshape(1, d))
            w_out = a_w_out
        elif kind == 1:
            (h,) = h_parts
            qt, k, vt = _b_proj(h, g0, mod_l, b_w_qkv, j, rope, dims)
            mix = (_b_attn(b_sinks[j], qt, k, vt, dims),)
            w_out = b_w_out
        else:
            (h,) = h_parts
            bg, u = _c_proj(h, g0, mod_l, c_w_in, j, dims)
            mix = (bg, u, c_conv_w[j], c_conv_b[j].reshape(1, d))
            w_out = c_w_out
        tile0 = dims.n_ctx_rows // ROW_TILE if last_layer else 0
        h_parts = (_post(kind, h_parts, mod_l, mix, w_out, j, g1, mlp_w1, mlp_w2, l, fg, dims,
                         tile0=tile0, final_norm=last_layer),)
    return h_parts[0].reshape(batch, seq, d)
```

```python
import functools

import jax
import jax.numpy as jnp
import numpy as np
from jax import lax
from jax.experimental import pallas as pl
from jax.experimental.pallas import tpu as pltpu

D_MODEL = 1024
GRID_W = 64
N_MIXERS = 3
A_HEADS = 8
A_QK_DIM = D_MODEL // 2
A_DK = A_QK_DIM // A_HEADS
A_DV = D_MODEL // A_HEADS
A_CHUNK = 128
A_STEP_CHUNKS = 2
B_Q_HEADS = 16
B_KV_HEADS = 4
B_HEAD_DIM = D_MODEL // B_Q_HEADS
B_GROUP = B_Q_HEADS // B_KV_HEADS
B_KV_DIM = B_KV_HEADS * B_HEAD_DIM
B_BLOCK = 128
ROPE_BASE = 10000.0
MLP_HIDDEN = 4 * D_MODEL
NORM_EPS = 1e-6
LOG2E = 1.4426950408889634

LANES = 128
SUBLANES = 8
VMEM_LIMIT_BYTES = 56 * 1024 * 1024

ROW_TILE = 512
MLP_HIDDEN_CHUNK = 1024
MOD_ROWS = 8
ADA_COL_TILE = 1536

_BF16 = jnp.bfloat16
_F32 = jnp.float32


def _cparams(*sem):
    return pltpu.CompilerParams(dimension_semantics=sem, vmem_limit_bytes=VMEM_LIMIT_BYTES)


def _resident(shape):
    nd = len(shape)
    return pl.BlockSpec(shape, lambda *_: (0,) * nd, pipeline_mode=pl.Buffered(1))


STAGE_COLS = 512


def _weight_spec():
    return pl.BlockSpec(memory_space=pl.ANY)


def _stage_scratch():
    return [pltpu.VMEM((2, D_MODEL, STAGE_COLS), _F32), pltpu.SemaphoreType.DMA((2,))]


def _stage_weights(jobs, stage_ref, sem_ref):
    def copy(i):
        src = jobs[i][0]
        dst = stage_ref.at[i % 2, pl.ds(0, src.shape[0]), pl.ds(0, src.shape[1])]
        return pltpu.make_async_copy(src, dst, sem_ref.at[i % 2])

    copy(0).start()
    for i, (src, store) in enumerate(jobs):
        if i + 1 < len(jobs):
            copy(i + 1).start()
        copy(i).wait()
        store(stage_ref[i % 2, :src.shape[0], :src.shape[1]])


def _col_tiles(w_hbm, layer, col0, ncols, store):
    jobs = []
    for c in range(0, ncols, STAGE_COLS):
        width = min(STAGE_COLS, ncols - c)
        jobs.append((w_hbm.at[layer, :, pl.ds(col0 + c, width)], functools.partial(store, c, width)))
    return jobs


def _norm_mod(h, g, shift, scale):
    ms = jnp.mean(h * h, axis=-1, keepdims=True)
    y = h * lax.rsqrt(ms + NORM_EPS) * g
    return y * (1.0 + scale) + shift


def _mod_part(mod_ref, k):
    return mod_ref[:, k * D_MODEL:(k + 1) * D_MODEL]


def _mod_spec(n_ctx_rows, seq, tm, tile0=0):
    def idx(i):
        r0 = (i + tile0) * tm
        grp = jnp.where(r0 < n_ctx_rows, 0, 1 + (r0 - n_ctx_rows) // seq)
        return (grp, 0, 0)
    return pl.BlockSpec((None, 1, 6 * D_MODEL), idx)


def _row_spec(tm, width, tile0=0):
    return pl.BlockSpec((tm, width), lambda i: (i + tile0, 0))


def _h_specs(tm, split):
    if not split:
        return [_row_spec(tm, D_MODEL)]
    return [pl.BlockSpec((tm, D_MODEL), lambda i: (jnp.maximum(i - 1, 0), 0)),
            pl.BlockSpec((tm, D_MODEL), lambda i: (0, 0))]


def _load_h(h_refs):
    if len(h_refs) == 1:
        return h_refs[0][...]
    lat_ref, ctx_ref = h_refs
    return jnp.where(pl.program_id(0) == 0, ctx_ref[...], lat_ref[...])


def _ada_kernel(cs_ref, w_ref, b_ref, o_ref):
    cs = cs_ref[...]
    s = (cs * jax.nn.sigmoid(cs)).astype(_BF16)
    o_ref[...] = jnp.dot(s, w_ref[...].astype(_BF16), preferred_element_type=_F32) + b_ref[...]


def _ada_table(cs, ada_w, ada_b):
    depth = ada_w.shape[0]
    n = ada_w.shape[2]
    return pl.pallas_call(
        _ada_kernel,
        out_shape=jax.ShapeDtypeStruct((depth, MOD_ROWS, n), _F32),
        grid=(depth, n // ADA_COL_TILE),
        in_specs=[
            pl.BlockSpec((MOD_ROWS, D_MODEL), lambda l, j: (0, 0)),
            pl.BlockSpec((None, D_MODEL, ADA_COL_TILE), lambda l, j: (l, 0, j)),
            pl.BlockSpec((None, 1, ADA_COL_TILE), lambda l, j: (l, 0, j)),
        ],
        out_specs=pl.BlockSpec((None, MOD_ROWS, ADA_COL_TILE), lambda l, j: (l, 0, j)),
        compiler_params=_cparams("arbitrary", "arbitrary"),
        name="ada_table",
    )(cs, ada_w, ada_b.reshape(depth, 1, n))


def _a_proj_kernel(*refs, n_h, layer):
    h_refs, (g_ref, mod_ref, win_hbm, wgt_ref, bg_ref,
             q_ref, kt_ref, v_ref, o_ref, gr_ref, gc_ref,
             wqvo_ref, wkt_ref, stage_ref, sem_ref) = refs[:n_h], refs[n_h:]

    @pl.when(pl.program_id(0) == 0)
    def _():
        def store_qvo(dst0, c, width, tile):
            wqvo_ref[:, dst0 + c:dst0 + c + width] = tile.astype(_BF16)

        def store_kt(c, width, tile):
            wkt_ref[c:c + width, :] = tile.T.astype(_BF16)

        _stage_weights(
            _col_tiles(win_hbm, layer, 0, A_QK_DIM, functools.partial(store_qvo, 0))
            + _col_tiles(win_hbm, layer, A_QK_DIM, A_QK_DIM, store_kt)
            + _col_tiles(win_hbm, layer, 2 * A_QK_DIM, 2 * D_MODEL, functools.partial(store_qvo, A_QK_DIM)),
            stage_ref, sem_ref)

    x = _norm_mod(_load_h(h_refs), g_ref[...], _mod_part(mod_ref, 0), _mod_part(mod_ref, 1)).astype(_BF16)
    nt = (((1,), (1,)), ((), ()))
    gt = lax.dot_general(wgt_ref[...], x, nt, preferred_element_type=_F32) + bg_ref[...]

    n_chunks = gt.shape[1] // A_CHUNK

    def by_chunk(rows):
        return jnp.concatenate([rows[:, c * A_CHUNK:(c + 1) * A_CHUNK] for c in range(n_chunks)], axis=0)

    def by_token(x):
        return jnp.concatenate([x[c * A_HEADS:(c + 1) * A_HEADS] for c in range(n_chunks)], axis=1)

    def log_sigmoid(z):
        return jnp.minimum(z, 0.0) - jnp.log1p(jnp.exp(-jnp.abs(z)))

    row_form, col_cm, col_b = [], [], []
    for d in range(2):
        li = by_chunk(gt[2 * d * A_HEADS:(2 * d + 1) * A_HEADS]) * LOG2E
        lf = log_sigmoid(by_chunk(gt[(2 * d + 1) * A_HEADS:(2 * d + 2) * A_HEADS])) * LOG2E
        b = _lane_scan(lf, jnp.add, 0.0, d == 1)
        a = li - b
        cm = _lane_scan(a, jnp.maximum, -jnp.inf, d == 1)
        row_form += [by_token(a), by_token(b)]
        col_cm.append(cm)
        col_b.append(b)
    gr_ref[...] = jnp.concatenate(row_form, axis=0)
    pad = jnp.zeros((LANES - 2 * A_HEADS, LANES), _F32)
    for c in range(n_chunks):
        rows = slice(c * A_HEADS, (c + 1) * A_HEADS)
        toks = slice(c * A_CHUNK, (c + 1) * A_CHUNK)
        gc_ref[toks, :LANES] = jnp.concatenate([col_cm[0][rows], col_cm[1][rows], pad], axis=0).T
        gc_ref[toks, LANES:] = jnp.concatenate([col_b[0][rows], col_b[1][rows], pad], axis=0).T

    y = jnp.dot(x, wqvo_ref[...], preferred_element_type=_F32)
    q_ref[...] = (y[:, :A_QK_DIM] * (A_DK ** -0.5)).astype(_BF16)
    v_ref[...] = y[:, A_QK_DIM:A_QK_DIM + D_MODEL].astype(_BF16)
    o_ref[...] = y[:, A_QK_DIM + D_MODEL:]
    kt_ref[...] = lax.dot_general(wkt_ref[...], x, nt, preferred_element_type=_F32).astype(_BF16)


def _a_proj(h_parts, g, mod_l, w_in, layer, wgt, bgate, dims):
    rows = dims.rows
    tm = ROW_TILE
    n_gate = 4 * A_HEADS
    return pl.pallas_call(
        functools.partial(_a_proj_kernel, n_h=len(h_parts), layer=layer),
        out_shape=(
            jax.ShapeDtypeStruct((rows, A_QK_DIM), _BF16),
            jax.ShapeDtypeStruct((A_QK_DIM, rows), _BF16),
            jax.ShapeDtypeStruct((rows, D_MODEL), _BF16),
            jax.ShapeDtypeStruct((rows, D_MODEL), _F32),
            jax.ShapeDtypeStruct((n_gate, rows), _F32),
            jax.ShapeDtypeStruct((rows, 2 * LANES), _F32),
        ),
        grid=(rows // tm,),
        in_specs=_h_specs(tm, len(h_parts) == 2) + [
            _resident((1, D_MODEL)),
            _mod_spec(dims.n_ctx_rows, dims.seq, tm),
            _weight_spec(),
            _resident(wgt.shape),
            _resident((n_gate, 1)),
        ],
        out_specs=(
            _row_spec(tm, A_QK_DIM),
            pl.BlockSpec((A_QK_DIM, tm), lambda i: (0, i)),
            _row_spec(tm, D_MODEL),
            _row_spec(tm, D_MODEL),
            pl.BlockSpec((n_gate, tm), lambda i: (0, i)),
            _row_spec(tm, 2 * LANES),
        ),
        scratch_shapes=[pltpu.VMEM((D_MODEL, A_QK_DIM + 2 * D_MODEL), _BF16),
                        pltpu.VMEM((A_QK_DIM, D_MODEL), _BF16)] + _stage_scratch(),
        compiler_params=_cparams("arbitrary"),
        name="a_proj",
    )(*h_parts, g, mod_l, w_in, wgt, bgate)


def _lane_scan(x, op, fill, reverse):
    lane = lax.broadcasted_iota(jnp.int32, x.shape, 1)
    k = 1
    while k < LANES:
        if reverse:
            shifted = jnp.where(lane < LANES - k, pltpu.roll(x, LANES - k, axis=1), fill)
        else:
            shifted = jnp.where(lane >= k, pltpu.roll(x, k, axis=1), fill)
        x = op(x, shifted)
        k *= 2
    return x


def _a_scan_kernel(qf_ref, ktf_ref, vf_ref, grf_ref, gcf_ref, qb_ref, ktb_ref, vb_ref, grb_ref, gcb_ref,
                   hf_ref, hb_ref, cf_ref, cb_ref, mrf_ref, mrb_ref, mc_ref):
    @pl.when(pl.program_id(1) == 0)
    def _():
        cf_ref[...] = jnp.zeros(cf_ref.shape, _F32)
        cb_ref[...] = jnp.zeros(cb_ref.shape, _F32)
        mrf_ref[...] = jnp.full(mrf_ref.shape, -jnp.inf, _F32)
        mrb_ref[...] = jnp.full(mrb_ref.shape, -jnp.inf, _F32)
        mc_ref[...] = jnp.full(mc_ref.shape, -jnp.inf, _F32)

    t_idx = lax.broadcasted_iota(jnp.int32, (A_CHUNK, A_CHUNK), 0)
    s_idx = lax.broadcasted_iota(jnp.int32, (A_CHUNK, A_CHUNK), 1)
    ones_blk = jnp.ones((A_CHUNK, A_DV), _BF16)
    lane = lax.broadcasted_iota(jnp.int32, (1, LANES), 1)
    for sub in range(A_STEP_CHUNKS):
        toks = (slice(sub * A_CHUNK, (sub + 1) * A_CHUNK),
                slice((A_STEP_CHUNKS - 1 - sub) * A_CHUNK, (A_STEP_CHUNKS - sub) * A_CHUNK))
        _a_chunk_pair(toks, (qf_ref, qb_ref), (ktf_ref, ktb_ref), (vf_ref, vb_ref), (grf_ref, grb_ref),
                      (gcf_ref, gcb_ref), (hf_ref, hb_ref), (cf_ref, cb_ref), (mrf_ref, mrb_ref), mc_ref,
                      t_idx, s_idx, ones_blk, lane)


def _a_chunk_pair(toks, q_refs, kt_refs, v_refs, gr_refs, gc_refs, h_refs, c_refs, mr_refs, mc_ref,
                  t_idx, s_idx, ones_blk, lane):
    m_col = mc_ref[0:1, :]
    m_col_new = []
    dirs = []
    for d in range(2):
        gr_ref, gc_ref, mr_ref, tok = gr_refs[d], gc_refs[d], mr_refs[d], toks[d]
        last = A_CHUNK - 1 if d == 0 else 0
        a = gr_ref[2 * d * A_HEADS:(2 * d + 1) * A_HEADS, tok]
        b = gr_ref[(2 * d + 1) * A_HEADS:(2 * d + 2) * A_HEADS, tok]
        m_row = mr_ref[:, 0:1]
        gg_end = jnp.maximum(m_row, jnp.max(a, axis=1, keepdims=True))
        ws = jnp.exp2(a - gg_end)
        decay = jnp.exp2(m_row - gg_end)
        mr_ref[...] = jnp.broadcast_to(b[:, last:last + 1] + gg_end, mr_ref.shape)
        gg_c = jnp.maximum(m_col, gc_ref[tok, :LANES])
        b_plus_gg = gc_ref[tok, LANES:] + gg_c
        clamp_c = jnp.exp2(-b_plus_gg)
        m_col_new.append(b_plus_gg[last:last + 1, :])
        dirs.append((a, m_row, ws, decay, gg_c, clamp_c))
    mc_ref[...] = jnp.broadcast_to(jnp.where(lane < A_HEADS, m_col_new[0], m_col_new[1]), mc_ref.shape)

    for hd in range(A_HEADS):
        for d in range(2):
            q_ref, kt_ref, v_ref, h_ref, c_ref, tok = q_refs[d], kt_refs[d], v_refs[d], h_refs[d], c_refs[d], toks[d]
            a, m_row, ws, decay, gg_c, clamp_c = dirs[d]
            mask = (s_idx <= t_idx) if d == 0 else (s_idx >= t_idx)
            col = d * A_HEADS + hd
            gg = jnp.broadcast_to(gg_c[:, col:col + 1], (A_CHUNK, A_CHUNK))
            clamp = jnp.broadcast_to(clamp_c[:, col:col + 1], (A_CHUNK, A_DV))
            q = q_ref[tok, hd * A_DK:(hd + 1) * A_DK]
            kt = kt_ref[hd * A_DK:(hd + 1) * A_DK, tok]
            v = v_ref[tok, hd * A_DV:(hd + 1) * A_DV]
            v_ext = jnp.concatenate([v, ones_blk], axis=1)
            c_old = c_ref[hd]
            s = jnp.dot(q, kt, preferred_element_type=_F32)
            e_keys = jnp.exp2(jnp.where(mask, a[hd:hd + 1, :] - gg, -jnp.inf))
            e_state = jnp.exp2(m_row[hd:hd + 1, :] - gg[:, :A_DK])
            p = jnp.concatenate([s * e_keys, q.astype(_F32) * e_state], axis=1).astype(_BF16)
            rhs = jnp.concatenate([v_ext, c_old.astype(_BF16)], axis=0)
            num = jnp.dot(p, rhs, preferred_element_type=_F32)
            den = jnp.maximum(jnp.abs(num[:, A_DV:]), clamp)
            h_ref[tok, hd * A_DV:(hd + 1) * A_DV] = num[:, :A_DV] / den
            kw = (kt.astype(_F32) * ws[hd:hd + 1, :]).astype(_BF16)
            c_ref[hd] = decay[hd:hd + 1, :] * c_old + jnp.dot(kw, v_ext, preferred_element_type=_F32)


def _a_scan(q, kt, v, gr, gc, dims):
    rows = q.shape[0]
    blk_rows = A_STEP_CHUNKS * A_CHUNK
    ncc = dims.ctx_len // blk_rows
    ncl = dims.seq // blk_rows
    lat0 = dims.batch * ncc

    def fwd_blk(b, j):
        return jnp.where(j < ncc, b * ncc + j, lat0 + b * ncl + (j - ncc))

    def bwd_blk(b, j):
        return jnp.where(j < ncc, b * ncc + (ncc - 1 - j), lat0 + b * ncl + (ncl - 1 - (j - ncc)))

    def specs(blk):
        return [
            pl.BlockSpec((blk_rows, A_QK_DIM), lambda b, j: (blk(b, j), 0)),
            pl.BlockSpec((A_QK_DIM, blk_rows), lambda b, j: (0, blk(b, j))),
            pl.BlockSpec((blk_rows, D_MODEL), lambda b, j: (blk(b, j), 0)),
            pl.BlockSpec((4 * A_HEADS, blk_rows), lambda b, j: (0, blk(b, j))),
            pl.BlockSpec((blk_rows, 2 * LANES), lambda b, j: (blk(b, j), 0)),
        ]

    state = pltpu.VMEM((A_HEADS, A_DK, 2 * A_DV), _F32)
    stab = pltpu.VMEM((A_HEADS, LANES), _F32)
    return pl.pallas_call(
        _a_scan_kernel,
        out_shape=(jax.ShapeDtypeStruct((rows, D_MODEL), _F32),) * 2,
        grid=(dims.batch, ncc + ncl),
        in_specs=specs(fwd_blk) + specs(bwd_blk),
        out_specs=(
            pl.BlockSpec((blk_rows, D_MODEL), lambda b, j: (fwd_blk(b, j), 0)),
            pl.BlockSpec((blk_rows, D_MODEL), lambda b, j: (bwd_blk(b, j), 0)),
        ),
        scratch_shapes=[state, state, stab, stab, stab],
        compiler_params=_cparams("arbitrary", "arbitrary"),
        name="a_scan",
    )(q, kt, v, gr, gc, q, kt, v, gr, gc)


def _a_mix(hf_ref, hb_ref, o_ref, hg_ref):
    hs = hf_ref[...] + hb_ref[...]
    parts = []
    for hd in range(A_HEADS):
        x = hs[:, hd * A_DV:(hd + 1) * A_DV]
        parts.append(x * lax.rsqrt(jnp.mean(x * x, axis=-1, keepdims=True) + NORM_EPS))
    y = jnp.concatenate(parts, axis=1) * hg_ref[...]
    return (jax.nn.sigmoid(o_ref[...]) * y).astype(_BF16)


ROPE_HALF = B_HEAD_DIM // 4


def _swap_halves_lanes(x):
    lane = lax.broadcasted_iota(jnp.int32, x.shape, 1)
    fwd = pltpu.roll(x, LANES - ROPE_HALF, axis=1)
    back = pltpu.roll(x, ROPE_HALF, axis=1)
    return jnp.where(lane % (2 * ROPE_HALF) < ROPE_HALF, fwd, back)


def _swap_halves_rows(x):
    parts = []
    for r0 in range(0, x.shape[0], 2 * ROPE_HALF):
        parts += [x[r0 + ROPE_HALF:r0 + 2 * ROPE_HALF], x[r0:r0 + ROPE_HALF]]
    return jnp.concatenate(parts, axis=0)


def _b_proj_kernel(h_ref, g_ref, mod_ref, wqkv_hbm, cos_ref, sin_ref, cost_ref, sint_ref,
                   qt_ref, k_ref, vt_ref, wqvt_ref, wk_ref, stage_ref, sem_ref, *, layer):
    @pl.when(pl.program_id(0) == 0)
    def _():
        def store_t(dst0, c, width, tile):
            wqvt_ref[dst0 + c:dst0 + c + width, :] = tile.T.astype(_BF16)

        def store_k(c, width, tile):
            wk_ref[:, c:c + width] = tile.astype(_BF16)

        _stage_weights(
            _col_tiles(wqkv_hbm, layer, 0, D_MODEL, functools.partial(store_t, 0))
            + _col_tiles(wqkv_hbm, layer, D_MODEL, B_KV_DIM, store_k)
            + _col_tiles(wqkv_hbm, layer, D_MODEL + B_KV_DIM, B_KV_DIM, functools.partial(store_t, D_MODEL)),
            stage_ref, sem_ref)

    x = _norm_mod(h_ref[...], g_ref[...], _mod_part(mod_ref, 0), _mod_part(mod_ref, 1)).astype(_BF16)
    nt = (((1,), (1,)), ((), ()))
    yt = lax.dot_general(wqvt_ref[...], x, nt, preferred_element_type=_F32)
    cost, sint = cost_ref[...], sint_ref[...]
    for hd in range(B_Q_HEADS):
        rows = slice(hd * B_HEAD_DIM, (hd + 1) * B_HEAD_DIM)
        qt = yt[rows, :]
        qt = (qt * cost + _swap_halves_rows(qt) * sint) * (B_HEAD_DIM ** -0.5 * LOG2E)
        qt_ref[rows, :] = qt.astype(_BF16)
    vt_ref[...] = yt[D_MODEL:, :].astype(_BF16)
    cos, sin = cos_ref[...], sin_ref[...]
    k = jnp.dot(x, wk_ref[...], preferred_element_type=_F32)
    for c0 in range(0, B_KV_DIM, LANES):
        kc = k[:, c0:c0 + LANES]
        k_ref[:, c0:c0 + LANES] = (kc * cos + _swap_halves_lanes(kc) * sin).astype(_BF16)


def _b_proj(h, g, mod_l, w_qkv, layer, rope, dims):
    rows = h.shape[0]
    tm = ROW_TILE
    cos, sin, cost, sint = rope
    return pl.pallas_call(
        functools.partial(_b_proj_kernel, layer=layer),
        out_shape=(
            jax.ShapeDtypeStruct((D_MODEL, rows), _BF16),
            jax.ShapeDtypeStruct((rows, B_KV_DIM), _BF16),
            jax.ShapeDtypeStruct((B_KV_DIM, rows), _BF16),
        ),
        grid=(rows // tm,),
        in_specs=[
            _row_spec(tm, D_MODEL),
            _resident((1, D_MODEL)),
            _mod_spec(dims.n_ctx_rows, dims.seq, tm),
            _weight_spec(),
            _row_spec(tm, LANES),
            _row_spec(tm, LANES),
            pl.BlockSpec((B_HEAD_DIM, tm), lambda i: (0, i)),
            pl.BlockSpec((B_HEAD_DIM, tm), lambda i: (0, i)),
        ],
        out_specs=(
            pl.BlockSpec((D_MODEL, tm), lambda i: (0, i)),
            _row_spec(tm, B_KV_DIM),
            pl.BlockSpec((B_KV_DIM, tm), lambda i: (0, i)),
        ),
        scratch_shapes=[pltpu.VMEM((D_MODEL + B_KV_DIM, D_MODEL), _BF16),
                        pltpu.VMEM((D_MODEL, B_KV_DIM), _BF16)] + _stage_scratch(),
        compiler_params=_cparams("arbitrary"),
        name="b_proj",
    )(h, g, mod_l, w_qkv, cos, sin, cost, sint)


B_ONES_ROWS = 16
B_STEP_BLOCKS = 2


def _b_attn_kernel(sink_ref, win_l_ref, win_r_ref, qt_ref, kl_ref, kc_ref, kr_ref, kx_ref,
                   vtl_ref, vtc_ref, vtr_ref, vtx_ref, o_ref, s_ref, *, n_ctx_blocks, blocks_per_seq, ctx_len):
    first = pl.program_id(0) * B_STEP_BLOCKS
    is_lat = first >= n_ctx_blocks
    n = (first - n_ctx_blocks) % blocks_per_seq
    neg = -jnp.inf
    win_l = jnp.where(is_lat, win_l_ref[...], neg)
    win_r = jnp.where(is_lat, win_r_ref[...], neg)
    bias_c = jnp.where(is_lat, 0.0, neg)
    bias_first_l = jnp.where(jnp.logical_and(is_lat, n >= 1), win_l_ref[...], neg)
    bias_last_r = jnp.where(jnp.logical_and(is_lat, n + B_STEP_BLOCKS <= blocks_per_seq - 1), win_r_ref[...], neg)

    n_q = B_GROUP * B_BLOCK
    head_of_lane = lax.broadcasted_iota(jnp.int32, (1, n_q), 1) // B_BLOCK
    ones_rows = jnp.ones((B_ONES_ROWS, B_BLOCK), _BF16)
    n_ctx_tiles = ctx_len // B_BLOCK

    def local_tiles(blk):
        own = slice(blk * B_BLOCK, (blk + 1) * B_BLOCK)
        if blk == 0:
            left = (kl_ref, vtl_ref, slice(0, B_BLOCK), bias_first_l)
        else:
            left = (kc_ref, vtc_ref, slice((blk - 1) * B_BLOCK, blk * B_BLOCK), win_l)
        if blk == B_STEP_BLOCKS - 1:
            right = (kr_ref, vtr_ref, slice(0, B_BLOCK), bias_last_r)
        else:
            right = (kc_ref, vtc_ref, slice((blk + 1) * B_BLOCK, (blk + 2) * B_BLOCK), win_r)
        return [left, (kc_ref, vtc_ref, own, bias_c), right]

    def scores(blk, g):
        cols = slice(blk * B_BLOCK, (blk + 1) * B_BLOCK)
        qt = jnp.concatenate(
            [qt_ref[(g * B_GROUP + j) * B_HEAD_DIM:(g * B_GROUP + j + 1) * B_HEAD_DIM, cols]
             for j in range(B_GROUP)], axis=1)
        sink = jnp.zeros((1, n_q), _F32)
        for j in range(B_GROUP):
            sink = jnp.where(head_of_lane == j, sink_ref[g * B_GROUP + j] * LOG2E, sink)
        ks = slice(g * B_HEAD_DIM, (g + 1) * B_HEAD_DIM)
        k_tiles = [(k_ref[rows, ks], bias) for k_ref, _, rows, bias in local_tiles(blk)]
        for t in range(n_ctx_tiles):
            k_tiles.append((kx_ref[t * B_BLOCK:(t + 1) * B_BLOCK, ks], None))
        m_tile = None
        for t, (k, bias) in enumerate(k_tiles):
            s = jnp.dot(k, qt, preferred_element_type=_F32)
            if bias is not None:
                s = s + bias
            s_ref[blk, g, t] = s
            m_tile = s if m_tile is None else jnp.maximum(m_tile, s)
        return jnp.maximum(sink, jnp.max(m_tile, axis=0, keepdims=True)), sink

    def attend(blk, g, m, sink):
        ks = slice(g * B_HEAD_DIM, (g + 1) * B_HEAD_DIM)
        vt_tiles = [vt_ref[ks, cols] for _, vt_ref, cols, _ in local_tiles(blk)]
        for t in range(n_ctx_tiles):
            vt_tiles.append(vtx_ref[ks, t * B_BLOCK:(t + 1) * B_BLOCK])
        acc = jnp.zeros((B_HEAD_DIM + B_ONES_ROWS, n_q), _F32)
        for t, vt in enumerate(vt_tiles):
            p = jnp.exp2(s_ref[blk, g, t] - m).astype(_BF16)
            vt_ext = jnp.concatenate([vt, ones_rows], axis=0)
            acc = acc + jnp.dot(vt_ext, p, preferred_element_type=_F32)
        denom = jnp.exp2(sink - m) + acc[B_HEAD_DIM:B_HEAD_DIM + 1, :]
        out_t = acc[:B_HEAD_DIM, :] * (1.0 / denom)
        rows = slice(blk * B_BLOCK, (blk + 1) * B_BLOCK)
        for pair in range(B_GROUP // 2):
            two = jnp.concatenate([out_t[:, (2 * pair) * B_BLOCK:(2 * pair + 1) * B_BLOCK],
                                   out_t[:, (2 * pair + 1) * B_BLOCK:(2 * pair + 2) * B_BLOCK]], axis=0)
            c0 = (g * B_GROUP + 2 * pair) * B_HEAD_DIM
            o_ref[rows, c0:c0 + 2 * B_HEAD_DIM] = two.T.astype(_BF16)

    units = [(blk, g) for blk in range(B_STEP_BLOCKS) for g in range(B_KV_HEADS)]
    pending = [scores(*units[0]), scores(*units[1])]
    for i, unit in enumerate(units):
        if i + 2 < len(units):
            pending.append(scores(*units[i + 2]))
        attend(*unit, *pending[i])


def _window_bias():
    key = np.arange(B_BLOCK)[:, None]
    qry = np.arange(B_BLOCK)[None, :]
    prev_blk = np.where(key >= qry, 0.0, -np.inf).astype(np.float32)
    next_blk = np.where(key <= qry, 0.0, -np.inf).astype(np.float32)
    return jnp.asarray(np.tile(prev_blk, (1, B_GROUP))), jnp.asarray(np.tile(next_blk, (1, B_GROUP)))


def _b_attn(sinks, qt, k, vt, dims):
    rows = k.shape[0]
    step_rows = B_STEP_BLOCKS * B_BLOCK
    ncc = dims.ctx_len // B_BLOCK
    ncl = dims.seq // B_BLOCK
    lat0 = dims.batch * ncc
    nblk = rows // B_BLOCK
    assert ncc % B_STEP_BLOCKS == 0 and ncl % B_STEP_BLOCKS == 0

    def batch_of(s):
        i = s * B_STEP_BLOCKS
        return jnp.where(i < lat0, i // ncc, (i - lat0) // ncl)

    def left(s):
        return jnp.maximum(s * B_STEP_BLOCKS - 1, 0)

    def right(s):
        return jnp.minimum((s + 1) * B_STEP_BLOCKS, nblk - 1)

    n_tiles = 3 + ncc
    return pl.pallas_call(
        functools.partial(_b_attn_kernel, n_ctx_blocks=lat0, blocks_per_seq=ncl, ctx_len=dims.ctx_len),
        out_shape=jax.ShapeDtypeStruct((rows, D_MODEL), _BF16),
        grid=(nblk // B_STEP_BLOCKS,),
        in_specs=[
            pl.BlockSpec(memory_space=pltpu.SMEM),
            _resident((B_BLOCK, B_GROUP * B_BLOCK)),
            _resident((B_BLOCK, B_GROUP * B_BLOCK)),
            pl.BlockSpec((D_MODEL, step_rows), lambda s: (0, s)),
            pl.BlockSpec((B_BLOCK, B_KV_DIM), lambda s: (left(s), 0)),
            pl.BlockSpec((step_rows, B_KV_DIM), lambda s: (s, 0)),
            pl.BlockSpec((B_BLOCK, B_KV_DIM), lambda s: (right(s), 0)),
            pl.BlockSpec((dims.ctx_len, B_KV_DIM), lambda s: (batch_of(s), 0)),
            pl.BlockSpec((B_KV_DIM, B_BLOCK), lambda s: (0, left(s))),
            pl.BlockSpec((B_KV_DIM, step_rows), lambda s: (0, s)),
            pl.BlockSpec((B_KV_DIM, B_BLOCK), lambda s: (0, right(s))),
            pl.BlockSpec((B_KV_DIM, dims.ctx_len), lambda s: (0, batch_of(s))),
        ],
        out_specs=pl.BlockSpec((step_rows, D_MODEL), lambda s: (s, 0)),
        scratch_shapes=[pltpu.VMEM((B_STEP_BLOCKS, B_KV_HEADS, n_tiles, B_BLOCK, B_GROUP * B_BLOCK), _F32)],
        compiler_params=_cparams("arbitrary"),
        name="b_attn",
    )(sinks, *_window_bias(), qt, k, k, k, k, vt, vt, vt, vt)


def _c_proj_kernel(h_ref, g_ref, mod_ref, win_hbm, bg_ref, u_ref, w_ref, stage_ref, sem_ref, *, layer):
    @pl.when(pl.program_id(0) == 0)
    def _():
        def store(c, width, tile):
            w_ref[:, c:c + width] = tile.astype(_BF16)

        _stage_weights(_col_tiles(win_hbm, layer, 0, 3 * D_MODEL, store), stage_ref, sem_ref)

    x = _norm_mod(h_ref[...], g_ref[...], _mod_part(mod_ref, 0), _mod_part(mod_ref, 1)).astype(_BF16)
    y = jnp.dot(x, w_ref[...], preferred_element_type=_F32)
    bg_ref[...] = y[:, :D_MODEL]
    u_ref[...] = y[:, D_MODEL:2 * D_MODEL] * y[:, 2 * D_MODEL:]


def _c_proj(h, g, mod_l, w_in, layer, dims):
    rows = h.shape[0]
    tm = ROW_TILE
    return pl.pallas_call(
        functools.partial(_c_proj_kernel, layer=layer),
        out_shape=(jax.ShapeDtypeStruct((rows, D_MODEL), _F32),) * 2,
        grid=(rows // tm,),
        in_specs=[
            _row_spec(tm, D_MODEL),
            _resident((1, D_MODEL)),
            _mod_spec(dims.n_ctx_rows, dims.seq, tm),
            _weight_spec(),
        ],
        out_specs=(_row_spec(tm, D_MODEL),) * 2,
        scratch_shapes=[pltpu.VMEM((D_MODEL, 3 * D_MODEL), _BF16)] + _stage_scratch(),
        compiler_params=_cparams("arbitrary"),
        name="c_proj",
    )(h, g, mod_l, w_in)


def _c_mix(bg_ref, u_ref, up_ref, un_ref, cw_ref, cb_ref, tile, dims):
    tm = u_ref.shape[0]
    u = u_ref[...]
    row = lax.broadcasted_iota(jnp.int32, (tm, 1), 0)
    g_row = tile * tm + row
    in_ctx = g_row < dims.n_ctx_rows
    pos = jnp.where(in_ctx, g_row % dims.ctx_len, (g_row - dims.n_ctx_rows) % dims.seq)
    length = jnp.where(in_ctx, dims.ctx_len, dims.seq)
    prev = jnp.where(row == 0, up_ref[SUBLANES - 1:SUBLANES, :], pltpu.roll(u, 1, axis=0))
    nxt = jnp.where(row == tm - 1, un_ref[0:1, :], pltpu.roll(u, tm - 1, axis=0))
    prev = jnp.where(pos == 0, 0.0, prev)
    nxt = jnp.where(pos == length - 1, 0.0, nxt)
    conv = prev * cw_ref[0:1, :] + u * cw_ref[1:2, :] + nxt * cw_ref[2:3, :] + cb_ref[...]
    return (bg_ref[...] * conv).astype(_BF16)


_N_MIX_REFS = (4, 1, 6)


def _post_kernel(*refs, kind, n_h, tile0, final_norm, dims, out_layer, layer):
    n_mix = _N_MIX_REFS[kind]
    h_refs, mod_ref = refs[:n_h], refs[n_h]
    mix_refs = refs[n_h + 1:n_h + 1 + n_mix]
    (wo_hbm, g_ref, w1_hbm, w2_hbm, fg_ref, out_ref,
     wo_ref, w1_ref, w2_ref, stage_ref, sem_ref) = refs[n_h + 1 + n_mix:]

    @pl.when(pl.program_id(0) == 0)
    def _():
        def store_to(dst_ref, row0, c, width, tile):
            dst_ref[row0:row0 + tile.shape[0], c:c + width] = tile.astype(_BF16)

        jobs = (_col_tiles(wo_hbm, out_layer, 0, D_MODEL, functools.partial(store_to, wo_ref, 0))
                + _col_tiles(w1_hbm, layer, 0, MLP_HIDDEN, functools.partial(store_to, w1_ref, 0)))
        for r0 in range(0, MLP_HIDDEN, D_MODEL):
            for c in range(0, D_MODEL, STAGE_COLS):
                jobs.append((w2_hbm.at[layer, pl.ds(r0, D_MODEL), pl.ds(c, STAGE_COLS)],
                             functools.partial(store_to, w2_ref, r0, c, STAGE_COLS)))
        _stage_weights(jobs, stage_ref, sem_ref)

    if kind == 0:
        y = _a_mix(*mix_refs)
    elif kind == 1:
        y = mix_refs[0][...]
    else:
        y = _c_mix(*mix_refs, pl.program_id(0) + tile0, dims)
    h = _load_h(h_refs)
    h = h + _mod_part(mod_ref, 2) * jnp.dot(y, wo_ref[...], preferred_element_type=_F32)
    x = _norm_mod(h, g_ref[...], _mod_part(mod_ref, 3), _mod_part(mod_ref, 4)).astype(_BF16)
    acc = jnp.zeros(h.shape, _F32)
    for c in range(MLP_HIDDEN // MLP_HIDDEN_CHUNK):
        cols = slice(c * MLP_HIDDEN_CHUNK, (c + 1) * MLP_HIDDEN_CHUNK)
        u = jnp.dot(x, w1_ref[:, cols], preferred_element_type=_F32)
        u = jnp.square(jnp.maximum(u, 0.0)).astype(_BF16)
        acc = acc + jnp.dot(u, w2_ref[cols, :], preferred_element_type=_F32)
    out = h + _mod_part(mod_ref, 5) * acc
    if final_norm:
        ms = jnp.mean(out * out, axis=-1, keepdims=True)
        out = out * lax.rsqrt(ms + NORM_EPS) * fg_ref[...]
    out_ref[...] = out


def _post(kind, h_parts, mod_l, mix, w_out, out_layer, g, w1, w2, layer, final_g, dims, tile0=0,
          final_norm=False):
    tm = ROW_TILE
    n_tiles = dims.rows // tm - tile0
    row = functools.partial(_row_spec, tm, D_MODEL, tile0)
    if kind == 0:
        mix_specs = [row(), row(), row(), _resident((1, D_MODEL))]
    elif kind == 1:
        mix_specs = [row()]
    else:
        bg, u, conv_w, conv_b = mix
        per = tm // SUBLANES
        last = dims.rows // SUBLANES - 1
        mix = (bg, u, u, u, conv_w, conv_b)
        mix_specs = [
            row(), row(),
            pl.BlockSpec((SUBLANES, D_MODEL), lambda i: (jnp.maximum((i + tile0) * per - 1, 0), 0)),
            pl.BlockSpec((SUBLANES, D_MODEL), lambda i: (jnp.minimum((i + tile0 + 1) * per, last), 0)),
            _resident(conv_w.shape), _resident((1, D_MODEL)),
        ]
    assert len(h_parts) == 1 or tile0 == 0
    h_specs = _h_specs(tm, True) if len(h_parts) == 2 else [row()]
    return pl.pallas_call(
        functools.partial(_post_kernel, kind=kind, n_h=len(h_parts), tile0=tile0,
                          final_norm=final_norm, dims=dims, out_layer=out_layer, layer=layer),
        out_shape=jax.ShapeDtypeStruct((n_tiles * tm, D_MODEL), _F32),
        grid=(n_tiles,),
        in_specs=h_specs + [_mod_spec(dims.n_ctx_rows, dims.seq, tm, tile0)] + mix_specs + [
            _weight_spec(),
            _resident((1, D_MODEL)),
            _weight_spec(),
            _weight_spec(),
            _resident((1, D_MODEL)),
        ],
        out_specs=_row_spec(tm, D_MODEL),
        scratch_shapes=[pltpu.VMEM((D_MODEL, D_MODEL), _BF16), pltpu.VMEM((D_MODEL, MLP_HIDDEN), _BF16),
                        pltpu.VMEM((MLP_HIDDEN, D_MODEL), _BF16)] + _stage_scratch(),
        compiler_params=_cparams("arbitrary"),
        name="post",
    )(*h_parts, mod_l, *mix, w_out, g, w1, w2, final_g)


class _Dims:
    def __init__(self, batch, seq, ctx_len):
        self.batch = batch
        self.seq = seq
        self.ctx_len = ctx_len
        self.n_ctx_rows = batch * ctx_len
        self.rows = self.n_ctx_rows + batch * seq


def _rope_tables(dims):
    n_freq = B_HEAD_DIM // 4
    t = np.arange(dims.seq)
    inv_freq = ROPE_BASE ** (-np.arange(n_freq, dtype=np.float64) / n_freq)
    ang_row = (t // GRID_W)[:, None] * inv_freq
    ang_col = (t % GRID_W)[:, None] * inv_freq
    ang = np.concatenate([ang_row, ang_row, ang_col, ang_col], axis=1)
    sign = np.tile(np.concatenate([-np.ones(n_freq), np.ones(n_freq)]), 2)
    cos = np.tile(np.cos(ang), (dims.batch, 1))
    sin = np.tile(np.sin(ang) * sign, (dims.batch, 1))
    cos = np.concatenate([np.ones((dims.n_ctx_rows, B_HEAD_DIM)), cos], axis=0).astype(np.float32)
    sin = np.concatenate([np.zeros((dims.n_ctx_rows, B_HEAD_DIM)), sin], axis=0).astype(np.float32)
    tables = (np.tile(cos, (1, 2)), np.tile(sin, (1, 2)), np.ascontiguousarray(cos.T), np.ascontiguousarray(sin.T))
    return tuple(jnp.asarray(tab) for tab in tables)


def kernel(x, c, ctx, c_ctx, ada_w, ada_b, norm_g, final_g, mlp_w1, mlp_w2,
           a_w_in, a_w_gate, a_b_gate, a_head_g, a_w_out,
           b_w_qkv, b_sinks, b_w_out, c_w_in, c_conv_w, c_conv_b, c_w_out):
    batch, seq, d = x.shape
    ctx_len = ctx.shape[1]
    depth = ada_w.shape[0]
    dims = _Dims(batch, seq, ctx_len)
    assert d == D_MODEL and seq % ROW_TILE == 0 and dims.n_ctx_rows % ROW_TILE == 0
    assert ctx_len % (A_STEP_CHUNKS * A_CHUNK) == 0 and 1 + batch <= MOD_ROWS

    assert dims.n_ctx_rows == ROW_TILE
    h_parts = (x.reshape(-1, d), ctx.reshape(-1, d))
    cs = jnp.concatenate([c_ctx[None], c, jnp.zeros((MOD_ROWS - 1 - batch, d), _F32)], axis=0)
    mod = _ada_table(cs, ada_w, ada_b).reshape(depth, MOD_ROWS, 1, 6 * d)
    rope = _rope_tables(dims)
    fg = final_g.reshape(1, d)

    for l in range(depth):
        kind, j = l % N_MIXERS, l // N_MIXERS
        mod_l = mod[l]
        g0 = norm_g[l, 0].reshape(1, d)
        g1 = norm_g[l, 1].reshape(1, d)
        last_layer = l == depth - 1
        if kind == 0:
            wgt = a_w_gate[j].T.astype(_BF16)
            q, kt, v, o, gr, gc = _a_proj(h_parts, g0, mod_l, a_w_in, j, wgt, a_b_gate[j].reshape(-1, 1), dims)
            hf, hb = _a_scan(q, kt, v, gr, gc, dims)
            mix = (hf, hb, o, a_head_g[j].reshape(1, d))
            w_out = a_w_out
        elif kind == 1:
            (h,) = h_parts
            qt, k, vt = _b_proj(h, g0, mod_l, b_w_qkv, j, rope, dims)
            mix = (_b_attn(b_sinks[j], qt, k, vt, dims),)
            w_out = b_w_out
        else:
            (h,) = h_parts
            bg, u = _c_proj(h, g0, mod_l, c_w_in, j, dims)
            mix = (bg, u, c_conv_w[j], c_conv_b[j].reshape(1, d))
            w_out = c_w_out
        tile0 = dims.n_ctx_rows // ROW_TILE if last_layer else 0
        h_parts = (_post(kind, h_parts, mod_l, mix, w_out, j, g1, mlp_w1, mlp_w2, l, fg, dims,
                         tile0=tile0, final_norm=last_layer),)
    return h_parts[0].reshape(batch, seq, d)
```

```python
import functools

import jax
import jax.numpy as jnp
import numpy as np
from jax import lax
from jax.experimental import pallas as pl
from jax.experimental.pallas import tpu as pltpu

D_MODEL = 1024
GRID_W = 64
N_MIXERS = 3
A_HEADS = 8
A_QK_DIM = D_MODEL // 2
A_DK = A_QK_DIM // A_HEADS
A_DV = D_MODEL // A_HEADS
A_CHUNK = 128
A_STEP_CHUNKS = 2
B_Q_HEADS = 16
B_KV_HEADS = 4
B_HEAD_DIM = D_MODEL // B_Q_HEADS
B_GROUP = B_Q_HEADS // B_KV_HEADS
B_KV_DIM = B_KV_HEADS * B_HEAD_DIM
B_BLOCK = 128
ROPE_BASE = 10000.0
MLP_HIDDEN = 4 * D_MODEL
NORM_EPS = 1e-6
LOG2E = 1.4426950408889634

LANES = 128
SUBLANES = 8
VMEM_LIMIT_BYTES = 56 * 1024 * 1024

ROW_TILE = 512
MLP_HIDDEN_CHUNK = 1024
MOD_ROWS = 8
ADA_COL_TILE = 1536

_BF16 = jnp.bfloat16
_F32 = jnp.float32


def _cparams(*sem):
    return pltpu.CompilerParams(dimension_semantics=sem, vmem_limit_bytes=VMEM_LIMIT_BYTES)


def _resident(shape):
    nd = len(shape)
    return pl.BlockSpec(shape, lambda *_: (0,) * nd, pipeline_mode=pl.Buffered(1))


STAGE_COLS = 512
STAGE_DEPTH = 3


def _weight_spec():
    return pl.BlockSpec(memory_space=pl.ANY)


def _stage_scratch():
    return [pltpu.VMEM((STAGE_DEPTH, D_MODEL, STAGE_COLS), _F32), pltpu.SemaphoreType.DMA((STAGE_DEPTH,))]


class _WeightStager:
    def __init__(self, jobs, stage_ref, sem_ref):
        self.jobs, self.stage_ref, self.sem_ref = jobs, stage_ref, sem_ref
        self.started = self.done = 0

    def _copy(self, i):
        src = self.jobs[i][0]
        dst = self.stage_ref.at[i % STAGE_DEPTH, pl.ds(0, src.shape[0]), pl.ds(0, src.shape[1])]
        return pltpu.make_async_copy(src, dst, self.sem_ref.at[i % STAGE_DEPTH])

    def _fill(self):
        while self.started < min(self.done + STAGE_DEPTH, len(self.jobs)):
            self._copy(self.started).start()
            self.started += 1

    def need(self, n):
        self._fill()
        while self.done < n:
            src, store = self.jobs[self.done]
            self._copy(self.done).wait()
            store(self.stage_ref[self.done % STAGE_DEPTH, :src.shape[0], :src.shape[1]])
            self.done += 1
            self._fill()


def _stage_weights(jobs, stage_ref, sem_ref):
    _WeightStager(jobs, stage_ref, sem_ref).need(len(jobs))


def _col_tiles(w_hbm, layer, col0, ncols, store):
    jobs = []
    for c in range(0, ncols, STAGE_COLS):
        width = min(STAGE_COLS, ncols - c)
        jobs.append((w_hbm.at[layer, :, pl.ds(col0 + c, width)], functools.partial(store, c, width)))
    return jobs


def _norm_mod(h, g, shift, scale):
    ms = jnp.mean(h * h, axis=-1, keepdims=True)
    y = h * lax.rsqrt(ms + NORM_EPS) * g
    return y * (1.0 + scale) + shift


def _mod_part(mod_ref, k):
    return mod_ref[:, k * D_MODEL:(k + 1) * D_MODEL]


def _mod_spec(n_ctx_rows, seq, tm, tile0=0):
    def idx(i):
        r0 = (i + tile0) * tm
        grp = jnp.where(r0 < n_ctx_rows, 0, 1 + (r0 - n_ctx_rows) // seq)
        return (grp, 0, 0)
    return pl.BlockSpec((None, 1, 6 * D_MODEL), idx)


def _row_spec(tm, width, tile0=0):
    return pl.BlockSpec((tm, width), lambda i: (i + tile0, 0))


def _h_specs(tm, split):
    if not split:
        return [_row_spec(tm, D_MODEL)]
    return [pl.BlockSpec((tm, D_MODEL), lambda i: (jnp.maximum(i - 1, 0), 0)),
            pl.BlockSpec((tm, D_MODEL), lambda i: (0, 0))]


def _load_h(h_refs):
    if len(h_refs) == 1:
        return h_refs[0][...]
    lat_ref, ctx_ref = h_refs
    return jnp.where(pl.program_id(0) == 0, ctx_ref[...], lat_ref[...])


def _ada_kernel(cs_ref, w_ref, b_ref, o_ref):
    cs = cs_ref[...]
    s = (cs * jax.nn.sigmoid(cs)).astype(_BF16)
    o_ref[...] = jnp.dot(s, w_ref[...].astype(_BF16), preferred_element_type=_F32) + b_ref[...]


def _ada_table(cs, ada_w, ada_b):
    depth = ada_w.shape[0]
    n = ada_w.shape[2]
    return pl.pallas_call(
        _ada_kernel,
        out_shape=jax.ShapeDtypeStruct((depth, MOD_ROWS, n), _F32),
        grid=(depth, n // ADA_COL_TILE),
        in_specs=[
            pl.BlockSpec((MOD_ROWS, D_MODEL), lambda l, j: (0, 0)),
            pl.BlockSpec((None, D_MODEL, ADA_COL_TILE), lambda l, j: (l, 0, j)),
            pl.BlockSpec((None, 1, ADA_COL_TILE), lambda l, j: (l, 0, j)),
        ],
        out_specs=pl.BlockSpec((None, MOD_ROWS, ADA_COL_TILE), lambda l, j: (l, 0, j)),
        compiler_params=_cparams("arbitrary", "arbitrary"),
        name="ada_table",
    )(cs, ada_w, ada_b.reshape(depth, 1, n))


def _a_proj_kernel(*refs, n_h, layer):
    h_refs, (g_ref, mod_ref, win_hbm, wgt_ref, bg_ref,
             q_ref, kt_ref, v_ref, o_ref, gr_ref, gc_ref,
             wqvo_ref, wkt_ref, stage_ref, sem_ref) = refs[:n_h], refs[n_h:]

    @pl.when(pl.program_id(0) == 0)
    def _():
        def store_qvo(dst0, c, width, tile):
            wqvo_ref[:, dst0 + c:dst0 + c + width] = tile.astype(_BF16)

        def store_kt(c, width, tile):
            wkt_ref[c:c + width, :] = tile.T.astype(_BF16)

        _stage_weights(
            _col_tiles(win_hbm, layer, 0, A_QK_DIM, functools.partial(store_qvo, 0))
            + _col_tiles(win_hbm, layer, A_QK_DIM, A_QK_DIM, store_kt)
            + _col_tiles(win_hbm, layer, 2 * A_QK_DIM, 2 * D_MODEL, functools.partial(store_qvo, A_QK_DIM)),
            stage_ref, sem_ref)

    x = _norm_mod(_load_h(h_refs), g_ref[...], _mod_part(mod_ref, 0), _mod_part(mod_ref, 1)).astype(_BF16)
    nt = (((1,), (1,)), ((), ()))
    gt = lax.dot_general(wgt_ref[...], x, nt, preferred_element_type=_F32) + bg_ref[...]

    n_chunks = gt.shape[1] // A_CHUNK

    def by_chunk(rows):
        return jnp.concatenate([rows[:, c * A_CHUNK:(c + 1) * A_CHUNK] for c in range(n_chunks)], axis=0)

    def by_token(x):
        return jnp.concatenate([x[c * A_HEADS:(c + 1) * A_HEADS] for c in range(n_chunks)], axis=1)

    def log_sigmoid(z):
        return jnp.minimum(z, 0.0) - jnp.log1p(jnp.exp(-jnp.abs(z)))

    row_form, col_cm, col_b = [], [], []
    for d in range(2):
        li = by_chunk(gt[2 * d * A_HEADS:(2 * d + 1) * A_HEADS]) * LOG2E
        lf = log_sigmoid(by_chunk(gt[(2 * d + 1) * A_HEADS:(2 * d + 2) * A_HEADS])) * LOG2E
        b = _lane_scan(lf, jnp.add, 0.0, d == 1)
        a = li - b
        cm = _lane_scan(a, jnp.maximum, -jnp.inf, d == 1)
        row_form += [by_token(a), by_token(b)]
        col_cm.append(cm)
        col_b.append(b)
    gr_ref[...] = jnp.concatenate(row_form, axis=0)
    pad = jnp.zeros((LANES - 2 * A_HEADS, LANES), _F32)
    for c in range(n_chunks):
        rows = slice(c * A_HEADS, (c + 1) * A_HEADS)
        toks = slice(c * A_CHUNK, (c + 1) * A_CHUNK)
        gc_ref[toks, :LANES] = jnp.concatenate([col_cm[0][rows], col_cm[1][rows], pad], axis=0).T
        gc_ref[toks, LANES:] = jnp.concatenate([col_b[0][rows], col_b[1][rows], pad], axis=0).T

    y = jnp.dot(x, wqvo_ref[...], preferred_element_type=_F32)
    q_ref[...] = (y[:, :A_QK_DIM] * (A_DK ** -0.5)).astype(_BF16)
    v_ref[...] = y[:, A_QK_DIM:A_QK_DIM + D_MODEL].astype(_BF16)
    o_ref[...] = y[:, A_QK_DIM + D_MODEL:]
    kt_ref[...] = lax.dot_general(wkt_ref[...], x, nt, preferred_element_type=_F32).astype(_BF16)


def _a_proj(h_parts, g, mod_l, w_in, layer, wgt, bgate, dims):
    rows = dims.rows
    tm = ROW_TILE
    n_gate = 4 * A_HEADS
    return pl.pallas_call(
        functools.partial(_a_proj_kernel, n_h=len(h_parts), layer=layer),
        out_shape=(
            jax.ShapeDtypeStruct((rows, A_QK_DIM), _BF16),
            jax.ShapeDtypeStruct((A_QK_DIM, rows), _BF16),
            jax.ShapeDtypeStruct((rows, D_MODEL), _BF16),
            jax.ShapeDtypeStruct((rows, D_MODEL), _F32),
            jax.ShapeDtypeStruct((n_gate, rows), _F32),
            jax.ShapeDtypeStruct((rows, 2 * LANES), _F32),
        ),
        grid=(rows // tm,),
        in_specs=_h_specs(tm, len(h_parts) == 2) + [
            _resident((1, D_MODEL)),
            _mod_spec(dims.n_ctx_rows, dims.seq, tm),
            _weight_spec(),
            _resident(wgt.shape),
            _resident((n_gate, 1)),
        ],
        out_specs=(
            _row_spec(tm, A_QK_DIM),
            pl.BlockSpec((A_QK_DIM, tm), lambda i: (0, i)),
            _row_spec(tm, D_MODEL),
            _row_spec(tm, D_MODEL),
            pl.BlockSpec((n_gate, tm), lambda i: (0, i)),
            _row_spec(tm, 2 * LANES),
        ),
        scratch_shapes=[pltpu.VMEM((D_MODEL, A_QK_DIM + 2 * D_MODEL), _BF16),
                        pltpu.VMEM((A_QK_DIM, D_MODEL), _BF16)] + _stage_scratch(),
        compiler_params=_cparams("arbitrary"),
        name="a_proj",
    )(*h_parts, g, mod_l, w_in, wgt, bgate)


def _lane_scan(x, op, fill, reverse):
    lane = lax.broadcasted_iota(jnp.int32, x.shape, 1)
    k = 1
    while k < LANES:
        if reverse:
            shifted = jnp.where(lane < LANES - k, pltpu.roll(x, LANES - k, axis=1), fill)
        else:
            shifted = jnp.where(lane >= k, pltpu.roll(x, k, axis=1), fill)
        x = op(x, shifted)
        k *= 2
    return x


def _a_scan_kernel(qf_ref, ktf_ref, vf_ref, grf_ref, gcf_ref, qb_ref, ktb_ref, vb_ref, grb_ref, gcb_ref,
                   hf_ref, hb_ref, cf_ref, cb_ref, mrf_ref, mrb_ref, mc_ref):
    @pl.when(pl.program_id(1) == 0)
    def _():
        cf_ref[...] = jnp.zeros(cf_ref.shape, _F32)
        cb_ref[...] = jnp.zeros(cb_ref.shape, _F32)
        mrf_ref[...] = jnp.full(mrf_ref.shape, -jnp.inf, _F32)
        mrb_ref[...] = jnp.full(mrb_ref.shape, -jnp.inf, _F32)
        mc_ref[...] = jnp.full(mc_ref.shape, -jnp.inf, _F32)

    t_idx = lax.broadcasted_iota(jnp.int32, (A_CHUNK, A_CHUNK), 0)
    s_idx = lax.broadcasted_iota(jnp.int32, (A_CHUNK, A_CHUNK), 1)
    ones_blk = jnp.ones((A_CHUNK, A_DV), _BF16)
    lane = lax.broadcasted_iota(jnp.int32, (1, LANES), 1)
    for sub in range(A_STEP_CHUNKS):
        toks = (slice(sub * A_CHUNK, (sub + 1) * A_CHUNK),
                slice((A_STEP_CHUNKS - 1 - sub) * A_CHUNK, (A_STEP_CHUNKS - sub) * A_CHUNK))
        _a_chunk_pair(toks, (qf_ref, qb_ref), (ktf_ref, ktb_ref), (vf_ref, vb_ref), (grf_ref, grb_ref),
                      (gcf_ref, gcb_ref), (hf_ref, hb_ref), (cf_ref, cb_ref), (mrf_ref, mrb_ref), mc_ref,
                      t_idx, s_idx, ones_blk, lane)


def _a_chunk_pair(toks, q_refs, kt_refs, v_refs, gr_refs, gc_refs, h_refs, c_refs, mr_refs, mc_ref,
                  t_idx, s_idx, ones_blk, lane):
    m_col = mc_ref[0:1, :]
    m_col_new = []
    dirs = []
    for d in range(2):
        gr_ref, gc_ref, mr_ref, tok = gr_refs[d], gc_refs[d], mr_refs[d], toks[d]
        last = A_CHUNK - 1 if d == 0 else 0
        a = gr_ref[2 * d * A_HEADS:(2 * d + 1) * A_HEADS, tok]
        b = gr_ref[(2 * d + 1) * A_HEADS:(2 * d + 2) * A_HEADS, tok]
        m_row = mr_ref[:, 0:1]
        gg_end = jnp.maximum(m_row, jnp.max(a, axis=1, keepdims=True))
        ws = jnp.exp2(a - gg_end)
        decay = jnp.exp2(m_row - gg_end)
        mr_ref[...] = jnp.broadcast_to(b[:, last:last + 1] + gg_end, mr_ref.shape)
        gg_c = jnp.maximum(m_col, gc_ref[tok, :LANES])
        b_plus_gg = gc_ref[tok, LANES:] + gg_c
        clamp_c = jnp.exp2(-b_plus_gg)
        m_col_new.append(b_plus_gg[last:last + 1, :])
        dirs.append((a, m_row, ws, decay, gg_c, clamp_c))
    mc_ref[...] = jnp.broadcast_to(jnp.where(lane < A_HEADS, m_col_new[0], m_col_new[1]), mc_ref.shape)

    for hd in range(A_HEADS):
        for d in range(2):
            q_ref, kt_ref, v_ref, h_ref, c_ref, tok = q_refs[d], kt_refs[d], v_refs[d], h_refs[d], c_refs[d], toks[d]
            a, m_row, ws, decay, gg_c, clamp_c = dirs[d]
            mask = (s_idx <= t_idx) if d == 0 else (s_idx >= t_idx)
            col = d * A_HEADS + hd
            gg = jnp.broadcast_to(gg_c[:, col:col + 1], (A_CHUNK, A_CHUNK))
            clamp = jnp.broadcast_to(clamp_c[:, col:col + 1], (A_CHUNK, A_DV))
            q = q_ref[tok, hd * A_DK:(hd + 1) * A_DK]
            kt = kt_ref[hd * A_DK:(hd + 1) * A_DK, tok]
            v = v_ref[tok, hd * A_DV:(hd + 1) * A_DV]
            v_ext = jnp.concatenate([v, ones_blk], axis=1)
            c_old = c_ref[hd]
            s = jnp.dot(q, kt, preferred_element_type=_F32)
            e_keys = jnp.exp2(jnp.where(mask, a[hd:hd + 1, :] - gg, -jnp.inf))
            e_state = jnp.exp2(m_row[hd:hd + 1, :] - gg[:, :A_DK])
            p = jnp.concatenate([s * e_keys, q.astype(_F32) * e_state], axis=1).astype(_BF16)
            rhs = jnp.concatenate([v_ext, c_old.astype(_BF16)], axis=0)
            num = jnp.dot(p, rhs, preferred_element_type=_F32)
            den = jnp.maximum(jnp.abs(num[:, A_DV:]), clamp)
            h_ref[tok, hd * A_DV:(hd + 1) * A_DV] = num[:, :A_DV] / den
            kw = (kt.astype(_F32) * ws[hd:hd + 1, :]).astype(_BF16)
            c_ref[hd] = decay[hd:hd + 1, :] * c_old + jnp.dot(kw, v_ext, preferred_element_type=_F32)


def _a_scan(q, kt, v, gr, gc, dims):
    rows = q.shape[0]
    blk_rows = A_STEP_CHUNKS * A_CHUNK
    ncc = dims.ctx_len // blk_rows
    ncl = dims.seq // blk_rows
    lat0 = dims.batch * ncc

    def fwd_blk(b, j):
        return jnp.where(j < ncc, b * ncc + j, lat0 + b * ncl + (j - ncc))

    def bwd_blk(b, j):
        return jnp.where(j < ncc, b * ncc + (ncc - 1 - j), lat0 + b * ncl + (ncl - 1 - (j - ncc)))

    def specs(blk):
        return [
            pl.BlockSpec((blk_rows, A_QK_DIM), lambda b, j: (blk(b, j), 0)),
            pl.BlockSpec((A_QK_DIM, blk_rows), lambda b, j: (0, blk(b, j))),
            pl.BlockSpec((blk_rows, D_MODEL), lambda b, j: (blk(b, j), 0)),
            pl.BlockSpec((4 * A_HEADS, blk_rows), lambda b, j: (0, blk(b, j))),
            pl.BlockSpec((blk_rows, 2 * LANES), lambda b, j: (blk(b, j), 0)),
        ]

    state = pltpu.VMEM((A_HEADS, A_DK, 2 * A_DV), _F32)
    stab = pltpu.VMEM((A_HEADS, LANES), _F32)
    return pl.pallas_call(
        _a_scan_kernel,
        out_shape=(jax.ShapeDtypeStruct((rows, D_MODEL), _F32),) * 2,
        grid=(dims.batch, ncc + ncl),
        in_specs=specs(fwd_blk) + specs(bwd_blk),
        out_specs=(
            pl.BlockSpec((blk_rows, D_MODEL), lambda b, j: (fwd_blk(b, j), 0)),
            pl.BlockSpec((blk_rows, D_MODEL), lambda b, j: (bwd_blk(b, j), 0)),
        ),
        scratch_shapes=[state, state, stab, stab, stab],
        compiler_params=_cparams("arbitrary", "arbitrary"),
        name="a_scan",
    )(q, kt, v, gr, gc, q, kt, v, gr, gc)


def _a_mix(hf_ref, hb_ref, o_ref, hg_ref):
    hs = hf_ref[...] + hb_ref[...]
    parts = []
    for hd in range(A_HEADS):
        x = hs[:, hd * A_DV:(hd + 1) * A_DV]
        parts.append(x * lax.rsqrt(jnp.mean(x * x, axis=-1, keepdims=True) + NORM_EPS))
    y = jnp.concatenate(parts, axis=1) * hg_ref[...]
    return (jax.nn.sigmoid(o_ref[...]) * y).astype(_BF16)


ROPE_HALF = B_HEAD_DIM // 4


def _swap_halves_lanes(x):
    lane = lax.broadcasted_iota(jnp.int32, x.shape, 1)
    fwd = pltpu.roll(x, LANES - ROPE_HALF, axis=1)
    back = pltpu.roll(x, ROPE_HALF, axis=1)
    return jnp.where(lane % (2 * ROPE_HALF) < ROPE_HALF, fwd, back)


def _swap_halves_rows(x):
    parts = []
    for r0 in range(0, x.shape[0], 2 * ROPE_HALF):
        parts += [x[r0 + ROPE_HALF:r0 + 2 * ROPE_HALF], x[r0:r0 + ROPE_HALF]]
    return jnp.concatenate(parts, axis=0)


def _b_proj_kernel(h_ref, g_ref, mod_ref, wqkv_hbm, cos_ref, sin_ref, cost_ref, sint_ref,
                   qt_ref, k_ref, vt_ref, wqvt_ref, wk_ref, stage_ref, sem_ref, *, layer):
    @pl.when(pl.program_id(0) == 0)
    def _():
        def store_t(dst0, c, width, tile):
            wqvt_ref[dst0 + c:dst0 + c + width, :] = tile.T.astype(_BF16)

        def store_k(c, width, tile):
            wk_ref[:, c:c + width] = tile.astype(_BF16)

        _stage_weights(
            _col_tiles(wqkv_hbm, layer, 0, D_MODEL, functools.partial(store_t, 0))
            + _col_tiles(wqkv_hbm, layer, D_MODEL, B_KV_DIM, store_k)
            + _col_tiles(wqkv_hbm, layer, D_MODEL + B_KV_DIM, B_KV_DIM, functools.partial(store_t, D_MODEL)),
            stage_ref, sem_ref)

    x = _norm_mod(h_ref[...], g_ref[...], _mod_part(mod_ref, 0), _mod_part(mod_ref, 1)).astype(_BF16)
    nt = (((1,), (1,)), ((), ()))
    yt = lax.dot_general(wqvt_ref[...], x, nt, preferred_element_type=_F32)
    cost, sint = cost_ref[...], sint_ref[...]
    for hd in range(B_Q_HEADS):
        rows = slice(hd * B_HEAD_DIM, (hd + 1) * B_HEAD_DIM)
        qt = yt[rows, :]
        qt = (qt * cost + _swap_halves_rows(qt) * sint) * (B_HEAD_DIM ** -0.5 * LOG2E)
        qt_ref[rows, :] = qt.astype(_BF16)
    vt_ref[...] = yt[D_MODEL:, :].astype(_BF16)
    cos, sin = cos_ref[...], sin_ref[...]
    k = jnp.dot(x, wk_ref[...], preferred_element_type=_F32)
    for c0 in range(0, B_KV_DIM, LANES):
        kc = k[:, c0:c0 + LANES]
        k_ref[:, c0:c0 + LANES] = (kc * cos + _swap_halves_lanes(kc) * sin).astype(_BF16)


def _b_proj(h, g, mod_l, w_qkv, layer, rope, dims):
    rows = h.shape[0]
    tm = ROW_TILE
    cos, sin, cost, sint = rope
    return pl.pallas_call(
        functools.partial(_b_proj_kernel, layer=layer),
        out_shape=(
            jax.ShapeDtypeStruct((D_MODEL, rows), _BF16),
            jax.ShapeDtypeStruct((rows, B_KV_DIM), _BF16),
            jax.ShapeDtypeStruct((B_KV_DIM, rows), _BF16),
        ),
        grid=(rows // tm,),
        in_specs=[
            _row_spec(tm, D_MODEL),
            _resident((1, D_MODEL)),
            _mod_spec(dims.n_ctx_rows, dims.seq, tm),
            _weight_spec(),
            _row_spec(tm, LANES),
            _row_spec(tm, LANES),
            pl.BlockSpec((B_HEAD_DIM, tm), lambda i: (0, i)),
            pl.BlockSpec((B_HEAD_DIM, tm), lambda i: (0, i)),
        ],
        out_specs=(
            pl.BlockSpec((D_MODEL, tm), lambda i: (0, i)),
            _row_spec(tm, B_KV_DIM),
            pl.BlockSpec((B_KV_DIM, tm), lambda i: (0, i)),
        ),
        scratch_shapes=[pltpu.VMEM((D_MODEL + B_KV_DIM, D_MODEL), _BF16),
                        pltpu.VMEM((D_MODEL, B_KV_DIM), _BF16)] + _stage_scratch(),
        compiler_params=_cparams("arbitrary"),
        name="b_proj",
    )(h, g, mod_l, w_qkv, cos, sin, cost, sint)


B_ONES_ROWS = 16
B_STEP_BLOCKS = 2


def _b_attn_kernel(sink_ref, win_l_ref, win_r_ref, qt_ref, kl_ref, kc_ref, kr_ref, kx_ref,
                   vtl_ref, vtc_ref, vtr_ref, vtx_ref, o_ref, s_ref, *, n_ctx_blocks, blocks_per_seq, ctx_len):
    first = pl.program_id(0) * B_STEP_BLOCKS
    is_lat = first >= n_ctx_blocks
    n = (first - n_ctx_blocks) % blocks_per_seq
    neg = -jnp.inf
    win_l = jnp.where(is_lat, win_l_ref[...], neg)
    win_r = jnp.where(is_lat, win_r_ref[...], neg)
    bias_c = jnp.where(is_lat, 0.0, neg)
    bias_first_l = jnp.where(jnp.logical_and(is_lat, n >= 1), win_l_ref[...], neg)
    bias_last_r = jnp.where(jnp.logical_and(is_lat, n + B_STEP_BLOCKS <= blocks_per_seq - 1), win_r_ref[...], neg)

    n_q = B_GROUP * B_BLOCK
    head_of_lane = lax.broadcasted_iota(jnp.int32, (1, n_q), 1) // B_BLOCK
    ones_rows = jnp.ones((B_ONES_ROWS, B_BLOCK), _BF16)
    n_ctx_tiles = ctx_len // B_BLOCK

    def local_tiles(blk):
        own = slice(blk * B_BLOCK, (blk + 1) * B_BLOCK)
        if blk == 0:
            left = (kl_ref, vtl_ref, slice(0, B_BLOCK), bias_first_l)
        else:
            left = (kc_ref, vtc_ref, slice((blk - 1) * B_BLOCK, blk * B_BLOCK), win_l)
        if blk == B_STEP_BLOCKS - 1:
            right = (kr_ref, vtr_ref, slice(0, B_BLOCK), bias_last_r)
        else:
            right = (kc_ref, vtc_ref, slice((blk + 1) * B_BLOCK, (blk + 2) * B_BLOCK), win_r)
        return [left, (kc_ref, vtc_ref, own, bias_c), right]

    def scores(blk, g):
        cols = slice(blk * B_BLOCK, (blk + 1) * B_BLOCK)
        qt = jnp.concatenate(
            [qt_ref[(g * B_GROUP + j) * B_HEAD_DIM:(g * B_GROUP + j + 1) * B_HEAD_DIM, cols]
             for j in range(B_GROUP)], axis=1)
        sink = jnp.zeros((1, n_q), _F32)
        for j in range(B_GROUP):
            sink = jnp.where(head_of_lane == j, sink_ref[g * B_GROUP + j] * LOG2E, sink)
        ks = slice(g * B_HEAD_DIM, (g + 1) * B_HEAD_DIM)
        k_tiles = [(k_ref[rows, ks], bias) for k_ref, _, rows, bias in local_tiles(blk)]
        for t in range(n_ctx_tiles):
            k_tiles.append((kx_ref[t * B_BLOCK:(t + 1) * B_BLOCK, ks], None))
        m_tile = None
        for t, (k, bias) in enumerate(k_tiles):
            s = jnp.dot(k, qt, preferred_element_type=_F32)
            if bias is not None:
                s = s + bias
            s_ref[blk, g, t] = s
            m_tile = s if m_tile is None else jnp.maximum(m_tile, s)
        return jnp.maximum(sink, jnp.max(m_tile, axis=0, keepdims=True)), sink

    def attend(blk, g, m, sink):
        ks = slice(g * B_HEAD_DIM, (g + 1) * B_HEAD_DIM)
        vt_tiles = [vt_ref[ks, cols] for _, vt_ref, cols, _ in local_tiles(blk)]
        for t in range(n_ctx_tiles):
            vt_tiles.append(vtx_ref[ks, t * B_BLOCK:(t + 1) * B_BLOCK])
        acc = jnp.zeros((B_HEAD_DIM + B_ONES_ROWS, n_q), _F32)
        for t, vt in enumerate(vt_tiles):
            p = jnp.exp2(s_ref[blk, g, t] - m).astype(_BF16)
            vt_ext = jnp.concatenate([vt, ones_rows], axis=0)
            acc = acc + jnp.dot(vt_ext, p, preferred_element_type=_F32)
        denom = jnp.exp2(sink - m) + acc[B_HEAD_DIM:B_HEAD_DIM + 1, :]
        out_t = acc[:B_HEAD_DIM, :] * (1.0 / denom)
        rows = slice(blk * B_BLOCK, (blk + 1) * B_BLOCK)
        for pair in range(B_GROUP // 2):
            two = jnp.concatenate([out_t[:, (2 * pair) * B_BLOCK:(2 * pair + 1) * B_BLOCK],
                                   out_t[:, (2 * pair + 1) * B_BLOCK:(2 * pair + 2) * B_BLOCK]], axis=0)
            c0 = (g * B_GROUP + 2 * pair) * B_HEAD_DIM
            o_ref[rows, c0:c0 + 2 * B_HEAD_DIM] = two.T.astype(_BF16)

    units = [(blk, g) for blk in range(B_STEP_BLOCKS) for g in range(B_KV_HEADS)]
    pending = [scores(*units[0]), scores(*units[1])]
    for i, unit in enumerate(units):
        if i + 2 < len(units):
            pending.append(scores(*units[i + 2]))
        attend(*unit, *pending[i])


def _window_bias():
    key = np.arange(B_BLOCK)[:, None]
    qry = np.arange(B_BLOCK)[None, :]
    prev_blk = np.where(key >= qry, 0.0, -np.inf).astype(np.float32)
    next_blk = np.where(key <= qry, 0.0, -np.inf).astype(np.float32)
    return jnp.asarray(np.tile(prev_blk, (1, B_GROUP))), jnp.asarray(np.tile(next_blk, (1, B_GROUP)))


def _b_attn(sinks, qt, k, vt, dims):
    rows = k.shape[0]
    step_rows = B_STEP_BLOCKS * B_BLOCK
    ncc = dims.ctx_len // B_BLOCK
    ncl = dims.seq // B_BLOCK
    lat0 = dims.batch * ncc
    nblk = rows // B_BLOCK
    assert ncc % B_STEP_BLOCKS == 0 and ncl % B_STEP_BLOCKS == 0

    def batch_of(s):
        i = s * B_STEP_BLOCKS
        return jnp.where(i < lat0, i // ncc, (i - lat0) // ncl)

    def left(s):
        return jnp.maximum(s * B_STEP_BLOCKS - 1, 0)

    def right(s):
        return jnp.minimum((s + 1) * B_STEP_BLOCKS, nblk - 1)

    n_tiles = 3 + ncc
    return pl.pallas_call(
        functools.partial(_b_attn_kernel, n_ctx_blocks=lat0, blocks_per_seq=ncl, ctx_len=dims.ctx_len),
        out_shape=jax.ShapeDtypeStruct((rows, D_MODEL), _BF16),
        grid=(nblk // B_STEP_BLOCKS,),
        in_specs=[
            pl.BlockSpec(memory_space=pltpu.SMEM),
            _resident((B_BLOCK, B_GROUP * B_BLOCK)),
            _resident((B_BLOCK, B_GROUP * B_BLOCK)),
            pl.BlockSpec((D_MODEL, step_rows), lambda s: (0, s)),
            pl.BlockSpec((B_BLOCK, B_KV_DIM), lambda s: (left(s), 0)),
            pl.BlockSpec((step_rows, B_KV_DIM), lambda s: (s, 0)),
            pl.BlockSpec((B_BLOCK, B_KV_DIM), lambda s: (right(s), 0)),
            pl.BlockSpec((dims.ctx_len, B_KV_DIM), lambda s: (batch_of(s), 0)),
            pl.BlockSpec((B_KV_DIM, B_BLOCK), lambda s: (0, left(s))),
            pl.BlockSpec((B_KV_DIM, step_rows), lambda s: (0, s)),
            pl.BlockSpec((B_KV_DIM, B_BLOCK), lambda s: (0, right(s))),
            pl.BlockSpec((B_KV_DIM, dims.ctx_len), lambda s: (0, batch_of(s))),
        ],
        out_specs=pl.BlockSpec((step_rows, D_MODEL), lambda s: (s, 0)),
        scratch_shapes=[pltpu.VMEM((B_STEP_BLOCKS, B_KV_HEADS, n_tiles, B_BLOCK, B_GROUP * B_BLOCK), _F32)],
        compiler_params=_cparams("arbitrary"),
        name="b_attn",
    )(sinks, *_window_bias(), qt, k, k, k, k, vt, vt, vt, vt)


def _c_proj_kernel(h_ref, g_ref, mod_ref, win_hbm, bg_ref, u_ref, w_ref, stage_ref, sem_ref, *, layer):
    @pl.when(pl.program_id(0) == 0)
    def _():
        def store(c, width, tile):
            w_ref[:, c:c + width] = tile.astype(_BF16)

        _stage_weights(_col_tiles(win_hbm, layer, 0, 3 * D_MODEL, store), stage_ref, sem_ref)

    x = _norm_mod(h_ref[...], g_ref[...], _mod_part(mod_ref, 0), _mod_part(mod_ref, 1)).astype(_BF16)
    y = jnp.dot(x, w_ref[...], preferred_element_type=_F32)
    bg_ref[...] = y[:, :D_MODEL]
    u_ref[...] = y[:, D_MODEL:2 * D_MODEL] * y[:, 2 * D_MODEL:]


def _c_proj(h, g, mod_l, w_in, layer, dims):
    rows = h.shape[0]
    tm = ROW_TILE
    return pl.pallas_call(
        functools.partial(_c_proj_kernel, layer=layer),
        out_shape=(jax.ShapeDtypeStruct((rows, D_MODEL), _F32),) * 2,
        grid=(rows // tm,),
        in_specs=[
            _row_spec(tm, D_MODEL),
            _resident((1, D_MODEL)),
            _mod_spec(dims.n_ctx_rows, dims.seq, tm),
            _weight_spec(),
        ],
        out_specs=(_row_spec(tm, D_MODEL),) * 2,
        scratch_shapes=[pltpu.VMEM((D_MODEL, 3 * D_MODEL), _BF16)] + _stage_scratch(),
        compiler_params=_cparams("arbitrary"),
        name="c_proj",
    )(h, g, mod_l, w_in)


def _c_mix(bg_ref, u_ref, up_ref, un_ref, cw_ref, cb_ref, tile, dims):
    tm = u_ref.shape[0]
    u = u_ref[...]
    row = lax.broadcasted_iota(jnp.int32, (tm, 1), 0)
    g_row = tile * tm + row
    in_ctx = g_row < dims.n_ctx_rows
    pos = jnp.where(in_ctx, g_row % dims.ctx_len, (g_row - dims.n_ctx_rows) % dims.seq)
    length = jnp.where(in_ctx, dims.ctx_len, dims.seq)
    prev = jnp.where(row == 0, up_ref[SUBLANES - 1:SUBLANES, :], pltpu.roll(u, 1, axis=0))
    nxt = jnp.where(row == tm - 1, un_ref[0:1, :], pltpu.roll(u, tm - 1, axis=0))
    prev = jnp.where(pos == 0, 0.0, prev)
    nxt = jnp.where(pos == length - 1, 0.0, nxt)
    conv = prev * cw_ref[0:1, :] + u * cw_ref[1:2, :] + nxt * cw_ref[2:3, :] + cb_ref[...]
    return (bg_ref[...] * conv).astype(_BF16)


_N_MIX_REFS = (4, 1, 6)


def _post_kernel(*refs, kind, n_h, tile0, final_norm, dims, out_layer, layer):
    n_mix = _N_MIX_REFS[kind]
    h_refs, mod_ref = refs[:n_h], refs[n_h]
    mix_refs = refs[n_h + 1:n_h + 1 + n_mix]
    (wo_hbm, g_ref, w1_hbm, w2_hbm, fg_ref, out_ref,
     wo_ref, w1_ref, w2_ref, stage_ref, sem_ref) = refs[n_h + 1 + n_mix:]
    n_chunks = MLP_HIDDEN // MLP_HIDDEN_CHUNK
    tiles_wo = D_MODEL // STAGE_COLS
    tiles_w1 = MLP_HIDDEN_CHUNK // STAGE_COLS
    tiles_w2 = D_MODEL // STAGE_COLS

    def compute(need):
        if kind == 0:
            y = _a_mix(*mix_refs)
        elif kind == 1:
            y = mix_refs[0][...]
        else:
            y = _c_mix(*mix_refs, pl.program_id(0) + tile0, dims)
        h = _load_h(h_refs)
        need(tiles_wo)
        h = h + _mod_part(mod_ref, 2) * jnp.dot(y, wo_ref[...], preferred_element_type=_F32)
        x = _norm_mod(h, g_ref[...], _mod_part(mod_ref, 3), _mod_part(mod_ref, 4)).astype(_BF16)
        acc = jnp.zeros(h.shape, _F32)
        for c in range(n_chunks):
            cols = slice(c * MLP_HIDDEN_CHUNK, (c + 1) * MLP_HIDDEN_CHUNK)
            need(tiles_wo + c * (tiles_w1 + tiles_w2) + tiles_w1)
            u = jnp.dot(x, w1_ref[:, cols], preferred_element_type=_F32)
            u = jnp.square(jnp.maximum(u, 0.0)).astype(_BF16)
            need(tiles_wo + (c + 1) * (tiles_w1 + tiles_w2))
            acc = acc + jnp.dot(u, w2_ref[cols, :], preferred_element_type=_F32)
        out = h + _mod_part(mod_ref, 5) * acc
        if final_norm:
            ms = jnp.mean(out * out, axis=-1, keepdims=True)
            out = out * lax.rsqrt(ms + NORM_EPS) * fg_ref[...]
        out_ref[...] = out

    @pl.when(pl.program_id(0) == 0)
    def _():
        def store_to(dst_ref, row0, c, width, tile):
            dst_ref[row0:row0 + tile.shape[0], c:c + width] = tile.astype(_BF16)

        jobs = _col_tiles(wo_hbm, out_layer, 0, D_MODEL, functools.partial(store_to, wo_ref, 0))
        for r0 in range(0, MLP_HIDDEN, MLP_HIDDEN_CHUNK):
            jobs += _col_tiles(w1_hbm, layer, r0, MLP_HIDDEN_CHUNK,
                               lambda c, width, tile, r0=r0: store_to(w1_ref, 0, r0 + c, width, tile))
            for c in range(0, D_MODEL, STAGE_COLS):
                jobs.append((w2_hbm.at[layer, pl.ds(r0, MLP_HIDDEN_CHUNK), pl.ds(c, STAGE_COLS)],
                             functools.partial(store_to, w2_ref, r0, c, STAGE_COLS)))
        compute(_WeightStager(jobs, stage_ref, sem_ref).need)

    @pl.when(pl.program_id(0) != 0)
    def _():
        compute(lambda n: None)


def _post(kind, h_parts, mod_l, mix, w_out, out_layer, g, w1, w2, layer, final_g, dims, tile0=0,
          final_norm=False):
    tm = ROW_TILE
    n_tiles = dims.rows // tm - tile0
    row = functools.partial(_row_spec, tm, D_MODEL, tile0)
    if kind == 0:
        mix_specs = [row(), row(), row(), _resident((1, D_MODEL))]
    elif kind == 1:
        mix_specs = [row()]
    else:
        bg, u, conv_w, conv_b = mix
        per = tm // SUBLANES
        last = dims.rows // SUBLANES - 1
        mix = (bg, u, u, u, conv_w, conv_b)
        mix_specs = [
            row(), row(),
            pl.BlockSpec((SUBLANES, D_MODEL), lambda i: (jnp.maximum((i + tile0) * per - 1, 0), 0)),
            pl.BlockSpec((SUBLANES, D_MODEL), lambda i: (jnp.minimum((i + tile0 + 1) * per, last), 0)),
            _resident(conv_w.shape), _resident((1, D_MODEL)),
        ]
    assert len(h_parts) == 1 or tile0 == 0
    h_specs = _h_specs(tm, True) if len(h_parts) == 2 else [row()]
    return pl.pallas_call(
        functools.partial(_post_kernel, kind=kind, n_h=len(h_parts), tile0=tile0,
                          final_norm=final_norm, dims=dims, out_layer=out_layer, layer=layer),
        out_shape=jax.ShapeDtypeStruct((n_tiles * tm, D_MODEL), _F32),
        grid=(n_tiles,),
        in_specs=h_specs + [_mod_spec(dims.n_ctx_rows, dims.seq, tm, tile0)] + mix_specs + [
            _weight_spec(),
            _resident((1, D_MODEL)),
            _weight_spec(),
            _weight_spec(),
            _resident((1, D_MODEL)),
        ],
        out_specs=_row_spec(tm, D_MODEL),
        scratch_shapes=[pltpu.VMEM((D_MODEL, D_MODEL), _BF16), pltpu.VMEM((D_MODEL, MLP_HIDDEN), _BF16),
                        pltpu.VMEM((MLP_HIDDEN, D_MODEL), _BF16)] + _stage_scratch(),
        compiler_params=_cparams("arbitrary"),
        name="post",
    )(*h_parts, mod_l, *mix, w_out, g, w1, w2, final_g)


class _Dims:
    def __init__(self, batch, seq, ctx_len):
        self.batch = batch
        self.seq = seq
        self.ctx_len = ctx_len
        self.n_ctx_rows = batch * ctx_len
        self.rows = self.n_ctx_rows + batch * seq


def _rope_tables(dims):
    n_freq = B_HEAD_DIM // 4
    t = np.arange(dims.seq)
    inv_freq = ROPE_BASE ** (-np.arange(n_freq, dtype=np.float64) / n_freq)
    ang_row = (t // GRID_W)[:, None] * inv_freq
    ang_col = (t % GRID_W)[:, None] * inv_freq
    ang = np.concatenate([ang_row, ang_row, ang_col, ang_col], axis=1)
    sign = np.tile(np.concatenate([-np.ones(n_freq), np.ones(n_freq)]), 2)
    cos = np.tile(np.cos(ang), (dims.batch, 1))
    sin = np.tile(np.sin(ang) * sign, (dims.batch, 1))
    cos = np.concatenate([np.ones((dims.n_ctx_rows, B_HEAD_DIM)), cos], axis=0).astype(np.float32)
    sin = np.concatenate([np.zeros((dims.n_ctx_rows, B_HEAD_DIM)), sin], axis=0).astype(np.float32)
    tables = (np.tile(cos, (1, 2)), np.tile(sin, (1, 2)), np.ascontiguousarray(cos.T), np.ascontiguousarray(sin.T))
    return tuple(jnp.asarray(tab) for tab in tables)


def kernel(x, c, ctx, c_ctx, ada_w, ada_b, norm_g, final_g, mlp_w1, mlp_w2,
           a_w_in, a_w_gate, a_b_gate, a_head_g, a_w_out,
           b_w_qkv, b_sinks, b_w_out, c_w_in, c_conv_w, c_conv_b, c_w_out):
    batch, seq, d = x.shape
    ctx_len = ctx.shape[1]
    depth = ada_w.shape[0]
    dims = _Dims(batch, seq, ctx_len)
    assert MLP_HIDDEN_CHUNK == D_MODEL and d == D_MODEL and seq % ROW_TILE == 0 and dims.n_ctx_rows % ROW_TILE == 0
    assert ctx_len % (A_STEP_CHUNKS * A_CHUNK) == 0 and 1 + batch <= MOD_ROWS

    assert dims.n_ctx_rows == ROW_TILE
    h_parts = (x.reshape(-1, d), ctx.reshape(-1, d))
    cs = jnp.concatenate([c_ctx[None], c, jnp.zeros((MOD_ROWS - 1 - batch, d), _F32)], axis=0)
    mod = _ada_table(cs, ada_w, ada_b).reshape(depth, MOD_ROWS, 1, 6 * d)
    rope = _rope_tables(dims)
    fg = final_g.reshape(1, d)

    for l in range(depth):
        kind, j = l % N_MIXERS, l // N_MIXERS
        mod_l = mod[l]
        g0 = norm_g[l, 0].reshape(1, d)
        g1 = norm_g[l, 1].reshape(1, d)
        last_layer = l == depth - 1
        if kind == 0:
            wgt = a_w_gate[j].T.astype(_BF16)
            q, kt, v, o, gr, gc = _a_proj(h_parts, g0, mod_l, a_w_in, j, wgt, a_b_gate[j].reshape(-1, 1), dims)
            hf, hb = _a_scan(q, kt, v, gr, gc, dims)
            mix = (hf, hb, o, a_head_g[j].reshape(1, d))
            w_out = a_w_out
        elif kind == 1:
            (h,) = h_parts
            qt, k, vt = _b_proj(h, g0, mod_l, b_w_qkv, j, rope, dims)
            mix = (_b_attn(b_sinks[j], qt, k, vt, dims),)
            w_out = b_w_out
        else:
            (h,) = h_parts
            bg, u = _c_proj(h, g0, mod_l, c_w_in, j, dims)
            mix = (bg, u, c_conv_w[j], c_conv_b[j].reshape(1, d))
            w_out = c_w_out
        tile0 = dims.n_ctx_rows // ROW_TILE if last_layer else 0
        h_parts = (_post(kind, h_parts, mod_l, mix, w_out, j, g1, mlp_w1, mlp_w2, l, fg, dims,
                         tile0=tile0, final_norm=last_layer),)
    return h_parts[0].reshape(batch, seq, d)
```

```python
import functools

import jax
import jax.numpy as jnp
import numpy as np
from jax import lax
from jax.experimental import pallas as pl
from jax.experimental.pallas import tpu as pltpu

D_MODEL = 1024
GRID_W = 64
N_MIXERS = 3
A_HEADS = 8
A_QK_DIM = D_MODEL // 2
A_DK = A_QK_DIM // A_HEADS
A_DV = D_MODEL // A_HEADS
A_CHUNK = 128
A_STEP_CHUNKS = 2
B_Q_HEADS = 16
B_KV_HEADS = 4
B_HEAD_DIM = D_MODEL // B_Q_HEADS
B_GROUP = B_Q_HEADS // B_KV_HEADS
B_KV_DIM = B_KV_HEADS * B_HEAD_DIM
B_BLOCK = 128
ROPE_BASE = 10000.0
MLP_HIDDEN = 4 * D_MODEL
NORM_EPS = 1e-6
LOG2E = 1.4426950408889634

LANES = 128
SUBLANES = 8
VMEM_LIMIT_BYTES = 56 * 1024 * 1024

ROW_TILE = 512
MLP_HIDDEN_CHUNK = 1024
MOD_ROWS = 8
ADA_COL_TILE = 1536

_BF16 = jnp.bfloat16
_F32 = jnp.float32


def _cparams(*sem):
    return pltpu.CompilerParams(dimension_semantics=sem, vmem_limit_bytes=VMEM_LIMIT_BYTES)


def _resident(shape):
    nd = len(shape)
    return pl.BlockSpec(shape, lambda *_: (0,) * nd, pipeline_mode=pl.Buffered(1))


STAGE_COLS = 512
STAGE_DEPTH = 3


def _weight_spec():
    return pl.BlockSpec(memory_space=pl.ANY)


def _stage_scratch():
    return [pltpu.VMEM((STAGE_DEPTH, D_MODEL, STAGE_COLS), _F32), pltpu.SemaphoreType.DMA((STAGE_DEPTH,))]


class _WeightStager:
    def __init__(self, jobs, stage_ref, sem_ref):
        self.jobs, self.stage_ref, self.sem_ref = jobs, stage_ref, sem_ref
        self.started = self.done = 0

    def _copy(self, i):
        src = self.jobs[i][0]
        dst = self.stage_ref.at[i % STAGE_DEPTH, pl.ds(0, src.shape[0]), pl.ds(0, src.shape[1])]
        return pltpu.make_async_copy(src, dst, self.sem_ref.at[i % STAGE_DEPTH])

    def _fill(self):
        while self.started < min(self.done + STAGE_DEPTH, len(self.jobs)):
            self._copy(self.started).start()
            self.started += 1

    def need(self, n):
        self._fill()
        while self.done < n:
            src, store = self.jobs[self.done]
            self._copy(self.done).wait()
            store(self.stage_ref[self.done % STAGE_DEPTH, :src.shape[0], :src.shape[1]])
            self.done += 1
            self._fill()


def _stage_weights(jobs, stage_ref, sem_ref):
    _WeightStager(jobs, stage_ref, sem_ref).need(len(jobs))


def _col_tiles(w_hbm, layer, col0, ncols, store):
    jobs = []
    for c in range(0, ncols, STAGE_COLS):
        width = min(STAGE_COLS, ncols - c)
        jobs.append((w_hbm.at[layer, :, pl.ds(col0 + c, width)], functools.partial(store, c, width)))
    return jobs


def _norm_mod(h, g, shift, scale):
    ms = jnp.mean(h * h, axis=-1, keepdims=True)
    y = h * lax.rsqrt(ms + NORM_EPS) * g
    return y * (1.0 + scale) + shift


def _mod_part(mod_ref, k):
    return mod_ref[:, k * D_MODEL:(k + 1) * D_MODEL]


def _mod_spec(n_ctx_rows, seq, tm, tile0=0):
    def idx(i):
        r0 = (i + tile0) * tm
        grp = jnp.where(r0 < n_ctx_rows, 0, 1 + (r0 - n_ctx_rows) // seq)
        return (grp, 0, 0)
    return pl.BlockSpec((None, 1, 6 * D_MODEL), idx)


def _row_spec(tm, width, tile0=0):
    return pl.BlockSpec((tm, width), lambda i: (i + tile0, 0))


def _h_specs(tm, split):
    if not split:
        return [_row_spec(tm, D_MODEL)]
    return [pl.BlockSpec((tm, D_MODEL), lambda i: (jnp.maximum(i - 1, 0), 0)),
            pl.BlockSpec((tm, D_MODEL), lambda i: (0, 0))]


def _load_h(h_refs):
    if len(h_refs) == 1:
        return h_refs[0][...]
    lat_ref, ctx_ref = h_refs
    return jnp.where(pl.program_id(0) == 0, ctx_ref[...], lat_ref[...])


def _ada_kernel(cs_ref, w_ref, b_ref, o_ref):
    cs = cs_ref[...]
    s = (cs * jax.nn.sigmoid(cs)).astype(_BF16)
    o_ref[...] = jnp.dot(s, w_ref[...].astype(_BF16), preferred_element_type=_F32) + b_ref[...]


def _ada_table(cs, ada_w, ada_b, layer):
    n = ada_w.shape[2]
    return pl.pallas_call(
        _ada_kernel,
        out_shape=jax.ShapeDtypeStruct((MOD_ROWS, n), _F32),
        grid=(n // ADA_COL_TILE,),
        in_specs=[
            pl.BlockSpec((MOD_ROWS, D_MODEL), lambda j: (0, 0)),
            pl.BlockSpec((None, D_MODEL, ADA_COL_TILE), lambda j: (layer, 0, j)),
            pl.BlockSpec((None, 1, ADA_COL_TILE), lambda j: (layer, 0, j)),
        ],
        out_specs=pl.BlockSpec((MOD_ROWS, ADA_COL_TILE), lambda j: (0, j)),
        compiler_params=_cparams("arbitrary"),
        name="ada_table",
    )(cs, ada_w, ada_b)


class _AdaSide:
    def __init__(self, cs, ada_w, ada_b, layer, step_of, n_steps):
        n = ada_w.shape[2]
        self.n_tiles = max(t for t in range(1, n // LANES + 1) if n % (t * LANES) == 0 and t <= n_steps)
        cols = n // self.n_tiles
        self.operands = (cs, ada_w, ada_b)

        def tile(*idx):
            return jnp.minimum(step_of(*idx), self.n_tiles - 1)

        self.in_specs = [
            pl.BlockSpec((MOD_ROWS, D_MODEL), lambda *idx: (0, 0)),
            pl.BlockSpec((None, D_MODEL, cols), lambda *idx: (layer, 0, tile(*idx))),
            pl.BlockSpec((None, 1, cols), lambda *idx: (layer, 0, tile(*idx))),
        ]
        self.out_spec = pl.BlockSpec((MOD_ROWS, cols), lambda *idx: (0, tile(*idx)))
        self.out_shape = jax.ShapeDtypeStruct((MOD_ROWS, n), _F32)


def _ada_side_step(step, n_tiles, cs_ref, w_ref, b_ref, o_ref):
    @pl.when(step < n_tiles)
    def _():
        _ada_kernel(cs_ref, w_ref, b_ref, o_ref)


def _a_proj_kernel(*refs, n_h, layer):
    h_refs, (g_ref, mod_ref, win_hbm, wgt_ref, bg_ref,
             q_ref, kt_ref, v_ref, o_ref, gr_ref, gc_ref,
             wqvo_ref, wkt_ref, stage_ref, sem_ref) = refs[:n_h], refs[n_h:]

    @pl.when(pl.program_id(0) == 0)
    def _():
        def store_qvo(dst0, c, width, tile):
            wqvo_ref[:, dst0 + c:dst0 + c + width] = tile.astype(_BF16)

        def store_kt(c, width, tile):
            wkt_ref[c:c + width, :] = tile.T.astype(_BF16)

        _stage_weights(
            _col_tiles(win_hbm, layer, 0, A_QK_DIM, functools.partial(store_qvo, 0))
            + _col_tiles(win_hbm, layer, A_QK_DIM, A_QK_DIM, store_kt)
            + _col_tiles(win_hbm, layer, 2 * A_QK_DIM, 2 * D_MODEL, functools.partial(store_qvo, A_QK_DIM)),
            stage_ref, sem_ref)

    x = _norm_mod(_load_h(h_refs), g_ref[...], _mod_part(mod_ref, 0), _mod_part(mod_ref, 1)).astype(_BF16)
    nt = (((1,), (1,)), ((), ()))
    gt = lax.dot_general(wgt_ref[...], x, nt, preferred_element_type=_F32) + bg_ref[...]

    n_chunks = gt.shape[1] // A_CHUNK

    def by_chunk(rows):
        return jnp.concatenate([rows[:, c * A_CHUNK:(c + 1) * A_CHUNK] for c in range(n_chunks)], axis=0)

    def by_token(x):
        return jnp.concatenate([x[c * A_HEADS:(c + 1) * A_HEADS] for c in range(n_chunks)], axis=1)

    def log_sigmoid(z):
        return jnp.minimum(z, 0.0) - jnp.log1p(jnp.exp(-jnp.abs(z)))

    row_form, col_cm, col_b = [], [], []
    for d in range(2):
        li = by_chunk(gt[2 * d * A_HEADS:(2 * d + 1) * A_HEADS]) * LOG2E
        lf = log_sigmoid(by_chunk(gt[(2 * d + 1) * A_HEADS:(2 * d + 2) * A_HEADS])) * LOG2E
        b = _lane_scan(lf, jnp.add, 0.0, d == 1)
        a = li - b
        cm = _lane_scan(a, jnp.maximum, -jnp.inf, d == 1)
        row_form += [by_token(a), by_token(b)]
        col_cm.append(cm)
        col_b.append(b)
    gr_ref[...] = jnp.concatenate(row_form, axis=0)
    pad = jnp.zeros((LANES - 2 * A_HEADS, LANES), _F32)
    for c in range(n_chunks):
        rows = slice(c * A_HEADS, (c + 1) * A_HEADS)
        toks = slice(c * A_CHUNK, (c + 1) * A_CHUNK)
        gc_ref[toks, :LANES] = jnp.concatenate([col_cm[0][rows], col_cm[1][rows], pad], axis=0).T
        gc_ref[toks, LANES:] = jnp.concatenate([col_b[0][rows], col_b[1][rows], pad], axis=0).T

    y = jnp.dot(x, wqvo_ref[...], preferred_element_type=_F32)
    q_ref[...] = (y[:, :A_QK_DIM] * (A_DK ** -0.5)).astype(_BF16)
    v_ref[...] = y[:, A_QK_DIM:A_QK_DIM + D_MODEL].astype(_BF16)
    o_ref[...] = y[:, A_QK_DIM + D_MODEL:]
    kt_ref[...] = lax.dot_general(wkt_ref[...], x, nt, preferred_element_type=_F32).astype(_BF16)


def _a_proj(h_parts, g, mod_l, w_in, layer, wgt, bgate, dims):
    rows = dims.rows
    tm = ROW_TILE
    n_gate = 4 * A_HEADS
    return pl.pallas_call(
        functools.partial(_a_proj_kernel, n_h=len(h_parts), layer=layer),
        out_shape=(
            jax.ShapeDtypeStruct((rows, A_QK_DIM), _BF16),
            jax.ShapeDtypeStruct((A_QK_DIM, rows), _BF16),
            jax.ShapeDtypeStruct((rows, D_MODEL), _BF16),
            jax.ShapeDtypeStruct((rows, D_MODEL), _F32),
            jax.ShapeDtypeStruct((n_gate, rows), _F32),
            jax.ShapeDtypeStruct((rows, 2 * LANES), _F32),
        ),
        grid=(rows // tm,),
        in_specs=_h_specs(tm, len(h_parts) == 2) + [
            _resident((1, D_MODEL)),
            _mod_spec(dims.n_ctx_rows, dims.seq, tm),
            _weight_spec(),
            _resident(wgt.shape),
            _resident((n_gate, 1)),
        ],
        out_specs=(
            _row_spec(tm, A_QK_DIM),
            pl.BlockSpec((A_QK_DIM, tm), lambda i: (0, i)),
            _row_spec(tm, D_MODEL),
            _row_spec(tm, D_MODEL),
            pl.BlockSpec((n_gate, tm), lambda i: (0, i)),
            _row_spec(tm, 2 * LANES),
        ),
        scratch_shapes=[pltpu.VMEM((D_MODEL, A_QK_DIM + 2 * D_MODEL), _BF16),
                        pltpu.VMEM((A_QK_DIM, D_MODEL), _BF16)] + _stage_scratch(),
        compiler_params=_cparams("arbitrary"),
        name="a_proj",
    )(*h_parts, g, mod_l, w_in, wgt, bgate)


def _lane_scan(x, op, fill, reverse):
    lane = lax.broadcasted_iota(jnp.int32, x.shape, 1)
    k = 1
    while k < LANES:
        if reverse:
            shifted = jnp.where(lane < LANES - k, pltpu.roll(x, LANES - k, axis=1), fill)
        else:
            shifted = jnp.where(lane >= k, pltpu.roll(x, k, axis=1), fill)
        x = op(x, shifted)
        k *= 2
    return x


def _a_scan_kernel(*refs, side_tiles):
    if side_tiles:
        side_refs, refs = refs[10:13] + refs[15:16], refs[:10] + refs[13:15] + refs[16:]
        _ada_side_step(pl.program_id(0) * pl.num_programs(1) + pl.program_id(1), side_tiles, *side_refs)
    (qf_ref, ktf_ref, vf_ref, grf_ref, gcf_ref, qb_ref, ktb_ref, vb_ref, grb_ref, gcb_ref,
     hf_ref, hb_ref, cf_ref, cb_ref, mrf_ref, mrb_ref, mc_ref) = refs
    @pl.when(pl.program_id(1) == 0)
    def _():
        cf_ref[...] = jnp.zeros(cf_ref.shape, _F32)
        cb_ref[...] = jnp.zeros(cb_ref.shape, _F32)
        mrf_ref[...] = jnp.full(mrf_ref.shape, -jnp.inf, _F32)
        mrb_ref[...] = jnp.full(mrb_ref.shape, -jnp.inf, _F32)
        mc_ref[...] = jnp.full(mc_ref.shape, -jnp.inf, _F32)

    t_idx = lax.broadcasted_iota(jnp.int32, (A_CHUNK, A_CHUNK), 0)
    s_idx = lax.broadcasted_iota(jnp.int32, (A_CHUNK, A_CHUNK), 1)
    ones_blk = jnp.ones((A_CHUNK, A_DV), _BF16)
    lane = lax.broadcasted_iota(jnp.int32, (1, LANES), 1)
    for sub in range(A_STEP_CHUNKS):
        toks = (slice(sub * A_CHUNK, (sub + 1) * A_CHUNK),
                slice((A_STEP_CHUNKS - 1 - sub) * A_CHUNK, (A_STEP_CHUNKS - sub) * A_CHUNK))
        _a_chunk_pair(toks, (qf_ref, qb_ref), (ktf_ref, ktb_ref), (vf_ref, vb_ref), (grf_ref, grb_ref),
                      (gcf_ref, gcb_ref), (hf_ref, hb_ref), (cf_ref, cb_ref), (mrf_ref, mrb_ref), mc_ref,
                      t_idx, s_idx, ones_blk, lane)


def _a_chunk_pair(toks, q_refs, kt_refs, v_refs, gr_refs, gc_refs, h_refs, c_refs, mr_refs, mc_ref,
                  t_idx, s_idx, ones_blk, lane):
    m_col = mc_ref[0:1, :]
    m_col_new = []
    dirs = []
    for d in range(2):
        gr_ref, gc_ref, mr_ref, tok = gr_refs[d], gc_refs[d], mr_refs[d], toks[d]
        last = A_CHUNK - 1 if d == 0 else 0
        a = gr_ref[2 * d * A_HEADS:(2 * d + 1) * A_HEADS, tok]
        b = gr_ref[(2 * d + 1) * A_HEADS:(2 * d + 2) * A_HEADS, tok]
        m_row = mr_ref[:, 0:1]
        gg_end = jnp.maximum(m_row, jnp.max(a, axis=1, keepdims=True))
        ws = jnp.exp2(a - gg_end)
        decay = jnp.exp2(m_row - gg_end)
        mr_ref[...] = jnp.broadcast_to(b[:, last:last + 1] + gg_end, mr_ref.shape)
        gg_c = jnp.maximum(m_col, gc_ref[tok, :LANES])
        b_plus_gg = gc_ref[tok, LANES:] + gg_c
        clamp_c = jnp.exp2(-b_plus_gg)
        m_col_new.append(b_plus_gg[last:last + 1, :])
        dirs.append((a, m_row, ws, decay, gg_c, clamp_c))
    mc_ref[...] = jnp.broadcast_to(jnp.where(lane < A_HEADS, m_col_new[0], m_col_new[1]), mc_ref.shape)

    for hd in range(A_HEADS):
        for d in range(2):
            q_ref, kt_ref, v_ref, h_ref, c_ref, tok = q_refs[d], kt_refs[d], v_refs[d], h_refs[d], c_refs[d], toks[d]
            a, m_row, ws, decay, gg_c, clamp_c = dirs[d]
            mask = (s_idx <= t_idx) if d == 0 else (s_idx >= t_idx)
            col = d * A_HEADS + hd
            gg = jnp.broadcast_to(gg_c[:, col:col + 1], (A_CHUNK, A_CHUNK))
            clamp = jnp.broadcast_to(clamp_c[:, col:col + 1], (A_CHUNK, A_DV))
            q = q_ref[tok, hd * A_DK:(hd + 1) * A_DK]
            kt = kt_ref[hd * A_DK:(hd + 1) * A_DK, tok]
            v = v_ref[tok, hd * A_DV:(hd + 1) * A_DV]
            v_ext = jnp.concatenate([v, ones_blk], axis=1)
            c_old = c_ref[hd]
            s = jnp.dot(q, kt, preferred_element_type=_F32)
            e_keys = jnp.exp2(jnp.where(mask, a[hd:hd + 1, :] - gg, -jnp.inf))
            e_state = jnp.exp2(m_row[hd:hd + 1, :] - gg[:, :A_DK])
            p = jnp.concatenate([s * e_keys, q.astype(_F32) * e_state], axis=1).astype(_BF16)
            rhs = jnp.concatenate([v_ext, c_old.astype(_BF16)], axis=0)
            num = jnp.dot(p, rhs, preferred_element_type=_F32)
            den = jnp.maximum(jnp.abs(num[:, A_DV:]), clamp)
            h_ref[tok, hd * A_DV:(hd + 1) * A_DV] = num[:, :A_DV] / den
            kw = (kt.astype(_F32) * ws[hd:hd + 1, :]).astype(_BF16)
            c_ref[hd] = decay[hd:hd + 1, :] * c_old + jnp.dot(kw, v_ext, preferred_element_type=_F32)


def _a_scan(q, kt, v, gr, gc, dims, ada_next=None):
    rows = q.shape[0]
    blk_rows = A_STEP_CHUNKS * A_CHUNK
    ncc = dims.ctx_len // blk_rows
    ncl = dims.seq // blk_rows
    lat0 = dims.batch * ncc

    def fwd_blk(b, j):
        return jnp.where(j < ncc, b * ncc + j, lat0 + b * ncl + (j - ncc))

    def bwd_blk(b, j):
        return jnp.where(j < ncc, b * ncc + (ncc - 1 - j), lat0 + b * ncl + (ncl - 1 - (j - ncc)))

    def specs(blk):
        return [
            pl.BlockSpec((blk_rows, A_QK_DIM), lambda b, j: (blk(b, j), 0)),
            pl.BlockSpec((A_QK_DIM, blk_rows), lambda b, j: (0, blk(b, j))),
            pl.BlockSpec((blk_rows, D_MODEL), lambda b, j: (blk(b, j), 0)),
            pl.BlockSpec((4 * A_HEADS, blk_rows), lambda b, j: (0, blk(b, j))),
            pl.BlockSpec((blk_rows, 2 * LANES), lambda b, j: (blk(b, j), 0)),
        ]

    state = pltpu.VMEM((A_HEADS, A_DK, 2 * A_DV), _F32)
    stab = pltpu.VMEM((A_HEADS, LANES), _F32)
    n_steps = ncc + ncl
    side = _AdaSide(*ada_next, lambda b, j: b * n_steps + j, dims.batch * n_steps) if ada_next else None
    return pl.pallas_call(
        functools.partial(_a_scan_kernel, side_tiles=side.n_tiles if side else 0),
        out_shape=(jax.ShapeDtypeStruct((rows, D_MODEL), _F32),) * 2 + ((side.out_shape,) if side else ()),
        grid=(dims.batch, n_steps),
        in_specs=specs(fwd_blk) + specs(bwd_blk) + (side.in_specs if side else []),
        out_specs=(
            pl.BlockSpec((blk_rows, D_MODEL), lambda b, j: (fwd_blk(b, j), 0)),
            pl.BlockSpec((blk_rows, D_MODEL), lambda b, j: (bwd_blk(b, j), 0)),
        ) + ((side.out_spec,) if side else ()),
        scratch_shapes=[state, state, stab, stab, stab],
        compiler_params=_cparams("arbitrary", "arbitrary"),
        name="a_scan",
    )(q, kt, v, gr, gc, q, kt, v, gr, gc, *(side.operands if side else ()))


def _a_mix(hf_ref, hb_ref, o_ref, hg_ref):
    hs = hf_ref[...] + hb_ref[...]
    parts = []
    for hd in range(A_HEADS):
        x = hs[:, hd * A_DV:(hd + 1) * A_DV]
        parts.append(x * lax.rsqrt(jnp.mean(x * x, axis=-1, keepdims=True) + NORM_EPS))
    y = jnp.concatenate(parts, axis=1) * hg_ref[...]
    return (jax.nn.sigmoid(o_ref[...]) * y).astype(_BF16)


ROPE_HALF = B_HEAD_DIM // 4


def _swap_halves_lanes(x):
    lane = lax.broadcasted_iota(jnp.int32, x.shape, 1)
    fwd = pltpu.roll(x, LANES - ROPE_HALF, axis=1)
    back = pltpu.roll(x, ROPE_HALF, axis=1)
    return jnp.where(lane % (2 * ROPE_HALF) < ROPE_HALF, fwd, back)


def _swap_halves_rows(x):
    parts = []
    for r0 in range(0, x.shape[0], 2 * ROPE_HALF):
        parts += [x[r0 + ROPE_HALF:r0 + 2 * ROPE_HALF], x[r0:r0 + ROPE_HALF]]
    return jnp.concatenate(parts, axis=0)


def _b_proj_kernel(h_ref, g_ref, mod_ref, wqkv_hbm, cos_ref, sin_ref, cost_ref, sint_ref,
                   qt_ref, k_ref, vt_ref, wqvt_ref, wk_ref, stage_ref, sem_ref, *, layer):
    @pl.when(pl.program_id(0) == 0)
    def _():
        def store_t(dst0, c, width, tile):
            wqvt_ref[dst0 + c:dst0 + c + width, :] = tile.T.astype(_BF16)

        def store_k(c, width, tile):
            wk_ref[:, c:c + width] = tile.astype(_BF16)

        _stage_weights(
            _col_tiles(wqkv_hbm, layer, 0, D_MODEL, functools.partial(store_t, 0))
            + _col_tiles(wqkv_hbm, layer, D_MODEL, B_KV_DIM, store_k)
            + _col_tiles(wqkv_hbm, layer, D_MODEL + B_KV_DIM, B_KV_DIM, functools.partial(store_t, D_MODEL)),
            stage_ref, sem_ref)

    x = _norm_mod(h_ref[...], g_ref[...], _mod_part(mod_ref, 0), _mod_part(mod_ref, 1)).astype(_BF16)
    nt = (((1,), (1,)), ((), ()))
    yt = lax.dot_general(wqvt_ref[...], x, nt, preferred_element_type=_F32)
    cost, sint = cost_ref[...], sint_ref[...]
    for hd in range(B_Q_HEADS):
        rows = slice(hd * B_HEAD_DIM, (hd + 1) * B_HEAD_DIM)
        qt = yt[rows, :]
        qt = (qt * cost + _swap_halves_rows(qt) * sint) * (B_HEAD_DIM ** -0.5 * LOG2E)
        qt_ref[rows, :] = qt.astype(_BF16)
    vt_ref[...] = yt[D_MODEL:, :].astype(_BF16)
    cos, sin = cos_ref[...], sin_ref[...]
    k = jnp.dot(x, wk_ref[...], preferred_element_type=_F32)
    for c0 in range(0, B_KV_DIM, LANES):
        kc = k[:, c0:c0 + LANES]
        k_ref[:, c0:c0 + LANES] = (kc * cos + _swap_halves_lanes(kc) * sin).astype(_BF16)


def _b_proj(h, g, mod_l, w_qkv, layer, rope, dims):
    rows = h.shape[0]
    tm = ROW_TILE
    cos, sin, cost, sint = rope
    return pl.pallas_call(
        functools.partial(_b_proj_kernel, layer=layer),
        out_shape=(
            jax.ShapeDtypeStruct((D_MODEL, rows), _BF16),
            jax.ShapeDtypeStruct((rows, B_KV_DIM), _BF16),
            jax.ShapeDtypeStruct((B_KV_DIM, rows), _BF16),
        ),
        grid=(rows // tm,),
        in_specs=[
            _row_spec(tm, D_MODEL),
            _resident((1, D_MODEL)),
            _mod_spec(dims.n_ctx_rows, dims.seq, tm),
            _weight_spec(),
            _row_spec(tm, LANES),
            _row_spec(tm, LANES),
            pl.BlockSpec((B_HEAD_DIM, tm), lambda i: (0, i)),
            pl.BlockSpec((B_HEAD_DIM, tm), lambda i: (0, i)),
        ],
        out_specs=(
            pl.BlockSpec((D_MODEL, tm), lambda i: (0, i)),
            _row_spec(tm, B_KV_DIM),
            pl.BlockSpec((B_KV_DIM, tm), lambda i: (0, i)),
        ),
        scratch_shapes=[pltpu.VMEM((D_MODEL + B_KV_DIM, D_MODEL), _BF16),
                        pltpu.VMEM((D_MODEL, B_KV_DIM), _BF16)] + _stage_scratch(),
        compiler_params=_cparams("arbitrary"),
        name="b_proj",
    )(h, g, mod_l, w_qkv, cos, sin, cost, sint)


B_ONES_ROWS = 16
B_STEP_BLOCKS = 2


def _b_attn_kernel(*refs, n_ctx_blocks, blocks_per_seq, ctx_len, side_tiles):
    if side_tiles:
        side_refs, refs = refs[12:15] + refs[16:17], refs[:12] + refs[15:16] + refs[17:]
        _ada_side_step(pl.program_id(0), side_tiles, *side_refs)
    (sink_ref, win_l_ref, win_r_ref, qt_ref, kl_ref, kc_ref, kr_ref, kx_ref,
     vtl_ref, vtc_ref, vtr_ref, vtx_ref, o_ref, s_ref) = refs
    first = pl.program_id(0) * B_STEP_BLOCKS
    is_lat = first >= n_ctx_blocks
    n = (first - n_ctx_blocks) % blocks_per_seq
    neg = -jnp.inf
    win_l = jnp.where(is_lat, win_l_ref[...], neg)
    win_r = jnp.where(is_lat, win_r_ref[...], neg)
    bias_c = jnp.where(is_lat, 0.0, neg)
    bias_first_l = jnp.where(jnp.logical_and(is_lat, n >= 1), win_l_ref[...], neg)
    bias_last_r = jnp.where(jnp.logical_and(is_lat, n + B_STEP_BLOCKS <= blocks_per_seq - 1), win_r_ref[...], neg)

    n_q = B_GROUP * B_BLOCK
    head_of_lane = lax.broadcasted_iota(jnp.int32, (1, n_q), 1) // B_BLOCK
    ones_rows = jnp.ones((B_ONES_ROWS, B_BLOCK), _BF16)
    n_ctx_tiles = ctx_len // B_BLOCK

    def local_tiles(blk):
        own = slice(blk * B_BLOCK, (blk + 1) * B_BLOCK)
        if blk == 0:
            left = (kl_ref, vtl_ref, slice(0, B_BLOCK), bias_first_l)
        else:
            left = (kc_ref, vtc_ref, slice((blk - 1) * B_BLOCK, blk * B_BLOCK), win_l)
        if blk == B_STEP_BLOCKS - 1:
            right = (kr_ref, vtr_ref, slice(0, B_BLOCK), bias_last_r)
        else:
            right = (kc_ref, vtc_ref, slice((blk + 1) * B_BLOCK, (blk + 2) * B_BLOCK), win_r)
        return [left, (kc_ref, vtc_ref, own, bias_c), right]

    def scores(blk, g):
        cols = slice(blk * B_BLOCK, (blk + 1) * B_BLOCK)
        qt = jnp.concatenate(
            [qt_ref[(g * B_GROUP + j) * B_HEAD_DIM:(g * B_GROUP + j + 1) * B_HEAD_DIM, cols]
             for j in range(B_GROUP)], axis=1)
        sink = jnp.zeros((1, n_q), _F32)
        for j in range(B_GROUP):
            sink = jnp.where(head_of_lane == j, sink_ref[g * B_GROUP + j] * LOG2E, sink)
        ks = slice(g * B_HEAD_DIM, (g + 1) * B_HEAD_DIM)
        k_tiles = [(k_ref[rows, ks], bias) for k_ref, _, rows, bias in local_tiles(blk)]
        for t in range(n_ctx_tiles):
            k_tiles.append((kx_ref[t * B_BLOCK:(t + 1) * B_BLOCK, ks], None))
        m_tile = None
        for t, (k, bias) in enumerate(k_tiles):
            s = jnp.dot(k, qt, preferred_element_type=_F32)
            if bias is not None:
                s = s + bias
            s_ref[blk, g, t] = s
            m_tile = s if m_tile is None else jnp.maximum(m_tile, s)
        return jnp.maximum(sink, jnp.max(m_tile, axis=0, keepdims=True)), sink

    def attend(blk, g, m, sink):
        ks = slice(g * B_HEAD_DIM, (g + 1) * B_HEAD_DIM)
        vt_tiles = [vt_ref[ks, cols] for _, vt_ref, cols, _ in local_tiles(blk)]
        for t in range(n_ctx_tiles):
            vt_tiles.append(vtx_ref[ks, t * B_BLOCK:(t + 1) * B_BLOCK])
        acc = jnp.zeros((B_HEAD_DIM + B_ONES_ROWS, n_q), _F32)
        for t, vt in enumerate(vt_tiles):
            p = jnp.exp2(s_ref[blk, g, t] - m).astype(_BF16)
            vt_ext = jnp.concatenate([vt, ones_rows], axis=0)
            acc = acc + jnp.dot(vt_ext, p, preferred_element_type=_F32)
        denom = jnp.exp2(sink - m) + acc[B_HEAD_DIM:B_HEAD_DIM + 1, :]
        out_t = acc[:B_HEAD_DIM, :] * (1.0 / denom)
        rows = slice(blk * B_BLOCK, (blk + 1) * B_BLOCK)
        for pair in range(B_GROUP // 2):
            two = jnp.concatenate([out_t[:, (2 * pair) * B_BLOCK:(2 * pair + 1) * B_BLOCK],
                                   out_t[:, (2 * pair + 1) * B_BLOCK:(2 * pair + 2) * B_BLOCK]], axis=0)
            c0 = (g * B_GROUP + 2 * pair) * B_HEAD_DIM
            o_ref[rows, c0:c0 + 2 * B_HEAD_DIM] = two.T.astype(_BF16)

    units = [(blk, g) for blk in range(B_STEP_BLOCKS) for g in range(B_KV_HEADS)]
    pending = [scores(*units[0]), scores(*units[1])]
    for i, unit in enumerate(units):
        if i + 2 < len(units):
            pending.append(scores(*units[i + 2]))
        attend(*unit, *pending[i])


def _window_bias():
    key = np.arange(B_BLOCK)[:, None]
    qry = np.arange(B_BLOCK)[None, :]
    prev_blk = np.where(key >= qry, 0.0, -np.inf).astype(np.float32)
    next_blk = np.where(key <= qry, 0.0, -np.inf).astype(np.float32)
    return jnp.asarray(np.tile(prev_blk, (1, B_GROUP))), jnp.asarray(np.tile(next_blk, (1, B_GROUP)))


def _b_attn(sinks, qt, k, vt, dims, ada_next=None):
    rows = k.shape[0]
    step_rows = B_STEP_BLOCKS * B_BLOCK
    ncc = dims.ctx_len // B_BLOCK
    ncl = dims.seq // B_BLOCK
    lat0 = dims.batch * ncc
    nblk = rows // B_BLOCK
    assert ncc % B_STEP_BLOCKS == 0 and ncl % B_STEP_BLOCKS == 0

    def batch_of(s):
        i = s * B_STEP_BLOCKS
        return jnp.where(i < lat0, i // ncc, (i - lat0) // ncl)

    def left(s):
        return jnp.maximum(s * B_STEP_BLOCKS - 1, 0)

    def right(s):
        return jnp.minimum((s + 1) * B_STEP_BLOCKS, nblk - 1)

    n_tiles = 3 + ncc
    side = _AdaSide(*ada_next, lambda s: s, nblk // B_STEP_BLOCKS) if ada_next else None
    out_spec = pl.BlockSpec((step_rows, D_MODEL), lambda s: (s, 0))
    out_shape = jax.ShapeDtypeStruct((rows, D_MODEL), _BF16)
    return pl.pallas_call(
        functools.partial(_b_attn_kernel, n_ctx_blocks=lat0, blocks_per_seq=ncl, ctx_len=dims.ctx_len,
                          side_tiles=side.n_tiles if side else 0),
        out_shape=(out_shape, side.out_shape) if side else out_shape,
        grid=(nblk // B_STEP_BLOCKS,),
        in_specs=[
            pl.BlockSpec(memory_space=pltpu.SMEM),
            _resident((B_BLOCK, B_GROUP * B_BLOCK)),
            _resident((B_BLOCK, B_GROUP * B_BLOCK)),
            pl.BlockSpec((D_MODEL, step_rows), lambda s: (0, s)),
            pl.BlockSpec((B_BLOCK, B_KV_DIM), lambda s: (left(s), 0)),
            pl.BlockSpec((step_rows, B_KV_DIM), lambda s: (s, 0)),
            pl.BlockSpec((B_BLOCK, B_KV_DIM), lambda s: (right(s), 0)),
            pl.BlockSpec((dims.ctx_len, B_KV_DIM), lambda s: (batch_of(s), 0)),
            pl.BlockSpec((B_KV_DIM, B_BLOCK), lambda s: (0, left(s))),
            pl.BlockSpec((B_KV_DIM, step_rows), lambda s: (0, s)),
            pl.BlockSpec((B_KV_DIM, B_BLOCK), lambda s: (0, right(s))),
            pl.BlockSpec((B_KV_DIM, dims.ctx_len), lambda s: (0, batch_of(s))),
        ] + (side.in_specs if side else []),
        out_specs=(out_spec, side.out_spec) if side else out_spec,
        scratch_shapes=[pltpu.VMEM((B_STEP_BLOCKS, B_KV_HEADS, n_tiles, B_BLOCK, B_GROUP * B_BLOCK), _F32)],
        compiler_params=_cparams("arbitrary"),
        name="b_attn",
    )(sinks, *_window_bias(), qt, k, k, k, k, vt, vt, vt, vt, *(side.operands if side else ()))


def _c_proj_kernel(*refs, layer, side_tiles):
    if side_tiles:
        side_refs, refs = refs[4:7] + refs[9:10], refs[:4] + refs[7:9] + refs[10:]
        _ada_side_step(pl.program_id(0), side_tiles, *side_refs)
    h_ref, g_ref, mod_ref, win_hbm, bg_ref, u_ref, w_ref, stage_ref, sem_ref = refs

    @pl.when(pl.program_id(0) == 0)
    def _():
        def store(c, width, tile):
            w_ref[:, c:c + width] = tile.astype(_BF16)

        _stage_weights(_col_tiles(win_hbm, layer, 0, 3 * D_MODEL, store), stage_ref, sem_ref)

    x = _norm_mod(h_ref[...], g_ref[...], _mod_part(mod_ref, 0), _mod_part(mod_ref, 1)).astype(_BF16)
    y = jnp.dot(x, w_ref[...], preferred_element_type=_F32)
    bg_ref[...] = y[:, :D_MODEL]
    u_ref[...] = y[:, D_MODEL:2 * D_MODEL] * y[:, 2 * D_MODEL:]


def _c_proj(h, g, mod_l, w_in, layer, dims, ada_next=None):
    rows = h.shape[0]
    tm = ROW_TILE
    side = _AdaSide(*ada_next, lambda i: i, rows // tm) if ada_next else None
    return pl.pallas_call(
        functools.partial(_c_proj_kernel, layer=layer, side_tiles=side.n_tiles if side else 0),
        out_shape=(jax.ShapeDtypeStruct((rows, D_MODEL), _F32),) * 2 + ((side.out_shape,) if side else ()),
        grid=(rows // tm,),
        in_specs=[
            _row_spec(tm, D_MODEL),
            _resident((1, D_MODEL)),
            _mod_spec(dims.n_ctx_rows, dims.seq, tm),
            _weight_spec(),
        ] + (side.in_specs if side else []),
        out_specs=(_row_spec(tm, D_MODEL),) * 2 + ((side.out_spec,) if side else ()),
        scratch_shapes=[pltpu.VMEM((D_MODEL, 3 * D_MODEL), _BF16)] + _stage_scratch(),
        compiler_params=_cparams("arbitrary"),
        name="c_proj",
    )(h, g, mod_l, w_in, *(side.operands if side else ()))


def _c_mix(bg_ref, u_ref, up_ref, un_ref, cw_ref, cb_ref, tile, dims):
    tm = u_ref.shape[0]
    u = u_ref[...]
    row = lax.broadcasted_iota(jnp.int32, (tm, 1), 0)
    g_row = tile * tm + row
    in_ctx = g_row < dims.n_ctx_rows
    pos = jnp.where(in_ctx, g_row % dims.ctx_len, (g_row - dims.n_ctx_rows) % dims.seq)
    length = jnp.where(in_ctx, dims.ctx_len, dims.seq)
    prev = jnp.where(row == 0, up_ref[SUBLANES - 1:SUBLANES, :], pltpu.roll(u, 1, axis=0))
    nxt = jnp.where(row == tm - 1, un_ref[0:1, :], pltpu.roll(u, tm - 1, axis=0))
    prev = jnp.where(pos == 0, 0.0, prev)
    nxt = jnp.where(pos == length - 1, 0.0, nxt)
    conv = prev * cw_ref[0:1, :] + u * cw_ref[1:2, :] + nxt * cw_ref[2:3, :] + cb_ref[...]
    return (bg_ref[...] * conv).astype(_BF16)


_N_MIX_REFS = (4, 1, 6)


def _post_kernel(*refs, kind, n_h, tile0, final_norm, dims, out_layer, layer):
    n_mix = _N_MIX_REFS[kind]
    h_refs, mod_ref = refs[:n_h], refs[n_h]
    mix_refs = refs[n_h + 1:n_h + 1 + n_mix]
    (wo_hbm, g_ref, w1_hbm, w2_hbm, fg_ref, out_ref,
     wo_ref, w1_ref, w2_ref, stage_ref, sem_ref) = refs[n_h + 1 + n_mix:]
    n_chunks = MLP_HIDDEN // MLP_HIDDEN_CHUNK
    tiles_wo = D_MODEL // STAGE_COLS
    tiles_w1 = MLP_HIDDEN_CHUNK // STAGE_COLS
    tiles_w2 = D_MODEL // STAGE_COLS

    def compute(need):
        if kind == 0:
            y = _a_mix(*mix_refs)
        elif kind == 1:
            y = mix_refs[0][...]
        else:
            y = _c_mix(*mix_refs, pl.program_id(0) + tile0, dims)
        h = _load_h(h_refs)
        need(tiles_wo)
        h = h + _mod_part(mod_ref, 2) * jnp.dot(y, wo_ref[...], preferred_element_type=_F32)
        x = _norm_mod(h, g_ref[...], _mod_part(mod_ref, 3), _mod_part(mod_ref, 4)).astype(_BF16)
        acc = jnp.zeros(h.shape, _F32)
        for c in range(n_chunks):
            cols = slice(c * MLP_HIDDEN_CHUNK, (c + 1) * MLP_HIDDEN_CHUNK)
            need(tiles_wo + c * (tiles_w1 + tiles_w2) + tiles_w1)
            u = jnp.dot(x, w1_ref[:, cols], preferred_element_type=_F32)
            u = jnp.square(jnp.maximum(u, 0.0)).astype(_BF16)
            need(tiles_wo + (c + 1) * (tiles_w1 + tiles_w2))
            acc = acc + jnp.dot(u, w2_ref[cols, :], preferred_element_type=_F32)
        out = h + _mod_part(mod_ref, 5) * acc
        if final_norm:
            ms = jnp.mean(out * out, axis=-1, keepdims=True)
            out = out * lax.rsqrt(ms + NORM_EPS) * fg_ref[...]
        out_ref[...] = out

    @pl.when(pl.program_id(0) == 0)
    def _():
        def store_to(dst_ref, row0, c, width, tile):
            dst_ref[row0:row0 + tile.shape[0], c:c + width] = tile.astype(_BF16)

        jobs = _col_tiles(wo_hbm, out_layer, 0, D_MODEL, functools.partial(store_to, wo_ref, 0))
        for r0 in range(0, MLP_HIDDEN, MLP_HIDDEN_CHUNK):
            jobs += _col_tiles(w1_hbm, layer, r0, MLP_HIDDEN_CHUNK,
                               lambda c, width, tile, r0=r0: store_to(w1_ref, 0, r0 + c, width, tile))
            for c in range(0, D_MODEL, STAGE_COLS):
                jobs.append((w2_hbm.at[layer, pl.ds(r0, MLP_HIDDEN_CHUNK), pl.ds(c, STAGE_COLS)],
                             functools.partial(store_to, w2_ref, r0, c, STAGE_COLS)))
        compute(_WeightStager(jobs, stage_ref, sem_ref).need)

    @pl.when(pl.program_id(0) != 0)
    def _():
        compute(lambda n: None)


def _post(kind, h_parts, mod_l, mix, w_out, out_layer, g, w1, w2, layer, final_g, dims, tile0=0,
          final_norm=False):
    tm = ROW_TILE
    n_tiles = dims.rows // tm - tile0
    row = functools.partial(_row_spec, tm, D_MODEL, tile0)
    if kind == 0:
        mix_specs = [row(), row(), row(), _resident((1, D_MODEL))]
    elif kind == 1:
        mix_specs = [row()]
    else:
        bg, u, conv_w, conv_b = mix
        per = tm // SUBLANES
        last = dims.rows // SUBLANES - 1
        mix = (bg, u, u, u, conv_w, conv_b)
        mix_specs = [
            row(), row(),
            pl.BlockSpec((SUBLANES, D_MODEL), lambda i: (jnp.maximum((i + tile0) * per - 1, 0), 0)),
            pl.BlockSpec((SUBLANES, D_MODEL), lambda i: (jnp.minimum((i + tile0 + 1) * per, last), 0)),
            _resident(conv_w.shape), _resident((1, D_MODEL)),
        ]
    assert len(h_parts) == 1 or tile0 == 0
    h_specs = _h_specs(tm, True) if len(h_parts) == 2 else [row()]
    return pl.pallas_call(
        functools.partial(_post_kernel, kind=kind, n_h=len(h_parts), tile0=tile0,
                          final_norm=final_norm, dims=dims, out_layer=out_layer, layer=layer),
        out_shape=jax.ShapeDtypeStruct((n_tiles * tm, D_MODEL), _F32),
        grid=(n_tiles,),
        in_specs=h_specs + [_mod_spec(dims.n_ctx_rows, dims.seq, tm, tile0)] + mix_specs + [
            _weight_spec(),
            _resident((1, D_MODEL)),
            _weight_spec(),
            _weight_spec(),
            _resident((1, D_MODEL)),
        ],
        out_specs=_row_spec(tm, D_MODEL),
        scratch_shapes=[pltpu.VMEM((D_MODEL, D_MODEL), _BF16), pltpu.VMEM((D_MODEL, MLP_HIDDEN), _BF16),
                        pltpu.VMEM((MLP_HIDDEN, D_MODEL), _BF16)] + _stage_scratch(),
        compiler_params=_cparams("arbitrary"),
        name="post",
    )(*h_parts, mod_l, *mix, w_out, g, w1, w2, final_g)


class _Dims:
    def __init__(self, batch, seq, ctx_len):
        self.batch = batch
        self.seq = seq
        self.ctx_len = ctx_len
        self.n_ctx_rows = batch * ctx_len
        self.rows = self.n_ctx_rows + batch * seq


def _rope_tables(dims):
    n_freq = B_HEAD_DIM // 4
    t = np.arange(dims.seq)
    inv_freq = ROPE_BASE ** (-np.arange(n_freq, dtype=np.float64) / n_freq)
    ang_row = (t // GRID_W)[:, None] * inv_freq
    ang_col = (t % GRID_W)[:, None] * inv_freq
    ang = np.concatenate([ang_row, ang_row, ang_col, ang_col], axis=1)
    sign = np.tile(np.concatenate([-np.ones(n_freq), np.ones(n_freq)]), 2)
    cos = np.tile(np.cos(ang), (dims.batch, 1))
    sin = np.tile(np.sin(ang) * sign, (dims.batch, 1))
    cos = np.concatenate([np.ones((dims.n_ctx_rows, B_HEAD_DIM)), cos], axis=0).astype(np.float32)
    sin = np.concatenate([np.zeros((dims.n_ctx_rows, B_HEAD_DIM)), sin], axis=0).astype(np.float32)
    tables = (np.tile(cos, (1, 2)), np.tile(sin, (1, 2)), np.ascontiguousarray(cos.T), np.ascontiguousarray(sin.T))
    return tuple(jnp.asarray(tab) for tab in tables)


def kernel(x, c, ctx, c_ctx, ada_w, ada_b, norm_g, final_g, mlp_w1, mlp_w2,
           a_w_in, a_w_gate, a_b_gate, a_head_g, a_w_out,
           b_w_qkv, b_sinks, b_w_out, c_w_in, c_conv_w, c_conv_b, c_w_out):
    batch, seq, d = x.shape
    ctx_len = ctx.shape[1]
    depth = ada_w.shape[0]
    dims = _Dims(batch, seq, ctx_len)
    assert MLP_HIDDEN_CHUNK == D_MODEL and d == D_MODEL and seq % ROW_TILE == 0 and dims.n_ctx_rows % ROW_TILE == 0
    assert ctx_len % (A_STEP_CHUNKS * A_CHUNK) == 0 and 1 + batch <= MOD_ROWS

    assert dims.n_ctx_rows == ROW_TILE
    h_parts = (x.reshape(-1, d), ctx.reshape(-1, d))
    cs = jnp.concatenate([c_ctx[None], c, jnp.zeros((MOD_ROWS - 1 - batch, d), _F32)], axis=0)
    ada_b3 = ada_b.reshape(depth, 1, 6 * d)
    mod_next = _ada_table(cs, ada_w, ada_b3, 0)
    rope = _rope_tables(dims)
    fg = final_g.reshape(1, d)

    for l in range(depth):
        kind, j = l % N_MIXERS, l // N_MIXERS
        last_layer = l == depth - 1
        mod_l = mod_next.reshape(MOD_ROWS, 1, 6 * d)
        ada_next = None if last_layer else (cs, ada_w, ada_b3, l + 1)
        g0 = norm_g[l, 0].reshape(1, d)
        g1 = norm_g[l, 1].reshape(1, d)
        if kind == 0:
            wgt = a_w_gate[j].T.astype(_BF16)
            q, kt, v, o, gr, gc = _a_proj(h_parts, g0, mod_l, a_w_in, j, wgt, a_b_gate[j].reshape(-1, 1), dims)
            hf, hb, *mod_out = _a_scan(q, kt, v, gr, gc, dims, ada_next)
            mix = (hf, hb, o, a_head_g[j].reshape(1, d))
            w_out = a_w_out
        elif kind == 1:
            (h,) = h_parts
            qt, k, vt = _b_proj(h, g0, mod_l, b_w_qkv, j, rope, dims)
            y = _b_attn(b_sinks[j], qt, k, vt, dims, ada_next)
            y, *mod_out = y if ada_next else (y,)
            mix = (y,)
            w_out = b_w_out
        else:
            (h,) = h_parts
            bg, u, *mod_out = _c_proj(h, g0, mod_l, c_w_in, j, dims, ada_next)
            mix = (bg, u, c_conv_w[j], c_conv_b[j].reshape(1, d))
            w_out = c_w_out
        if mod_out:
            (mod_next,) = mod_out
        tile0 = dims.n_ctx_rows // ROW_TILE if last_layer else 0
        h_parts = (_post(kind, h_parts, mod_l, mix, w_out, j, g1, mlp_w1, mlp_w2, l, fg, dims,
                         tile0=tile0, final_norm=last_layer),)
    return h_parts[0].reshape(batch, seq, d)
```

```python
import functools

import jax
import jax.numpy as jnp
import numpy as np
from jax import lax
from jax.experimental import pallas as pl
from jax.experimental.pallas import tpu as pltpu

D_MODEL = 1024
GRID_W = 64
N_MIXERS = 3
A_HEADS = 8
A_QK_DIM = D_MODEL // 2
A_DK = A_QK_DIM // A_HEADS
A_DV = D_MODEL // A_HEADS
A_CHUNK = 128
A_STEP_CHUNKS = 2
B_Q_HEADS = 16
B_KV_HEADS = 4
B_HEAD_DIM = D_MODEL // B_Q_HEADS
B_GROUP = B_Q_HEADS // B_KV_HEADS
B_KV_DIM = B_KV_HEADS * B_HEAD_DIM
B_BLOCK = 128
ROPE_BASE = 10000.0
MLP_HIDDEN = 4 * D_MODEL
NORM_EPS = 1e-6
LOG2E = 1.4426950408889634

LANES = 128
SUBLANES = 8
VMEM_LIMIT_BYTES = 56 * 1024 * 1024

ROW_TILE = 512
MLP_HIDDEN_CHUNK = 1024
MOD_ROWS = 8
ADA_COL_TILE = 1536

_BF16 = jnp.bfloat16
_F32 = jnp.float32


def _cparams(*sem):
    return pltpu.CompilerParams(dimension_semantics=sem, vmem_limit_bytes=VMEM_LIMIT_BYTES)


def _resident(shape):
    nd = len(shape)
    return pl.BlockSpec(shape, lambda *_: (0,) * nd, pipeline_mode=pl.Buffered(1))


STAGE_COLS = 512
STAGE_DEPTH = 3


def _weight_spec():
    return pl.BlockSpec(memory_space=pl.ANY)


def _stage_scratch():
    return [pltpu.VMEM((STAGE_DEPTH, D_MODEL, STAGE_COLS), _F32), pltpu.SemaphoreType.DMA((STAGE_DEPTH,))]


class _WeightStager:
    def __init__(self, jobs, stage_ref, sem_ref):
        self.jobs, self.stage_ref, self.sem_ref = jobs, stage_ref, sem_ref
        self.started = self.done = 0

    def _copy(self, i):
        src = self.jobs[i][0]
        dst = self.stage_ref.at[i % STAGE_DEPTH, pl.ds(0, src.shape[0]), pl.ds(0, src.shape[1])]
        return pltpu.make_async_copy(src, dst, self.sem_ref.at[i % STAGE_DEPTH])

    def _fill(self):
        while self.started < min(self.done + STAGE_DEPTH, len(self.jobs)):
            self._copy(self.started).start()
            self.started += 1

    def need(self, n):
        self._fill()
        while self.done < n:
            src, store = self.jobs[self.done]
            self._copy(self.done).wait()
            store(self.stage_ref[self.done % STAGE_DEPTH, :src.shape[0], :src.shape[1]])
            self.done += 1
            self._fill()


def _stage_weights(jobs, stage_ref, sem_ref):
    _WeightStager(jobs, stage_ref, sem_ref).need(len(jobs))


def _col_tiles(w_hbm, layer, col0, ncols, store):
    jobs = []
    for c in range(0, ncols, STAGE_COLS):
        width = min(STAGE_COLS, ncols - c)
        jobs.append((w_hbm.at[layer, :, pl.ds(col0 + c, width)], functools.partial(store, c, width)))
    return jobs


def _norm_mod(h, g, shift, scale):
    ms = jnp.mean(h * h, axis=-1, keepdims=True)
    y = h * lax.rsqrt(ms + NORM_EPS) * g
    return y * (1.0 + scale) + shift


def _mod_part(mod_ref, k):
    return mod_ref[:, k * D_MODEL:(k + 1) * D_MODEL]


def _mod_spec(n_ctx_rows, seq, tm, tile0=0):
    def idx(i):
        r0 = (i + tile0) * tm
        grp = jnp.where(r0 < n_ctx_rows, 0, 1 + (r0 - n_ctx_rows) // seq)
        return (grp, 0, 0)
    return pl.BlockSpec((None, 1, 6 * D_MODEL), idx)


def _row_spec(tm, width, tile0=0):
    return pl.BlockSpec((tm, width), lambda i: (i + tile0, 0))


def _h_specs(tm, split):
    if not split:
        return [_row_spec(tm, D_MODEL)]
    return [pl.BlockSpec((tm, D_MODEL), lambda i: (jnp.maximum(i - 1, 0), 0)),
            pl.BlockSpec((tm, D_MODEL), lambda i: (0, 0))]


def _load_h(h_refs):
    if len(h_refs) == 1:
        return h_refs[0][...]
    lat_ref, ctx_ref = h_refs
    return jnp.where(pl.program_id(0) == 0, ctx_ref[...], lat_ref[...])


def _ada_kernel(cs_ref, w_ref, b_ref, o_ref):
    cs = cs_ref[...]
    s = (cs * jax.nn.sigmoid(cs)).astype(_BF16)
    o_ref[...] = jnp.dot(s, w_ref[...].astype(_BF16), preferred_element_type=_F32) + b_ref[...]


def _ada_table(cs, ada_w, ada_b):
    depth = ada_w.shape[0]
    n = ada_w.shape[2]
    return pl.pallas_call(
        _ada_kernel,
        out_shape=jax.ShapeDtypeStruct((depth, MOD_ROWS, n), _F32),
        grid=(depth, n // ADA_COL_TILE),
        in_specs=[
            pl.BlockSpec((MOD_ROWS, D_MODEL), lambda l, j: (0, 0)),
            pl.BlockSpec((None, D_MODEL, ADA_COL_TILE), lambda l, j: (l, 0, j)),
            pl.BlockSpec((None, 1, ADA_COL_TILE), lambda l, j: (l, 0, j)),
        ],
        out_specs=pl.BlockSpec((None, MOD_ROWS, ADA_COL_TILE), lambda l, j: (l, 0, j)),
        compiler_params=_cparams("arbitrary", "arbitrary"),
        name="ada_table",
    )(cs, ada_w, ada_b.reshape(depth, 1, n))


def _a_proj_kernel(*refs, n_h, layer):
    h_refs, (g_ref, mod_ref, win_hbm, wgt_ref, bg_ref,
             q_ref, kt_ref, v_ref, o_ref, gr_ref, gc_ref,
             wqvo_ref, wkt_ref, stage_ref, sem_ref) = refs[:n_h], refs[n_h:]

    @pl.when(pl.program_id(0) == 0)
    def _():
        def store_qvo(dst0, c, width, tile):
            wqvo_ref[:, dst0 + c:dst0 + c + width] = tile.astype(_BF16)

        def store_kt(c, width, tile):
            wkt_ref[c:c + width, :] = tile.T.astype(_BF16)

        _stage_weights(
            _col_tiles(win_hbm, layer, 0, A_QK_DIM, functools.partial(store_qvo, 0))
            + _col_tiles(win_hbm, layer, A_QK_DIM, A_QK_DIM, store_kt)
            + _col_tiles(win_hbm, layer, 2 * A_QK_DIM, 2 * D_MODEL, functools.partial(store_qvo, A_QK_DIM)),
            stage_ref, sem_ref)

    x = _norm_mod(_load_h(h_refs), g_ref[...], _mod_part(mod_ref, 0), _mod_part(mod_ref, 1)).astype(_BF16)
    nt = (((1,), (1,)), ((), ()))
    gt = lax.dot_general(wgt_ref[...], x, nt, preferred_element_type=_F32) + bg_ref[...]

    n_chunks = gt.shape[1] // A_CHUNK

    def by_chunk(rows):
        return jnp.concatenate([rows[:, c * A_CHUNK:(c + 1) * A_CHUNK] for c in range(n_chunks)], axis=0)

    def by_token(x):
        return jnp.concatenate([x[c * A_HEADS:(c + 1) * A_HEADS] for c in range(n_chunks)], axis=1)

    def log_sigmoid(z):
        return jnp.minimum(z, 0.0) - jnp.log1p(jnp.exp(-jnp.abs(z)))

    row_form, col_cm, col_b = [], [], []
    for d in range(2):
        li = by_chunk(gt[2 * d * A_HEADS:(2 * d + 1) * A_HEADS]) * LOG2E
        lf = log_sigmoid(by_chunk(gt[(2 * d + 1) * A_HEADS:(2 * d + 2) * A_HEADS])) * LOG2E
        b = _lane_scan(lf, jnp.add, 0.0, d == 1)
        a = li - b
        cm = _lane_scan(a, jnp.maximum, -jnp.inf, d == 1)
        row_form += [by_token(a), by_token(b)]
        col_cm.append(cm)
        col_b.append(b)
    gr_ref[...] = jnp.concatenate(row_form, axis=0)
    pad = jnp.zeros((LANES - 2 * A_HEADS, LANES), _F32)
    for c in range(n_chunks):
        rows = slice(c * A_HEADS, (c + 1) * A_HEADS)
        toks = slice(c * A_CHUNK, (c + 1) * A_CHUNK)
        gc_ref[toks, :LANES] = jnp.concatenate([col_cm[0][rows], col_cm[1][rows], pad], axis=0).T
        gc_ref[toks, LANES:] = jnp.concatenate([col_b[0][rows], col_b[1][rows], pad], axis=0).T

    y = jnp.dot(x, wqvo_ref[...], preferred_element_type=_F32)
    q_ref[...] = (y[:, :A_QK_DIM] * (A_DK ** -0.5)).astype(_BF16)
    v_ref[...] = y[:, A_QK_DIM:A_QK_DIM + D_MODEL].astype(_BF16)
    o_ref[...] = y[:, A_QK_DIM + D_MODEL:]
    kt_ref[...] = lax.dot_general(wkt_ref[...], x, nt, preferred_element_type=_F32).astype(_BF16)


def _a_proj(h_parts, g, mod_l, w_in, layer, wgt, bgate, dims):
    rows = dims.rows
    tm = ROW_TILE
    n_gate = 4 * A_HEADS
    return pl.pallas_call(
        functools.partial(_a_proj_kernel, n_h=len(h_parts), layer=layer),
        out_shape=(
            jax.ShapeDtypeStruct((rows, A_QK_DIM), _BF16),
            jax.ShapeDtypeStruct((A_QK_DIM, rows), _BF16),
            jax.ShapeDtypeStruct((rows, D_MODEL), _BF16),
            jax.ShapeDtypeStruct((rows, D_MODEL), _F32),
            jax.ShapeDtypeStruct((n_gate, rows), _F32),
            jax.ShapeDtypeStruct((rows, 2 * LANES), _F32),
        ),
        grid=(rows // tm,),
        in_specs=_h_specs(tm, len(h_parts) == 2) + [
            _resident((1, D_MODEL)),
            _mod_spec(dims.n_ctx_rows, dims.seq, tm),
            _weight_spec(),
            _resident(wgt.shape),
            _resident((n_gate, 1)),
        ],
        out_specs=(
            _row_spec(tm, A_QK_DIM),
            pl.BlockSpec((A_QK_DIM, tm), lambda i: (0, i)),
            _row_spec(tm, D_MODEL),
            _row_spec(tm, D_MODEL),
            pl.BlockSpec((n_gate, tm), lambda i: (0, i)),
            _row_spec(tm, 2 * LANES),
        ),
        scratch_shapes=[pltpu.VMEM((D_MODEL, A_QK_DIM + 2 * D_MODEL), _BF16),
                        pltpu.VMEM((A_QK_DIM, D_MODEL), _BF16)] + _stage_scratch(),
        compiler_params=_cparams("arbitrary"),
        name="a_proj",
    )(*h_parts, g, mod_l, w_in, wgt, bgate)


def _lane_scan(x, op, fill, reverse):
    lane = lax.broadcasted_iota(jnp.int32, x.shape, 1)
    k = 1
    while k < LANES:
        if reverse:
            shifted = jnp.where(lane < LANES - k, pltpu.roll(x, LANES - k, axis=1), fill)
        else:
            shifted = jnp.where(lane >= k, pltpu.roll(x, k, axis=1), fill)
        x = op(x, shifted)
        k *= 2
    return x


def _a_scan_kernel(qf_ref, ktf_ref, vf_ref, grf_ref, gcf_ref, qb_ref, ktb_ref, vb_ref, grb_ref, gcb_ref,
                   hf_ref, hb_ref, cf_ref, cb_ref, mrf_ref, mrb_ref, mc_ref):
    @pl.when(pl.program_id(1) == 0)
    def _():
        cf_ref[...] = jnp.zeros(cf_ref.shape, _F32)
        cb_ref[...] = jnp.zeros(cb_ref.shape, _F32)
        mrf_ref[...] = jnp.full(mrf_ref.shape, -jnp.inf, _F32)
        mrb_ref[...] = jnp.full(mrb_ref.shape, -jnp.inf, _F32)
        mc_ref[...] = jnp.full(mc_ref.shape, -jnp.inf, _F32)

    t_idx = lax.broadcasted_iota(jnp.int32, (A_CHUNK, A_CHUNK), 0)
    s_idx = lax.broadcasted_iota(jnp.int32, (A_CHUNK, A_CHUNK), 1)
    ones_blk = jnp.ones((A_CHUNK, A_DV), _BF16)
    lane = lax.broadcasted_iota(jnp.int32, (1, LANES), 1)
    for sub in range(A_STEP_CHUNKS):
        toks = (slice(sub * A_CHUNK, (sub + 1) * A_CHUNK),
                slice((A_STEP_CHUNKS - 1 - sub) * A_CHUNK, (A_STEP_CHUNKS - sub) * A_CHUNK))
        _a_chunk_pair(toks, (qf_ref, qb_ref), (ktf_ref, ktb_ref), (vf_ref, vb_ref), (grf_ref, grb_ref),
                      (gcf_ref, gcb_ref), (hf_ref, hb_ref), (cf_ref, cb_ref), (mrf_ref, mrb_ref), mc_ref,
                      t_idx, s_idx, ones_blk, lane)


def _a_chunk_pair(toks, q_refs, kt_refs, v_refs, gr_refs, gc_refs, h_refs, c_refs, mr_refs, mc_ref,
                  t_idx, s_idx, ones_blk, lane):
    m_col = mc_ref[0:1, :]
    m_col_new = []
    dirs = []
    for d in range(2):
        gr_ref, gc_ref, mr_ref, tok = gr_refs[d], gc_refs[d], mr_refs[d], toks[d]
        last = A_CHUNK - 1 if d == 0 else 0
        a = gr_ref[2 * d * A_HEADS:(2 * d + 1) * A_HEADS, tok]
        b = gr_ref[(2 * d + 1) * A_HEADS:(2 * d + 2) * A_HEADS, tok]
        m_row = mr_ref[:, 0:1]
        gg_end = jnp.maximum(m_row, jnp.max(a, axis=1, keepdims=True))
        ws = jnp.exp2(a - gg_end)
        decay = jnp.exp2(m_row - gg_end)
        mr_ref[...] = jnp.broadcast_to(b[:, last:last + 1] + gg_end, mr_ref.shape)
        gg_c = jnp.maximum(m_col, gc_ref[tok, :LANES])
        b_plus_gg = gc_ref[tok, LANES:] + gg_c
        clamp_c = jnp.exp2(-b_plus_gg)
        m_col_new.append(b_plus_gg[last:last + 1, :])
        dirs.append((a, m_row, ws, decay, gg_c, clamp_c))
    mc_ref[...] = jnp.broadcast_to(jnp.where(lane < A_HEADS, m_col_new[0], m_col_new[1]), mc_ref.shape)

    pair_lane = lax.broadcasted_iota(jnp.int32, (A_CHUNK, 2 * A_DK), 1)
    zeros_kt = jnp.zeros((A_DK, A_CHUNK), _BF16)
    zeros_c = jnp.zeros((A_DK, 2 * A_DV), _BF16)
    for hp in range(A_HEADS // 2):
        for d in range(2):
            q_ref, kt_ref, v_ref, h_ref, c_ref, tok = q_refs[d], kt_refs[d], v_refs[d], h_refs[d], c_refs[d], toks[d]
            a, m_row, ws, decay, gg_c, clamp_c = dirs[d]
            mask = (s_idx <= t_idx) if d == 0 else (s_idx >= t_idx)
            heads = (2 * hp, 2 * hp + 1)
            q2 = q_ref[tok, 2 * hp * A_DK:(2 * hp + 2) * A_DK]
            kts = [kt_ref[hd * A_DK:(hd + 1) * A_DK, tok] for hd in heads]
            kt_bd = jnp.concatenate([jnp.concatenate([kts[0], zeros_kt], axis=1),
                                     jnp.concatenate([zeros_kt, kts[1]], axis=1)], axis=0)
            s2 = jnp.dot(q2, kt_bd, preferred_element_type=_F32)
            ggs = [jnp.broadcast_to(gg_c[:, d * A_HEADS + hd:d * A_HEADS + hd + 1], (A_CHUNK, A_CHUNK))
                   for hd in heads]
            e_state = jnp.exp2(jnp.where(pair_lane < A_DK, m_row[heads[0]:heads[0] + 1, :] - ggs[0],
                                         m_row[heads[1]:heads[1] + 1, :] - ggs[1]))
            qe = q2.astype(_F32) * e_state
            for i, hd in enumerate(heads):
                col = d * A_HEADS + hd
                clamp = jnp.broadcast_to(clamp_c[:, col:col + 1], (A_CHUNK, A_DV))
                v = v_ref[tok, hd * A_DV:(hd + 1) * A_DV]
                v_ext = jnp.concatenate([v, ones_blk], axis=1)
                c_old = c_ref[hd]
                e_keys = jnp.exp2(jnp.where(mask, a[hd:hd + 1, :] - ggs[i], -jnp.inf))
                p = jnp.concatenate([s2[:, i * A_CHUNK:(i + 1) * A_CHUNK] * e_keys, qe], axis=1).astype(_BF16)
                c_rows = [c_old.astype(_BF16), zeros_c] if i == 0 else [zeros_c, c_old.astype(_BF16)]
                rhs = jnp.concatenate([v_ext] + c_rows, axis=0)
                num = jnp.dot(p, rhs, preferred_element_type=_F32)
                den = jnp.maximum(jnp.abs(num[:, A_DV:]), clamp)
                h_ref[tok, hd * A_DV:(hd + 1) * A_DV] = num[:, :A_DV] / den
                kw = (kts[i].astype(_F32) * ws[hd:hd + 1, :]).astype(_BF16)
                c_ref[hd] = decay[hd:hd + 1, :] * c_old + jnp.dot(kw, v_ext, preferred_element_type=_F32)


def _a_scan(q, kt, v, gr, gc, dims):
    rows = q.shape[0]
    blk_rows = A_STEP_CHUNKS * A_CHUNK
    ncc = dims.ctx_len // blk_rows
    ncl = dims.seq // blk_rows
    lat0 = dims.batch * ncc

    def fwd_blk(b, j):
        return jnp.where(j < ncc, b * ncc + j, lat0 + b * ncl + (j - ncc))

    def bwd_blk(b, j):
        return jnp.where(j < ncc, b * ncc + (ncc - 1 - j), lat0 + b * ncl + (ncl - 1 - (j - ncc)))

    def specs(blk):
        return [
            pl.BlockSpec((blk_rows, A_QK_DIM), lambda b, j: (blk(b, j), 0)),
            pl.BlockSpec((A_QK_DIM, blk_rows), lambda b, j: (0, blk(b, j))),
            pl.BlockSpec((blk_rows, D_MODEL), lambda b, j: (blk(b, j), 0)),
            pl.BlockSpec((4 * A_HEADS, blk_rows), lambda b, j: (0, blk(b, j))),
            pl.BlockSpec((blk_rows, 2 * LANES), lambda b, j: (blk(b, j), 0)),
        ]

    state = pltpu.VMEM((A_HEADS, A_DK, 2 * A_DV), _F32)
    stab = pltpu.VMEM((A_HEADS, LANES), _F32)
    return pl.pallas_call(
        _a_scan_kernel,
        out_shape=(jax.ShapeDtypeStruct((rows, D_MODEL), _F32),) * 2,
        grid=(dims.batch, ncc + ncl),
        in_specs=specs(fwd_blk) + specs(bwd_blk),
        out_specs=(
            pl.BlockSpec((blk_rows, D_MODEL), lambda b, j: (fwd_blk(b, j), 0)),
            pl.BlockSpec((blk_rows, D_MODEL), lambda b, j: (bwd_blk(b, j), 0)),
        ),
        scratch_shapes=[state, state, stab, stab, stab],
        compiler_params=_cparams("arbitrary", "arbitrary"),
        name="a_scan",
    )(q, kt, v, gr, gc, q, kt, v, gr, gc)


def _a_mix(hf_ref, hb_ref, o_ref, hg_ref):
    hs = hf_ref[...] + hb_ref[...]
    parts = []
    for hd in range(A_HEADS):
        x = hs[:, hd * A_DV:(hd + 1) * A_DV]
        parts.append(x * lax.rsqrt(jnp.mean(x * x, axis=-1, keepdims=True) + NORM_EPS))
    y = jnp.concatenate(parts, axis=1) * hg_ref[...]
    return (jax.nn.sigmoid(o_ref[...]) * y).astype(_BF16)


ROPE_HALF = B_HEAD_DIM // 4


def _swap_halves_lanes(x):
    lane = lax.broadcasted_iota(jnp.int32, x.shape, 1)
    fwd = pltpu.roll(x, LANES - ROPE_HALF, axis=1)
    back = pltpu.roll(x, ROPE_HALF, axis=1)
    return jnp.where(lane % (2 * ROPE_HALF) < ROPE_HALF, fwd, back)


def _swap_halves_rows(x):
    parts = []
    for r0 in range(0, x.shape[0], 2 * ROPE_HALF):
        parts += [x[r0 + ROPE_HALF:r0 + 2 * ROPE_HALF], x[r0:r0 + ROPE_HALF]]
    return jnp.concatenate(parts, axis=0)


def _b_proj_kernel(h_ref, g_ref, mod_ref, wqkv_hbm, cos_ref, sin_ref, cost_ref, sint_ref,
                   qt_ref, k_ref, vt_ref, wqvt_ref, wk_ref, stage_ref, sem_ref, *, layer):
    @pl.when(pl.program_id(0) == 0)
    def _():
        def store_t(dst0, c, width, tile):
            wqvt_ref[dst0 + c:dst0 + c + width, :] = tile.T.astype(_BF16)

        def store_k(c, width, tile):
            wk_ref[:, c:c + width] = tile.astype(_BF16)

        _stage_weights(
            _col_tiles(wqkv_hbm, layer, 0, D_MODEL, functools.partial(store_t, 0))
            + _col_tiles(wqkv_hbm, layer, D_MODEL, B_KV_DIM, store_k)
            + _col_tiles(wqkv_hbm, layer, D_MODEL + B_KV_DIM, B_KV_DIM, functools.partial(store_t, D_MODEL)),
            stage_ref, sem_ref)

    x = _norm_mod(h_ref[...], g_ref[...], _mod_part(mod_ref, 0), _mod_part(mod_ref, 1)).astype(_BF16)
    nt = (((1,), (1,)), ((), ()))
    yt = lax.dot_general(wqvt_ref[...], x, nt, preferred_element_type=_F32)
    cost, sint = cost_ref[...], sint_ref[...]
    for hd in range(B_Q_HEADS):
        rows = slice(hd * B_HEAD_DIM, (hd + 1) * B_HEAD_DIM)
        qt = yt[rows, :]
        qt = (qt * cost + _swap_halves_rows(qt) * sint) * (B_HEAD_DIM ** -0.5 * LOG2E)
        qt_ref[rows, :] = qt.astype(_BF16)
    vt_ref[...] = yt[D_MODEL:, :].astype(_BF16)
    cos, sin = cos_ref[...], sin_ref[...]
    k = jnp.dot(x, wk_ref[...], preferred_element_type=_F32)
    for c0 in range(0, B_KV_DIM, LANES):
        kc = k[:, c0:c0 + LANES]
        k_ref[:, c0:c0 + LANES] = (kc * cos + _swap_halves_lanes(kc) * sin).astype(_BF16)


def _b_proj(h, g, mod_l, w_qkv, layer, rope, dims):
    rows = h.shape[0]
    tm = ROW_TILE
    cos, sin, cost, sint = rope
    return pl.pallas_call(
        functools.partial(_b_proj_kernel, layer=layer),
        out_shape=(
            jax.ShapeDtypeStruct((D_MODEL, rows), _BF16),
            jax.ShapeDtypeStruct((rows, B_KV_DIM), _BF16),
            jax.ShapeDtypeStruct((B_KV_DIM, rows), _BF16),
        ),
        grid=(rows // tm,),
        in_specs=[
            _row_spec(tm, D_MODEL),
            _resident((1, D_MODEL)),
            _mod_spec(dims.n_ctx_rows, dims.seq, tm),
            _weight_spec(),
            _row_spec(tm, LANES),
            _row_spec(tm, LANES),
            pl.BlockSpec((B_HEAD_DIM, tm), lambda i: (0, i)),
            pl.BlockSpec((B_HEAD_DIM, tm), lambda i: (0, i)),
        ],
        out_specs=(
            pl.BlockSpec((D_MODEL, tm), lambda i: (0, i)),
            _row_spec(tm, B_KV_DIM),
            pl.BlockSpec((B_KV_DIM, tm), lambda i: (0, i)),
        ),
        scratch_shapes=[pltpu.VMEM((D_MODEL + B_KV_DIM, D_MODEL), _BF16),
                        pltpu.VMEM((D_MODEL, B_KV_DIM), _BF16)] + _stage_scratch(),
        compiler_params=_cparams("arbitrary"),
        name="b_proj",
    )(h, g, mod_l, w_qkv, cos, sin, cost, sint)


B_ONES_ROWS = 16
B_STEP_BLOCKS = 2


def _b_attn_kernel(sink_ref, win_l_ref, win_r_ref, qt_ref, kl_ref, kc_ref, kr_ref, kx_ref,
                   vtl_ref, vtc_ref, vtr_ref, vtx_ref, o_ref, s_ref, *, n_ctx_blocks, blocks_per_seq, ctx_len):
    first = pl.program_id(0) * B_STEP_BLOCKS
    is_lat = first >= n_ctx_blocks
    n = (first - n_ctx_blocks) % blocks_per_seq
    neg = -jnp.inf
    win_l = jnp.where(is_lat, win_l_ref[...], neg)
    win_r = jnp.where(is_lat, win_r_ref[...], neg)
    bias_c = jnp.where(is_lat, 0.0, neg)
    bias_first_l = jnp.where(jnp.logical_and(is_lat, n >= 1), win_l_ref[...], neg)
    bias_last_r = jnp.where(jnp.logical_and(is_lat, n + B_STEP_BLOCKS <= blocks_per_seq - 1), win_r_ref[...], neg)

    n_q = B_GROUP * B_BLOCK
    head_of_lane = lax.broadcasted_iota(jnp.int32, (1, n_q), 1) // B_BLOCK
    ones_rows = jnp.ones((B_ONES_ROWS, B_BLOCK), _BF16)
    n_ctx_tiles = ctx_len // B_BLOCK

    def local_tiles(blk):
        own = slice(blk * B_BLOCK, (blk + 1) * B_BLOCK)
        if blk == 0:
            left = (kl_ref, vtl_ref, slice(0, B_BLOCK), bias_first_l)
        else:
            left = (kc_ref, vtc_ref, slice((blk - 1) * B_BLOCK, blk * B_BLOCK), win_l)
        if blk == B_STEP_BLOCKS - 1:
            right = (kr_ref, vtr_ref, slice(0, B_BLOCK), bias_last_r)
        else:
            right = (kc_ref, vtc_ref, slice((blk + 1) * B_BLOCK, (blk + 2) * B_BLOCK), win_r)
        return [left, (kc_ref, vtc_ref, own, bias_c), right]

    def scores(blk, g):
        cols = slice(blk * B_BLOCK, (blk + 1) * B_BLOCK)
        qt = jnp.concatenate(
            [qt_ref[(g * B_GROUP + j) * B_HEAD_DIM:(g * B_GROUP + j + 1) * B_HEAD_DIM, cols]
             for j in range(B_GROUP)], axis=1)
        sink = jnp.zeros((1, n_q), _F32)
        for j in range(B_GROUP):
            sink = jnp.where(head_of_lane == j, sink_ref[g * B_GROUP + j] * LOG2E, sink)
        ks = slice(g * B_HEAD_DIM, (g + 1) * B_HEAD_DIM)
        k_tiles = [(k_ref[rows, ks], bias) for k_ref, _, rows, bias in local_tiles(blk)]
        for t in range(n_ctx_tiles):
            k_tiles.append((kx_ref[t * B_BLOCK:(t + 1) * B_BLOCK, ks], None))
        m_tile = None
        for t, (k, bias) in enumerate(k_tiles):
            s = jnp.dot(k, qt, preferred_element_type=_F32)
            if bias is not None:
                s = s + bias
            s_ref[blk, g, t] = s
            m_tile = s if m_tile is None else jnp.maximum(m_tile, s)
        return jnp.maximum(sink, jnp.max(m_tile, axis=0, keepdims=True)), sink

    def attend(blk, g, m, sink):
        ks = slice(g * B_HEAD_DIM, (g + 1) * B_HEAD_DIM)
        vt_tiles = [vt_ref[ks, cols] for _, vt_ref, cols, _ in local_tiles(blk)]
        for t in range(n_ctx_tiles):
            vt_tiles.append(vtx_ref[ks, t * B_BLOCK:(t + 1) * B_BLOCK])
        acc = jnp.zeros((B_HEAD_DIM + B_ONES_ROWS, n_q), _F32)
        for t, vt in enumerate(vt_tiles):
            p = jnp.exp2(s_ref[blk, g, t] - m).astype(_BF16)
            vt_ext = jnp.concatenate([vt, ones_rows], axis=0)
            acc = acc + jnp.dot(vt_ext, p, preferred_element_type=_F32)
        denom = jnp.exp2(sink - m) + acc[B_HEAD_DIM:B_HEAD_DIM + 1, :]
        out_t = acc[:B_HEAD_DIM, :] * (1.0 / denom)
        rows = slice(blk * B_BLOCK, (blk + 1) * B_BLOCK)
        for pair in range(B_GROUP // 2):
            two = jnp.concatenate([out_t[:, (2 * pair) * B_BLOCK:(2 * pair + 1) * B_BLOCK],
                                   out_t[:, (2 * pair + 1) * B_BLOCK:(2 * pair + 2) * B_BLOCK]], axis=0)
            c0 = (g * B_GROUP + 2 * pair) * B_HEAD_DIM
            o_ref[rows, c0:c0 + 2 * B_HEAD_DIM] = two.T.astype(_BF16)

    units = [(blk, g) for blk in range(B_STEP_BLOCKS) for g in range(B_KV_HEADS)]
    pending = [scores(*units[0]), scores(*units[1])]
    for i, unit in enumerate(units):
        if i + 2 < len(units):
            pending.append(scores(*units[i + 2]))
        attend(*unit, *pending[i])


def _window_bias():
    key = np.arange(B_BLOCK)[:, None]
    qry = np.arange(B_BLOCK)[None, :]
    prev_blk = np.where(key >= qry, 0.0, -np.inf).astype(np.float32)
    next_blk = np.where(key <= qry, 0.0, -np.inf).astype(np.float32)
    return jnp.asarray(np.tile(prev_blk, (1, B_GROUP))), jnp.asarray(np.tile(next_blk, (1, B_GROUP)))


def _b_attn(sinks, qt, k, vt, dims):
    rows = k.shape[0]
    step_rows = B_STEP_BLOCKS * B_BLOCK
    ncc = dims.ctx_len // B_BLOCK
    ncl = dims.seq // B_BLOCK
    lat0 = dims.batch * ncc
    nblk = rows // B_BLOCK
    assert ncc % B_STEP_BLOCKS == 0 and ncl % B_STEP_BLOCKS == 0

    def batch_of(s):
        i = s * B_STEP_BLOCKS
        return jnp.where(i < lat0, i // ncc, (i - lat0) // ncl)

    def left(s):
        return jnp.maximum(s * B_STEP_BLOCKS - 1, 0)

    def right(s):
        return jnp.minimum((s + 1) * B_STEP_BLOCKS, nblk - 1)

    n_tiles = 3 + ncc
    return pl.pallas_call(
        functools.partial(_b_attn_kernel, n_ctx_blocks=lat0, blocks_per_seq=ncl, ctx_len=dims.ctx_len),
        out_shape=jax.ShapeDtypeStruct((rows, D_MODEL), _BF16),
        grid=(nblk // B_STEP_BLOCKS,),
        in_specs=[
            pl.BlockSpec(memory_space=pltpu.SMEM),
            _resident((B_BLOCK, B_GROUP * B_BLOCK)),
            _resident((B_BLOCK, B_GROUP * B_BLOCK)),
            pl.BlockSpec((D_MODEL, step_rows), lambda s: (0, s)),
            pl.BlockSpec((B_BLOCK, B_KV_DIM), lambda s: (left(s), 0)),
            pl.BlockSpec((step_rows, B_KV_DIM), lambda s: (s, 0)),
            pl.BlockSpec((B_BLOCK, B_KV_DIM), lambda s: (right(s), 0)),
            pl.BlockSpec((dims.ctx_len, B_KV_DIM), lambda s: (batch_of(s), 0)),
            pl.BlockSpec((B_KV_DIM, B_BLOCK), lambda s: (0, left(s))),
            pl.BlockSpec((B_KV_DIM, step_rows), lambda s: (0, s)),
            pl.BlockSpec((B_KV_DIM, B_BLOCK), lambda s: (0, right(s))),
            pl.BlockSpec((B_KV_DIM, dims.ctx_len), lambda s: (0, batch_of(s))),
        ],
        out_specs=pl.BlockSpec((step_rows, D_MODEL), lambda s: (s, 0)),
        scratch_shapes=[pltpu.VMEM((B_STEP_BLOCKS, B_KV_HEADS, n_tiles, B_BLOCK, B_GROUP * B_BLOCK), _F32)],
        compiler_params=_cparams("arbitrary"),
        name="b_attn",
    )(sinks, *_window_bias(), qt, k, k, k, k, vt, vt, vt, vt)


def _c_proj_kernel(h_ref, g_ref, mod_ref, win_hbm, bg_ref, u_ref, w_ref, stage_ref, sem_ref, *, layer):
    @pl.when(pl.program_id(0) == 0)
    def _():
        def store(c, width, tile):
            w_ref[:, c:c + width] = tile.astype(_BF16)

        _stage_weights(_col_tiles(win_hbm, layer, 0, 3 * D_MODEL, store), stage_ref, sem_ref)

    x = _norm_mod(h_ref[...], g_ref[...], _mod_part(mod_ref, 0), _mod_part(mod_ref, 1)).astype(_BF16)
    y = jnp.dot(x, w_ref[...], preferred_element_type=_F32)
    bg_ref[...] = y[:, :D_MODEL]
    u_ref[...] = y[:, D_MODEL:2 * D_MODEL] * y[:, 2 * D_MODEL:]


def _c_proj(h, g, mod_l, w_in, layer, dims):
    rows = h.shape[0]
    tm = ROW_TILE
    return pl.pallas_call(
        functools.partial(_c_proj_kernel, layer=layer),
        out_shape=(jax.ShapeDtypeStruct((rows, D_MODEL), _F32),) * 2,
        grid=(rows // tm,),
        in_specs=[
            _row_spec(tm, D_MODEL),
            _resident((1, D_MODEL)),
            _mod_spec(dims.n_ctx_rows, dims.seq, tm),
            _weight_spec(),
        ],
        out_specs=(_row_spec(tm, D_MODEL),) * 2,
        scratch_shapes=[pltpu.VMEM((D_MODEL, 3 * D_MODEL), _BF16)] + _stage_scratch(),
        compiler_params=_cparams("arbitrary"),
        name="c_proj",
    )(h, g, mod_l, w_in)


def _c_mix(bg_ref, u_ref, up_ref, un_ref, cw_ref, cb_ref, tile, dims):
    tm = u_ref.shape[0]
    u = u_ref[...]
    row = lax.broadcasted_iota(jnp.int32, (tm, 1), 0)
    g_row = tile * tm + row
    in_ctx = g_row < dims.n_ctx_rows
    pos = jnp.where(in_ctx, g_row % dims.ctx_len, (g_row - dims.n_ctx_rows) % dims.seq)
    length = jnp.where(in_ctx, dims.ctx_len, dims.seq)
    prev = jnp.where(row == 0, up_ref[SUBLANES - 1:SUBLANES, :], pltpu.roll(u, 1, axis=0))
    nxt = jnp.where(row == tm - 1, un_ref[0:1, :], pltpu.roll(u, tm - 1, axis=0))
    prev = jnp.where(pos == 0, 0.0, prev)
    nxt = jnp.where(pos == length - 1, 0.0, nxt)
    conv = prev * cw_ref[0:1, :] + u * cw_ref[1:2, :] + nxt * cw_ref[2:3, :] + cb_ref[...]
    return (bg_ref[...] * conv).astype(_BF16)


_N_MIX_REFS = (4, 1, 6)


def _post_kernel(*refs, kind, n_h, tile0, final_norm, dims, out_layer, layer):
    n_mix = _N_MIX_REFS[kind]
    h_refs, mod_ref = refs[:n_h], refs[n_h]
    mix_refs = refs[n_h + 1:n_h + 1 + n_mix]
    (wo_hbm, g_ref, w1_hbm, w2_hbm, fg_ref, out_ref,
     wo_ref, w1_ref, w2_ref, stage_ref, sem_ref) = refs[n_h + 1 + n_mix:]
    n_chunks = MLP_HIDDEN // MLP_HIDDEN_CHUNK
    tiles_wo = D_MODEL // STAGE_COLS
    tiles_w1 = MLP_HIDDEN_CHUNK // STAGE_COLS
    tiles_w2 = D_MODEL // STAGE_COLS

    def compute(need):
        if kind == 0:
            y = _a_mix(*mix_refs)
        elif kind == 1:
            y = mix_refs[0][...]
        else:
            y = _c_mix(*mix_refs, pl.program_id(0) + tile0, dims)
        h = _load_h(h_refs)
        need(tiles_wo)
        h = h + _mod_part(mod_ref, 2) * jnp.dot(y, wo_ref[...], preferred_element_type=_F32)
        x = _norm_mod(h, g_ref[...], _mod_part(mod_ref, 3), _mod_part(mod_ref, 4)).astype(_BF16)
        acc = jnp.zeros(h.shape, _F32)
        for c in range(n_chunks):
            cols = slice(c * MLP_HIDDEN_CHUNK, (c + 1) * MLP_HIDDEN_CHUNK)
            need(tiles_wo + c * (tiles_w1 + tiles_w2) + tiles_w1)
            u = jnp.dot(x, w1_ref[:, cols], preferred_element_type=_F32)
            u = jnp.square(jnp.maximum(u, 0.0)).astype(_BF16)
            need(tiles_wo + (c + 1) * (tiles_w1 + tiles_w2))
            acc = acc + jnp.dot(u, w2_ref[cols, :], preferred_element_type=_F32)
        out = h + _mod_part(mod_ref, 5) * acc
        if final_norm:
            ms = jnp.mean(out * out, axis=-1, keepdims=True)
            out = out * lax.rsqrt(ms + NORM_EPS) * fg_ref[...]
        out_ref[...] = out

    @pl.when(pl.program_id(0) == 0)
    def _():
        def store_to(dst_ref, row0, c, width, tile):
            dst_ref[row0:row0 + tile.shape[0], c:c + width] = tile.astype(_BF16)

        jobs = _col_tiles(wo_hbm, out_layer, 0, D_MODEL, functools.partial(store_to, wo_ref, 0))
        for r0 in range(0, MLP_HIDDEN, MLP_HIDDEN_CHUNK):
            jobs += _col_tiles(w1_hbm, layer, r0, MLP_HIDDEN_CHUNK,
                               lambda c, width, tile, r0=r0: store_to(w1_ref, 0, r0 + c, width, tile))
            for c in range(0, D_MODEL, STAGE_COLS):
                jobs.append((w2_hbm.at[layer, pl.ds(r0, MLP_HIDDEN_CHUNK), pl.ds(c, STAGE_COLS)],
                             functools.partial(store_to, w2_ref, r0, c, STAGE_COLS)))
        compute(_WeightStager(jobs, stage_ref, sem_ref).need)

    @pl.when(pl.program_id(0) != 0)
    def _():
        compute(lambda n: None)


def _post(kind, h_parts, mod_l, mix, w_out, out_layer, g, w1, w2, layer, final_g, dims, tile0=0,
          final_norm=False):
    tm = ROW_TILE
    n_tiles = dims.rows // tm - tile0
    row = functools.partial(_row_spec, tm, D_MODEL, tile0)
    if kind == 0:
        mix_specs = [row(), row(), row(), _resident((1, D_MODEL))]
    elif kind == 1:
        mix_specs = [row()]
    else:
        bg, u, conv_w, conv_b = mix
        per = tm // SUBLANES
        last = dims.rows // SUBLANES - 1
        mix = (bg, u, u, u, conv_w, conv_b)
        mix_specs = [
            row(), row(),
            pl.BlockSpec((SUBLANES, D_MODEL), lambda i: (jnp.maximum((i + tile0) * per - 1, 0), 0)),
            pl.BlockSpec((SUBLANES, D_MODEL), lambda i: (jnp.minimum((i + tile0 + 1) * per, last), 0)),
            _resident(conv_w.shape), _resident((1, D_MODEL)),
        ]
    assert len(h_parts) == 1 or tile0 == 0
    h_specs = _h_specs(tm, True) if len(h_parts) == 2 else [row()]
    return pl.pallas_call(
        functools.partial(_post_kernel, kind=kind, n_h=len(h_parts), tile0=tile0,
                          final_norm=final_norm, dims=dims, out_layer=out_layer, layer=layer),
        out_shape=jax.ShapeDtypeStruct((n_tiles * tm, D_MODEL), _F32),
        grid=(n_tiles,),
        in_specs=h_specs + [_mod_spec(dims.n_ctx_rows, dims.seq, tm, tile0)] + mix_specs + [
            _weight_spec(),
            _resident((1, D_MODEL)),
            _weight_spec(),
            _weight_spec(),
            _resident((1, D_MODEL)),
        ],
        out_specs=_row_spec(tm, D_MODEL),
        scratch_shapes=[pltpu.VMEM((D_MODEL, D_MODEL), _BF16), pltpu.VMEM((D_MODEL, MLP_HIDDEN), _BF16),
                        pltpu.VMEM((MLP_HIDDEN, D_MODEL), _BF16)] + _stage_scratch(),
        compiler_params=_cparams("arbitrary"),
        name="post",
    )(*h_parts, mod_l, *mix, w_out, g, w1, w2, final_g)


class _Dims:
    def __init__(self, batch, seq, ctx_len):
        self.batch = batch
        self.seq = seq
        self.ctx_len = ctx_len
        self.n_ctx_rows = batch * ctx_len
        self.rows = self.n_ctx_rows + batch * seq


def _rope_tables(dims):
    n_freq = B_HEAD_DIM // 4
    t = np.arange(dims.seq)
    inv_freq = ROPE_BASE ** (-np.arange(n_freq, dtype=np.float64) / n_freq)
    ang_row = (t // GRID_W)[:, None] * inv_freq
    ang_col = (t % GRID_W)[:, None] * inv_freq
    ang = np.concatenate([ang_row, ang_row, ang_col, ang_col], axis=1)
    sign = np.tile(np.concatenate([-np.ones(n_freq), np.ones(n_freq)]), 2)
    cos = np.tile(np.cos(ang), (dims.batch, 1))
    sin = np.tile(np.sin(ang) * sign, (dims.batch, 1))
    cos = np.concatenate([np.ones((dims.n_ctx_rows, B_HEAD_DIM)), cos], axis=0).astype(np.float32)
    sin = np.concatenate([np.zeros((dims.n_ctx_rows, B_HEAD_DIM)), sin], axis=0).astype(np.float32)
    tables = (np.tile(cos, (1, 2)), np.tile(sin, (1, 2)), np.ascontiguousarray(cos.T), np.ascontiguousarray(sin.T))
    return tuple(jnp.asarray(tab) for tab in tables)


def kernel(x, c, ctx, c_ctx, ada_w, ada_b, norm_g, final_g, mlp_w1, mlp_w2,
           a_w_in, a_w_gate, a_b_gate, a_head_g, a_w_out,
           b_w_qkv, b_sinks, b_w_out, c_w_in, c_conv_w, c_conv_b, c_w_out):
    batch, seq, d = x.shape
    ctx_len = ctx.shape[1]
    depth = ada_w.shape[0]
    dims = _Dims(batch, seq, ctx_len)
    assert MLP_HIDDEN_CHUNK == D_MODEL and d == D_MODEL and seq % ROW_TILE == 0 and dims.n_ctx_rows % ROW_TILE == 0
    assert ctx_len % (A_STEP_CHUNKS * A_CHUNK) == 0 and 1 + batch <= MOD_ROWS

    assert dims.n_ctx_rows == ROW_TILE
    h_parts = (x.reshape(-1, d), ctx.reshape(-1, d))
    cs = jnp.concatenate([c_ctx[None], c, jnp.zeros((MOD_ROWS - 1 - batch, d), _F32)], axis=0)
    mod = _ada_table(cs, ada_w, ada_b).reshape(depth, MOD_ROWS, 1, 6 * d)
    rope = _rope_tables(dims)
    fg = final_g.reshape(1, d)

    for l in range(depth):
        kind, j = l % N_MIXERS, l // N_MIXERS
        mod_l = mod[l]
        g0 = norm_g[l, 0].reshape(1, d)
        g1 = norm_g[l, 1].reshape(1, d)
        last_layer = l == depth - 1
        if kind == 0:
            wgt = a_w_gate[j].T.astype(_BF16)
            q, kt, v, o, gr, gc = _a_proj(h_parts, g0, mod_l, a_w_in, j, wgt, a_b_gate[j].reshape(-1, 1), dims)
            hf, hb = _a_scan(q, kt, v, gr, gc, dims)
            mix = (hf, hb, o, a_head_g[j].reshape(1, d))
            w_out = a_w_out
        elif kind == 1:
            (h,) = h_parts
            qt, k, vt = _b_proj(h, g0, mod_l, b_w_qkv, j, rope, dims)
            mix = (_b_attn(b_sinks[j], qt, k, vt, dims),)
            w_out = b_w_out
        else:
            (h,) = h_parts
            bg, u = _c_proj(h, g0, mod_l, c_w_in, j, dims)
            mix = (bg, u, c_conv_w[j], c_conv_b[j].reshape(1, d))
            w_out = c_w_out
        tile0 = dims.n_ctx_rows // ROW_TILE if last_layer else 0
        h_parts = (_post(kind, h_parts, mod_l, mix, w_out, j, g1, mlp_w1, mlp_w2, l, fg, dims,
                         tile0=tile0, final_norm=last_layer),)
    return h_parts[0].reshape(batch, seq, d)
```

```python
import functools

import jax
import jax.numpy as jnp
import numpy as np
from jax import lax
from jax.experimental import pallas as pl
from jax.experimental.pallas import tpu as pltpu

D_MODEL = 1024
GRID_W = 64
N_MIXERS = 3
A_HEADS = 8
A_QK_DIM = D_MODEL // 2
A_DK = A_QK_DIM // A_HEADS
A_DV = D_MODEL // A_HEADS
A_CHUNK = 128
A_STEP_CHUNKS = 2
B_Q_HEADS = 16
B_KV_HEADS = 4
B_HEAD_DIM = D_MODEL // B_Q_HEADS
B_GROUP = B_Q_HEADS // B_KV_HEADS
B_KV_DIM = B_KV_HEADS * B_HEAD_DIM
B_BLOCK = 128
ROPE_BASE = 10000.0
MLP_HIDDEN = 4 * D_MODEL
NORM_EPS = 1e-6
LOG2E = 1.4426950408889634

LANES = 128
SUBLANES = 8
VMEM_LIMIT_BYTES = 56 * 1024 * 1024

ROW_TILE = 512
MLP_HIDDEN_CHUNK = 1024
MOD_ROWS = 8
ADA_COL_TILE = 1536

_BF16 = jnp.bfloat16
_F32 = jnp.float32


def _cparams(*sem):
    return pltpu.CompilerParams(dimension_semantics=sem, vmem_limit_bytes=VMEM_LIMIT_BYTES)


def _resident(shape):
    nd = len(shape)
    return pl.BlockSpec(shape, lambda *_: (0,) * nd, pipeline_mode=pl.Buffered(1))


STAGE_COLS = 512
STAGE_DEPTH = 3


def _weight_spec():
    return pl.BlockSpec(memory_space=pl.ANY)


def _stage_scratch():
    return [pltpu.VMEM((STAGE_DEPTH, D_MODEL, STAGE_COLS), _F32), pltpu.SemaphoreType.DMA((STAGE_DEPTH,))]


class _WeightStager:
    def __init__(self, jobs, stage_ref, sem_ref):
        self.jobs, self.stage_ref, self.sem_ref = jobs, stage_ref, sem_ref
        self.started = self.done = 0

    def _copy(self, i):
        src = self.jobs[i][0]
        dst = self.stage_ref.at[i % STAGE_DEPTH, pl.ds(0, src.shape[0]), pl.ds(0, src.shape[1])]
        return pltpu.make_async_copy(src, dst, self.sem_ref.at[i % STAGE_DEPTH])

    def _fill(self):
        while self.started < min(self.done + STAGE_DEPTH, len(self.jobs)):
            self._copy(self.started).start()
            self.started += 1

    def need(self, n):
        self._fill()
        while self.done < n:
            src, store = self.jobs[self.done]
            self._copy(self.done).wait()
            store(self.stage_ref[self.done % STAGE_DEPTH, :src.shape[0], :src.shape[1]])
            self.done += 1
            self._fill()


def _stage_weights(jobs, stage_ref, sem_ref):
    _WeightStager(jobs, stage_ref, sem_ref).need(len(jobs))


def _col_tiles(w_hbm, layer, col0, ncols, store):
    jobs = []
    for c in range(0, ncols, STAGE_COLS):
        width = min(STAGE_COLS, ncols - c)
        jobs.append((w_hbm.at[layer, :, pl.ds(col0 + c, width)], functools.partial(store, c, width)))
    return jobs


def _norm_mod(h, g, shift, scale):
    ms = jnp.mean(h * h, axis=-1, keepdims=True)
    y = h * lax.rsqrt(ms + NORM_EPS) * g
    return y * (1.0 + scale) + shift


def _mod_part(mod_ref, k):
    return mod_ref[:, k * D_MODEL:(k + 1) * D_MODEL]


def _mod_spec(n_ctx_rows, seq, tm, tile0=0):
    def idx(i):
        r0 = (i + tile0) * tm
        grp = jnp.where(r0 < n_ctx_rows, 0, 1 + (r0 - n_ctx_rows) // seq)
        return (grp, 0, 0)
    return pl.BlockSpec((None, 1, 6 * D_MODEL), idx)


def _row_spec(tm, width, tile0=0):
    return pl.BlockSpec((tm, width), lambda i: (i + tile0, 0))


def _h_specs(tm, split):
    if not split:
        return [_row_spec(tm, D_MODEL)]
    return [pl.BlockSpec((tm, D_MODEL), lambda i: (jnp.maximum(i - 1, 0), 0)),
            pl.BlockSpec((tm, D_MODEL), lambda i: (0, 0))]


def _load_h(h_refs):
    if len(h_refs) == 1:
        return h_refs[0][...]
    lat_ref, ctx_ref = h_refs
    return jnp.where(pl.program_id(0) == 0, ctx_ref[...], lat_ref[...])


def _ada_kernel(cs_ref, w_ref, b_ref, o_ref):
    cs = cs_ref[...]
    s = (cs * jax.nn.sigmoid(cs)).astype(_BF16)
    o_ref[...] = jnp.dot(s, w_ref[...].astype(_BF16), preferred_element_type=_F32) + b_ref[...]


def _ada_table(cs, ada_w, ada_b):
    depth = ada_w.shape[0]
    n = ada_w.shape[2]
    return pl.pallas_call(
        _ada_kernel,
        out_shape=jax.ShapeDtypeStruct((depth, MOD_ROWS, n), _F32),
        grid=(depth, n // ADA_COL_TILE),
        in_specs=[
            pl.BlockSpec((MOD_ROWS, D_MODEL), lambda l, j: (0, 0)),
            pl.BlockSpec((None, D_MODEL, ADA_COL_TILE), lambda l, j: (l, 0, j)),
            pl.BlockSpec((None, 1, ADA_COL_TILE), lambda l, j: (l, 0, j)),
        ],
        out_specs=pl.BlockSpec((None, MOD_ROWS, ADA_COL_TILE), lambda l, j: (l, 0, j)),
        compiler_params=_cparams("arbitrary", "arbitrary"),
        name="ada_table",
    )(cs, ada_w, ada_b.reshape(depth, 1, n))


def _a_proj_kernel(*refs, n_h, layer):
    h_refs, (g_ref, mod_ref, win_hbm, wgt_ref, bg_ref,
             q_ref, kt_ref, v_ref, o_ref, gr_ref, gc_ref,
             wqvo_ref, wkt_ref, stage_ref, sem_ref) = refs[:n_h], refs[n_h:]

    @pl.when(pl.program_id(0) == 0)
    def _():
        def store_qvo(dst0, c, width, tile):
            wqvo_ref[:, dst0 + c:dst0 + c + width] = tile.astype(_BF16)

        def store_kt(c, width, tile):
            wkt_ref[c:c + width, :] = tile.T.astype(_BF16)

        _stage_weights(
            _col_tiles(win_hbm, layer, 0, A_QK_DIM, functools.partial(store_qvo, 0))
            + _col_tiles(win_hbm, layer, A_QK_DIM, A_QK_DIM, store_kt)
            + _col_tiles(win_hbm, layer, 2 * A_QK_DIM, 2 * D_MODEL, functools.partial(store_qvo, A_QK_DIM)),
            stage_ref, sem_ref)

    x = _norm_mod(_load_h(h_refs), g_ref[...], _mod_part(mod_ref, 0), _mod_part(mod_ref, 1)).astype(_BF16)
    nt = (((1,), (1,)), ((), ()))
    gt = lax.dot_general(wgt_ref[...], x, nt, preferred_element_type=_F32) + bg_ref[...]

    n_chunks = gt.shape[1] // A_CHUNK

    def by_chunk(rows):
        return jnp.concatenate([rows[:, c * A_CHUNK:(c + 1) * A_CHUNK] for c in range(n_chunks)], axis=0)

    def by_token(x):
        return jnp.concatenate([x[c * A_HEADS:(c + 1) * A_HEADS] for c in range(n_chunks)], axis=1)

    def log_sigmoid(z):
        return jnp.minimum(z, 0.0) - jnp.log1p(jnp.exp(-jnp.abs(z)))

    row_form, col_cm, col_b = [], [], []
    for d in range(2):
        li = by_chunk(gt[2 * d * A_HEADS:(2 * d + 1) * A_HEADS]) * LOG2E
        lf = log_sigmoid(by_chunk(gt[(2 * d + 1) * A_HEADS:(2 * d + 2) * A_HEADS])) * LOG2E
        b = _lane_scan(lf, jnp.add, 0.0, d == 1)
        a = li - b
        cm = _lane_scan(a, jnp.maximum, -jnp.inf, d == 1)
        row_form += [by_token(a), by_token(b)]
        col_cm.append(cm)
        col_b.append(b)
    gr_ref[...] = jnp.concatenate(row_form, axis=0)
    pad = jnp.zeros((LANES - 2 * A_HEADS, LANES), _F32)
    for c in range(n_chunks):
        rows = slice(c * A_HEADS, (c + 1) * A_HEADS)
        toks = slice(c * A_CHUNK, (c + 1) * A_CHUNK)
        gc_ref[toks, :LANES] = jnp.concatenate([col_cm[0][rows], col_cm[1][rows], pad], axis=0).T
        gc_ref[toks, LANES:] = jnp.concatenate([col_b[0][rows], col_b[1][rows], pad], axis=0).T

    y = jnp.dot(x, wqvo_ref[...], preferred_element_type=_F32)
    q_ref[...] = (y[:, :A_QK_DIM] * (A_DK ** -0.5)).astype(_BF16)
    v_ref[...] = y[:, A_QK_DIM:A_QK_DIM + D_MODEL].astype(_BF16)
    o_ref[...] = y[:, A_QK_DIM + D_MODEL:]
    kt_ref[...] = lax.dot_general(wkt_ref[...], x, nt, preferred_element_type=_F32).astype(_BF16)


def _a_proj(h_parts, g, mod_l, w_in, layer, wgt, bgate, dims):
    rows = dims.rows
    tm = ROW_TILE
    n_gate = 4 * A_HEADS
    return pl.pallas_call(
        functools.partial(_a_proj_kernel, n_h=len(h_parts), layer=layer),
        out_shape=(
            jax.ShapeDtypeStruct((rows, A_QK_DIM), _BF16),
            jax.ShapeDtypeStruct((A_QK_DIM, rows), _BF16),
            jax.ShapeDtypeStruct((rows, D_MODEL), _BF16),
            jax.ShapeDtypeStruct((rows, D_MODEL), _F32),
            jax.ShapeDtypeStruct((n_gate, rows), _F32),
            jax.ShapeDtypeStruct((rows, 2 * LANES), _F32),
        ),
        grid=(rows // tm,),
        in_specs=_h_specs(tm, len(h_parts) == 2) + [
            _resident((1, D_MODEL)),
            _mod_spec(dims.n_ctx_rows, dims.seq, tm),
            _weight_spec(),
            _resident(wgt.shape),
            _resident((n_gate, 1)),
        ],
        out_specs=(
            _row_spec(tm, A_QK_DIM),
            pl.BlockSpec((A_QK_DIM, tm), lambda i: (0, i)),
            _row_spec(tm, D_MODEL),
            _row_spec(tm, D_MODEL),
            pl.BlockSpec((n_gate, tm), lambda i: (0, i)),
            _row_spec(tm, 2 * LANES),
        ),
        scratch_shapes=[pltpu.VMEM((D_MODEL, A_QK_DIM + 2 * D_MODEL), _BF16),
                        pltpu.VMEM((A_QK_DIM, D_MODEL), _BF16)] + _stage_scratch(),
        compiler_params=_cparams("arbitrary"),
        name="a_proj",
    )(*h_parts, g, mod_l, w_in, wgt, bgate)


def _lane_scan(x, op, fill, reverse):
    lane = lax.broadcasted_iota(jnp.int32, x.shape, 1)
    k = 1
    while k < LANES:
        if reverse:
            shifted = jnp.where(lane < LANES - k, pltpu.roll(x, LANES - k, axis=1), fill)
        else:
            shifted = jnp.where(lane >= k, pltpu.roll(x, k, axis=1), fill)
        x = op(x, shifted)
        k *= 2
    return x


def _a_scan_kernel(qf_ref, ktf_ref, vf_ref, grf_ref, gcf_ref, qb_ref, ktb_ref, vb_ref, grb_ref, gcb_ref,
                   hf_ref, hb_ref, cf_ref, cb_ref, mrf_ref, mrb_ref, mc_ref):
    @pl.when(pl.program_id(1) == 0)
    def _():
        cf_ref[...] = jnp.zeros(cf_ref.shape, _F32)
        cb_ref[...] = jnp.zeros(cb_ref.shape, _F32)
        mrf_ref[...] = jnp.full(mrf_ref.shape, -jnp.inf, _F32)
        mrb_ref[...] = jnp.full(mrb_ref.shape, -jnp.inf, _F32)
        mc_ref[...] = jnp.full(mc_ref.shape, -jnp.inf, _F32)

    t_idx = lax.broadcasted_iota(jnp.int32, (A_CHUNK, A_CHUNK), 0)
    s_idx = lax.broadcasted_iota(jnp.int32, (A_CHUNK, A_CHUNK), 1)
    ones_blk = jnp.ones((A_CHUNK, A_DV), _BF16)
    lane = lax.broadcasted_iota(jnp.int32, (1, LANES), 1)
    for sub in range(A_STEP_CHUNKS):
        toks = (slice(sub * A_CHUNK, (sub + 1) * A_CHUNK),
                slice((A_STEP_CHUNKS - 1 - sub) * A_CHUNK, (A_STEP_CHUNKS - sub) * A_CHUNK))
        _a_chunk_pair(toks, (qf_ref, qb_ref), (ktf_ref, ktb_ref), (vf_ref, vb_ref), (grf_ref, grb_ref),
                      (gcf_ref, gcb_ref), (hf_ref, hb_ref), (cf_ref, cb_ref), (mrf_ref, mrb_ref), mc_ref,
                      t_idx, s_idx, ones_blk, lane)


def _a_chunk_pair(toks, q_refs, kt_refs, v_refs, gr_refs, gc_refs, h_refs, c_refs, mr_refs, mc_ref,
                  t_idx, s_idx, ones_blk, lane):
    m_col = mc_ref[0:1, :]
    m_col_new = []
    dirs = []
    for d in range(2):
        gr_ref, gc_ref, mr_ref, tok = gr_refs[d], gc_refs[d], mr_refs[d], toks[d]
        last = A_CHUNK - 1 if d == 0 else 0
        a = gr_ref[2 * d * A_HEADS:(2 * d + 1) * A_HEADS, tok]
        b = gr_ref[(2 * d + 1) * A_HEADS:(2 * d + 2) * A_HEADS, tok]
        m_row = mr_ref[:, 0:1]
        gg_end = jnp.maximum(m_row, jnp.max(a, axis=1, keepdims=True))
        ws = jnp.exp2(a - gg_end)
        decay = jnp.exp2(m_row - gg_end)
        mr_ref[...] = jnp.broadcast_to(b[:, last:last + 1] + gg_end, mr_ref.shape)
        gg_c = jnp.maximum(m_col, gc_ref[tok, :LANES])
        b_plus_gg = gc_ref[tok, LANES:] + gg_c
        clamp_c = jnp.exp2(-b_plus_gg)
        m_col_new.append(b_plus_gg[last:last + 1, :])
        dirs.append((a, m_row, ws, decay, gg_c, clamp_c))
    mc_ref[...] = jnp.broadcast_to(jnp.where(lane < A_HEADS, m_col_new[0], m_col_new[1]), mc_ref.shape)

    pair_lane = lax.broadcasted_iota(jnp.int32, (A_CHUNK, 2 * A_DK), 1)
    zeros_kt = jnp.zeros((A_DK, A_CHUNK), _BF16)
    zeros_c = jnp.zeros((A_DK, 2 * A_DV), _BF16)
    for hp in range(A_HEADS // 2):
        for d in range(2):
            q_ref, kt_ref, v_ref, h_ref, c_ref, tok = q_refs[d], kt_refs[d], v_refs[d], h_refs[d], c_refs[d], toks[d]
            a, m_row, ws, decay, gg_c, clamp_c = dirs[d]
            mask = (s_idx <= t_idx) if d == 0 else (s_idx >= t_idx)
            heads = (2 * hp, 2 * hp + 1)
            q2 = q_ref[tok, 2 * hp * A_DK:(2 * hp + 2) * A_DK]
            kts = [kt_ref[hd * A_DK:(hd + 1) * A_DK, tok] for hd in heads]
            kt_bd = jnp.concatenate([jnp.concatenate([kts[0], zeros_kt], axis=1),
                                     jnp.concatenate([zeros_kt, kts[1]], axis=1)], axis=0)
            s2 = jnp.dot(q2, kt_bd, preferred_element_type=_F32)
            ggs = [jnp.broadcast_to(gg_c[:, d * A_HEADS + hd:d * A_HEADS + hd + 1], (A_CHUNK, A_CHUNK))
                   for hd in heads]
            e_state = jnp.exp2(jnp.where(pair_lane < A_DK, m_row[heads[0]:heads[0] + 1, :] - ggs[0],
                                         m_row[heads[1]:heads[1] + 1, :] - ggs[1]))
            qe = q2.astype(_F32) * e_state
            for i, hd in enumerate(heads):
                col = d * A_HEADS + hd
                clamp = jnp.broadcast_to(clamp_c[:, col:col + 1], (A_CHUNK, A_DV))
                v = v_ref[tok, hd * A_DV:(hd + 1) * A_DV]
                v_ext = jnp.concatenate([v, ones_blk], axis=1)
                c_old = c_ref[hd]
                e_keys = jnp.exp2(jnp.where(mask, a[hd:hd + 1, :] - ggs[i], -jnp.inf))
                p = jnp.concatenate([s2[:, i * A_CHUNK:(i + 1) * A_CHUNK] * e_keys, qe], axis=1).astype(_BF16)
                c_rows = [c_old.astype(_BF16), zeros_c] if i == 0 else [zeros_c, c_old.astype(_BF16)]
                rhs = jnp.concatenate([v_ext] + c_rows, axis=0)
                num = jnp.dot(p, rhs, preferred_element_type=_F32)
                den = jnp.maximum(jnp.abs(num[:, A_DV:]), clamp)
                h_ref[tok, hd * A_DV:(hd + 1) * A_DV] = num[:, :A_DV] / den
                kw = (kts[i].astype(_F32) * ws[hd:hd + 1, :]).astype(_BF16)
                c_ref[hd] = decay[hd:hd + 1, :] * c_old + jnp.dot(kw, v_ext, preferred_element_type=_F32)


def _a_scan(q, kt, v, gr, gc, dims):
    rows = q.shape[0]
    blk_rows = A_STEP_CHUNKS * A_CHUNK
    ncc = dims.ctx_len // blk_rows
    ncl = dims.seq // blk_rows
    lat0 = dims.batch * ncc

    def fwd_blk(b, j):
        return jnp.where(j < ncc, b * ncc + j, lat0 + b * ncl + (j - ncc))

    def bwd_blk(b, j):
        return jnp.where(j < ncc, b * ncc + (ncc - 1 - j), lat0 + b * ncl + (ncl - 1 - (j - ncc)))

    def specs(blk):
        return [
            pl.BlockSpec((blk_rows, A_QK_DIM), lambda b, j: (blk(b, j), 0)),
            pl.BlockSpec((A_QK_DIM, blk_rows), lambda b, j: (0, blk(b, j))),
            pl.BlockSpec((blk_rows, D_MODEL), lambda b, j: (blk(b, j), 0)),
            pl.BlockSpec((4 * A_HEADS, blk_rows), lambda b, j: (0, blk(b, j))),
            pl.BlockSpec((blk_rows, 2 * LANES), lambda b, j: (blk(b, j), 0)),
        ]

    state = pltpu.VMEM((A_HEADS, A_DK, 2 * A_DV), _F32)
    stab = pltpu.VMEM((A_HEADS, LANES), _F32)
    return pl.pallas_call(
        _a_scan_kernel,
        out_shape=(jax.ShapeDtypeStruct((rows, D_MODEL), _F32),) * 2,
        grid=(dims.batch, ncc + ncl),
        in_specs=specs(fwd_blk) + specs(bwd_blk),
        out_specs=(
            pl.BlockSpec((blk_rows, D_MODEL), lambda b, j: (fwd_blk(b, j), 0)),
            pl.BlockSpec((blk_rows, D_MODEL), lambda b, j: (bwd_blk(b, j), 0)),
        ),
        scratch_shapes=[state, state, stab, stab, stab],
        compiler_params=_cparams("arbitrary", "arbitrary"),
        name="a_scan",
    )(q, kt, v, gr, gc, q, kt, v, gr, gc)


def _a_mix(hf_ref, hb_ref, o_ref, hg_ref):
    hs = hf_ref[...] + hb_ref[...]
    parts = []
    for hd in range(A_HEADS):
        x = hs[:, hd * A_DV:(hd + 1) * A_DV]
        parts.append(x * lax.rsqrt(jnp.mean(x * x, axis=-1, keepdims=True) + NORM_EPS))
    y = jnp.concatenate(parts, axis=1) * hg_ref[...]
    return (jax.nn.sigmoid(o_ref[...]) * y).astype(_BF16)


ROPE_HALF = B_HEAD_DIM // 4


def _swap_halves_lanes(x):
    lane = lax.broadcasted_iota(jnp.int32, x.shape, 1)
    fwd = pltpu.roll(x, LANES - ROPE_HALF, axis=1)
    back = pltpu.roll(x, ROPE_HALF, axis=1)
    return jnp.where(lane % (2 * ROPE_HALF) < ROPE_HALF, fwd, back)


def _swap_halves_rows(x):
    parts = []
    for r0 in range(0, x.shape[0], 2 * ROPE_HALF):
        parts += [x[r0 + ROPE_HALF:r0 + 2 * ROPE_HALF], x[r0:r0 + ROPE_HALF]]
    return jnp.concatenate(parts, axis=0)


def _b_proj_kernel(h_ref, g_ref, mod_ref, wqkv_hbm, cos_ref, sin_ref, cost_ref, sint_ref,
                   qt_ref, k_ref, vt_ref, wqvt_ref, wk_ref, stage_ref, sem_ref, *, layer):
    @pl.when(pl.program_id(0) == 0)
    def _():
        def store_t(dst0, c, width, tile):
            wqvt_ref[dst0 + c:dst0 + c + width, :] = tile.T.astype(_BF16)

        def store_k(c, width, tile):
            wk_ref[:, c:c + width] = tile.astype(_BF16)

        _stage_weights(
            _col_tiles(wqkv_hbm, layer, 0, D_MODEL, functools.partial(store_t, 0))
            + _col_tiles(wqkv_hbm, layer, D_MODEL, B_KV_DIM, store_k)
            + _col_tiles(wqkv_hbm, layer, D_MODEL + B_KV_DIM, B_KV_DIM, functools.partial(store_t, D_MODEL)),
            stage_ref, sem_ref)

    x = _norm_mod(h_ref[...], g_ref[...], _mod_part(mod_ref, 0), _mod_part(mod_ref, 1)).astype(_BF16)
    nt = (((1,), (1,)), ((), ()))
    yt = lax.dot_general(wqvt_ref[...], x, nt, preferred_element_type=_F32)
    cost, sint = cost_ref[...], sint_ref[...]
    for hd in range(B_Q_HEADS):
        rows = slice(hd * B_HEAD_DIM, (hd + 1) * B_HEAD_DIM)
        qt = yt[rows, :]
        qt = (qt * cost + _swap_halves_rows(qt) * sint) * (B_HEAD_DIM ** -0.5 * LOG2E)
        qt_ref[rows, :] = qt.astype(_BF16)
    vt_ref[...] = yt[D_MODEL:, :].astype(_BF16)
    cos, sin = cos_ref[...], sin_ref[...]
    k = jnp.dot(x, wk_ref[...], preferred_element_type=_F32)
    for c0 in range(0, B_KV_DIM, LANES):
        kc = k[:, c0:c0 + LANES]
        k_ref[:, c0:c0 + LANES] = (kc * cos + _swap_halves_lanes(kc) * sin).astype(_BF16)


def _b_proj(h, g, mod_l, w_qkv, layer, rope, dims):
    rows = h.shape[0]
    tm = ROW_TILE
    cos, sin, cost, sint = rope
    return pl.pallas_call(
        functools.partial(_b_proj_kernel, layer=layer),
        out_shape=(
            jax.ShapeDtypeStruct((D_MODEL, rows), _BF16),
            jax.ShapeDtypeStruct((rows, B_KV_DIM), _BF16),
            jax.ShapeDtypeStruct((B_KV_DIM, rows), _BF16),
        ),
        grid=(rows // tm,),
        in_specs=[
            _row_spec(tm, D_MODEL),
            _resident((1, D_MODEL)),
            _mod_spec(dims.n_ctx_rows, dims.seq, tm),
            _weight_spec(),
            _row_spec(tm, LANES),
            _row_spec(tm, LANES),
            pl.BlockSpec((B_HEAD_DIM, tm), lambda i: (0, i)),
            pl.BlockSpec((B_HEAD_DIM, tm), lambda i: (0, i)),
        ],
        out_specs=(
            pl.BlockSpec((D_MODEL, tm), lambda i: (0, i)),
            _row_spec(tm, B_KV_DIM),
            pl.BlockSpec((B_KV_DIM, tm), lambda i: (0, i)),
        ),
        scratch_shapes=[pltpu.VMEM((D_MODEL + B_KV_DIM, D_MODEL), _BF16),
                        pltpu.VMEM((D_MODEL, B_KV_DIM), _BF16)] + _stage_scratch(),
        compiler_params=_cparams("arbitrary"),
        name="b_proj",
    )(h, g, mod_l, w_qkv, cos, sin, cost, sint)


B_ONES_ROWS = 16
B_STEP_BLOCKS_CTX = 2
B_STEP_BLOCKS_LAT = 4


def _b_attn_kernel(sink_ref, win_l_ref, win_r_ref, qt_ref, kl_ref, kc_ref, kr_ref, kx_ref,
                   vtl_ref, vtc_ref, vtr_ref, vtx_ref, o_ref, s_ref,
                   *, n_ctx_blocks, blocks_per_seq, ctx_len, step_blocks, first_block):
    first = first_block + pl.program_id(0) * step_blocks
    is_lat = first >= n_ctx_blocks
    n = (first - n_ctx_blocks) % blocks_per_seq
    neg = -jnp.inf
    win_l = jnp.where(is_lat, win_l_ref[...], neg)
    win_r = jnp.where(is_lat, win_r_ref[...], neg)
    bias_c = jnp.where(is_lat, 0.0, neg)
    bias_first_l = jnp.where(jnp.logical_and(is_lat, n >= 1), win_l_ref[...], neg)
    bias_last_r = jnp.where(jnp.logical_and(is_lat, n + step_blocks <= blocks_per_seq - 1), win_r_ref[...], neg)

    n_q = B_GROUP * B_BLOCK
    head_of_lane = lax.broadcasted_iota(jnp.int32, (1, n_q), 1) // B_BLOCK
    ones_rows = jnp.ones((B_ONES_ROWS, B_BLOCK), _BF16)
    n_ctx_tiles = ctx_len // B_BLOCK

    def local_tiles(blk):
        own = slice(blk * B_BLOCK, (blk + 1) * B_BLOCK)
        if blk == 0:
            left = (kl_ref, vtl_ref, slice(0, B_BLOCK), bias_first_l)
        else:
            left = (kc_ref, vtc_ref, slice((blk - 1) * B_BLOCK, blk * B_BLOCK), win_l)
        if blk == step_blocks - 1:
            right = (kr_ref, vtr_ref, slice(0, B_BLOCK), bias_last_r)
        else:
            right = (kc_ref, vtc_ref, slice((blk + 1) * B_BLOCK, (blk + 2) * B_BLOCK), win_r)
        return [left, (kc_ref, vtc_ref, own, bias_c), right]

    def scores(blk, g):
        cols = slice(blk * B_BLOCK, (blk + 1) * B_BLOCK)
        qt = jnp.concatenate(
            [qt_ref[(g * B_GROUP + j) * B_HEAD_DIM:(g * B_GROUP + j + 1) * B_HEAD_DIM, cols]
             for j in range(B_GROUP)], axis=1)
        sink = jnp.zeros((1, n_q), _F32)
        for j in range(B_GROUP):
            sink = jnp.where(head_of_lane == j, sink_ref[g * B_GROUP + j] * LOG2E, sink)
        ks = slice(g * B_HEAD_DIM, (g + 1) * B_HEAD_DIM)
        k_tiles = [(k_ref[rows, ks], bias) for k_ref, _, rows, bias in local_tiles(blk)]
        for t in range(n_ctx_tiles):
            k_tiles.append((kx_ref[t * B_BLOCK:(t + 1) * B_BLOCK, ks], None))
        m_tile = None
        for t, (k, bias) in enumerate(k_tiles):
            s = jnp.dot(k, qt, preferred_element_type=_F32)
            if bias is not None:
                s = s + bias
            s_ref[blk, g, t] = s
            m_tile = s if m_tile is None else jnp.maximum(m_tile, s)
        return jnp.maximum(sink, jnp.max(m_tile, axis=0, keepdims=True)), sink

    def attend(blk, g, m, sink):
        ks = slice(g * B_HEAD_DIM, (g + 1) * B_HEAD_DIM)
        vt_tiles = [vt_ref[ks, cols] for _, vt_ref, cols, _ in local_tiles(blk)]
        for t in range(n_ctx_tiles):
            vt_tiles.append(vtx_ref[ks, t * B_BLOCK:(t + 1) * B_BLOCK])
        acc = jnp.zeros((B_HEAD_DIM + B_ONES_ROWS, n_q), _F32)
        for t, vt in enumerate(vt_tiles):
            p = jnp.exp2(s_ref[blk, g, t] - m).astype(_BF16)
            vt_ext = jnp.concatenate([vt, ones_rows], axis=0)
            acc = acc + jnp.dot(vt_ext, p, preferred_element_type=_F32)
        denom = jnp.exp2(sink - m) + acc[B_HEAD_DIM:B_HEAD_DIM + 1, :]
        out_t = acc[:B_HEAD_DIM, :] * (1.0 / denom)
        rows = slice(blk * B_BLOCK, (blk + 1) * B_BLOCK)
        for pair in range(B_GROUP // 2):
            two = jnp.concatenate([out_t[:, (2 * pair) * B_BLOCK:(2 * pair + 1) * B_BLOCK],
                                   out_t[:, (2 * pair + 1) * B_BLOCK:(2 * pair + 2) * B_BLOCK]], axis=0)
            c0 = (g * B_GROUP + 2 * pair) * B_HEAD_DIM
            o_ref[rows, c0:c0 + 2 * B_HEAD_DIM] = two.T.astype(_BF16)

    units = [(blk, g) for blk in range(step_blocks) for g in range(B_KV_HEADS)]
    pending = [scores(*units[0]), scores(*units[1])]
    for i, unit in enumerate(units):
        if i + 2 < len(units):
            pending.append(scores(*units[i + 2]))
        attend(*unit, *pending[i])


def _window_bias():
    key = np.arange(B_BLOCK)[:, None]
    qry = np.arange(B_BLOCK)[None, :]
    prev_blk = np.where(key >= qry, 0.0, -np.inf).astype(np.float32)
    next_blk = np.where(key <= qry, 0.0, -np.inf).astype(np.float32)
    return jnp.asarray(np.tile(prev_blk, (1, B_GROUP))), jnp.asarray(np.tile(next_blk, (1, B_GROUP)))


def _b_attn_call(sinks, qt, k, vt, dims, step_blocks, first_block, n_blocks):
    step_rows = step_blocks * B_BLOCK
    ncc = dims.ctx_len // B_BLOCK
    ncl = dims.seq // B_BLOCK
    lat0 = dims.batch * ncc
    nblk = k.shape[0] // B_BLOCK
    assert first_block % step_blocks == 0 and n_blocks % step_blocks == 0
    assert ncc % step_blocks == 0 if first_block < lat0 else ncl % step_blocks == 0
    off = first_block // step_blocks

    def batch_of(s):
        i = first_block + s * step_blocks
        return jnp.where(i < lat0, i // ncc, (i - lat0) // ncl)

    def left(s):
        return jnp.maximum(first_block + s * step_blocks - 1, 0)

    def right(s):
        return jnp.minimum(first_block + (s + 1) * step_blocks, nblk - 1)

    n_tiles = 3 + ncc
    return pl.pallas_call(
        functools.partial(_b_attn_kernel, n_ctx_blocks=lat0, blocks_per_seq=ncl, ctx_len=dims.ctx_len,
                          step_blocks=step_blocks, first_block=first_block),
        out_shape=jax.ShapeDtypeStruct((n_blocks * B_BLOCK, D_MODEL), _BF16),
        grid=(n_blocks // step_blocks,),
        in_specs=[
            pl.BlockSpec(memory_space=pltpu.SMEM),
            _resident((B_BLOCK, B_GROUP * B_BLOCK)),
            _resident((B_BLOCK, B_GROUP * B_BLOCK)),
            pl.BlockSpec((D_MODEL, step_rows), lambda s: (0, off + s)),
            pl.BlockSpec((B_BLOCK, B_KV_DIM), lambda s: (left(s), 0)),
            pl.BlockSpec((step_rows, B_KV_DIM), lambda s: (off + s, 0)),
            pl.BlockSpec((B_BLOCK, B_KV_DIM), lambda s: (right(s), 0)),
            pl.BlockSpec((dims.ctx_len, B_KV_DIM), lambda s: (batch_of(s), 0)),
            pl.BlockSpec((B_KV_DIM, B_BLOCK), lambda s: (0, left(s))),
            pl.BlockSpec((B_KV_DIM, step_rows), lambda s: (0, off + s)),
            pl.BlockSpec((B_KV_DIM, B_BLOCK), lambda s: (0, right(s))),
            pl.BlockSpec((B_KV_DIM, dims.ctx_len), lambda s: (0, batch_of(s))),
        ],
        out_specs=pl.BlockSpec((step_rows, D_MODEL), lambda s: (s, 0)),
        scratch_shapes=[pltpu.VMEM((step_blocks, B_KV_HEADS, n_tiles, B_BLOCK, B_GROUP * B_BLOCK), _F32)],
        compiler_params=_cparams("arbitrary"),
        name="b_attn",
    )(sinks, *_window_bias(), qt, k, k, k, k, vt, vt, vt, vt)


def _b_attn(sinks, qt, k, vt, dims):
    lat0 = dims.n_ctx_rows // B_BLOCK
    nblk = k.shape[0] // B_BLOCK
    y_ctx = _b_attn_call(sinks, qt, k, vt, dims, B_STEP_BLOCKS_CTX, 0, lat0)
    y_lat = _b_attn_call(sinks, qt, k, vt, dims, B_STEP_BLOCKS_LAT, lat0, nblk - lat0)
    return y_lat, y_ctx


def _c_proj_kernel(h_ref, g_ref, mod_ref, win_hbm, bg_ref, u_ref, w_ref, stage_ref, sem_ref, *, layer):
    @pl.when(pl.program_id(0) == 0)
    def _():
        def store(c, width, tile):
            w_ref[:, c:c + width] = tile.astype(_BF16)

        _stage_weights(_col_tiles(win_hbm, layer, 0, 3 * D_MODEL, store), stage_ref, sem_ref)

    x = _norm_mod(h_ref[...], g_ref[...], _mod_part(mod_ref, 0), _mod_part(mod_ref, 1)).astype(_BF16)
    y = jnp.dot(x, w_ref[...], preferred_element_type=_F32)
    bg_ref[...] = y[:, :D_MODEL]
    u_ref[...] = y[:, D_MODEL:2 * D_MODEL] * y[:, 2 * D_MODEL:]


def _c_proj(h, g, mod_l, w_in, layer, dims):
    rows = h.shape[0]
    tm = ROW_TILE
    return pl.pallas_call(
        functools.partial(_c_proj_kernel, layer=layer),
        out_shape=(jax.ShapeDtypeStruct((rows, D_MODEL), _F32),) * 2,
        grid=(rows // tm,),
        in_specs=[
            _row_spec(tm, D_MODEL),
            _resident((1, D_MODEL)),
            _mod_spec(dims.n_ctx_rows, dims.seq, tm),
            _weight_spec(),
        ],
        out_specs=(_row_spec(tm, D_MODEL),) * 2,
        scratch_shapes=[pltpu.VMEM((D_MODEL, 3 * D_MODEL), _BF16)] + _stage_scratch(),
        compiler_params=_cparams("arbitrary"),
        name="c_proj",
    )(h, g, mod_l, w_in)


def _c_mix(bg_ref, u_ref, up_ref, un_ref, cw_ref, cb_ref, tile, dims):
    tm = u_ref.shape[0]
    u = u_ref[...]
    row = lax.broadcasted_iota(jnp.int32, (tm, 1), 0)
    g_row = tile * tm + row
    in_ctx = g_row < dims.n_ctx_rows
    pos = jnp.where(in_ctx, g_row % dims.ctx_len, (g_row - dims.n_ctx_rows) % dims.seq)
    length = jnp.where(in_ctx, dims.ctx_len, dims.seq)
    prev = jnp.where(row == 0, up_ref[SUBLANES - 1:SUBLANES, :], pltpu.roll(u, 1, axis=0))
    nxt = jnp.where(row == tm - 1, un_ref[0:1, :], pltpu.roll(u, tm - 1, axis=0))
    prev = jnp.where(pos == 0, 0.0, prev)
    nxt = jnp.where(pos == length - 1, 0.0, nxt)
    conv = prev * cw_ref[0:1, :] + u * cw_ref[1:2, :] + nxt * cw_ref[2:3, :] + cb_ref[...]
    return (bg_ref[...] * conv).astype(_BF16)


_N_MIX_REFS = (4, 2, 6)


def _post_kernel(*refs, kind, n_h, tile0, final_norm, dims, out_layer, layer):
    n_mix = _N_MIX_REFS[kind]
    h_refs, mod_ref = refs[:n_h], refs[n_h]
    mix_refs = refs[n_h + 1:n_h + 1 + n_mix]
    (wo_hbm, g_ref, w1_hbm, w2_hbm, fg_ref, out_ref,
     wo_ref, w1_ref, w2_ref, stage_ref, sem_ref) = refs[n_h + 1 + n_mix:]
    n_chunks = MLP_HIDDEN // MLP_HIDDEN_CHUNK
    tiles_wo = D_MODEL // STAGE_COLS
    tiles_w1 = MLP_HIDDEN_CHUNK // STAGE_COLS
    tiles_w2 = D_MODEL // STAGE_COLS

    def compute(need):
        if kind == 0:
            y = _a_mix(*mix_refs)
        elif kind == 1:
            y = _load_h(mix_refs)
        else:
            y = _c_mix(*mix_refs, pl.program_id(0) + tile0, dims)
        h = _load_h(h_refs)
        need(tiles_wo)
        h = h + _mod_part(mod_ref, 2) * jnp.dot(y, wo_ref[...], preferred_element_type=_F32)
        x = _norm_mod(h, g_ref[...], _mod_part(mod_ref, 3), _mod_part(mod_ref, 4)).astype(_BF16)
        acc = jnp.zeros(h.shape, _F32)
        for c in range(n_chunks):
            cols = slice(c * MLP_HIDDEN_CHUNK, (c + 1) * MLP_HIDDEN_CHUNK)
            need(tiles_wo + c * (tiles_w1 + tiles_w2) + tiles_w1)
            u = jnp.dot(x, w1_ref[:, cols], preferred_element_type=_F32)
            u = jnp.square(jnp.maximum(u, 0.0)).astype(_BF16)
            need(tiles_wo + (c + 1) * (tiles_w1 + tiles_w2))
            acc = acc + jnp.dot(u, w2_ref[cols, :], preferred_element_type=_F32)
        out = h + _mod_part(mod_ref, 5) * acc
        if final_norm:
            ms = jnp.mean(out * out, axis=-1, keepdims=True)
            out = out * lax.rsqrt(ms + NORM_EPS) * fg_ref[...]
        out_ref[...] = out

    @pl.when(pl.program_id(0) == 0)
    def _():
        def store_to(dst_ref, row0, c, width, tile):
            dst_ref[row0:row0 + tile.shape[0], c:c + width] = tile.astype(_BF16)

        jobs = _col_tiles(wo_hbm, out_layer, 0, D_MODEL, functools.partial(store_to, wo_ref, 0))
        for r0 in range(0, MLP_HIDDEN, MLP_HIDDEN_CHUNK):
            jobs += _col_tiles(w1_hbm, layer, r0, MLP_HIDDEN_CHUNK,
                               lambda c, width, tile, r0=r0: store_to(w1_ref, 0, r0 + c, width, tile))
            for c in range(0, D_MODEL, STAGE_COLS):
                jobs.append((w2_hbm.at[layer, pl.ds(r0, MLP_HIDDEN_CHUNK), pl.ds(c, STAGE_COLS)],
                             functools.partial(store_to, w2_ref, r0, c, STAGE_COLS)))
        compute(_WeightStager(jobs, stage_ref, sem_ref).need)

    @pl.when(pl.program_id(0) != 0)
    def _():
        compute(lambda n: None)


def _post(kind, h_parts, mod_l, mix, w_out, out_layer, g, w1, w2, layer, final_g, dims, tile0=0,
          final_norm=False):
    tm = ROW_TILE
    n_tiles = dims.rows // tm - tile0
    row = functools.partial(_row_spec, tm, D_MODEL, tile0)
    if kind == 0:
        mix_specs = [row(), row(), row(), _resident((1, D_MODEL))]
    elif kind == 1:
        assert tile0 == 0
        mix_specs = _h_specs(tm, True)
    else:
        bg, u, conv_w, conv_b = mix
        per = tm // SUBLANES
        last = dims.rows // SUBLANES - 1
        mix = (bg, u, u, u, conv_w, conv_b)
        mix_specs = [
            row(), row(),
            pl.BlockSpec((SUBLANES, D_MODEL), lambda i: (jnp.maximum((i + tile0) * per - 1, 0), 0)),
            pl.BlockSpec((SUBLANES, D_MODEL), lambda i: (jnp.minimum((i + tile0 + 1) * per, last), 0)),
            _resident(conv_w.shape), _resident((1, D_MODEL)),
        ]
    assert len(h_parts) == 1 or tile0 == 0
    h_specs = _h_specs(tm, True) if len(h_parts) == 2 else [row()]
    return pl.pallas_call(
        functools.partial(_post_kernel, kind=kind, n_h=len(h_parts), tile0=tile0,
                          final_norm=final_norm, dims=dims, out_layer=out_layer, layer=layer),
        out_shape=jax.ShapeDtypeStruct((n_tiles * tm, D_MODEL), _F32),
        grid=(n_tiles,),
        in_specs=h_specs + [_mod_spec(dims.n_ctx_rows, dims.seq, tm, tile0)] + mix_specs + [
            _weight_spec(),
            _resident((1, D_MODEL)),
            _weight_spec(),
            _weight_spec(),
            _resident((1, D_MODEL)),
        ],
        out_specs=_row_spec(tm, D_MODEL),
        scratch_shapes=[pltpu.VMEM((D_MODEL, D_MODEL), _BF16), pltpu.VMEM((D_MODEL, MLP_HIDDEN), _BF16),
                        pltpu.VMEM((MLP_HIDDEN, D_MODEL), _BF16)] + _stage_scratch(),
        compiler_params=_cparams("arbitrary"),
        name="post",
    )(*h_parts, mod_l, *mix, w_out, g, w1, w2, final_g)


class _Dims:
    def __init__(self, batch, seq, ctx_len):
        self.batch = batch
        self.seq = seq
        self.ctx_len = ctx_len
        self.n_ctx_rows = batch * ctx_len
        self.rows = self.n_ctx_rows + batch * seq


def _rope_tables(dims):
    n_freq = B_HEAD_DIM // 4
    t = np.arange(dims.seq)
    inv_freq = ROPE_BASE ** (-np.arange(n_freq, dtype=np.float64) / n_freq)
    ang_row = (t // GRID_W)[:, None] * inv_freq
    ang_col = (t % GRID_W)[:, None] * inv_freq
    ang = np.concatenate([ang_row, ang_row, ang_col, ang_col], axis=1)
    sign = np.tile(np.concatenate([-np.ones(n_freq), np.ones(n_freq)]), 2)
    cos = np.tile(np.cos(ang), (dims.batch, 1))
    sin = np.tile(np.sin(ang) * sign, (dims.batch, 1))
    cos = np.concatenate([np.ones((dims.n_ctx_rows, B_HEAD_DIM)), cos], axis=0).astype(np.float32)
    sin = np.concatenate([np.zeros((dims.n_ctx_rows, B_HEAD_DIM)), sin], axis=0).astype(np.float32)
    tables = (np.tile(cos, (1, 2)), np.tile(sin, (1, 2)), np.ascontiguousarray(cos.T), np.ascontiguousarray(sin.T))
    return tuple(jnp.asarray(tab) for tab in tables)


def kernel(x, c, ctx, c_ctx, ada_w, ada_b, norm_g, final_g, mlp_w1, mlp_w2,
           a_w_in, a_w_gate, a_b_gate, a_head_g, a_w_out,
           b_w_qkv, b_sinks, b_w_out, c_w_in, c_conv_w, c_conv_b, c_w_out):
    batch, seq, d = x.shape
    ctx_len = ctx.shape[1]
    depth = ada_w.shape[0]
    dims = _Dims(batch, seq, ctx_len)
    assert MLP_HIDDEN_CHUNK == D_MODEL and d == D_MODEL and seq % ROW_TILE == 0 and dims.n_ctx_rows % ROW_TILE == 0
    assert ctx_len % (A_STEP_CHUNKS * A_CHUNK) == 0 and 1 + batch <= MOD_ROWS

    assert dims.n_ctx_rows == ROW_TILE
    h_parts = (x.reshape(-1, d), ctx.reshape(-1, d))
    cs = jnp.concatenate([c_ctx[None], c, jnp.zeros((MOD_ROWS - 1 - batch, d), _F32)], axis=0)
    mod = _ada_table(cs, ada_w, ada_b).reshape(depth, MOD_ROWS, 1, 6 * d)
    rope = _rope_tables(dims)
    fg = final_g.reshape(1, d)

    for l in range(depth):
        kind, j = l % N_MIXERS, l // N_MIXERS
        mod_l = mod[l]
        g0 = norm_g[l, 0].reshape(1, d)
        g1 = norm_g[l, 1].reshape(1, d)
        last_layer = l == depth - 1
        if kind == 0:
            wgt = a_w_gate[j].T.astype(_BF16)
            q, kt, v, o, gr, gc = _a_proj(h_parts, g0, mod_l, a_w_in, j, wgt, a_b_gate[j].reshape(-1, 1), dims)
            hf, hb = _a_scan(q, kt, v, gr, gc, dims)
            mix = (hf, hb, o, a_head_g[j].reshape(1, d))
            w_out = a_w_out
        elif kind == 1:
            (h,) = h_parts
            qt, k, vt = _b_proj(h, g0, mod_l, b_w_qkv, j, rope, dims)
            mix = _b_attn(b_sinks[j], qt, k, vt, dims)
            w_out = b_w_out
        else:
            (h,) = h_parts
            bg, u = _c_proj(h, g0, mod_l, c_w_in, j, dims)
            mix = (bg, u, c_conv_w[j], c_conv_b[j].reshape(1, d))
            w_out = c_w_out
        tile0 = dims.n_ctx_rows // ROW_TILE if last_layer else 0
        h_parts = (_post(kind, h_parts, mod_l, mix, w_out, j, g1, mlp_w1, mlp_w2, l, fg, dims,
                         tile0=tile0, final_norm=last_layer),)
    return h_parts[0].reshape(batch, seq, d)
```

```python
import functools

import jax
import jax.numpy as jnp
import numpy as np
from jax import lax
from jax.experimental import pallas as pl
from jax.experimental.pallas import tpu as pltpu

D_MODEL = 1024
GRID_W = 64
N_MIXERS = 3
A_HEADS = 8
A_QK_DIM = D_MODEL // 2
A_DK = A_QK_DIM // A_HEADS
A_DV = D_MODEL // A_HEADS
A_CHUNK = 128
A_STEP_CHUNKS_CTX = 2
A_STEP_CHUNKS_LAT = 4
B_Q_HEADS = 16
B_KV_HEADS = 4
B_HEAD_DIM = D_MODEL // B_Q_HEADS
B_GROUP = B_Q_HEADS // B_KV_HEADS
B_KV_DIM = B_KV_HEADS * B_HEAD_DIM
B_BLOCK = 128
ROPE_BASE = 10000.0
MLP_HIDDEN = 4 * D_MODEL
NORM_EPS = 1e-6
LOG2E = 1.4426950408889634

LANES = 128
SUBLANES = 8
VMEM_LIMIT_BYTES = 60 * 1024 * 1024

ROW_TILE = 512
MLP_HIDDEN_CHUNK = 1024
MOD_ROWS = 8
ADA_COL_TILE = 1536

_BF16 = jnp.bfloat16
_F32 = jnp.float32


def _cparams(*sem):
    return pltpu.CompilerParams(dimension_semantics=sem, vmem_limit_bytes=VMEM_LIMIT_BYTES)


def _resident(shape):
    nd = len(shape)
    return pl.BlockSpec(shape, lambda *_: (0,) * nd, pipeline_mode=pl.Buffered(1))


STAGE_COLS = 512
STAGE_DEPTH = 3


def _weight_spec():
    return pl.BlockSpec(memory_space=pl.ANY)


def _stage_scratch():
    return [pltpu.VMEM((STAGE_DEPTH, D_MODEL, STAGE_COLS), _F32), pltpu.SemaphoreType.DMA((STAGE_DEPTH,))]


class _WeightStager:
    def __init__(self, jobs, stage_ref, sem_ref):
        self.jobs, self.stage_ref, self.sem_ref = jobs, stage_ref, sem_ref
        self.started = self.done = 0

    def _copy(self, i):
        src = self.jobs[i][0]
        dst = self.stage_ref.at[i % STAGE_DEPTH, pl.ds(0, src.shape[0]), pl.ds(0, src.shape[1])]
        return pltpu.make_async_copy(src, dst, self.sem_ref.at[i % STAGE_DEPTH])

    def _fill(self):
        while self.started < min(self.done + STAGE_DEPTH, len(self.jobs)):
            self._copy(self.started).start()
            self.started += 1

    def need(self, n):
        self._fill()
        while self.done < n:
            src, store = self.jobs[self.done]
            self._copy(self.done).wait()
            store(self.stage_ref[self.done % STAGE_DEPTH, :src.shape[0], :src.shape[1]])
            self.done += 1
            self._fill()


def _stage_weights(jobs, stage_ref, sem_ref):
    _WeightStager(jobs, stage_ref, sem_ref).need(len(jobs))


def _col_tiles(w_hbm, layer, col0, ncols, store):
    jobs = []
    for c in range(0, ncols, STAGE_COLS):
        width = min(STAGE_COLS, ncols - c)
        jobs.append((w_hbm.at[layer, :, pl.ds(col0 + c, width)], functools.partial(store, c, width)))
    return jobs


def _norm_mod(h, g, shift, scale):
    ms = jnp.mean(h * h, axis=-1, keepdims=True)
    y = h * lax.rsqrt(ms + NORM_EPS) * g
    return y * (1.0 + scale) + shift


def _mod_part(mod_ref, k):
    return mod_ref[:, k * D_MODEL:(k + 1) * D_MODEL]


def _mod_spec(n_ctx_rows, seq, tm, tile0=0):
    def idx(i):
        r0 = (i + tile0) * tm
        grp = jnp.where(r0 < n_ctx_rows, 0, 1 + (r0 - n_ctx_rows) // seq)
        return (grp, 0, 0)
    return pl.BlockSpec((None, 1, 6 * D_MODEL), idx)


def _row_spec(tm, width, tile0=0):
    return pl.BlockSpec((tm, width), lambda i: (i + tile0, 0))


def _h_specs(tm, split, tile0=0):
    if not split:
        return [_row_spec(tm, D_MODEL, tile0)]
    return [pl.BlockSpec((tm, D_MODEL), lambda i: (jnp.maximum(i + tile0 - 1, 0), 0)),
            _resident((tm, D_MODEL))]


def _load_h(h_refs, tile0=0):
    if len(h_refs) == 1:
        return h_refs[0][...]
    lat_ref, ctx_ref = h_refs
    if tile0 > 0:
        return lat_ref[...]
    return jnp.where(pl.program_id(0) == 0, ctx_ref[...], lat_ref[...])


def _ada_kernel(cs_ref, w_ref, b_ref, o_ref):
    cs = cs_ref[...]
    s = (cs * jax.nn.sigmoid(cs)).astype(_BF16)
    o_ref[...] = jnp.dot(s, w_ref[...].astype(_BF16), preferred_element_type=_F32) + b_ref[...]


def _ada_table(cs, ada_w, ada_b):
    depth = ada_w.shape[0]
    n = ada_w.shape[2]
    return pl.pallas_call(
        _ada_kernel,
        out_shape=jax.ShapeDtypeStruct((depth, MOD_ROWS, n), _F32),
        grid=(depth, n // ADA_COL_TILE),
        in_specs=[
            pl.BlockSpec((MOD_ROWS, D_MODEL), lambda l, j: (0, 0)),
            pl.BlockSpec((None, D_MODEL, ADA_COL_TILE), lambda l, j: (l, 0, j)),
            pl.BlockSpec((None, 1, ADA_COL_TILE), lambda l, j: (l, 0, j)),
        ],
        out_specs=pl.BlockSpec((None, MOD_ROWS, ADA_COL_TILE), lambda l, j: (l, 0, j)),
        compiler_params=_cparams("arbitrary", "arbitrary"),
        name="ada_table",
    )(cs, ada_w, ada_b.reshape(depth, 1, n))


def _a_proj_kernel(*refs, n_h, layer):
    h_refs, (g_ref, mod_ref, win_hbm, wgt_ref, bg_ref,
             q_ref, kt_ref, v_ref, o_ref, gr_ref, gc_ref,
             wqvo_ref, wkt_ref, stage_ref, sem_ref) = refs[:n_h], refs[n_h:]

    @pl.when(pl.program_id(0) == 0)
    def _():
        def store_qvo(dst0, c, width, tile):
            wqvo_ref[:, dst0 + c:dst0 + c + width] = tile.astype(_BF16)

        def store_kt(c, width, tile):
            wkt_ref[c:c + width, :] = tile.T.astype(_BF16)

        _stage_weights(
            _col_tiles(win_hbm, layer, 0, A_QK_DIM, functools.partial(store_qvo, 0))
            + _col_tiles(win_hbm, layer, A_QK_DIM, A_QK_DIM, store_kt)
            + _col_tiles(win_hbm, layer, 2 * A_QK_DIM, 2 * D_MODEL, functools.partial(store_qvo, A_QK_DIM)),
            stage_ref, sem_ref)

    x = _norm_mod(_load_h(h_refs), g_ref[...], _mod_part(mod_ref, 0), _mod_part(mod_ref, 1)).astype(_BF16)
    nt = (((1,), (1,)), ((), ()))
    gt = lax.dot_general(wgt_ref[...], x, nt, preferred_element_type=_F32) + bg_ref[...]

    n_chunks = gt.shape[1] // A_CHUNK

    def by_chunk(rows):
        return jnp.concatenate([rows[:, c * A_CHUNK:(c + 1) * A_CHUNK] for c in range(n_chunks)], axis=0)

    def by_token(x):
        return jnp.concatenate([x[c * A_HEADS:(c + 1) * A_HEADS] for c in range(n_chunks)], axis=1)

    def log_sigmoid(z):
        return jnp.minimum(z, 0.0) - jnp.log1p(jnp.exp(-jnp.abs(z)))

    row_form, col_cm, col_b = [], [], []
    for d in range(2):
        li = by_chunk(gt[2 * d * A_HEADS:(2 * d + 1) * A_HEADS]) * LOG2E
        lf = log_sigmoid(by_chunk(gt[(2 * d + 1) * A_HEADS:(2 * d + 2) * A_HEADS])) * LOG2E
        b = _lane_scan(lf, jnp.add, 0.0, d == 1)
        a = li - b
        cm = _lane_scan(a, jnp.maximum, -jnp.inf, d == 1)
        row_form += [by_token(a), by_token(b)]
        col_cm.append(cm)
        col_b.append(b)
    gr_ref[...] = jnp.concatenate(row_form, axis=0)
    pad = jnp.zeros((LANES - 2 * A_HEADS, LANES), _F32)
    for c in range(n_chunks):
        rows = slice(c * A_HEADS, (c + 1) * A_HEADS)
        toks = slice(c * A_CHUNK, (c + 1) * A_CHUNK)
        gc_ref[toks, :LANES] = jnp.concatenate([col_cm[0][rows], col_cm[1][rows], pad], axis=0).T
        gc_ref[toks, LANES:] = jnp.concatenate([col_b[0][rows], col_b[1][rows], pad], axis=0).T

    y = jnp.dot(x, wqvo_ref[...], preferred_element_type=_F32)
    q_ref[...] = (y[:, :A_QK_DIM] * (A_DK ** -0.5)).astype(_BF16)
    v_ref[...] = y[:, A_QK_DIM:A_QK_DIM + D_MODEL].astype(_BF16)
    o_ref[...] = y[:, A_QK_DIM + D_MODEL:]
    kt_ref[...] = lax.dot_general(wkt_ref[...], x, nt, preferred_element_type=_F32).astype(_BF16)


def _a_proj(h_parts, g, mod_l, w_in, layer, wgt, bgate, dims):
    rows = dims.rows
    tm = ROW_TILE
    n_gate = 4 * A_HEADS
    return pl.pallas_call(
        functools.partial(_a_proj_kernel, n_h=len(h_parts), layer=layer),
        out_shape=(
            jax.ShapeDtypeStruct((rows, A_QK_DIM), _BF16),
            jax.ShapeDtypeStruct((A_QK_DIM, rows), _BF16),
            jax.ShapeDtypeStruct((rows, D_MODEL), _BF16),
            jax.ShapeDtypeStruct((rows, D_MODEL), _F32),
            jax.ShapeDtypeStruct((n_gate, rows), _F32),
            jax.ShapeDtypeStruct((rows, 2 * LANES), _F32),
        ),
        grid=(rows // tm,),
        in_specs=_h_specs(tm, len(h_parts) == 2) + [
            _resident((1, D_MODEL)),
            _mod_spec(dims.n_ctx_rows, dims.seq, tm),
            _weight_spec(),
            _resident(wgt.shape),
            _resident((n_gate, 1)),
        ],
        out_specs=(
            _row_spec(tm, A_QK_DIM),
            pl.BlockSpec((A_QK_DIM, tm), lambda i: (0, i)),
            _row_spec(tm, D_MODEL),
            _row_spec(tm, D_MODEL),
            pl.BlockSpec((n_gate, tm), lambda i: (0, i)),
            _row_spec(tm, 2 * LANES),
        ),
        scratch_shapes=[pltpu.VMEM((D_MODEL, A_QK_DIM + 2 * D_MODEL), _BF16),
                        pltpu.VMEM((A_QK_DIM, D_MODEL), _BF16)] + _stage_scratch(),
        compiler_params=_cparams("arbitrary"),
        name="a_proj",
    )(*h_parts, g, mod_l, w_in, wgt, bgate)


def _lane_scan(x, op, fill, reverse):
    lane = lax.broadcasted_iota(jnp.int32, x.shape, 1)
    k = 1
    while k < LANES:
        if reverse:
            shifted = jnp.where(lane < LANES - k, pltpu.roll(x, LANES - k, axis=1), fill)
        else:
            shifted = jnp.where(lane >= k, pltpu.roll(x, k, axis=1), fill)
        x = op(x, shifted)
        k *= 2
    return x


def _a_scan_kernel(*refs, step_chunks, has_init, emit_state):
    n_state = 5
    (qf_ref, ktf_ref, vf_ref, grf_ref, gcf_ref, qb_ref, ktb_ref, vb_ref, grb_ref, gcb_ref), refs = refs[:10], refs[10:]
    init_refs, refs = (refs[:n_state], refs[n_state:]) if has_init else ((), refs)
    (hf_ref, hb_ref), refs = refs[:2], refs[2:]
    final_refs, refs = (refs[:n_state], refs[n_state:]) if emit_state else ((), refs)
    cf_ref, cb_ref, mrf_ref, mrb_ref, mc_ref = state_refs = refs

    @pl.when(pl.program_id(1) == 0)
    def _():
        if has_init:
            for ref, init in zip(state_refs, init_refs):
                ref[...] = init[...]
        else:
            cf_ref[...] = jnp.zeros(cf_ref.shape, _F32)
            cb_ref[...] = jnp.zeros(cb_ref.shape, _F32)
            mrf_ref[...] = jnp.full(mrf_ref.shape, -jnp.inf, _F32)
            mrb_ref[...] = jnp.full(mrb_ref.shape, -jnp.inf, _F32)
            mc_ref[...] = jnp.full(mc_ref.shape, -jnp.inf, _F32)

    t_idx = lax.broadcasted_iota(jnp.int32, (A_CHUNK, A_CHUNK), 0)
    s_idx = lax.broadcasted_iota(jnp.int32, (A_CHUNK, A_CHUNK), 1)
    ones_blk = jnp.ones((A_CHUNK, A_DV), _BF16)
    lane = lax.broadcasted_iota(jnp.int32, (1, LANES), 1)
    for sub in range(step_chunks):
        toks = (slice(sub * A_CHUNK, (sub + 1) * A_CHUNK),
                slice((step_chunks - 1 - sub) * A_CHUNK, (step_chunks - sub) * A_CHUNK))
        _a_chunk_pair(toks, (qf_ref, qb_ref), (ktf_ref, ktb_ref), (vf_ref, vb_ref), (grf_ref, grb_ref),
                      (gcf_ref, gcb_ref), (hf_ref, hb_ref), (cf_ref, cb_ref), (mrf_ref, mrb_ref), mc_ref,
                      t_idx, s_idx, ones_blk, lane)

    if emit_state:
        @pl.when(pl.program_id(1) == pl.num_programs(1) - 1)
        def _():
            for ref, final in zip(state_refs, final_refs):
                final[...] = ref[...]


def _a_chunk_pair(toks, q_refs, kt_refs, v_refs, gr_refs, gc_refs, h_refs, c_refs, mr_refs, mc_ref,
                  t_idx, s_idx, ones_blk, lane):
    m_col = mc_ref[0:1, :]
    m_col_new = []
    dirs = []
    for d in range(2):
        gr_ref, gc_ref, mr_ref, tok = gr_refs[d], gc_refs[d], mr_refs[d], toks[d]
        last = A_CHUNK - 1 if d == 0 else 0
        a = gr_ref[2 * d * A_HEADS:(2 * d + 1) * A_HEADS, tok]
        b = gr_ref[(2 * d + 1) * A_HEADS:(2 * d + 2) * A_HEADS, tok]
        m_row = mr_ref[:, 0:1]
        gg_end = jnp.maximum(m_row, jnp.max(a, axis=1, keepdims=True))
        ws = jnp.exp2(a - gg_end)
        decay = jnp.exp2(m_row - gg_end)
        mr_ref[...] = jnp.broadcast_to(b[:, last:last + 1] + gg_end, mr_ref.shape)
        gg_c = jnp.maximum(m_col, gc_ref[tok, :LANES])
        b_plus_gg = gc_ref[tok, LANES:] + gg_c
        clamp_c = jnp.exp2(-b_plus_gg)
        m_col_new.append(b_plus_gg[last:last + 1, :])
        dirs.append((a, m_row, ws, decay, gg_c, clamp_c))
    mc_ref[...] = jnp.broadcast_to(jnp.where(lane < A_HEADS, m_col_new[0], m_col_new[1]), mc_ref.shape)

    pair_lane = lax.broadcasted_iota(jnp.int32, (A_CHUNK, 2 * A_DK), 1)
    zeros_kt = jnp.zeros((A_DK, A_CHUNK), _BF16)
    zeros_c = jnp.zeros((A_DK, 2 * A_DV), _BF16)
    for hp in range(A_HEADS // 2):
        for d in range(2):
            q_ref, kt_ref, v_ref, h_ref, c_ref, tok = q_refs[d], kt_refs[d], v_refs[d], h_refs[d], c_refs[d], toks[d]
            a, m_row, ws, decay, gg_c, clamp_c = dirs[d]
            mask = (s_idx <= t_idx) if d == 0 else (s_idx >= t_idx)
            heads = (2 * hp, 2 * hp + 1)
            q2 = q_ref[tok, 2 * hp * A_DK:(2 * hp + 2) * A_DK]
            kts = [kt_ref[hd * A_DK:(hd + 1) * A_DK, tok] for hd in heads]
            kt_bd = jnp.concatenate([jnp.concatenate([kts[0], zeros_kt], axis=1),
                                     jnp.concatenate([zeros_kt, kts[1]], axis=1)], axis=0)
            s2 = jnp.dot(q2, kt_bd, preferred_element_type=_F32)
            ggs = [jnp.broadcast_to(gg_c[:, d * A_HEADS + hd:d * A_HEADS + hd + 1], (A_CHUNK, A_CHUNK))
                   for hd in heads]
            e_state = jnp.exp2(jnp.where(pair_lane < A_DK, m_row[heads[0]:heads[0] + 1, :] - ggs[0],
                                         m_row[heads[1]:heads[1] + 1, :] - ggs[1]))
            qe = q2.astype(_F32) * e_state
            for i, hd in enumerate(heads):
                col = d * A_HEADS + hd
                clamp = jnp.broadcast_to(clamp_c[:, col:col + 1], (A_CHUNK, A_DV))
                v = v_ref[tok, hd * A_DV:(hd + 1) * A_DV]
                v_ext = jnp.concatenate([v, ones_blk], axis=1)
                c_old = c_ref[hd]
                e_keys = jnp.exp2(jnp.where(mask, a[hd:hd + 1, :] - ggs[i], -jnp.inf))
                p = jnp.concatenate([s2[:, i * A_CHUNK:(i + 1) * A_CHUNK] * e_keys, qe], axis=1).astype(_BF16)
                c_rows = [c_old.astype(_BF16), zeros_c] if i == 0 else [zeros_c, c_old.astype(_BF16)]
                rhs = jnp.concatenate([v_ext] + c_rows, axis=0)
                num = jnp.dot(p, rhs, preferred_element_type=_F32)
                den = jnp.maximum(jnp.abs(num[:, A_DV:]), clamp)
                h_ref[tok, hd * A_DV:(hd + 1) * A_DV] = num[:, :A_DV] / den
                kw = (kts[i].astype(_F32) * ws[hd:hd + 1, :]).astype(_BF16)
                c_ref[hd] = decay[hd:hd + 1, :] * c_old + jnp.dot(kw, v_ext, preferred_element_type=_F32)


def _a_scan_call(q, kt, v, gr, gc, dims, step_chunks, first_row, seq_rows, init=None, emit_state=False):
    blk_rows = step_chunks * A_CHUNK
    n_steps = seq_rows // blk_rows
    assert seq_rows % blk_rows == 0 and first_row % blk_rows == 0
    blk0 = first_row // blk_rows

    def fwd_blk(b, j):
        return blk0 + b * n_steps + j

    def bwd_blk(b, j):
        return blk0 + b * n_steps + (n_steps - 1 - j)

    def specs(blk):
        return [
            pl.BlockSpec((blk_rows, A_QK_DIM), lambda b, j: (blk(b, j), 0)),
            pl.BlockSpec((A_QK_DIM, blk_rows), lambda b, j: (0, blk(b, j))),
            pl.BlockSpec((blk_rows, D_MODEL), lambda b, j: (blk(b, j), 0)),
            pl.BlockSpec((4 * A_HEADS, blk_rows), lambda b, j: (0, blk(b, j))),
            pl.BlockSpec((blk_rows, 2 * LANES), lambda b, j: (blk(b, j), 0)),
        ]

    state_shapes = [(A_HEADS, A_DK, 2 * A_DV)] * 2 + [(A_HEADS, LANES)] * 3
    state_specs = [pl.BlockSpec((None,) + shp, lambda b, j, n=len(shp): (b,) + (0,) * n) for shp in state_shapes]
    h_shape = jax.ShapeDtypeStruct((dims.batch * seq_rows, D_MODEL), _F32)
    out_shape = [h_shape, h_shape]
    out_specs = [pl.BlockSpec((blk_rows, D_MODEL), lambda b, j: (b * n_steps + j, 0)),
                 pl.BlockSpec((blk_rows, D_MODEL), lambda b, j: (b * n_steps + (n_steps - 1 - j), 0))]
    if emit_state:
        out_shape += [jax.ShapeDtypeStruct((dims.batch,) + shp, _F32) for shp in state_shapes]
        out_specs += state_specs
    return pl.pallas_call(
        functools.partial(_a_scan_kernel, step_chunks=step_chunks, has_init=init is not None,
                          emit_state=emit_state),
        out_shape=tuple(out_shape),
        grid=(dims.batch, n_steps),
        in_specs=specs(fwd_blk) + specs(bwd_blk) + (state_specs if init is not None else []),
        out_specs=tuple(out_specs),
        scratch_shapes=[pltpu.VMEM(shp, _F32) for shp in state_shapes],
        compiler_params=_cparams("arbitrary", "arbitrary"),
        name="a_scan",
    )(q, kt, v, gr, gc, q, kt, v, gr, gc, *(init or ()))


def _a_scan(q, kt, v, gr, gc, dims):
    hf_c, hb_c, *state = _a_scan_call(q, kt, v, gr, gc, dims, A_STEP_CHUNKS_CTX, 0, dims.ctx_len,
                                      emit_state=True)
    hf_l, hb_l = _a_scan_call(q, kt, v, gr, gc, dims, A_STEP_CHUNKS_LAT, dims.n_ctx_rows, dims.seq,
                              init=state)
    return (hf_l, hf_c), (hb_l, hb_c)


def _a_mix(hf_lat_ref, hf_ctx_ref, hb_lat_ref, hb_ctx_ref, o_ref, hg_ref, tile0):
    hs = _load_h((hf_lat_ref, hf_ctx_ref), tile0) + _load_h((hb_lat_ref, hb_ctx_ref), tile0)
    parts = []
    for hd in range(A_HEADS):
        x = hs[:, hd * A_DV:(hd + 1) * A_DV]
        parts.append(x * lax.rsqrt(jnp.mean(x * x, axis=-1, keepdims=True) + NORM_EPS))
    y = jnp.concatenate(parts, axis=1) * hg_ref[...]
    return (jax.nn.sigmoid(o_ref[...]) * y).astype(_BF16)


ROPE_HALF = B_HEAD_DIM // 4


def _swap_halves_lanes(x):
    lane = lax.broadcasted_iota(jnp.int32, x.shape, 1)
    fwd = pltpu.roll(x, LANES - ROPE_HALF, axis=1)
    back = pltpu.roll(x, ROPE_HALF, axis=1)
    return jnp.where(lane % (2 * ROPE_HALF) < ROPE_HALF, fwd, back)


def _swap_halves_rows(x):
    parts = []
    for r0 in range(0, x.shape[0], 2 * ROPE_HALF):
        parts += [x[r0 + ROPE_HALF:r0 + 2 * ROPE_HALF], x[r0:r0 + ROPE_HALF]]
    return jnp.concatenate(parts, axis=0)


def _b_proj_kernel(h_ref, g_ref, mod_ref, wqkv_hbm, cos_ref, sin_ref, cost_ref, sint_ref,
                   qt_ref, k_ref, vt_ref, wqvt_ref, wk_ref, stage_ref, sem_ref, *, layer):
    @pl.when(pl.program_id(0) == 0)
    def _():
        def store_t(dst0, c, width, tile):
            wqvt_ref[dst0 + c:dst0 + c + width, :] = tile.T.astype(_BF16)

        def store_k(c, width, tile):
            wk_ref[:, c:c + width] = tile.astype(_BF16)

        _stage_weights(
            _col_tiles(wqkv_hbm, layer, 0, D_MODEL, functools.partial(store_t, 0))
            + _col_tiles(wqkv_hbm, layer, D_MODEL, B_KV_DIM, store_k)
            + _col_tiles(wqkv_hbm, layer, D_MODEL + B_KV_DIM, B_KV_DIM, functools.partial(store_t, D_MODEL)),
            stage_ref, sem_ref)

    x = _norm_mod(h_ref[...], g_ref[...], _mod_part(mod_ref, 0), _mod_part(mod_ref, 1)).astype(_BF16)
    nt = (((1,), (1,)), ((), ()))
    yt = lax.dot_general(wqvt_ref[...], x, nt, preferred_element_type=_F32)
    cost, sint = cost_ref[...], sint_ref[...]
    for hd in range(B_Q_HEADS):
        rows = slice(hd * B_HEAD_DIM, (hd + 1) * B_HEAD_DIM)
        qt = yt[rows, :]
        qt = (qt * cost + _swap_halves_rows(qt) * sint) * (B_HEAD_DIM ** -0.5 * LOG2E)
        qt_ref[rows, :] = qt.astype(_BF16)
    vt_ref[...] = yt[D_MODEL:, :].astype(_BF16)
    cos, sin = cos_ref[...], sin_ref[...]
    k = jnp.dot(x, wk_ref[...], preferred_element_type=_F32)
    for c0 in range(0, B_KV_DIM, LANES):
        kc = k[:, c0:c0 + LANES]
        k_ref[:, c0:c0 + LANES] = (kc * cos + _swap_halves_lanes(kc) * sin).astype(_BF16)


def _b_proj(h, g, mod_l, w_qkv, layer, rope, dims):
    rows = h.shape[0]
    tm = ROW_TILE
    cos, sin, cost, sint = rope
    return pl.pallas_call(
        functools.partial(_b_proj_kernel, layer=layer),
        out_shape=(
            jax.ShapeDtypeStruct((D_MODEL, rows), _BF16),
            jax.ShapeDtypeStruct((rows, B_KV_DIM), _BF16),
            jax.ShapeDtypeStruct((B_KV_DIM, rows), _BF16),
        ),
        grid=(rows // tm,),
        in_specs=[
            _row_spec(tm, D_MODEL),
            _resident((1, D_MODEL)),
            _mod_spec(dims.n_ctx_rows, dims.seq, tm),
            _weight_spec(),
            _row_spec(tm, LANES),
            _row_spec(tm, LANES),
            pl.BlockSpec((B_HEAD_DIM, tm), lambda i: (0, i)),
            pl.BlockSpec((B_HEAD_DIM, tm), lambda i: (0, i)),
        ],
        out_specs=(
            pl.BlockSpec((D_MODEL, tm), lambda i: (0, i)),
            _row_spec(tm, B_KV_DIM),
            pl.BlockSpec((B_KV_DIM, tm), lambda i: (0, i)),
        ),
        scratch_shapes=[pltpu.VMEM((D_MODEL + B_KV_DIM, D_MODEL), _BF16),
                        pltpu.VMEM((D_MODEL, B_KV_DIM), _BF16)] + _stage_scratch(),
        compiler_params=_cparams("arbitrary"),
        name="b_proj",
    )(h, g, mod_l, w_qkv, cos, sin, cost, sint)


B_ONES_ROWS = 16
B_STEP_BLOCKS_CTX = 2
B_STEP_BLOCKS_LAT = 4


def _b_attn_kernel(sink_ref, win_l_ref, win_r_ref, qt_ref, kl_ref, kc_ref, kr_ref, kx_ref,
                   vtl_ref, vtc_ref, vtr_ref, vtx_ref, o_ref, s_ref,
                   *, n_ctx_blocks, blocks_per_seq, ctx_len, step_blocks, first_block):
    first = first_block + pl.program_id(0) * step_blocks
    is_lat = first >= n_ctx_blocks
    n = (first - n_ctx_blocks) % blocks_per_seq
    neg = -jnp.inf
    win_l = jnp.where(is_lat, win_l_ref[...], neg)
    win_r = jnp.where(is_lat, win_r_ref[...], neg)
    bias_c = jnp.where(is_lat, 0.0, neg)
    bias_first_l = jnp.where(jnp.logical_and(is_lat, n >= 1), win_l_ref[...], neg)
    bias_last_r = jnp.where(jnp.logical_and(is_lat, n + step_blocks <= blocks_per_seq - 1), win_r_ref[...], neg)

    n_q = B_GROUP * B_BLOCK
    head_of_lane = lax.broadcasted_iota(jnp.int32, (1, n_q), 1) // B_BLOCK
    ones_rows = jnp.ones((B_ONES_ROWS, B_BLOCK), _BF16)
    n_ctx_tiles = ctx_len // B_BLOCK

    def local_tiles(blk):
        own = slice(blk * B_BLOCK, (blk + 1) * B_BLOCK)
        if blk == 0:
            left = (kl_ref, vtl_ref, slice(0, B_BLOCK), bias_first_l)
        else:
            left = (kc_ref, vtc_ref, slice((blk - 1) * B_BLOCK, blk * B_BLOCK), win_l)
        if blk == step_blocks - 1:
            right = (kr_ref, vtr_ref, slice(0, B_BLOCK), bias_last_r)
        else:
            right = (kc_ref, vtc_ref, slice((blk + 1) * B_BLOCK, (blk + 2) * B_BLOCK), win_r)
        return [left, (kc_ref, vtc_ref, own, bias_c), right]

    def scores(blk, g):
        cols = slice(blk * B_BLOCK, (blk + 1) * B_BLOCK)
        qt = jnp.concatenate(
            [qt_ref[(g * B_GROUP + j) * B_HEAD_DIM:(g * B_GROUP + j + 1) * B_HEAD_DIM, cols]
             for j in range(B_GROUP)], axis=1)
        sink = jnp.zeros((1, n_q), _F32)
        for j in range(B_GROUP):
            sink = jnp.where(head_of_lane == j, sink_ref[g * B_GROUP + j] * LOG2E, sink)
        ks = slice(g * B_HEAD_DIM, (g + 1) * B_HEAD_DIM)
        k_tiles = [(k_ref[rows, ks], bias) for k_ref, _, rows, bias in local_tiles(blk)]
        for t in range(n_ctx_tiles):
            k_tiles.append((kx_ref[t * B_BLOCK:(t + 1) * B_BLOCK, ks], None))
        m_tile = None
        for t, (k, bias) in enumerate(k_tiles):
            s = jnp.dot(k, qt, preferred_element_type=_F32)
            if bias is not None:
                s = s + bias
            s_ref[blk, g, t] = s
            m_tile = s if m_tile is None else jnp.maximum(m_tile, s)
        return jnp.maximum(sink, jnp.max(m_tile, axis=0, keepdims=True)), sink

    def attend(blk, g, m, sink):
        ks = slice(g * B_HEAD_DIM, (g + 1) * B_HEAD_DIM)
        vt_tiles = [vt_ref[ks, cols] for _, vt_ref, cols, _ in local_tiles(blk)]
        for t in range(n_ctx_tiles):
            vt_tiles.append(vtx_ref[ks, t * B_BLOCK:(t + 1) * B_BLOCK])
        acc = jnp.zeros((B_HEAD_DIM + B_ONES_ROWS, n_q), _F32)
        for t, vt in enumerate(vt_tiles):
            p = jnp.exp2(s_ref[blk, g, t] - m).astype(_BF16)
            vt_ext = jnp.concatenate([vt, ones_rows], axis=0)
            acc = acc + jnp.dot(vt_ext, p, preferred_element_type=_F32)
        denom = jnp.exp2(sink - m) + acc[B_HEAD_DIM:B_HEAD_DIM + 1, :]
        out_t = acc[:B_HEAD_DIM, :] * (1.0 / denom)
        rows = slice(blk * B_BLOCK, (blk + 1) * B_BLOCK)
        for pair in range(B_GROUP // 2):
            two = jnp.concatenate([out_t[:, (2 * pair) * B_BLOCK:(2 * pair + 1) * B_BLOCK],
                                   out_t[:, (2 * pair + 1) * B_BLOCK:(2 * pair + 2) * B_BLOCK]], axis=0)
            c0 = (g * B_GROUP + 2 * pair) * B_HEAD_DIM
            o_ref[rows, c0:c0 + 2 * B_HEAD_DIM] = two.T.astype(_BF16)

    units = [(blk, g) for blk in range(step_blocks) for g in range(B_KV_HEADS)]
    pending = [scores(*units[0]), scores(*units[1])]
    for i, unit in enumerate(units):
        if i + 2 < len(units):
            pending.append(scores(*units[i + 2]))
        attend(*unit, *pending[i])


def _window_bias():
    key = np.arange(B_BLOCK)[:, None]
    qry = np.arange(B_BLOCK)[None, :]
    prev_blk = np.where(key >= qry, 0.0, -np.inf).astype(np.float32)
    next_blk = np.where(key <= qry, 0.0, -np.inf).astype(np.float32)
    return jnp.asarray(np.tile(prev_blk, (1, B_GROUP))), jnp.asarray(np.tile(next_blk, (1, B_GROUP)))


def _b_attn_call(sinks, qt, k, vt, dims, step_blocks, first_block, n_blocks):
    step_rows = step_blocks * B_BLOCK
    ncc = dims.ctx_len // B_BLOCK
    ncl = dims.seq // B_BLOCK
    lat0 = dims.batch * ncc
    nblk = k.shape[0] // B_BLOCK
    assert first_block % step_blocks == 0 and n_blocks % step_blocks == 0
    assert ncc % step_blocks == 0 if first_block < lat0 else ncl % step_blocks == 0
    off = first_block // step_blocks

    def batch_of(s):
        i = first_block + s * step_blocks
        return jnp.where(i < lat0, i // ncc, (i - lat0) // ncl)

    def left(s):
        return jnp.maximum(first_block + s * step_blocks - 1, 0)

    def right(s):
        return jnp.minimum(first_block + (s + 1) * step_blocks, nblk - 1)

    n_tiles = 3 + ncc
    return pl.pallas_call(
        functools.partial(_b_attn_kernel, n_ctx_blocks=lat0, blocks_per_seq=ncl, ctx_len=dims.ctx_len,
                          step_blocks=step_blocks, first_block=first_block),
        out_shape=jax.ShapeDtypeStruct((n_blocks * B_BLOCK, D_MODEL), _BF16),
        grid=(n_blocks // step_blocks,),
        in_specs=[
            pl.BlockSpec(memory_space=pltpu.SMEM),
            _resident((B_BLOCK, B_GROUP * B_BLOCK)),
            _resident((B_BLOCK, B_GROUP * B_BLOCK)),
            pl.BlockSpec((D_MODEL, step_rows), lambda s: (0, off + s)),
            pl.BlockSpec((B_BLOCK, B_KV_DIM), lambda s: (left(s), 0)),
            pl.BlockSpec((step_rows, B_KV_DIM), lambda s: (off + s, 0)),
            pl.BlockSpec((B_BLOCK, B_KV_DIM), lambda s: (right(s), 0)),
            pl.BlockSpec((dims.ctx_len, B_KV_DIM), lambda s: (batch_of(s), 0)),
            pl.BlockSpec((B_KV_DIM, B_BLOCK), lambda s: (0, left(s))),
            pl.BlockSpec((B_KV_DIM, step_rows), lambda s: (0, off + s)),
            pl.BlockSpec((B_KV_DIM, B_BLOCK), lambda s: (0, right(s))),
            pl.BlockSpec((B_KV_DIM, dims.ctx_len), lambda s: (0, batch_of(s))),
        ],
        out_specs=pl.BlockSpec((step_rows, D_MODEL), lambda s: (s, 0)),
        scratch_shapes=[pltpu.VMEM((step_blocks, B_KV_HEADS, n_tiles, B_BLOCK, B_GROUP * B_BLOCK), _F32)],
        compiler_params=_cparams("arbitrary"),
        name="b_attn",
    )(sinks, *_window_bias(), qt, k, k, k, k, vt, vt, vt, vt)


def _b_attn(sinks, qt, k, vt, dims):
    lat0 = dims.n_ctx_rows // B_BLOCK
    nblk = k.shape[0] // B_BLOCK
    y_ctx = _b_attn_call(sinks, qt, k, vt, dims, B_STEP_BLOCKS_CTX, 0, lat0)
    y_lat = _b_attn_call(sinks, qt, k, vt, dims, B_STEP_BLOCKS_LAT, lat0, nblk - lat0)
    return y_lat, y_ctx


def _c_proj_kernel(h_ref, g_ref, mod_ref, win_hbm, bg_ref, u_ref, w_ref, stage_ref, sem_ref, *, layer):
    @pl.when(pl.program_id(0) == 0)
    def _():
        def store(c, width, tile):
            w_ref[:, c:c + width] = tile.astype(_BF16)

        _stage_weights(_col_tiles(win_hbm, layer, 0, 3 * D_MODEL, store), stage_ref, sem_ref)

    x = _norm_mod(h_ref[...], g_ref[...], _mod_part(mod_ref, 0), _mod_part(mod_ref, 1)).astype(_BF16)
    y = jnp.dot(x, w_ref[...], preferred_element_type=_F32)
    bg_ref[...] = y[:, :D_MODEL]
    u_ref[...] = y[:, D_MODEL:2 * D_MODEL] * y[:, 2 * D_MODEL:]


def _c_proj(h, g, mod_l, w_in, layer, dims):
    rows = h.shape[0]
    tm = ROW_TILE
    return pl.pallas_call(
        functools.partial(_c_proj_kernel, layer=layer),
        out_shape=(jax.ShapeDtypeStruct((rows, D_MODEL), _F32),) * 2,
        grid=(rows // tm,),
        in_specs=[
            _row_spec(tm, D_MODEL),
            _resident((1, D_MODEL)),
            _mod_spec(dims.n_ctx_rows, dims.seq, tm),
            _weight_spec(),
        ],
        out_specs=(_row_spec(tm, D_MODEL),) * 2,
        scratch_shapes=[pltpu.VMEM((D_MODEL, 3 * D_MODEL), _BF16)] + _stage_scratch(),
        compiler_params=_cparams("arbitrary"),
        name="c_proj",
    )(h, g, mod_l, w_in)


def _c_mix(bg_ref, u_ref, up_ref, un_ref, cw_ref, cb_ref, tile, dims):
    tm = u_ref.shape[0]
    u = u_ref[...]
    row = lax.broadcasted_iota(jnp.int32, (tm, 1), 0)
    g_row = tile * tm + row
    in_ctx = g_row < dims.n_ctx_rows
    pos = jnp.where(in_ctx, g_row % dims.ctx_len, (g_row - dims.n_ctx_rows) % dims.seq)
    length = jnp.where(in_ctx, dims.ctx_len, dims.seq)
    prev = jnp.where(row == 0, up_ref[SUBLANES - 1:SUBLANES, :], pltpu.roll(u, 1, axis=0))
    nxt = jnp.where(row == tm - 1, un_ref[0:1, :], pltpu.roll(u, tm - 1, axis=0))
    prev = jnp.where(pos == 0, 0.0, prev)
    nxt = jnp.where(pos == length - 1, 0.0, nxt)
    conv = prev * cw_ref[0:1, :] + u * cw_ref[1:2, :] + nxt * cw_ref[2:3, :] + cb_ref[...]
    return (bg_ref[...] * conv).astype(_BF16)


_N_MIX_REFS = (6, 2, 6)


def _post_kernel(*refs, kind, n_h, tile0, final_norm, dims, out_layer, layer):
    n_mix = _N_MIX_REFS[kind]
    h_refs, mod_ref = refs[:n_h], refs[n_h]
    mix_refs = refs[n_h + 1:n_h + 1 + n_mix]
    (wo_hbm, g_ref, w1_hbm, w2_hbm, fg_ref, out_ref,
     wo_ref, w1_ref, w2_ref, stage_ref, sem_ref) = refs[n_h + 1 + n_mix:]
    n_chunks = MLP_HIDDEN // MLP_HIDDEN_CHUNK
    tiles_wo = D_MODEL // STAGE_COLS
    tiles_w1 = MLP_HIDDEN_CHUNK // STAGE_COLS
    tiles_w2 = D_MODEL // STAGE_COLS

    def compute(need):
        if kind == 0:
            y = _a_mix(*mix_refs, tile0)
        elif kind == 1:
            y = _load_h(mix_refs)
        else:
            y = _c_mix(*mix_refs, pl.program_id(0) + tile0, dims)
        h = _load_h(h_refs, tile0)
        need(tiles_wo)
        h = h + _mod_part(mod_ref, 2) * jnp.dot(y, wo_ref[...], preferred_element_type=_F32)
        x = _norm_mod(h, g_ref[...], _mod_part(mod_ref, 3), _mod_part(mod_ref, 4)).astype(_BF16)
        acc = jnp.zeros(h.shape, _F32)
        for c in range(n_chunks):
            cols = slice(c * MLP_HIDDEN_CHUNK, (c + 1) * MLP_HIDDEN_CHUNK)
            need(tiles_wo + c * (tiles_w1 + tiles_w2) + tiles_w1)
            u = jnp.dot(x, w1_ref[:, cols], preferred_element_type=_F32)
            u = jnp.square(jnp.maximum(u, 0.0)).astype(_BF16)
            need(tiles_wo + (c + 1) * (tiles_w1 + tiles_w2))
            acc = acc + jnp.dot(u, w2_ref[cols, :], preferred_element_type=_F32)
        out = h + _mod_part(mod_ref, 5) * acc
        if final_norm:
            ms = jnp.mean(out * out, axis=-1, keepdims=True)
            out = out * lax.rsqrt(ms + NORM_EPS) * fg_ref[...]
        out_ref[...] = out

    @pl.when(pl.program_id(0) == 0)
    def _():
        def store_to(dst_ref, row0, c, width, tile):
            dst_ref[row0:row0 + tile.shape[0], c:c + width] = tile.astype(_BF16)

        jobs = _col_tiles(wo_hbm, out_layer, 0, D_MODEL, functools.partial(store_to, wo_ref, 0))
        for r0 in range(0, MLP_HIDDEN, MLP_HIDDEN_CHUNK):
            jobs += _col_tiles(w1_hbm, layer, r0, MLP_HIDDEN_CHUNK,
                               lambda c, width, tile, r0=r0: store_to(w1_ref, 0, r0 + c, width, tile))
            for c in range(0, D_MODEL, STAGE_COLS):
                jobs.append((w2_hbm.at[layer, pl.ds(r0, MLP_HIDDEN_CHUNK), pl.ds(c, STAGE_COLS)],
                             functools.partial(store_to, w2_ref, r0, c, STAGE_COLS)))
        compute(_WeightStager(jobs, stage_ref, sem_ref).need)

    @pl.when(pl.program_id(0) != 0)
    def _():
        compute(lambda n: None)


def _post(kind, h_parts, mod_l, mix, w_out, out_layer, g, w1, w2, layer, final_g, dims, tile0=0,
          final_norm=False):
    tm = ROW_TILE
    n_tiles = dims.rows // tm - tile0
    row = functools.partial(_row_spec, tm, D_MODEL, tile0)
    if kind == 0:
        (hf_l, hf_c), (hb_l, hb_c), o, head_g = mix
        mix = (hf_l, hf_c, hb_l, hb_c, o, head_g)
        mix_specs = _h_specs(tm, True, tile0) * 2 + [row(), _resident((1, D_MODEL))]
    elif kind == 1:
        assert tile0 == 0
        mix_specs = _h_specs(tm, True)
    else:
        bg, u, conv_w, conv_b = mix
        per = tm // SUBLANES
        last = dims.rows // SUBLANES - 1
        mix = (bg, u, u, u, conv_w, conv_b)
        mix_specs = [
            row(), row(),
            pl.BlockSpec((SUBLANES, D_MODEL), lambda i: (jnp.maximum((i + tile0) * per - 1, 0), 0)),
            pl.BlockSpec((SUBLANES, D_MODEL), lambda i: (jnp.minimum((i + tile0 + 1) * per, last), 0)),
            _resident(conv_w.shape), _resident((1, D_MODEL)),
        ]
    assert len(h_parts) == 1 or tile0 == 0
    h_specs = _h_specs(tm, True) if len(h_parts) == 2 else [row()]
    return pl.pallas_call(
        functools.partial(_post_kernel, kind=kind, n_h=len(h_parts), tile0=tile0,
                          final_norm=final_norm, dims=dims, out_layer=out_layer, layer=layer),
        out_shape=jax.ShapeDtypeStruct((n_tiles * tm, D_MODEL), _F32),
        grid=(n_tiles,),
        in_specs=h_specs + [_mod_spec(dims.n_ctx_rows, dims.seq, tm, tile0)] + mix_specs + [
            _weight_spec(),
            _resident((1, D_MODEL)),
            _weight_spec(),
            _weight_spec(),
            _resident((1, D_MODEL)),
        ],
        out_specs=_row_spec(tm, D_MODEL),
        scratch_shapes=[pltpu.VMEM((D_MODEL, D_MODEL), _BF16), pltpu.VMEM((D_MODEL, MLP_HIDDEN), _BF16),
                        pltpu.VMEM((MLP_HIDDEN, D_MODEL), _BF16)] + _stage_scratch(),
        compiler_params=_cparams("arbitrary"),
        name="post",
    )(*h_parts, mod_l, *mix, w_out, g, w1, w2, final_g)


class _Dims:
    def __init__(self, batch, seq, ctx_len):
        self.batch = batch
        self.seq = seq
        self.ctx_len = ctx_len
        self.n_ctx_rows = batch * ctx_len
        self.rows = self.n_ctx_rows + batch * seq


def _rope_tables(dims):
    n_freq = B_HEAD_DIM // 4
    t = np.arange(dims.seq)
    inv_freq = ROPE_BASE ** (-np.arange(n_freq, dtype=np.float64) / n_freq)
    ang_row = (t // GRID_W)[:, None] * inv_freq
    ang_col = (t % GRID_W)[:, None] * inv_freq
    ang = np.concatenate([ang_row, ang_row, ang_col, ang_col], axis=1)
    sign = np.tile(np.concatenate([-np.ones(n_freq), np.ones(n_freq)]), 2)
    cos = np.tile(np.cos(ang), (dims.batch, 1))
    sin = np.tile(np.sin(ang) * sign, (dims.batch, 1))
    cos = np.concatenate([np.ones((dims.n_ctx_rows, B_HEAD_DIM)), cos], axis=0).astype(np.float32)
    sin = np.concatenate([np.zeros((dims.n_ctx_rows, B_HEAD_DIM)), sin], axis=0).astype(np.float32)
    tables = (np.tile(cos, (1, 2)), np.tile(sin, (1, 2)), np.ascontiguousarray(cos.T), np.ascontiguousarray(sin.T))
    return tuple(jnp.asarray(tab) for tab in tables)


def kernel(x, c, ctx, c_ctx, ada_w, ada_b, norm_g, final_g, mlp_w1, mlp_w2,
           a_w_in, a_w_gate, a_b_gate, a_head_g, a_w_out,
           b_w_qkv, b_sinks, b_w_out, c_w_in, c_conv_w, c_conv_b, c_w_out):
    batch, seq, d = x.shape
    ctx_len = ctx.shape[1]
    depth = ada_w.shape[0]
    dims = _Dims(batch, seq, ctx_len)
    assert MLP_HIDDEN_CHUNK == D_MODEL and d == D_MODEL and seq % ROW_TILE == 0 and dims.n_ctx_rows % ROW_TILE == 0
    assert 1 + batch <= MOD_ROWS

    assert dims.n_ctx_rows == ROW_TILE
    h_parts = (x.reshape(-1, d), ctx.reshape(-1, d))
    cs = jnp.concatenate([c_ctx[None], c, jnp.zeros((MOD_ROWS - 1 - batch, d), _F32)], axis=0)
    mod = _ada_table(cs, ada_w, ada_b).reshape(depth, MOD_ROWS, 1, 6 * d)
    rope = _rope_tables(dims)
    fg = final_g.reshape(1, d)

    for l in range(depth):
        kind, j = l % N_MIXERS, l // N_MIXERS
        mod_l = mod[l]
        g0 = norm_g[l, 0].reshape(1, d)
        g1 = norm_g[l, 1].reshape(1, d)
        last_layer = l == depth - 1
        if kind == 0:
            wgt = a_w_gate[j].T.astype(_BF16)
            q, kt, v, o, gr, gc = _a_proj(h_parts, g0, mod_l, a_w_in, j, wgt, a_b_gate[j].reshape(-1, 1), dims)
            hf, hb = _a_scan(q, kt, v, gr, gc, dims)
            mix = (hf, hb, o, a_head_g[j].reshape(1, d))
            w_out = a_w_out
        elif kind == 1:
            (h,) = h_parts
            qt, k, vt = _b_proj(h, g0, mod_l, b_w_qkv, j, rope, dims)
            mix = _b_attn(b_sinks[j], qt, k, vt, dims)
            w_out = b_w_out
        else:
            (h,) = h_parts
            bg, u = _c_proj(h, g0, mod_l, c_w_in, j, dims)
            mix = (bg, u, c_conv_w[j], c_conv_b[j].reshape(1, d))
            w_out = c_w_out
        tile0 = dims.n_ctx_rows // ROW_TILE if last_layer else 0
        h_parts = (_post(kind, h_parts, mod_l, mix, w_out, j, g1, mlp_w1, mlp_w2, l, fg, dims,
                         tile0=tile0, final_norm=last_layer),)
    return h_parts[0].reshape(batch, seq, d)
```

```python
import functools

import jax
import jax.numpy as jnp
import numpy as np
from jax import lax
from jax.experimental import pallas as pl
from jax.experimental.pallas import tpu as pltpu

D_MODEL = 1024
GRID_W = 64
N_MIXERS = 3
A_HEADS = 8
A_QK_DIM = D_MODEL // 2
A_DK = A_QK_DIM // A_HEADS
A_DV = D_MODEL // A_HEADS
A_CHUNK = 128
A_STEP_CHUNKS_CTX = 2
A_STEP_CHUNKS_LAT = 4
B_Q_HEADS = 16
B_KV_HEADS = 4
B_HEAD_DIM = D_MODEL // B_Q_HEADS
B_GROUP = B_Q_HEADS // B_KV_HEADS
B_KV_DIM = B_KV_HEADS * B_HEAD_DIM
B_BLOCK = 128
ROPE_BASE = 10000.0
MLP_HIDDEN = 4 * D_MODEL
NORM_EPS = 1e-6
LOG2E = 1.4426950408889634

LANES = 128
SUBLANES = 8
VMEM_LIMIT_BYTES = 60 * 1024 * 1024

ROW_TILE = 512
MLP_HIDDEN_CHUNK = 1024
MOD_ROWS = 8
ADA_COL_TILE = 1536

_BF16 = jnp.bfloat16
_F32 = jnp.float32


def _cparams(*sem):
    return pltpu.CompilerParams(dimension_semantics=sem, vmem_limit_bytes=VMEM_LIMIT_BYTES)


def _resident(shape):
    nd = len(shape)
    return pl.BlockSpec(shape, lambda *_: (0,) * nd, pipeline_mode=pl.Buffered(1))


STAGE_COLS = 512
STAGE_DEPTH = 3


def _weight_spec():
    return pl.BlockSpec(memory_space=pl.ANY)


def _stage_scratch():
    return [pltpu.VMEM((STAGE_DEPTH, D_MODEL, STAGE_COLS), _F32), pltpu.SemaphoreType.DMA((STAGE_DEPTH,))]


class _WeightStager:
    def __init__(self, jobs, stage_ref, sem_ref):
        self.jobs, self.stage_ref, self.sem_ref = jobs, stage_ref, sem_ref
        self.started = self.done = 0

    def _copy(self, i):
        src = self.jobs[i][0]
        dst = self.stage_ref.at[i % STAGE_DEPTH, pl.ds(0, src.shape[0]), pl.ds(0, src.shape[1])]
        return pltpu.make_async_copy(src, dst, self.sem_ref.at[i % STAGE_DEPTH])

    def _fill(self):
        while self.started < min(self.done + STAGE_DEPTH, len(self.jobs)):
            self._copy(self.started).start()
            self.started += 1

    def need(self, n):
        self._fill()
        while self.done < n:
            src, store = self.jobs[self.done]
            self._copy(self.done).wait()
            store(self.stage_ref[self.done % STAGE_DEPTH, :src.shape[0], :src.shape[1]])
            self.done += 1
            self._fill()


def _stage_weights(jobs, stage_ref, sem_ref):
    _WeightStager(jobs, stage_ref, sem_ref).need(len(jobs))


def _col_tiles(w_hbm, layer, col0, ncols, store):
    jobs = []
    for c in range(0, ncols, STAGE_COLS):
        width = min(STAGE_COLS, ncols - c)
        jobs.append((w_hbm.at[layer, :, pl.ds(col0 + c, width)], functools.partial(store, c, width)))
    return jobs


def _norm_mod(h, g, shift, scale):
    ms = jnp.mean(h * h, axis=-1, keepdims=True)
    y = h * lax.rsqrt(ms + NORM_EPS) * g
    return y * (1.0 + scale) + shift


def _mod_part(mod_ref, k):
    return mod_ref[:, k * D_MODEL:(k + 1) * D_MODEL]


def _mod_spec(n_ctx_rows, seq, tm, tile0=0):
    def idx(i):
        r0 = (i + tile0) * tm
        grp = jnp.where(r0 < n_ctx_rows, 0, 1 + (r0 - n_ctx_rows) // seq)
        return (grp, 0, 0)
    return pl.BlockSpec((None, 1, 6 * D_MODEL), idx)


def _row_spec(tm, width, tile0=0):
    return pl.BlockSpec((tm, width), lambda i: (i + tile0, 0))


def _h_specs(tm, split, tile0=0):
    if not split:
        return [_row_spec(tm, D_MODEL, tile0)]
    return [pl.BlockSpec((tm, D_MODEL), lambda i: (jnp.maximum(i + tile0 - 1, 0), 0)),
            _resident((tm, D_MODEL))]


def _load_h(h_refs, tile0=0):
    if len(h_refs) == 1:
        return h_refs[0][...]
    lat_ref, ctx_ref = h_refs
    if tile0 > 0:
        return lat_ref[...]
    return jnp.where(pl.program_id(0) == 0, ctx_ref[...], lat_ref[...])


def _ada_kernel(cs_ref, w_ref, b_ref, o_ref):
    cs = cs_ref[...]
    s = (cs * jax.nn.sigmoid(cs)).astype(_BF16)
    o_ref[...] = jnp.dot(s, w_ref[...].astype(_BF16), preferred_element_type=_F32) + b_ref[...]


def _ada_table(cs, ada_w, ada_b):
    depth = ada_w.shape[0]
    n = ada_w.shape[2]
    return pl.pallas_call(
        _ada_kernel,
        out_shape=jax.ShapeDtypeStruct((depth, MOD_ROWS, n), _F32),
        grid=(depth, n // ADA_COL_TILE),
        in_specs=[
            pl.BlockSpec((MOD_ROWS, D_MODEL), lambda l, j: (0, 0)),
            pl.BlockSpec((None, D_MODEL, ADA_COL_TILE), lambda l, j: (l, 0, j)),
            pl.BlockSpec((None, 1, ADA_COL_TILE), lambda l, j: (l, 0, j)),
        ],
        out_specs=pl.BlockSpec((None, MOD_ROWS, ADA_COL_TILE), lambda l, j: (l, 0, j)),
        compiler_params=_cparams("arbitrary", "arbitrary"),
        name="ada_table",
    )(cs, ada_w, ada_b.reshape(depth, 1, n))


def _a_proj_kernel(*refs, n_h, layer):
    h_refs, (g_ref, mod_ref, win_hbm, wgt_ref, bg_ref,
             q_ref, kt_ref, v_ref, o_ref, gr_ref, gc_ref,
             wqvo_ref, wkt_ref, stage_ref, sem_ref) = refs[:n_h], refs[n_h:]

    @pl.when(pl.program_id(0) == 0)
    def _():
        def store_qvo(dst0, c, width, tile):
            wqvo_ref[:, dst0 + c:dst0 + c + width] = tile.astype(_BF16)

        def store_kt(c, width, tile):
            wkt_ref[c:c + width, :] = tile.T.astype(_BF16)

        _stage_weights(
            _col_tiles(win_hbm, layer, 0, A_QK_DIM, functools.partial(store_qvo, 0))
            + _col_tiles(win_hbm, layer, A_QK_DIM, A_QK_DIM, store_kt)
            + _col_tiles(win_hbm, layer, 2 * A_QK_DIM, 2 * D_MODEL, functools.partial(store_qvo, A_QK_DIM)),
            stage_ref, sem_ref)
        wkt_ref[A_QK_DIM:, :] = wgt_ref[...]

    x = _norm_mod(_load_h(h_refs), g_ref[...], _mod_part(mod_ref, 0), _mod_part(mod_ref, 1)).astype(_BF16)
    nt = (((1,), (1,)), ((), ()))
    ktg = lax.dot_general(wkt_ref[...], x, nt, preferred_element_type=_F32)
    kt_ref[...] = ktg[:A_QK_DIM].astype(_BF16)
    gt = ktg[A_QK_DIM:] + bg_ref[...]

    n_chunks = gt.shape[1] // A_CHUNK

    def by_chunk(rows):
        return jnp.concatenate([rows[:, c * A_CHUNK:(c + 1) * A_CHUNK] for c in range(n_chunks)], axis=0)

    def by_token(x):
        return jnp.concatenate([x[c * A_HEADS:(c + 1) * A_HEADS] for c in range(n_chunks)], axis=1)

    def log_sigmoid(z):
        return jnp.minimum(z, 0.0) - jnp.log1p(jnp.exp(-jnp.abs(z)))

    row_form, col_cm, col_b = [], [], []
    for d in range(2):
        li = by_chunk(gt[2 * d * A_HEADS:(2 * d + 1) * A_HEADS]) * LOG2E
        lf = log_sigmoid(by_chunk(gt[(2 * d + 1) * A_HEADS:(2 * d + 2) * A_HEADS])) * LOG2E
        b = _lane_scan(lf, jnp.add, 0.0, d == 1)
        a = li - b
        cm = _lane_scan(a, jnp.maximum, -jnp.inf, d == 1)
        row_form += [by_token(a), by_token(b)]
        col_cm.append(cm)
        col_b.append(b)
    gr_ref[...] = jnp.concatenate(row_form, axis=0)
    pad = jnp.zeros((LANES - 2 * A_HEADS, LANES), _F32)
    for c in range(n_chunks):
        rows = slice(c * A_HEADS, (c + 1) * A_HEADS)
        toks = slice(c * A_CHUNK, (c + 1) * A_CHUNK)
        gc_ref[toks, :LANES] = jnp.concatenate([col_cm[0][rows], col_cm[1][rows], pad], axis=0).T
        gc_ref[toks, LANES:] = jnp.concatenate([col_b[0][rows], col_b[1][rows], pad], axis=0).T

    y = jnp.dot(x, wqvo_ref[...], preferred_element_type=_F32)
    q_ref[...] = (y[:, :A_QK_DIM] * (A_DK ** -0.5)).astype(_BF16)
    v_ref[...] = y[:, A_QK_DIM:A_QK_DIM + D_MODEL].astype(_BF16)
    o_ref[...] = y[:, A_QK_DIM + D_MODEL:]


def _a_proj(h_parts, g, mod_l, w_in, layer, wgt, bgate, dims):
    rows = dims.rows
    tm = ROW_TILE
    n_gate = 4 * A_HEADS
    return pl.pallas_call(
        functools.partial(_a_proj_kernel, n_h=len(h_parts), layer=layer),
        out_shape=(
            jax.ShapeDtypeStruct((rows, A_QK_DIM), _BF16),
            jax.ShapeDtypeStruct((A_QK_DIM, rows), _BF16),
            jax.ShapeDtypeStruct((rows, D_MODEL), _BF16),
            jax.ShapeDtypeStruct((rows, D_MODEL), _F32),
            jax.ShapeDtypeStruct((n_gate, rows), _F32),
            jax.ShapeDtypeStruct((rows, 2 * LANES), _F32),
        ),
        grid=(rows // tm,),
        in_specs=_h_specs(tm, len(h_parts) == 2) + [
            _resident((1, D_MODEL)),
            _mod_spec(dims.n_ctx_rows, dims.seq, tm),
            _weight_spec(),
            _resident(wgt.shape),
            _resident((n_gate, 1)),
        ],
        out_specs=(
            _row_spec(tm, A_QK_DIM),
            pl.BlockSpec((A_QK_DIM, tm), lambda i: (0, i)),
            _row_spec(tm, D_MODEL),
            _row_spec(tm, D_MODEL),
            pl.BlockSpec((n_gate, tm), lambda i: (0, i)),
            _row_spec(tm, 2 * LANES),
        ),
        scratch_shapes=[pltpu.VMEM((D_MODEL, A_QK_DIM + 2 * D_MODEL), _BF16),
                        pltpu.VMEM((A_QK_DIM + 4 * A_HEADS, D_MODEL), _BF16)] + _stage_scratch(),
        compiler_params=_cparams("arbitrary"),
        name="a_proj",
    )(*h_parts, g, mod_l, w_in, wgt, bgate)


def _lane_scan(x, op, fill, reverse):
    lane = lax.broadcasted_iota(jnp.int32, x.shape, 1)
    k = 1
    while k < LANES:
        if reverse:
            shifted = jnp.where(lane < LANES - k, pltpu.roll(x, LANES - k, axis=1), fill)
        else:
            shifted = jnp.where(lane >= k, pltpu.roll(x, k, axis=1), fill)
        x = op(x, shifted)
        k *= 2
    return x


def _a_scan_kernel(*refs, step_chunks, has_init, emit_state):
    n_state = 5
    (qf_ref, ktf_ref, vf_ref, grf_ref, gcf_ref, qb_ref, ktb_ref, vb_ref, grb_ref, gcb_ref), refs = refs[:10], refs[10:]
    init_refs, refs = (refs[:n_state], refs[n_state:]) if has_init else ((), refs)
    (hf_ref, hb_ref), refs = refs[:2], refs[2:]
    final_refs, refs = (refs[:n_state], refs[n_state:]) if emit_state else ((), refs)
    cf_ref, cb_ref, mrf_ref, mrb_ref, mc_ref = state_refs = refs

    @pl.when(pl.program_id(1) == 0)
    def _():
        if has_init:
            for ref, init in zip(state_refs, init_refs):
                ref[...] = init[...]
        else:
            cf_ref[...] = jnp.zeros(cf_ref.shape, _F32)
            cb_ref[...] = jnp.zeros(cb_ref.shape, _F32)
            mrf_ref[...] = jnp.full(mrf_ref.shape, -jnp.inf, _F32)
            mrb_ref[...] = jnp.full(mrb_ref.shape, -jnp.inf, _F32)
            mc_ref[...] = jnp.full(mc_ref.shape, -jnp.inf, _F32)

    t_idx = lax.broadcasted_iota(jnp.int32, (A_CHUNK, A_CHUNK), 0)
    s_idx = lax.broadcasted_iota(jnp.int32, (A_CHUNK, A_CHUNK), 1)
    ones_blk = jnp.ones((A_CHUNK, A_DV), _BF16)
    lane = lax.broadcasted_iota(jnp.int32, (1, LANES), 1)
    for sub in range(step_chunks):
        toks = (slice(sub * A_CHUNK, (sub + 1) * A_CHUNK),
                slice((step_chunks - 1 - sub) * A_CHUNK, (step_chunks - sub) * A_CHUNK))
        _a_chunk_pair(toks, (qf_ref, qb_ref), (ktf_ref, ktb_ref), (vf_ref, vb_ref), (grf_ref, grb_ref),
                      (gcf_ref, gcb_ref), (hf_ref, hb_ref), (cf_ref, cb_ref), (mrf_ref, mrb_ref), mc_ref,
                      t_idx, s_idx, ones_blk, lane)

    if emit_state:
        @pl.when(pl.program_id(1) == pl.num_programs(1) - 1)
        def _():
            for ref, final in zip(state_refs, final_refs):
                final[...] = ref[...]


def _a_chunk_pair(toks, q_refs, kt_refs, v_refs, gr_refs, gc_refs, h_refs, c_refs, mr_refs, mc_ref,
                  t_idx, s_idx, ones_blk, lane):
    m_col = mc_ref[0:1, :]
    m_col_new = []
    dirs = []
    for d in range(2):
        gr_ref, gc_ref, mr_ref, tok = gr_refs[d], gc_refs[d], mr_refs[d], toks[d]
        last = A_CHUNK - 1 if d == 0 else 0
        a = gr_ref[2 * d * A_HEADS:(2 * d + 1) * A_HEADS, tok]
        b = gr_ref[(2 * d + 1) * A_HEADS:(2 * d + 2) * A_HEADS, tok]
        m_row = mr_ref[:, 0:1]
        gg_end = jnp.maximum(m_row, jnp.max(a, axis=1, keepdims=True))
        ws = jnp.exp2(a - gg_end)
        decay = jnp.exp2(m_row - gg_end)
        mr_ref[...] = jnp.broadcast_to(b[:, last:last + 1] + gg_end, mr_ref.shape)
        gg_c = jnp.maximum(m_col, gc_ref[tok, :LANES])
        b_plus_gg = gc_ref[tok, LANES:] + gg_c
        clamp_c = jnp.exp2(-b_plus_gg)
        m_col_new.append(b_plus_gg[last:last + 1, :])
        dirs.append((a, m_row, ws, decay, gg_c, clamp_c))
    mc_ref[...] = jnp.broadcast_to(jnp.where(lane < A_HEADS, m_col_new[0], m_col_new[1]), mc_ref.shape)

    pair_lane = lax.broadcasted_iota(jnp.int32, (A_CHUNK, 2 * A_DK), 1)
    zeros_kt = jnp.zeros((A_DK, A_CHUNK), _BF16)
    zeros_c = jnp.zeros((A_DK, 2 * A_DV), _BF16)
    for hp in range(A_HEADS // 2):
        for d in range(2):
            q_ref, kt_ref, v_ref, h_ref, c_ref, tok = q_refs[d], kt_refs[d], v_refs[d], h_refs[d], c_refs[d], toks[d]
            a, m_row, ws, decay, gg_c, clamp_c = dirs[d]
            mask = (s_idx <= t_idx) if d == 0 else (s_idx >= t_idx)
            heads = (2 * hp, 2 * hp + 1)
            q2 = q_ref[tok, 2 * hp * A_DK:(2 * hp + 2) * A_DK]
            kts = [kt_ref[hd * A_DK:(hd + 1) * A_DK, tok] for hd in heads]
            kt_bd = jnp.concatenate([jnp.concatenate([kts[0], zeros_kt], axis=1),
                                     jnp.concatenate([zeros_kt, kts[1]], axis=1)], axis=0)
            s2 = jnp.dot(q2, kt_bd, preferred_element_type=_F32)
            ggs = [jnp.broadcast_to(gg_c[:, d * A_HEADS + hd:d * A_HEADS + hd + 1], (A_CHUNK, A_CHUNK))
                   for hd in heads]
            e_state = jnp.exp2(jnp.where(pair_lane < A_DK, m_row[heads[0]:heads[0] + 1, :] - ggs[0],
                                         m_row[heads[1]:heads[1] + 1, :] - ggs[1]))
            qe = q2.astype(_F32) * e_state
            for i, hd in enumerate(heads):
                col = d * A_HEADS + hd
                clamp = jnp.broadcast_to(clamp_c[:, col:col + 1], (A_CHUNK, A_DV))
                v = v_ref[tok, hd * A_DV:(hd + 1) * A_DV]
                v_ext = jnp.concatenate([v, ones_blk], axis=1)
                c_old = c_ref[hd]
                e_keys = jnp.exp2(jnp.where(mask, a[hd:hd + 1, :] - ggs[i], -jnp.inf))
                p = jnp.concatenate([s2[:, i * A_CHUNK:(i + 1) * A_CHUNK] * e_keys, qe], axis=1).astype(_BF16)
                c_rows = [c_old.astype(_BF16), zeros_c] if i == 0 else [zeros_c, c_old.astype(_BF16)]
                rhs = jnp.concatenate([v_ext] + c_rows, axis=0)
                num = jnp.dot(p, rhs, preferred_element_type=_F32)
                den = jnp.maximum(jnp.abs(num[:, A_DV:]), clamp)
                h_ref[tok, hd * A_DV:(hd + 1) * A_DV] = num[:, :A_DV] / den
                kw = (kts[i].astype(_F32) * ws[hd:hd + 1, :]).astype(_BF16)
                c_ref[hd] = decay[hd:hd + 1, :] * c_old + jnp.dot(kw, v_ext, preferred_element_type=_F32)


def _a_scan_call(q, kt, v, gr, gc, dims, step_chunks, first_row, seq_rows, init=None, emit_state=False):
    blk_rows = step_chunks * A_CHUNK
    n_steps = seq_rows // blk_rows
    assert seq_rows % blk_rows == 0 and first_row % blk_rows == 0
    blk0 = first_row // blk_rows

    def fwd_blk(b, j):
        return blk0 + b * n_steps + j

    def bwd_blk(b, j):
        return blk0 + b * n_steps + (n_steps - 1 - j)

    def specs(blk):
        return [
            pl.BlockSpec((blk_rows, A_QK_DIM), lambda b, j: (blk(b, j), 0)),
            pl.BlockSpec((A_QK_DIM, blk_rows), lambda b, j: (0, blk(b, j))),
            pl.BlockSpec((blk_rows, D_MODEL), lambda b, j: (blk(b, j), 0)),
            pl.BlockSpec((4 * A_HEADS, blk_rows), lambda b, j: (0, blk(b, j))),
            pl.BlockSpec((blk_rows, 2 * LANES), lambda b, j: (blk(b, j), 0)),
        ]

    state_shapes = [(A_HEADS, A_DK, 2 * A_DV)] * 2 + [(A_HEADS, LANES)] * 3
    state_specs = [pl.BlockSpec((None,) + shp, lambda b, j, n=len(shp): (b,) + (0,) * n) for shp in state_shapes]
    h_shape = jax.ShapeDtypeStruct((dims.batch * seq_rows, D_MODEL), _F32)
    out_shape = [h_shape, h_shape]
    out_specs = [pl.BlockSpec((blk_rows, D_MODEL), lambda b, j: (b * n_steps + j, 0)),
                 pl.BlockSpec((blk_rows, D_MODEL), lambda b, j: (b * n_steps + (n_steps - 1 - j), 0))]
    if emit_state:
        out_shape += [jax.ShapeDtypeStruct((dims.batch,) + shp, _F32) for shp in state_shapes]
        out_specs += state_specs
    return pl.pallas_call(
        functools.partial(_a_scan_kernel, step_chunks=step_chunks, has_init=init is not None,
                          emit_state=emit_state),
        out_shape=tuple(out_shape),
        grid=(dims.batch, n_steps),
        in_specs=specs(fwd_blk) + specs(bwd_blk) + (state_specs if init is not None else []),
        out_specs=tuple(out_specs),
        scratch_shapes=[pltpu.VMEM(shp, _F32) for shp in state_shapes],
        compiler_params=_cparams("arbitrary", "arbitrary"),
        name="a_scan",
    )(q, kt, v, gr, gc, q, kt, v, gr, gc, *(init or ()))


def _a_scan(q, kt, v, gr, gc, dims):
    hf_c, hb_c, *state = _a_scan_call(q, kt, v, gr, gc, dims, A_STEP_CHUNKS_CTX, 0, dims.ctx_len,
                                      emit_state=True)
    hf_l, hb_l = _a_scan_call(q, kt, v, gr, gc, dims, A_STEP_CHUNKS_LAT, dims.n_ctx_rows, dims.seq,
                              init=state)
    return (hf_l, hf_c), (hb_l, hb_c)


def _a_mix(hf_lat_ref, hf_ctx_ref, hb_lat_ref, hb_ctx_ref, o_ref, hg_ref, tile0):
    hs = _load_h((hf_lat_ref, hf_ctx_ref), tile0) + _load_h((hb_lat_ref, hb_ctx_ref), tile0)
    parts = []
    for hd in range(A_HEADS):
        x = hs[:, hd * A_DV:(hd + 1) * A_DV]
        parts.append(x * lax.rsqrt(jnp.mean(x * x, axis=-1, keepdims=True) + NORM_EPS))
    y = jnp.concatenate(parts, axis=1) * hg_ref[...]
    return (jax.nn.sigmoid(o_ref[...]) * y).astype(_BF16)


ROPE_HALF = B_HEAD_DIM // 4


def _swap_halves_lanes(x):
    lane = lax.broadcasted_iota(jnp.int32, x.shape, 1)
    fwd = pltpu.roll(x, LANES - ROPE_HALF, axis=1)
    back = pltpu.roll(x, ROPE_HALF, axis=1)
    return jnp.where(lane % (2 * ROPE_HALF) < ROPE_HALF, fwd, back)


def _swap_halves_rows(x):
    parts = []
    for r0 in range(0, x.shape[0], 2 * ROPE_HALF):
        parts += [x[r0 + ROPE_HALF:r0 + 2 * ROPE_HALF], x[r0:r0 + ROPE_HALF]]
    return jnp.concatenate(parts, axis=0)


def _b_proj_kernel(h_ref, g_ref, mod_ref, wqkv_hbm, cos_ref, sin_ref, cost_ref, sint_ref,
                   qt_ref, k_ref, vt_ref, wqvt_ref, wk_ref, stage_ref, sem_ref, *, layer):
    @pl.when(pl.program_id(0) == 0)
    def _():
        def store_t(dst0, c, width, tile):
            wqvt_ref[dst0 + c:dst0 + c + width, :] = tile.T.astype(_BF16)

        def store_k(c, width, tile):
            wk_ref[:, c:c + width] = tile.astype(_BF16)

        _stage_weights(
            _col_tiles(wqkv_hbm, layer, 0, D_MODEL, functools.partial(store_t, 0))
            + _col_tiles(wqkv_hbm, layer, D_MODEL, B_KV_DIM, store_k)
            + _col_tiles(wqkv_hbm, layer, D_MODEL + B_KV_DIM, B_KV_DIM, functools.partial(store_t, D_MODEL)),
            stage_ref, sem_ref)

    x = _norm_mod(h_ref[...], g_ref[...], _mod_part(mod_ref, 0), _mod_part(mod_ref, 1)).astype(_BF16)
    nt = (((1,), (1,)), ((), ()))
    yt = lax.dot_general(wqvt_ref[...], x, nt, preferred_element_type=_F32)
    cost, sint = cost_ref[...], sint_ref[...]
    for hd in range(B_Q_HEADS):
        rows = slice(hd * B_HEAD_DIM, (hd + 1) * B_HEAD_DIM)
        qt = yt[rows, :]
        qt = (qt * cost + _swap_halves_rows(qt) * sint) * (B_HEAD_DIM ** -0.5 * LOG2E)
        qt_ref[rows, :] = qt.astype(_BF16)
    vt_ref[...] = yt[D_MODEL:, :].astype(_BF16)
    cos, sin = cos_ref[...], sin_ref[...]
    k = jnp.dot(x, wk_ref[...], preferred_element_type=_F32)
    for c0 in range(0, B_KV_DIM, LANES):
        kc = k[:, c0:c0 + LANES]
        k_ref[:, c0:c0 + LANES] = (kc * cos + _swap_halves_lanes(kc) * sin).astype(_BF16)


def _b_proj(h, g, mod_l, w_qkv, layer, rope, dims):
    rows = h.shape[0]
    tm = ROW_TILE
    cos, sin, cost, sint = rope
    return pl.pallas_call(
        functools.partial(_b_proj_kernel, layer=layer),
        out_shape=(
            jax.ShapeDtypeStruct((D_MODEL, rows), _BF16),
            jax.ShapeDtypeStruct((rows, B_KV_DIM), _BF16),
            jax.ShapeDtypeStruct((B_KV_DIM, rows), _BF16),
        ),
        grid=(rows // tm,),
        in_specs=[
            _row_spec(tm, D_MODEL),
            _resident((1, D_MODEL)),
            _mod_spec(dims.n_ctx_rows, dims.seq, tm),
            _weight_spec(),
            _row_spec(tm, LANES),
            _row_spec(tm, LANES),
            pl.BlockSpec((B_HEAD_DIM, tm), lambda i: (0, i)),
            pl.BlockSpec((B_HEAD_DIM, tm), lambda i: (0, i)),
        ],
        out_specs=(
            pl.BlockSpec((D_MODEL, tm), lambda i: (0, i)),
            _row_spec(tm, B_KV_DIM),
            pl.BlockSpec((B_KV_DIM, tm), lambda i: (0, i)),
        ),
        scratch_shapes=[pltpu.VMEM((D_MODEL + B_KV_DIM, D_MODEL), _BF16),
                        pltpu.VMEM((D_MODEL, B_KV_DIM), _BF16)] + _stage_scratch(),
        compiler_params=_cparams("arbitrary"),
        name="b_proj",
    )(h, g, mod_l, w_qkv, cos, sin, cost, sint)


B_ONES_ROWS = 16
B_STEP_BLOCKS_CTX = 2
B_STEP_BLOCKS_LAT = 4


def _b_attn_kernel(sink_ref, win_l_ref, win_r_ref, qt_ref, kl_ref, kc_ref, kr_ref, kx_ref,
                   vtl_ref, vtc_ref, vtr_ref, vtx_ref, o_ref, s_ref,
                   *, n_ctx_blocks, blocks_per_seq, ctx_len, step_blocks, first_block):
    first = first_block + pl.program_id(0) * step_blocks
    is_lat = first >= n_ctx_blocks
    n = (first - n_ctx_blocks) % blocks_per_seq
    neg = -jnp.inf
    win_l = jnp.where(is_lat, win_l_ref[...], neg)
    win_r = jnp.where(is_lat, win_r_ref[...], neg)
    bias_c = jnp.where(is_lat, 0.0, neg)
    bias_first_l = jnp.where(jnp.logical_and(is_lat, n >= 1), win_l_ref[...], neg)
    bias_last_r = jnp.where(jnp.logical_and(is_lat, n + step_blocks <= blocks_per_seq - 1), win_r_ref[...], neg)

    n_q = B_GROUP * B_BLOCK
    head_of_lane = lax.broadcasted_iota(jnp.int32, (1, n_q), 1) // B_BLOCK
    ones_rows = jnp.ones((B_ONES_ROWS, B_BLOCK), _BF16)
    n_ctx_tiles = ctx_len // B_BLOCK

    def local_tiles(blk):
        own = slice(blk * B_BLOCK, (blk + 1) * B_BLOCK)
        if blk == 0:
            left = (kl_ref, vtl_ref, slice(0, B_BLOCK), bias_first_l)
        else:
            left = (kc_ref, vtc_ref, slice((blk - 1) * B_BLOCK, blk * B_BLOCK), win_l)
        if blk == step_blocks - 1:
            right = (kr_ref, vtr_ref, slice(0, B_BLOCK), bias_last_r)
        else:
            right = (kc_ref, vtc_ref, slice((blk + 1) * B_BLOCK, (blk + 2) * B_BLOCK), win_r)
        return [left, (kc_ref, vtc_ref, own, bias_c), right]

    def scores(blk, g):
        cols = slice(blk * B_BLOCK, (blk + 1) * B_BLOCK)
        qt = jnp.concatenate(
            [qt_ref[(g * B_GROUP + j) * B_HEAD_DIM:(g * B_GROUP + j + 1) * B_HEAD_DIM, cols]
             for j in range(B_GROUP)], axis=1)
        sink = jnp.zeros((1, n_q), _F32)
        for j in range(B_GROUP):
            sink = jnp.where(head_of_lane == j, sink_ref[g * B_GROUP + j] * LOG2E, sink)
        ks = slice(g * B_HEAD_DIM, (g + 1) * B_HEAD_DIM)
        k_tiles = [(k_ref[rows, ks], bias) for k_ref, _, rows, bias in local_tiles(blk)]
        for t in range(n_ctx_tiles):
            k_tiles.append((kx_ref[t * B_BLOCK:(t + 1) * B_BLOCK, ks], None))
        m_tile = None
        for t, (k, bias) in enumerate(k_tiles):
            s = jnp.dot(k, qt, preferred_element_type=_F32)
            if bias is not None:
                s = s + bias
            s_ref[blk, g, t] = s
            m_tile = s if m_tile is None else jnp.maximum(m_tile, s)
        return jnp.maximum(sink, jnp.max(m_tile, axis=0, keepdims=True)), sink

    def attend(blk, g, m, sink):
        ks = slice(g * B_HEAD_DIM, (g + 1) * B_HEAD_DIM)
        vt_tiles = [vt_ref[ks, cols] for _, vt_ref, cols, _ in local_tiles(blk)]
        for t in range(n_ctx_tiles):
            vt_tiles.append(vtx_ref[ks, t * B_BLOCK:(t + 1) * B_BLOCK])
        acc = jnp.zeros((B_HEAD_DIM + B_ONES_ROWS, n_q), _F32)
        for t, vt in enumerate(vt_tiles):
            p = jnp.exp2(s_ref[blk, g, t] - m).astype(_BF16)
            vt_ext = jnp.concatenate([vt, ones_rows], axis=0)
            acc = acc + jnp.dot(vt_ext, p, preferred_element_type=_F32)
        denom = jnp.exp2(sink - m) + acc[B_HEAD_DIM:B_HEAD_DIM + 1, :]
        out_t = acc[:B_HEAD_DIM, :] * (1.0 / denom)
        rows = slice(blk * B_BLOCK, (blk + 1) * B_BLOCK)
        for pair in range(B_GROUP // 2):
            two = jnp.concatenate([out_t[:, (2 * pair) * B_BLOCK:(2 * pair + 1) * B_BLOCK],
                                   out_t[:, (2 * pair + 1) * B_BLOCK:(2 * pair + 2) * B_BLOCK]], axis=0)
            c0 = (g * B_GROUP + 2 * pair) * B_HEAD_DIM
            o_ref[rows, c0:c0 + 2 * B_HEAD_DIM] = two.T.astype(_BF16)

    units = [(blk, g) for blk in range(step_blocks) for g in range(B_KV_HEADS)]
    pending = [scores(*units[0]), scores(*units[1])]
    for i, unit in enumerate(units):
        if i + 2 < len(units):
            pending.append(scores(*units[i + 2]))
        attend(*unit, *pending[i])


def _window_bias():
    key = np.arange(B_BLOCK)[:, None]
    qry = np.arange(B_BLOCK)[None, :]
    prev_blk = np.where(key >= qry, 0.0, -np.inf).astype(np.float32)
    next_blk = np.where(key <= qry, 0.0, -np.inf).astype(np.float32)
    return jnp.asarray(np.tile(prev_blk, (1, B_GROUP))), jnp.asarray(np.tile(next_blk, (1, B_GROUP)))


def _b_attn_call(sinks, qt, k, vt, dims, step_blocks, first_block, n_blocks):
    step_rows = step_blocks * B_BLOCK
    ncc = dims.ctx_len // B_BLOCK
    ncl = dims.seq // B_BLOCK
    lat0 = dims.batch * ncc
    nblk = k.shape[0] // B_BLOCK
    assert first_block % step_blocks == 0 and n_blocks % step_blocks == 0
    assert ncc % step_blocks == 0 if first_block < lat0 else ncl % step_blocks == 0
    off = first_block // step_blocks

    def batch_of(s):
        i = first_block + s * step_blocks
        return jnp.where(i < lat0, i // ncc, (i - lat0) // ncl)

    def left(s):
        return jnp.maximum(first_block + s * step_blocks - 1, 0)

    def right(s):
        return jnp.minimum(first_block + (s + 1) * step_blocks, nblk - 1)

    n_tiles = 3 + ncc
    return pl.pallas_call(
        functools.partial(_b_attn_kernel, n_ctx_blocks=lat0, blocks_per_seq=ncl, ctx_len=dims.ctx_len,
                          step_blocks=step_blocks, first_block=first_block),
        out_shape=jax.ShapeDtypeStruct((n_blocks * B_BLOCK, D_MODEL), _BF16),
        grid=(n_blocks // step_blocks,),
        in_specs=[
            pl.BlockSpec(memory_space=pltpu.SMEM),
            _resident((B_BLOCK, B_GROUP * B_BLOCK)),
            _resident((B_BLOCK, B_GROUP * B_BLOCK)),
            pl.BlockSpec((D_MODEL, step_rows), lambda s: (0, off + s)),
            pl.BlockSpec((B_BLOCK, B_KV_DIM), lambda s: (left(s), 0)),
            pl.BlockSpec((step_rows, B_KV_DIM), lambda s: (off + s, 0)),
            pl.BlockSpec((B_BLOCK, B_KV_DIM), lambda s: (right(s), 0)),
            pl.BlockSpec((dims.ctx_len, B_KV_DIM), lambda s: (batch_of(s), 0)),
            pl.BlockSpec((B_KV_DIM, B_BLOCK), lambda s: (0, left(s))),
            pl.BlockSpec((B_KV_DIM, step_rows), lambda s: (0, off + s)),
            pl.BlockSpec((B_KV_DIM, B_BLOCK), lambda s: (0, right(s))),
            pl.BlockSpec((B_KV_DIM, dims.ctx_len), lambda s: (0, batch_of(s))),
        ],
        out_specs=pl.BlockSpec((step_rows, D_MODEL), lambda s: (s, 0)),
        scratch_shapes=[pltpu.VMEM((step_blocks, B_KV_HEADS, n_tiles, B_BLOCK, B_GROUP * B_BLOCK), _F32)],
        compiler_params=_cparams("arbitrary"),
        name="b_attn",
    )(sinks, *_window_bias(), qt, k, k, k, k, vt, vt, vt, vt)


def _b_attn(sinks, qt, k, vt, dims):
    lat0 = dims.n_ctx_rows // B_BLOCK
    nblk = k.shape[0] // B_BLOCK
    y_ctx = _b_attn_call(sinks, qt, k, vt, dims, B_STEP_BLOCKS_CTX, 0, lat0)
    y_lat = _b_attn_call(sinks, qt, k, vt, dims, B_STEP_BLOCKS_LAT, lat0, nblk - lat0)
    return y_lat, y_ctx


def _c_proj_kernel(h_ref, g_ref, mod_ref, win_hbm, bg_ref, u_ref, w_ref, stage_ref, sem_ref, *, layer):
    @pl.when(pl.program_id(0) == 0)
    def _():
        def store(c, width, tile):
            w_ref[:, c:c + width] = tile.astype(_BF16)

        _stage_weights(_col_tiles(win_hbm, layer, 0, 3 * D_MODEL, store), stage_ref, sem_ref)

    x = _norm_mod(h_ref[...], g_ref[...], _mod_part(mod_ref, 0), _mod_part(mod_ref, 1)).astype(_BF16)
    y = jnp.dot(x, w_ref[...], preferred_element_type=_F32)
    bg_ref[...] = y[:, :D_MODEL]
    u_ref[...] = y[:, D_MODEL:2 * D_MODEL] * y[:, 2 * D_MODEL:]


def _c_proj(h, g, mod_l, w_in, layer, dims):
    rows = h.shape[0]
    tm = ROW_TILE
    return pl.pallas_call(
        functools.partial(_c_proj_kernel, layer=layer),
        out_shape=(jax.ShapeDtypeStruct((rows, D_MODEL), _F32),) * 2,
        grid=(rows // tm,),
        in_specs=[
            _row_spec(tm, D_MODEL),
            _resident((1, D_MODEL)),
            _mod_spec(dims.n_ctx_rows, dims.seq, tm),
            _weight_spec(),
        ],
        out_specs=(_row_spec(tm, D_MODEL),) * 2,
        scratch_shapes=[pltpu.VMEM((D_MODEL, 3 * D_MODEL), _BF16)] + _stage_scratch(),
        compiler_params=_cparams("arbitrary"),
        name="c_proj",
    )(h, g, mod_l, w_in)


def _c_mix(bg_ref, u_ref, up_ref, un_ref, cw_ref, cb_ref, tile, dims):
    tm = u_ref.shape[0]
    u = u_ref[...]
    row = lax.broadcasted_iota(jnp.int32, (tm, 1), 0)
    g_row = tile * tm + row
    in_ctx = g_row < dims.n_ctx_rows
    pos = jnp.where(in_ctx, g_row % dims.ctx_len, (g_row - dims.n_ctx_rows) % dims.seq)
    length = jnp.where(in_ctx, dims.ctx_len, dims.seq)
    prev = jnp.where(row == 0, up_ref[SUBLANES - 1:SUBLANES, :], pltpu.roll(u, 1, axis=0))
    nxt = jnp.where(row == tm - 1, un_ref[0:1, :], pltpu.roll(u, tm - 1, axis=0))
    prev = jnp.where(pos == 0, 0.0, prev)
    nxt = jnp.where(pos == length - 1, 0.0, nxt)
    conv = prev * cw_ref[0:1, :] + u * cw_ref[1:2, :] + nxt * cw_ref[2:3, :] + cb_ref[...]
    return (bg_ref[...] * conv).astype(_BF16)


_N_MIX_REFS = (6, 2, 6)


def _post_kernel(*refs, kind, n_h, tile0, final_norm, dims, out_layer, layer):
    n_mix = _N_MIX_REFS[kind]
    h_refs, mod_ref = refs[:n_h], refs[n_h]
    mix_refs = refs[n_h + 1:n_h + 1 + n_mix]
    (wo_hbm, g_ref, w1_hbm, w2_hbm, fg_ref, out_ref,
     wo_ref, w1_ref, w2_ref, stage_ref, sem_ref) = refs[n_h + 1 + n_mix:]
    n_chunks = MLP_HIDDEN // MLP_HIDDEN_CHUNK
    tiles_wo = D_MODEL // STAGE_COLS
    tiles_w1 = MLP_HIDDEN_CHUNK // STAGE_COLS
    tiles_w2 = D_MODEL // STAGE_COLS

    def compute(need):
        if kind == 0:
            y = _a_mix(*mix_refs, tile0)
        elif kind == 1:
            y = _load_h(mix_refs)
        else:
            y = _c_mix(*mix_refs, pl.program_id(0) + tile0, dims)
        h = _load_h(h_refs, tile0)
        need(tiles_wo)
        h = h + _mod_part(mod_ref, 2) * jnp.dot(y, wo_ref[...], preferred_element_type=_F32)
        x = _norm_mod(h, g_ref[...], _mod_part(mod_ref, 3), _mod_part(mod_ref, 4)).astype(_BF16)
        acc = jnp.zeros(h.shape, _F32)
        for c in range(n_chunks):
            cols = slice(c * MLP_HIDDEN_CHUNK, (c + 1) * MLP_HIDDEN_CHUNK)
            need(tiles_wo + c * (tiles_w1 + tiles_w2) + tiles_w1)
            u = jnp.dot(x, w1_ref[:, cols], preferred_element_type=_F32)
            u = jnp.square(jnp.maximum(u, 0.0)).astype(_BF16)
            need(tiles_wo + (c + 1) * (tiles_w1 + tiles_w2))
            acc = acc + jnp.dot(u, w2_ref[cols, :], preferred_element_type=_F32)
        out = h + _mod_part(mod_ref, 5) * acc
        if final_norm:
            ms = jnp.mean(out * out, axis=-1, keepdims=True)
            out = out * lax.rsqrt(ms + NORM_EPS) * fg_ref[...]
        out_ref[...] = out

    @pl.when(pl.program_id(0) == 0)
    def _():
        def store_to(dst_ref, row0, c, width, tile):
            dst_ref[row0:row0 + tile.shape[0], c:c + width] = tile.astype(_BF16)

        jobs = _col_tiles(wo_hbm, out_layer, 0, D_MODEL, functools.partial(store_to, wo_ref, 0))
        for r0 in range(0, MLP_HIDDEN, MLP_HIDDEN_CHUNK):
            jobs += _col_tiles(w1_hbm, layer, r0, MLP_HIDDEN_CHUNK,
                               lambda c, width, tile, r0=r0: store_to(w1_ref, 0, r0 + c, width, tile))
            for c in range(0, D_MODEL, STAGE_COLS):
                jobs.append((w2_hbm.at[layer, pl.ds(r0, MLP_HIDDEN_CHUNK), pl.ds(c, STAGE_COLS)],
                             functools.partial(store_to, w2_ref, r0, c, STAGE_COLS)))
        compute(_WeightStager(jobs, stage_ref, sem_ref).need)

    @pl.when(pl.program_id(0) != 0)
    def _():
        compute(lambda n: None)


def _post(kind, h_parts, mod_l, mix, w_out, out_layer, g, w1, w2, layer, final_g, dims, tile0=0,
          final_norm=False):
    tm = ROW_TILE
    n_tiles = dims.rows // tm - tile0
    row = functools.partial(_row_spec, tm, D_MODEL, tile0)
    if kind == 0:
        (hf_l, hf_c), (hb_l, hb_c), o, head_g = mix
        mix = (hf_l, hf_c, hb_l, hb_c, o, head_g)
        mix_specs = _h_specs(tm, True, tile0) * 2 + [row(), _resident((1, D_MODEL))]
    elif kind == 1:
        assert tile0 == 0
        mix_specs = _h_specs(tm, True)
    else:
        bg, u, conv_w, conv_b = mix
        per = tm // SUBLANES
        last = dims.rows // SUBLANES - 1
        mix = (bg, u, u, u, conv_w, conv_b)
        mix_specs = [
            row(), row(),
            pl.BlockSpec((SUBLANES, D_MODEL), lambda i: (jnp.maximum((i + tile0) * per - 1, 0), 0)),
            pl.BlockSpec((SUBLANES, D_MODEL), lambda i: (jnp.minimum((i + tile0 + 1) * per, last), 0)),
            _resident(conv_w.shape), _resident((1, D_MODEL)),
        ]
    assert len(h_parts) == 1 or tile0 == 0
    h_specs = _h_specs(tm, True) if len(h_parts) == 2 else [row()]
    return pl.pallas_call(
        functools.partial(_post_kernel, kind=kind, n_h=len(h_parts), tile0=tile0,
                          final_norm=final_norm, dims=dims, out_layer=out_layer, layer=layer),
        out_shape=jax.ShapeDtypeStruct((n_tiles * tm, D_MODEL), _F32),
        grid=(n_tiles,),
        in_specs=h_specs + [_mod_spec(dims.n_ctx_rows, dims.seq, tm, tile0)] + mix_specs + [
            _weight_spec(),
            _resident((1, D_MODEL)),
            _weight_spec(),
            _weight_spec(),
            _resident((1, D_MODEL)),
        ],
        out_specs=_row_spec(tm, D_MODEL),
        scratch_shapes=[pltpu.VMEM((D_MODEL, D_MODEL), _BF16), pltpu.VMEM((D_MODEL, MLP_HIDDEN), _BF16),
                        pltpu.VMEM((MLP_HIDDEN, D_MODEL), _BF16)] + _stage_scratch(),
        compiler_params=_cparams("arbitrary"),
        name="post",
    )(*h_parts, mod_l, *mix, w_out, g, w1, w2, final_g)


class _Dims:
    def __init__(self, batch, seq, ctx_len):
        self.batch = batch
        self.seq = seq
        self.ctx_len = ctx_len
        self.n_ctx_rows = batch * ctx_len
        self.rows = self.n_ctx_rows + batch * seq


def _rope_tables(dims):
    n_freq = B_HEAD_DIM // 4
    t = np.arange(dims.seq)
    inv_freq = ROPE_BASE ** (-np.arange(n_freq, dtype=np.float64) / n_freq)
    ang_row = (t // GRID_W)[:, None] * inv_freq
    ang_col = (t % GRID_W)[:, None] * inv_freq
    ang = np.concatenate([ang_row, ang_row, ang_col, ang_col], axis=1)
    sign = np.tile(np.concatenate([-np.ones(n_freq), np.ones(n_freq)]), 2)
    cos = np.tile(np.cos(ang), (dims.batch, 1))
    sin = np.tile(np.sin(ang) * sign, (dims.batch, 1))
    cos = np.concatenate([np.ones((dims.n_ctx_rows, B_HEAD_DIM)), cos], axis=0).astype(np.float32)
    sin = np.concatenate([np.zeros((dims.n_ctx_rows, B_HEAD_DIM)), sin], axis=0).astype(np.float32)
    tables = (np.tile(cos, (1, 2)), np.tile(sin, (1, 2)), np.ascontiguousarray(cos.T), np.ascontiguousarray(sin.T))
    return tuple(jnp.asarray(tab) for tab in tables)


def kernel(x, c, ctx, c_ctx, ada_w, ada_b, norm_g, final_g, mlp_w1, mlp_w2,
           a_w_in, a_w_gate, a_b_gate, a_head_g, a_w_out,
           b_w_qkv, b_sinks, b_w_out, c_w_in, c_conv_w, c_conv_b, c_w_out):
    batch, seq, d = x.shape
    ctx_len = ctx.shape[1]
    depth = ada_w.shape[0]
    dims = _Dims(batch, seq, ctx_len)
    assert MLP_HIDDEN_CHUNK == D_MODEL and d == D_MODEL and seq % ROW_TILE == 0 and dims.n_ctx_rows % ROW_TILE == 0
    assert 1 + batch <= MOD_ROWS

    assert dims.n_ctx_rows == ROW_TILE
    h_parts = (x.reshape(-1, d), ctx.reshape(-1, d))
    cs = jnp.concatenate([c_ctx[None], c, jnp.zeros((MOD_ROWS - 1 - batch, d), _F32)], axis=0)
    mod = _ada_table(cs, ada_w, ada_b).reshape(depth, MOD_ROWS, 1, 6 * d)
    rope = _rope_tables(dims)
    fg = final_g.reshape(1, d)

    for l in range(depth):
        kind, j = l % N_MIXERS, l // N_MIXERS
        mod_l = mod[l]
        g0 = norm_g[l, 0].reshape(1, d)
        g1 = norm_g[l, 1].reshape(1, d)
        last_layer = l == depth - 1
        if kind == 0:
            wgt = a_w_gate[j].T.astype(_BF16)
            q, kt, v, o, gr, gc = _a_proj(h_parts, g0, mod_l, a_w_in, j, wgt, a_b_gate[j].reshape(-1, 1), dims)
            hf, hb = _a_scan(q, kt, v, gr, gc, dims)
            mix = (hf, hb, o, a_head_g[j].reshape(1, d))
            w_out = a_w_out
        elif kind == 1:
            (h,) = h_parts
            qt, k, vt = _b_proj(h, g0, mod_l, b_w_qkv, j, rope, dims)
            mix = _b_attn(b_sinks[j], qt, k, vt, dims)
            w_out = b_w_out
        else:
            (h,) = h_parts
            bg, u = _c_proj(h, g0, mod_l, c_w_in, j, dims)
            mix = (bg, u, c_conv_w[j], c_conv_b[j].reshape(1, d))
            w_out = c_w_out
        tile0 = dims.n_ctx_rows // ROW_TILE if last_layer else 0
        h_parts = (_post(kind, h_parts, mod_l, mix, w_out, j, g1, mlp_w1, mlp_w2, l, fg, dims,
                         tile0=tile0, final_norm=last_layer),)
    return h_parts[0].reshape(batch, seq, d)
```

```python
import functools

import jax
import jax.numpy as jnp
import numpy as np
from jax import lax
from jax.experimental import pallas as pl
from jax.experimental.pallas import tpu as pltpu

D_MODEL = 1024
GRID_W = 64
N_MIXERS = 3
A_HEADS = 8
A_QK_DIM = D_MODEL // 2
A_DK = A_QK_DIM // A_HEADS
A_DV = D_MODEL // A_HEADS
A_CHUNK = 128
A_STEP_CHUNKS_CTX = 2
A_STEP_CHUNKS_LAT = 4
B_Q_HEADS = 16
B_KV_HEADS = 4
B_HEAD_DIM = D_MODEL // B_Q_HEADS
B_GROUP = B_Q_HEADS // B_KV_HEADS
B_KV_DIM = B_KV_HEADS * B_HEAD_DIM
B_BLOCK = 128
ROPE_BASE = 10000.0
MLP_HIDDEN = 4 * D_MODEL
NORM_EPS = 1e-6
LOG2E = 1.4426950408889634

LANES = 128
SUBLANES = 8
VMEM_LIMIT_BYTES = 60 * 1024 * 1024

ROW_TILE = 512
MLP_HIDDEN_CHUNK = 1024
MOD_ROWS = 8
OUT_PROJ_COLS = 256
ADA_COL_TILE = 1536

_BF16 = jnp.bfloat16
_F32 = jnp.float32


def _cparams(*sem):
    return pltpu.CompilerParams(dimension_semantics=sem, vmem_limit_bytes=VMEM_LIMIT_BYTES)


def _resident(shape):
    nd = len(shape)
    return pl.BlockSpec(shape, lambda *_: (0,) * nd, pipeline_mode=pl.Buffered(1))


STAGE_COLS = 512
STAGE_DEPTH = 3


def _weight_spec():
    return pl.BlockSpec(memory_space=pl.ANY)


def _stage_scratch():
    return [pltpu.VMEM((STAGE_DEPTH, D_MODEL, STAGE_COLS), _F32), pltpu.SemaphoreType.DMA((STAGE_DEPTH,))]


class _WeightStager:
    def __init__(self, jobs, stage_ref, sem_ref):
        self.jobs, self.stage_ref, self.sem_ref = jobs, stage_ref, sem_ref
        self.started = self.done = 0

    def _copy(self, i):
        src = self.jobs[i][0]
        dst = self.stage_ref.at[i % STAGE_DEPTH, pl.ds(0, src.shape[0]), pl.ds(0, src.shape[1])]
        return pltpu.make_async_copy(src, dst, self.sem_ref.at[i % STAGE_DEPTH])

    def _fill(self):
        while self.started < min(self.done + STAGE_DEPTH, len(self.jobs)):
            self._copy(self.started).start()
            self.started += 1

    def need(self, n):
        self._fill()
        while self.done < n:
            src, store = self.jobs[self.done]
            self._copy(self.done).wait()
            store(self.stage_ref[self.done % STAGE_DEPTH, :src.shape[0], :src.shape[1]])
            self.done += 1
            self._fill()


def _stage_weights(jobs, stage_ref, sem_ref):
    _WeightStager(jobs, stage_ref, sem_ref).need(len(jobs))


def _col_tiles(w_hbm, layer, col0, ncols, store):
    jobs = []
    for c in range(0, ncols, STAGE_COLS):
        width = min(STAGE_COLS, ncols - c)
        jobs.append((w_hbm.at[layer, :, pl.ds(col0 + c, width)], functools.partial(store, c, width)))
    return jobs


def _norm_mod(h, g, shift, scale):
    ms = jnp.mean(h * h, axis=-1, keepdims=True)
    y = h * lax.rsqrt(ms + NORM_EPS) * g
    return y * (1.0 + scale) + shift


def _mod_part(mod_ref, k):
    return mod_ref[:, k * D_MODEL:(k + 1) * D_MODEL]


def _mod_spec(n_ctx_rows, seq, tm, tile0=0):
    def idx(i):
        r0 = (i + tile0) * tm
        grp = jnp.where(r0 < n_ctx_rows, 0, 1 + (r0 - n_ctx_rows) // seq)
        return (grp, 0, 0)
    return pl.BlockSpec((None, 1, 6 * D_MODEL), idx)


def _row_spec(tm, width, tile0=0):
    return pl.BlockSpec((tm, width), lambda i: (i + tile0, 0))


def _h_specs(tm, split, tile0=0):
    if not split:
        return [_row_spec(tm, D_MODEL, tile0)]
    return [pl.BlockSpec((tm, D_MODEL), lambda i: (jnp.maximum(i + tile0 - 1, 0), 0)),
            _resident((tm, D_MODEL))]


def _load_h(h_refs, tile0=0, cols=slice(None)):
    if len(h_refs) == 1:
        return h_refs[0][:, cols]
    lat_ref, ctx_ref = h_refs
    if tile0 > 0:
        return lat_ref[:, cols]
    return jnp.where(pl.program_id(0) == 0, ctx_ref[:, cols], lat_ref[:, cols])


def _ada_kernel(cs_ref, w_ref, b_ref, o_ref):
    cs = cs_ref[...]
    s = (cs * jax.nn.sigmoid(cs)).astype(_BF16)
    o_ref[...] = jnp.dot(s, w_ref[...].astype(_BF16), preferred_element_type=_F32) + b_ref[...]


def _ada_table(cs, ada_w, ada_b):
    depth = ada_w.shape[0]
    n = ada_w.shape[2]
    return pl.pallas_call(
        _ada_kernel,
        out_shape=jax.ShapeDtypeStruct((depth, MOD_ROWS, n), _F32),
        grid=(depth, n // ADA_COL_TILE),
        in_specs=[
            pl.BlockSpec((MOD_ROWS, D_MODEL), lambda l, j: (0, 0)),
            pl.BlockSpec((None, D_MODEL, ADA_COL_TILE), lambda l, j: (l, 0, j)),
            pl.BlockSpec((None, 1, ADA_COL_TILE), lambda l, j: (l, 0, j)),
        ],
        out_specs=pl.BlockSpec((None, MOD_ROWS, ADA_COL_TILE), lambda l, j: (l, 0, j)),
        compiler_params=_cparams("arbitrary", "arbitrary"),
        name="ada_table",
    )(cs, ada_w, ada_b.reshape(depth, 1, n))


def _a_proj_kernel(*refs, n_h, layer):
    h_refs, (g_ref, mod_ref, win_hbm, wgt_ref, bg_ref,
             q_ref, kt_ref, v_ref, o_ref, gr_ref, gc_ref,
             wqvo_ref, wkt_ref, stage_ref, sem_ref) = refs[:n_h], refs[n_h:]

    @pl.when(pl.program_id(0) == 0)
    def _():
        def store_qvo(dst0, c, width, tile):
            wqvo_ref[:, dst0 + c:dst0 + c + width] = tile.astype(_BF16)

        def store_kt(c, width, tile):
            wkt_ref[c:c + width, :] = tile.T.astype(_BF16)

        _stage_weights(
            _col_tiles(win_hbm, layer, 0, A_QK_DIM, functools.partial(store_qvo, 0))
            + _col_tiles(win_hbm, layer, A_QK_DIM, A_QK_DIM, store_kt)
            + _col_tiles(win_hbm, layer, 2 * A_QK_DIM, 2 * D_MODEL, functools.partial(store_qvo, A_QK_DIM)),
            stage_ref, sem_ref)
        wkt_ref[A_QK_DIM:, :] = wgt_ref[...]

    x = _norm_mod(_load_h(h_refs), g_ref[...], _mod_part(mod_ref, 0), _mod_part(mod_ref, 1)).astype(_BF16)
    nt = (((1,), (1,)), ((), ()))
    ktg = lax.dot_general(wkt_ref[...], x, nt, preferred_element_type=_F32)
    kt_ref[...] = ktg[:A_QK_DIM].astype(_BF16)
    gt = ktg[A_QK_DIM:] + bg_ref[...]

    n_chunks = gt.shape[1] // A_CHUNK

    def by_chunk(rows):
        return jnp.concatenate([rows[:, c * A_CHUNK:(c + 1) * A_CHUNK] for c in range(n_chunks)], axis=0)

    def by_token(x):
        return jnp.concatenate([x[c * A_HEADS:(c + 1) * A_HEADS] for c in range(n_chunks)], axis=1)

    def log_sigmoid(z):
        return jnp.minimum(z, 0.0) - jnp.log1p(jnp.exp(-jnp.abs(z)))

    row_form, col_cm, col_b = [], [], []
    for d in range(2):
        li = by_chunk(gt[2 * d * A_HEADS:(2 * d + 1) * A_HEADS]) * LOG2E
        lf = log_sigmoid(by_chunk(gt[(2 * d + 1) * A_HEADS:(2 * d + 2) * A_HEADS])) * LOG2E
        b = _lane_scan(lf, jnp.add, 0.0, d == 1)
        a = li - b
        cm = _lane_scan(a, jnp.maximum, -jnp.inf, d == 1)
        row_form += [by_token(a), by_token(b)]
        col_cm.append(cm)
        col_b.append(b)
    gr_ref[...] = jnp.concatenate(row_form, axis=0)
    pad = jnp.zeros((LANES - 2 * A_HEADS, LANES), _F32)
    for c in range(n_chunks):
        rows = slice(c * A_HEADS, (c + 1) * A_HEADS)
        toks = slice(c * A_CHUNK, (c + 1) * A_CHUNK)
        gc_ref[toks, :LANES] = jnp.concatenate([col_cm[0][rows], col_cm[1][rows], pad], axis=0).T
        gc_ref[toks, LANES:] = jnp.concatenate([col_b[0][rows], col_b[1][rows], pad], axis=0).T

    y = jnp.dot(x, wqvo_ref[...], preferred_element_type=_F32)
    q_ref[...] = (y[:, :A_QK_DIM] * (A_DK ** -0.5)).astype(_BF16)
    v_ref[...] = y[:, A_QK_DIM:A_QK_DIM + D_MODEL].astype(_BF16)
    o_ref[...] = y[:, A_QK_DIM + D_MODEL:]


def _a_proj(h_parts, g, mod_l, w_in, layer, wgt, bgate, dims):
    rows = dims.rows
    tm = ROW_TILE
    n_gate = 4 * A_HEADS
    return pl.pallas_call(
        functools.partial(_a_proj_kernel, n_h=len(h_parts), layer=layer),
        out_shape=(
            jax.ShapeDtypeStruct((rows, A_QK_DIM), _BF16),
            jax.ShapeDtypeStruct((A_QK_DIM, rows), _BF16),
            jax.ShapeDtypeStruct((rows, D_MODEL), _BF16),
            jax.ShapeDtypeStruct((rows, D_MODEL), _F32),
            jax.ShapeDtypeStruct((n_gate, rows), _F32),
            jax.ShapeDtypeStruct((rows, 2 * LANES), _F32),
        ),
        grid=(rows // tm,),
        in_specs=_h_specs(tm, len(h_parts) == 2) + [
            _resident((1, D_MODEL)),
            _mod_spec(dims.n_ctx_rows, dims.seq, tm),
            _weight_spec(),
            _resident(wgt.shape),
            _resident((n_gate, 1)),
        ],
        out_specs=(
            _row_spec(tm, A_QK_DIM),
            pl.BlockSpec((A_QK_DIM, tm), lambda i: (0, i)),
            _row_spec(tm, D_MODEL),
            _row_spec(tm, D_MODEL),
            pl.BlockSpec((n_gate, tm), lambda i: (0, i)),
            _row_spec(tm, 2 * LANES),
        ),
        scratch_shapes=[pltpu.VMEM((D_MODEL, A_QK_DIM + 2 * D_MODEL), _BF16),
                        pltpu.VMEM((A_QK_DIM + 4 * A_HEADS, D_MODEL), _BF16)] + _stage_scratch(),
        compiler_params=_cparams("arbitrary"),
        name="a_proj",
    )(*h_parts, g, mod_l, w_in, wgt, bgate)


def _lane_scan(x, op, fill, reverse):
    lane = lax.broadcasted_iota(jnp.int32, x.shape, 1)
    k = 1
    while k < LANES:
        if reverse:
            shifted = jnp.where(lane < LANES - k, pltpu.roll(x, LANES - k, axis=1), fill)
        else:
            shifted = jnp.where(lane >= k, pltpu.roll(x, k, axis=1), fill)
        x = op(x, shifted)
        k *= 2
    return x


def _a_scan_kernel(*refs, step_chunks, has_init, emit_state):
    n_state = 5
    (qf_ref, ktf_ref, vf_ref, grf_ref, gcf_ref, qb_ref, ktb_ref, vb_ref, grb_ref, gcb_ref), refs = refs[:10], refs[10:]
    init_refs, refs = (refs[:n_state], refs[n_state:]) if has_init else ((), refs)
    (hf_ref, hb_ref), refs = refs[:2], refs[2:]
    final_refs, refs = (refs[:n_state], refs[n_state:]) if emit_state else ((), refs)
    cf_ref, cb_ref, mrf_ref, mrb_ref, mc_ref = state_refs = refs

    @pl.when(pl.program_id(1) == 0)
    def _():
        if has_init:
            for ref, init in zip(state_refs, init_refs):
                ref[...] = init[...]
        else:
            cf_ref[...] = jnp.zeros(cf_ref.shape, _F32)
            cb_ref[...] = jnp.zeros(cb_ref.shape, _F32)
            mrf_ref[...] = jnp.full(mrf_ref.shape, -jnp.inf, _F32)
            mrb_ref[...] = jnp.full(mrb_ref.shape, -jnp.inf, _F32)
            mc_ref[...] = jnp.full(mc_ref.shape, -jnp.inf, _F32)

    t_idx = lax.broadcasted_iota(jnp.int32, (A_CHUNK, A_CHUNK), 0)
    s_idx = lax.broadcasted_iota(jnp.int32, (A_CHUNK, A_CHUNK), 1)
    ones_blk = jnp.ones((A_CHUNK, A_DV), _BF16)
    lane = lax.broadcasted_iota(jnp.int32, (1, LANES), 1)
    for sub in range(step_chunks):
        toks = (slice(sub * A_CHUNK, (sub + 1) * A_CHUNK),
                slice((step_chunks - 1 - sub) * A_CHUNK, (step_chunks - sub) * A_CHUNK))
        _a_chunk_pair(toks, (qf_ref, qb_ref), (ktf_ref, ktb_ref), (vf_ref, vb_ref), (grf_ref, grb_ref),
                      (gcf_ref, gcb_ref), (hf_ref, hb_ref), (cf_ref, cb_ref), (mrf_ref, mrb_ref), mc_ref,
                      t_idx, s_idx, ones_blk, lane)

    if emit_state:
        @pl.when(pl.program_id(1) == pl.num_programs(1) - 1)
        def _():
            for ref, final in zip(state_refs, final_refs):
                final[...] = ref[...]


def _a_chunk_pair(toks, q_refs, kt_refs, v_refs, gr_refs, gc_refs, h_refs, c_refs, mr_refs, mc_ref,
                  t_idx, s_idx, ones_blk, lane):
    m_col = mc_ref[0:1, :]
    m_col_new = []
    dirs = []
    for d in range(2):
        gr_ref, gc_ref, mr_ref, tok = gr_refs[d], gc_refs[d], mr_refs[d], toks[d]
        last = A_CHUNK - 1 if d == 0 else 0
        a = gr_ref[2 * d * A_HEADS:(2 * d + 1) * A_HEADS, tok]
        b = gr_ref[(2 * d + 1) * A_HEADS:(2 * d + 2) * A_HEADS, tok]
        m_row = mr_ref[:, 0:1]
        gg_end = jnp.maximum(m_row, jnp.max(a, axis=1, keepdims=True))
        ws = jnp.exp2(a - gg_end)
        decay = jnp.exp2(m_row - gg_end)
        mr_ref[...] = jnp.broadcast_to(b[:, last:last + 1] + gg_end, mr_ref.shape)
        gg_c = jnp.maximum(m_col, gc_ref[tok, :LANES])
        b_plus_gg = gc_ref[tok, LANES:] + gg_c
        clamp_c = jnp.exp2(-b_plus_gg)
        m_col_new.append(b_plus_gg[last:last + 1, :])
        dirs.append((a, m_row, ws, decay, gg_c, clamp_c))
    mc_ref[...] = jnp.broadcast_to(jnp.where(lane < A_HEADS, m_col_new[0], m_col_new[1]), mc_ref.shape)

    pair_lane = lax.broadcasted_iota(jnp.int32, (A_CHUNK, 2 * A_DK), 1)
    zeros_kt = jnp.zeros((A_DK, A_CHUNK), _BF16)
    zeros_c = jnp.zeros((A_DK, 2 * A_DV), _BF16)
    for hp in range(A_HEADS // 2):
        for d in range(2):
            q_ref, kt_ref, v_ref, h_ref, c_ref, tok = q_refs[d], kt_refs[d], v_refs[d], h_refs[d], c_refs[d], toks[d]
            a, m_row, ws, decay, gg_c, clamp_c = dirs[d]
            mask = (s_idx <= t_idx) if d == 0 else (s_idx >= t_idx)
            heads = (2 * hp, 2 * hp + 1)
            q2 = q_ref[tok, 2 * hp * A_DK:(2 * hp + 2) * A_DK]
            kts = [kt_ref[hd * A_DK:(hd + 1) * A_DK, tok] for hd in heads]
            kt_bd = jnp.concatenate([jnp.concatenate([kts[0], zeros_kt], axis=1),
                                     jnp.concatenate([zeros_kt, kts[1]], axis=1)], axis=0)
            s2 = jnp.dot(q2, kt_bd, preferred_element_type=_F32)
            ggs = [jnp.broadcast_to(gg_c[:, d * A_HEADS + hd:d * A_HEADS + hd + 1], (A_CHUNK, A_CHUNK))
                   for hd in heads]
            e_state = jnp.exp2(jnp.where(pair_lane < A_DK, m_row[heads[0]:heads[0] + 1, :] - ggs[0],
                                         m_row[heads[1]:heads[1] + 1, :] - ggs[1]))
            qe = q2.astype(_F32) * e_state
            for i, hd in enumerate(heads):
                col = d * A_HEADS + hd
                clamp = jnp.broadcast_to(clamp_c[:, col:col + 1], (A_CHUNK, A_DV))
                v = v_ref[tok, hd * A_DV:(hd + 1) * A_DV]
                v_ext = jnp.concatenate([v, ones_blk], axis=1)
                c_old = c_ref[hd]
                e_keys = jnp.exp2(jnp.where(mask, a[hd:hd + 1, :] - ggs[i], -jnp.inf))
                p = jnp.concatenate([s2[:, i * A_CHUNK:(i + 1) * A_CHUNK] * e_keys, qe], axis=1).astype(_BF16)
                c_rows = [c_old.astype(_BF16), zeros_c] if i == 0 else [zeros_c, c_old.astype(_BF16)]
                rhs = jnp.concatenate([v_ext] + c_rows, axis=0)
                num = jnp.dot(p, rhs, preferred_element_type=_F32)
                den = jnp.maximum(jnp.abs(num[:, A_DV:]), clamp)
                h_ref[tok, hd * A_DV:(hd + 1) * A_DV] = num[:, :A_DV] / den
                kw = (kts[i].astype(_F32) * ws[hd:hd + 1, :]).astype(_BF16)
                c_ref[hd] = decay[hd:hd + 1, :] * c_old + jnp.dot(kw, v_ext, preferred_element_type=_F32)


def _a_scan_call(q, kt, v, gr, gc, dims, step_chunks, first_row, seq_rows, init=None, emit_state=False):
    blk_rows = step_chunks * A_CHUNK
    n_steps = seq_rows // blk_rows
    assert seq_rows % blk_rows == 0 and first_row % blk_rows == 0
    blk0 = first_row // blk_rows

    def fwd_blk(b, j):
        return blk0 + b * n_steps + j

    def bwd_blk(b, j):
        return blk0 + b * n_steps + (n_steps - 1 - j)

    def specs(blk):
        return [
            pl.BlockSpec((blk_rows, A_QK_DIM), lambda b, j: (blk(b, j), 0)),
            pl.BlockSpec((A_QK_DIM, blk_rows), lambda b, j: (0, blk(b, j))),
            pl.BlockSpec((blk_rows, D_MODEL), lambda b, j: (blk(b, j), 0)),
            pl.BlockSpec((4 * A_HEADS, blk_rows), lambda b, j: (0, blk(b, j))),
            pl.BlockSpec((blk_rows, 2 * LANES), lambda b, j: (blk(b, j), 0)),
        ]

    state_shapes = [(A_HEADS, A_DK, 2 * A_DV)] * 2 + [(A_HEADS, LANES)] * 3
    state_specs = [pl.BlockSpec((None,) + shp, lambda b, j, n=len(shp): (b,) + (0,) * n) for shp in state_shapes]
    h_shape = jax.ShapeDtypeStruct((dims.batch * seq_rows, D_MODEL), _F32)
    out_shape = [h_shape, h_shape]
    out_specs = [pl.BlockSpec((blk_rows, D_MODEL), lambda b, j: (b * n_steps + j, 0)),
                 pl.BlockSpec((blk_rows, D_MODEL), lambda b, j: (b * n_steps + (n_steps - 1 - j), 0))]
    if emit_state:
        out_shape += [jax.ShapeDtypeStruct((dims.batch,) + shp, _F32) for shp in state_shapes]
        out_specs += state_specs
    return pl.pallas_call(
        functools.partial(_a_scan_kernel, step_chunks=step_chunks, has_init=init is not None,
                          emit_state=emit_state),
        out_shape=tuple(out_shape),
        grid=(dims.batch, n_steps),
        in_specs=specs(fwd_blk) + specs(bwd_blk) + (state_specs if init is not None else []),
        out_specs=tuple(out_specs),
        scratch_shapes=[pltpu.VMEM(shp, _F32) for shp in state_shapes],
        compiler_params=_cparams("arbitrary", "arbitrary"),
        name="a_scan",
    )(q, kt, v, gr, gc, q, kt, v, gr, gc, *(init or ()))


def _a_scan(q, kt, v, gr, gc, dims):
    hf_c, hb_c, *state = _a_scan_call(q, kt, v, gr, gc, dims, A_STEP_CHUNKS_CTX, 0, dims.ctx_len,
                                      emit_state=True)
    hf_l, hb_l = _a_scan_call(q, kt, v, gr, gc, dims, A_STEP_CHUNKS_LAT, dims.n_ctx_rows, dims.seq,
                              init=state)
    return (hf_l, hf_c), (hb_l, hb_c)


def _a_mix(hf_lat_ref, hf_ctx_ref, hb_lat_ref, hb_ctx_ref, o_ref, hg_ref, tile0, cols):
    hs = _load_h((hf_lat_ref, hf_ctx_ref), tile0, cols) + _load_h((hb_lat_ref, hb_ctx_ref), tile0, cols)
    parts = []
    for c0 in range(0, hs.shape[1], A_DV):
        x = hs[:, c0:c0 + A_DV]
        parts.append(x * lax.rsqrt(jnp.mean(x * x, axis=-1, keepdims=True) + NORM_EPS))
    y = jnp.concatenate(parts, axis=1) * hg_ref[:, cols]
    return (jax.nn.sigmoid(o_ref[:, cols]) * y).astype(_BF16)


ROPE_HALF = B_HEAD_DIM // 4


def _swap_halves_lanes(x):
    lane = lax.broadcasted_iota(jnp.int32, x.shape, 1)
    fwd = pltpu.roll(x, LANES - ROPE_HALF, axis=1)
    back = pltpu.roll(x, ROPE_HALF, axis=1)
    return jnp.where(lane % (2 * ROPE_HALF) < ROPE_HALF, fwd, back)


def _swap_halves_rows(x):
    parts = []
    for r0 in range(0, x.shape[0], 2 * ROPE_HALF):
        parts += [x[r0 + ROPE_HALF:r0 + 2 * ROPE_HALF], x[r0:r0 + ROPE_HALF]]
    return jnp.concatenate(parts, axis=0)


def _b_proj_kernel(h_ref, g_ref, mod_ref, wqkv_hbm, cos_ref, sin_ref, cost_ref, sint_ref,
                   qt_ref, k_ref, vt_ref, wqvt_ref, wk_ref, stage_ref, sem_ref, *, layer):
    @pl.when(pl.program_id(0) == 0)
    def _():
        def store_t(dst0, c, width, tile):
            wqvt_ref[dst0 + c:dst0 + c + width, :] = tile.T.astype(_BF16)

        def store_k(c, width, tile):
            wk_ref[:, c:c + width] = tile.astype(_BF16)

        _stage_weights(
            _col_tiles(wqkv_hbm, layer, 0, D_MODEL, functools.partial(store_t, 0))
            + _col_tiles(wqkv_hbm, layer, D_MODEL, B_KV_DIM, store_k)
            + _col_tiles(wqkv_hbm, layer, D_MODEL + B_KV_DIM, B_KV_DIM, functools.partial(store_t, D_MODEL)),
            stage_ref, sem_ref)

    x = _norm_mod(h_ref[...], g_ref[...], _mod_part(mod_ref, 0), _mod_part(mod_ref, 1)).astype(_BF16)
    nt = (((1,), (1,)), ((), ()))
    yt = lax.dot_general(wqvt_ref[...], x, nt, preferred_element_type=_F32)
    cost, sint = cost_ref[...], sint_ref[...]
    for hd in range(B_Q_HEADS):
        rows = slice(hd * B_HEAD_DIM, (hd + 1) * B_HEAD_DIM)
        qt = yt[rows, :]
        qt = (qt * cost + _swap_halves_rows(qt) * sint) * (B_HEAD_DIM ** -0.5 * LOG2E)
        qt_ref[rows, :] = qt.astype(_BF16)
    vt_ref[...] = yt[D_MODEL:, :].astype(_BF16)
    cos, sin = cos_ref[...], sin_ref[...]
    k = jnp.dot(x, wk_ref[...], preferred_element_type=_F32)
    for c0 in range(0, B_KV_DIM, LANES):
        kc = k[:, c0:c0 + LANES]
        k_ref[:, c0:c0 + LANES] = (kc * cos + _swap_halves_lanes(kc) * sin).astype(_BF16)


def _b_proj(h, g, mod_l, w_qkv, layer, rope, dims):
    rows = h.shape[0]
    tm = ROW_TILE
    cos, sin, cost, sint = rope
    return pl.pallas_call(
        functools.partial(_b_proj_kernel, layer=layer),
        out_shape=(
            jax.ShapeDtypeStruct((D_MODEL, rows), _BF16),
            jax.ShapeDtypeStruct((rows, B_KV_DIM), _BF16),
            jax.ShapeDtypeStruct((B_KV_DIM, rows), _BF16),
        ),
        grid=(rows // tm,),
        in_specs=[
            _row_spec(tm, D_MODEL),
            _resident((1, D_MODEL)),
            _mod_spec(dims.n_ctx_rows, dims.seq, tm),
            _weight_spec(),
            _row_spec(tm, LANES),
            _row_spec(tm, LANES),
            pl.BlockSpec((B_HEAD_DIM, tm), lambda i: (0, i)),
            pl.BlockSpec((B_HEAD_DIM, tm), lambda i: (0, i)),
        ],
        out_specs=(
            pl.BlockSpec((D_MODEL, tm), lambda i: (0, i)),
            _row_spec(tm, B_KV_DIM),
            pl.BlockSpec((B_KV_DIM, tm), lambda i: (0, i)),
        ),
        scratch_shapes=[pltpu.VMEM((D_MODEL + B_KV_DIM, D_MODEL), _BF16),
                        pltpu.VMEM((D_MODEL, B_KV_DIM), _BF16)] + _stage_scratch(),
        compiler_params=_cparams("arbitrary"),
        name="b_proj",
    )(h, g, mod_l, w_qkv, cos, sin, cost, sint)


B_ONES_ROWS = 16
B_STEP_BLOCKS_CTX = 2
B_STEP_BLOCKS_LAT = 4


def _b_attn_kernel(sink_ref, win_l_ref, win_r_ref, qt_ref, kl_ref, kc_ref, kr_ref, kx_ref,
                   vtl_ref, vtc_ref, vtr_ref, vtx_ref, o_ref, s_ref,
                   *, n_ctx_blocks, blocks_per_seq, ctx_len, step_blocks, first_block):
    first = first_block + pl.program_id(0) * step_blocks
    is_lat = first >= n_ctx_blocks
    n = (first - n_ctx_blocks) % blocks_per_seq
    neg = -jnp.inf
    win_l = jnp.where(is_lat, win_l_ref[...], neg)
    win_r = jnp.where(is_lat, win_r_ref[...], neg)
    bias_c = jnp.where(is_lat, 0.0, neg)
    bias_first_l = jnp.where(jnp.logical_and(is_lat, n >= 1), win_l_ref[...], neg)
    bias_last_r = jnp.where(jnp.logical_and(is_lat, n + step_blocks <= blocks_per_seq - 1), win_r_ref[...], neg)

    n_q = B_GROUP * B_BLOCK
    head_of_lane = lax.broadcasted_iota(jnp.int32, (1, n_q), 1) // B_BLOCK
    ones_rows = jnp.ones((B_ONES_ROWS, B_BLOCK), _BF16)
    n_ctx_tiles = ctx_len // B_BLOCK

    def local_tiles(blk):
        if first_block < n_ctx_blocks:
            return []
        own = slice(blk * B_BLOCK, (blk + 1) * B_BLOCK)
        if blk == 0:
            left = (kl_ref, vtl_ref, slice(0, B_BLOCK), bias_first_l)
        else:
            left = (kc_ref, vtc_ref, slice((blk - 1) * B_BLOCK, blk * B_BLOCK), win_l)
        if blk == step_blocks - 1:
            right = (kr_ref, vtr_ref, slice(0, B_BLOCK), bias_last_r)
        else:
            right = (kc_ref, vtc_ref, slice((blk + 1) * B_BLOCK, (blk + 2) * B_BLOCK), win_r)
        return [left, (kc_ref, vtc_ref, own, bias_c), right]

    def scores(blk, g):
        cols = slice(blk * B_BLOCK, (blk + 1) * B_BLOCK)
        qt = jnp.concatenate(
            [qt_ref[(g * B_GROUP + j) * B_HEAD_DIM:(g * B_GROUP + j + 1) * B_HEAD_DIM, cols]
             for j in range(B_GROUP)], axis=1)
        sink = jnp.zeros((1, n_q), _F32)
        for j in range(B_GROUP):
            sink = jnp.where(head_of_lane == j, sink_ref[g * B_GROUP + j] * LOG2E, sink)
        ks = slice(g * B_HEAD_DIM, (g + 1) * B_HEAD_DIM)
        k_tiles = [(k_ref[rows, ks], bias) for k_ref, _, rows, bias in local_tiles(blk)]
        for t in range(n_ctx_tiles):
            k_tiles.append((kx_ref[t * B_BLOCK:(t + 1) * B_BLOCK, ks], None))
        m_tile = None
        s_all = jnp.dot(jnp.concatenate([k for k, _ in k_tiles], axis=0), qt, preferred_element_type=_F32)
        for t, (_, bias) in enumerate(k_tiles):
            s = s_all[t * B_BLOCK:(t + 1) * B_BLOCK]
            if bias is not None:
                s = s + bias
            s_ref[blk, g, t] = s
            m_tile = s if m_tile is None else jnp.maximum(m_tile, s)
        return jnp.maximum(sink, jnp.max(m_tile, axis=0, keepdims=True)), sink

    def attend(blk, g, m, sink):
        ks = slice(g * B_HEAD_DIM, (g + 1) * B_HEAD_DIM)
        vt_tiles = [vt_ref[ks, cols] for _, vt_ref, cols, _ in local_tiles(blk)]
        for t in range(n_ctx_tiles):
            vt_tiles.append(vtx_ref[ks, t * B_BLOCK:(t + 1) * B_BLOCK])
        acc = jnp.zeros((B_HEAD_DIM + B_ONES_ROWS, n_q), _F32)
        for t, vt in enumerate(vt_tiles):
            p = jnp.exp2(s_ref[blk, g, t] - m).astype(_BF16)
            vt_ext = jnp.concatenate([vt, ones_rows], axis=0)
            acc = acc + jnp.dot(vt_ext, p, preferred_element_type=_F32)
        denom = jnp.exp2(sink - m) + acc[B_HEAD_DIM:B_HEAD_DIM + 1, :]
        out_t = acc[:B_HEAD_DIM, :] * (1.0 / denom)
        rows = slice(blk * B_BLOCK, (blk + 1) * B_BLOCK)
        for pair in range(B_GROUP // 2):
            two = jnp.concatenate([out_t[:, (2 * pair) * B_BLOCK:(2 * pair + 1) * B_BLOCK],
                                   out_t[:, (2 * pair + 1) * B_BLOCK:(2 * pair + 2) * B_BLOCK]], axis=0)
            c0 = (g * B_GROUP + 2 * pair) * B_HEAD_DIM
            o_ref[rows, c0:c0 + 2 * B_HEAD_DIM] = two.T.astype(_BF16)

    units = [(blk, g) for blk in range(step_blocks) for g in range(B_KV_HEADS)]
    pending = [scores(*units[0]), scores(*units[1])]
    for i, unit in enumerate(units):
        if i + 2 < len(units):
            pending.append(scores(*units[i + 2]))
        attend(*unit, *pending[i])


def _window_bias():
    key = np.arange(B_BLOCK)[:, None]
    qry = np.arange(B_BLOCK)[None, :]
    prev_blk = np.where(key >= qry, 0.0, -np.inf).astype(np.float32)
    next_blk = np.where(key <= qry, 0.0, -np.inf).astype(np.float32)
    return jnp.asarray(np.tile(prev_blk, (1, B_GROUP))), jnp.asarray(np.tile(next_blk, (1, B_GROUP)))


def _b_attn_call(sinks, qt, k, vt, dims, step_blocks, first_block, n_blocks):
    step_rows = step_blocks * B_BLOCK
    ncc = dims.ctx_len // B_BLOCK
    ncl = dims.seq // B_BLOCK
    lat0 = dims.batch * ncc
    nblk = k.shape[0] // B_BLOCK
    assert first_block % step_blocks == 0 and n_blocks % step_blocks == 0
    assert ncc % step_blocks == 0 if first_block < lat0 else ncl % step_blocks == 0
    assert first_block >= lat0 or first_block + n_blocks <= lat0
    off = first_block // step_blocks

    def batch_of(s):
        i = first_block + s * step_blocks
        return jnp.where(i < lat0, i // ncc, (i - lat0) // ncl)

    def left(s):
        return jnp.maximum(first_block + s * step_blocks - 1, 0)

    def right(s):
        return jnp.minimum(first_block + (s + 1) * step_blocks, nblk - 1)

    n_tiles = 3 + ncc
    return pl.pallas_call(
        functools.partial(_b_attn_kernel, n_ctx_blocks=lat0, blocks_per_seq=ncl, ctx_len=dims.ctx_len,
                          step_blocks=step_blocks, first_block=first_block),
        out_shape=jax.ShapeDtypeStruct((n_blocks * B_BLOCK, D_MODEL), _BF16),
        grid=(n_blocks // step_blocks,),
        in_specs=[
            pl.BlockSpec(memory_space=pltpu.SMEM),
            _resident((B_BLOCK, B_GROUP * B_BLOCK)),
            _resident((B_BLOCK, B_GROUP * B_BLOCK)),
            pl.BlockSpec((D_MODEL, step_rows), lambda s: (0, off + s)),
            pl.BlockSpec((B_BLOCK, B_KV_DIM), lambda s: (left(s), 0)),
            pl.BlockSpec((step_rows, B_KV_DIM), lambda s: (off + s, 0)),
            pl.BlockSpec((B_BLOCK, B_KV_DIM), lambda s: (right(s), 0)),
            pl.BlockSpec((dims.ctx_len, B_KV_DIM), lambda s: (batch_of(s), 0)),
            pl.BlockSpec((B_KV_DIM, B_BLOCK), lambda s: (0, left(s))),
            pl.BlockSpec((B_KV_DIM, step_rows), lambda s: (0, off + s)),
            pl.BlockSpec((B_KV_DIM, B_BLOCK), lambda s: (0, right(s))),
            pl.BlockSpec((B_KV_DIM, dims.ctx_len), lambda s: (0, batch_of(s))),
        ],
        out_specs=pl.BlockSpec((step_rows, D_MODEL), lambda s: (s, 0)),
        scratch_shapes=[pltpu.VMEM((step_blocks, B_KV_HEADS, n_tiles, B_BLOCK, B_GROUP * B_BLOCK), _F32)],
        compiler_params=_cparams("arbitrary"),
        name="b_attn",
    )(sinks, *_window_bias(), qt, k, k, k, k, vt, vt, vt, vt)


def _b_attn(sinks, qt, k, vt, dims):
    lat0 = dims.n_ctx_rows // B_BLOCK
    nblk = k.shape[0] // B_BLOCK
    y_ctx = _b_attn_call(sinks, qt, k, vt, dims, B_STEP_BLOCKS_CTX, 0, lat0)
    y_lat = _b_attn_call(sinks, qt, k, vt, dims, B_STEP_BLOCKS_LAT, lat0, nblk - lat0)
    return y_lat, y_ctx


def _c_proj_kernel(h_ref, g_ref, mod_ref, win_hbm, bg_ref, u_ref, w_ref, stage_ref, sem_ref, *, layer):
    @pl.when(pl.program_id(0) == 0)
    def _():
        def store(c, width, tile):
            w_ref[:, c:c + width] = tile.astype(_BF16)

        _stage_weights(_col_tiles(win_hbm, layer, 0, 3 * D_MODEL, store), stage_ref, sem_ref)

    x = _norm_mod(h_ref[...], g_ref[...], _mod_part(mod_ref, 0), _mod_part(mod_ref, 1)).astype(_BF16)
    y = jnp.dot(x, w_ref[...], preferred_element_type=_F32)
    bg_ref[...] = y[:, :D_MODEL]
    u_ref[...] = y[:, D_MODEL:2 * D_MODEL] * y[:, 2 * D_MODEL:]


def _c_proj(h, g, mod_l, w_in, layer, dims):
    rows = h.shape[0]
    tm = ROW_TILE
    return pl.pallas_call(
        functools.partial(_c_proj_kernel, layer=layer),
        out_shape=(jax.ShapeDtypeStruct((rows, D_MODEL), _F32),) * 2,
        grid=(rows // tm,),
        in_specs=[
            _row_spec(tm, D_MODEL),
            _resident((1, D_MODEL)),
            _mod_spec(dims.n_ctx_rows, dims.seq, tm),
            _weight_spec(),
        ],
        out_specs=(_row_spec(tm, D_MODEL),) * 2,
        scratch_shapes=[pltpu.VMEM((D_MODEL, 3 * D_MODEL), _BF16)] + _stage_scratch(),
        compiler_params=_cparams("arbitrary"),
        name="c_proj",
    )(h, g, mod_l, w_in)


def _c_mix(bg_ref, u_ref, up_ref, un_ref, cw_ref, cb_ref, tile, dims, cols):
    tm = u_ref.shape[0]
    u = u_ref[:, cols]
    row = lax.broadcasted_iota(jnp.int32, (tm, 1), 0)
    g_row = tile * tm + row
    in_ctx = g_row < dims.n_ctx_rows
    pos = jnp.where(in_ctx, g_row % dims.ctx_len, (g_row - dims.n_ctx_rows) % dims.seq)
    length = jnp.where(in_ctx, dims.ctx_len, dims.seq)
    prev = jnp.where(row == 0, up_ref[SUBLANES - 1:SUBLANES, cols], pltpu.roll(u, 1, axis=0))
    nxt = jnp.where(row == tm - 1, un_ref[0:1, cols], pltpu.roll(u, tm - 1, axis=0))
    prev = jnp.where(pos == 0, 0.0, prev)
    nxt = jnp.where(pos == length - 1, 0.0, nxt)
    conv = prev * cw_ref[0:1, cols] + u * cw_ref[1:2, cols] + nxt * cw_ref[2:3, cols] + cb_ref[:, cols]
    return (bg_ref[:, cols] * conv).astype(_BF16)


_N_MIX_REFS = (6, 2, 6)


def _post_kernel(*refs, kind, n_h, tile0, final_norm, dims, out_layer, layer):
    n_mix = _N_MIX_REFS[kind]
    h_refs, mod_ref = refs[:n_h], refs[n_h]
    mix_refs = refs[n_h + 1:n_h + 1 + n_mix]
    (wo_hbm, g_ref, w1_hbm, w2_hbm, fg_ref, out_ref,
     wo_ref, w1_ref, w2_ref, stage_ref, sem_ref) = refs[n_h + 1 + n_mix:]
    n_chunks = MLP_HIDDEN // MLP_HIDDEN_CHUNK
    tiles_wo = D_MODEL // STAGE_COLS
    tiles_w1 = MLP_HIDDEN_CHUNK // STAGE_COLS
    tiles_w2 = D_MODEL // STAGE_COLS

    def compute(need):
        proj = None
        for c0 in range(0, D_MODEL, OUT_PROJ_COLS):
            cols = slice(c0, c0 + OUT_PROJ_COLS)
            if kind == 0:
                y = _a_mix(*mix_refs, tile0, cols)
            elif kind == 1:
                y = _load_h(mix_refs, 0, cols)
            else:
                y = _c_mix(*mix_refs, pl.program_id(0) + tile0, dims, cols)
            if c0 == 0:
                need(tiles_wo)
            part = jnp.dot(y, wo_ref[cols, :], preferred_element_type=_F32)
            proj = part if proj is None else proj + part
        h = _load_h(h_refs, tile0) + _mod_part(mod_ref, 2) * proj
        x = _norm_mod(h, g_ref[...], _mod_part(mod_ref, 3), _mod_part(mod_ref, 4)).astype(_BF16)
        acc = jnp.zeros(h.shape, _F32)
        for c in range(n_chunks):
            cols = slice(c * MLP_HIDDEN_CHUNK, (c + 1) * MLP_HIDDEN_CHUNK)
            need(tiles_wo + c * (tiles_w1 + tiles_w2) + tiles_w1)
            u = jnp.dot(x, w1_ref[:, cols], preferred_element_type=_F32)
            u = jnp.square(jnp.maximum(u, 0.0)).astype(_BF16)
            need(tiles_wo + (c + 1) * (tiles_w1 + tiles_w2))
            acc = acc + jnp.dot(u, w2_ref[cols, :], preferred_element_type=_F32)
        out = h + _mod_part(mod_ref, 5) * acc
        if final_norm:
            ms = jnp.mean(out * out, axis=-1, keepdims=True)
            out = out * lax.rsqrt(ms + NORM_EPS) * fg_ref[...]
        out_ref[...] = out

    @pl.when(pl.program_id(0) == 0)
    def _():
        def store_to(dst_ref, row0, c, width, tile):
            dst_ref[row0:row0 + tile.shape[0], c:c + width] = tile.astype(_BF16)

        jobs = _col_tiles(wo_hbm, out_layer, 0, D_MODEL, functools.partial(store_to, wo_ref, 0))
        for r0 in range(0, MLP_HIDDEN, MLP_HIDDEN_CHUNK):
            jobs += _col_tiles(w1_hbm, layer, r0, MLP_HIDDEN_CHUNK,
                               lambda c, width, tile, r0=r0: store_to(w1_ref, 0, r0 + c, width, tile))
            for c in range(0, D_MODEL, STAGE_COLS):
                jobs.append((w2_hbm.at[layer, pl.ds(r0, MLP_HIDDEN_CHUNK), pl.ds(c, STAGE_COLS)],
                             functools.partial(store_to, w2_ref, r0, c, STAGE_COLS)))
        compute(_WeightStager(jobs, stage_ref, sem_ref).need)

    @pl.when(pl.program_id(0) != 0)
    def _():
        compute(lambda n: None)


def _post(kind, h_parts, mod_l, mix, w_out, out_layer, g, w1, w2, layer, final_g, dims, tile0=0,
          final_norm=False):
    tm = ROW_TILE
    n_tiles = dims.rows // tm - tile0
    row = functools.partial(_row_spec, tm, D_MODEL, tile0)
    if kind == 0:
        (hf_l, hf_c), (hb_l, hb_c), o, head_g = mix
        mix = (hf_l, hf_c, hb_l, hb_c, o, head_g)
        mix_specs = _h_specs(tm, True, tile0) * 2 + [row(), _resident((1, D_MODEL))]
    elif kind == 1:
        assert tile0 == 0
        mix_specs = _h_specs(tm, True)
    else:
        bg, u, conv_w, conv_b = mix
        per = tm // SUBLANES
        last = dims.rows // SUBLANES - 1
        mix = (bg, u, u, u, conv_w, conv_b)
        mix_specs = [
            row(), row(),
            pl.BlockSpec((SUBLANES, D_MODEL), lambda i: (jnp.maximum((i + tile0) * per - 1, 0), 0)),
            pl.BlockSpec((SUBLANES, D_MODEL), lambda i: (jnp.minimum((i + tile0 + 1) * per, last), 0)),
            _resident(conv_w.shape), _resident((1, D_MODEL)),
        ]
    assert len(h_parts) == 1 or tile0 == 0
    h_specs = _h_specs(tm, True) if len(h_parts) == 2 else [row()]
    return pl.pallas_call(
        functools.partial(_post_kernel, kind=kind, n_h=len(h_parts), tile0=tile0,
                          final_norm=final_norm, dims=dims, out_layer=out_layer, layer=layer),
        out_shape=jax.ShapeDtypeStruct((n_tiles * tm, D_MODEL), _F32),
        grid=(n_tiles,),
        in_specs=h_specs + [_mod_spec(dims.n_ctx_rows, dims.seq, tm, tile0)] + mix_specs + [
            _weight_spec(),
            _resident((1, D_MODEL)),
            _weight_spec(),
            _weight_spec(),
            _resident((1, D_MODEL)),
        ],
        out_specs=_row_spec(tm, D_MODEL),
        scratch_shapes=[pltpu.VMEM((D_MODEL, D_MODEL), _BF16), pltpu.VMEM((D_MODEL, MLP_HIDDEN), _BF16),
                        pltpu.VMEM((MLP_HIDDEN, D_MODEL), _BF16)] + _stage_scratch(),
        compiler_params=_cparams("arbitrary"),
        name="post",
    )(*h_parts, mod_l, *mix, w_out, g, w1, w2, final_g)


class _Dims:
    def __init__(self, batch, seq, ctx_len):
        self.batch = batch
        self.seq = seq
        self.ctx_len = ctx_len
        self.n_ctx_rows = batch * ctx_len
        self.rows = self.n_ctx_rows + batch * seq


def _rope_tables(dims):
    n_freq = B_HEAD_DIM // 4
    t = np.arange(dims.seq)
    inv_freq = ROPE_BASE ** (-np.arange(n_freq, dtype=np.float64) / n_freq)
    ang_row = (t // GRID_W)[:, None] * inv_freq
    ang_col = (t % GRID_W)[:, None] * inv_freq
    ang = np.concatenate([ang_row, ang_row, ang_col, ang_col], axis=1)
    sign = np.tile(np.concatenate([-np.ones(n_freq), np.ones(n_freq)]), 2)
    cos = np.tile(np.cos(ang), (dims.batch, 1))
    sin = np.tile(np.sin(ang) * sign, (dims.batch, 1))
    cos = np.concatenate([np.ones((dims.n_ctx_rows, B_HEAD_DIM)), cos], axis=0).astype(np.float32)
    sin = np.concatenate([np.zeros((dims.n_ctx_rows, B_HEAD_DIM)), sin], axis=0).astype(np.float32)
    tables = (np.tile(cos, (1, 2)), np.tile(sin, (1, 2)), np.ascontiguousarray(cos.T), np.ascontiguousarray(sin.T))
    return tuple(jnp.asarray(tab) for tab in tables)


def kernel(x, c, ctx, c_ctx, ada_w, ada_b, norm_g, final_g, mlp_w1, mlp_w2,
           a_w_in, a_w_gate, a_b_gate, a_head_g, a_w_out,
           b_w_qkv, b_sinks, b_w_out, c_w_in, c_conv_w, c_conv_b, c_w_out):
    batch, seq, d = x.shape
    ctx_len = ctx.shape[1]
    depth = ada_w.shape[0]
    dims = _Dims(batch, seq, ctx_len)
    assert MLP_HIDDEN_CHUNK == D_MODEL and d == D_MODEL and seq % ROW_TILE == 0 and dims.n_ctx_rows % ROW_TILE == 0
    assert 1 + batch <= MOD_ROWS

    assert dims.n_ctx_rows == ROW_TILE
    h_parts = (x.reshape(-1, d), ctx.reshape(-1, d))
    cs = jnp.concatenate([c_ctx[None], c, jnp.zeros((MOD_ROWS - 1 - batch, d), _F32)], axis=0)
    mod = _ada_table(cs, ada_w, ada_b).reshape(depth, MOD_ROWS, 1, 6 * d)
    rope = _rope_tables(dims)
    fg = final_g.reshape(1, d)

    for l in range(depth):
        kind, j = l % N_MIXERS, l // N_MIXERS
        mod_l = mod[l]
        g0 = norm_g[l, 0].reshape(1, d)
        g1 = norm_g[l, 1].reshape(1, d)
        last_layer = l == depth - 1
        if kind == 0:
            wgt = a_w_gate[j].T.astype(_BF16)
            q, kt, v, o, gr, gc = _a_proj(h_parts, g0, mod_l, a_w_in, j, wgt, a_b_gate[j].reshape(-1, 1), dims)
            hf, hb = _a_scan(q, kt, v, gr, gc, dims)
            mix = (hf, hb, o, a_head_g[j].reshape(1, d))
            w_out = a_w_out
        elif kind == 1:
            (h,) = h_parts
            qt, k, vt = _b_proj(h, g0, mod_l, b_w_qkv, j, rope, dims)
            mix = _b_attn(b_sinks[j], qt, k, vt, dims)
            w_out = b_w_out
        else:
            (h,) = h_parts
            bg, u = _c_proj(h, g0, mod_l, c_w_in, j, dims)
            mix = (bg, u, c_conv_w[j], c_conv_b[j].reshape(1, d))
            w_out = c_w_out
        tile0 = dims.n_ctx_rows // ROW_TILE if last_layer else 0
        h_parts = (_post(kind, h_parts, mod_l, mix, w_out, j, g1, mlp_w1, mlp_w2, l, fg, dims,
                         tile0=tile0, final_norm=last_layer),)
    return h_parts[0].reshape(batch, seq, d)
```

```python
import functools

import jax
import jax.numpy as jnp
import numpy as np
from jax import lax
from jax.experimental import pallas as pl
from jax.experimental.pallas import tpu as pltpu

D_MODEL = 1024
GRID_W = 64
N_MIXERS = 3
A_HEADS = 8
A_QK_DIM = D_MODEL // 2
A_DK = A_QK_DIM // A_HEADS
A_DV = D_MODEL // A_HEADS
A_CHUNK = 128
A_STEP_CHUNKS_CTX = 2
A_STEP_CHUNKS_LAT = 4
B_Q_HEADS = 16
B_KV_HEADS = 4
B_HEAD_DIM = D_MODEL // B_Q_HEADS
B_GROUP = B_Q_HEADS // B_KV_HEADS
B_KV_DIM = B_KV_HEADS * B_HEAD_DIM
B_BLOCK = 128
ROPE_BASE = 10000.0
MLP_HIDDEN = 4 * D_MODEL
NORM_EPS = 1e-6
LOG2E = 1.4426950408889634

LANES = 128
SUBLANES = 8
VMEM_LIMIT_BYTES = 60 * 1024 * 1024

ROW_TILE = 512
MLP_HIDDEN_CHUNK = 1024
MOD_ROWS = 8
OUT_PROJ_COLS = 256
ADA_COL_TILE = 1536

_BF16 = jnp.bfloat16
_F32 = jnp.float32


def _cparams(*sem):
    return pltpu.CompilerParams(dimension_semantics=sem, vmem_limit_bytes=VMEM_LIMIT_BYTES)


def _resident(shape):
    nd = len(shape)
    return pl.BlockSpec(shape, lambda *_: (0,) * nd, pipeline_mode=pl.Buffered(1))


STAGE_COLS = 512
STAGE_DEPTH = 3


def _weight_spec():
    return pl.BlockSpec(memory_space=pl.ANY)


def _stage_scratch():
    return [pltpu.VMEM((STAGE_DEPTH, D_MODEL, STAGE_COLS), _F32), pltpu.SemaphoreType.DMA((STAGE_DEPTH,))]


class _WeightStager:
    def __init__(self, jobs, stage_ref, sem_ref):
        self.jobs, self.stage_ref, self.sem_ref = jobs, stage_ref, sem_ref
        self.started = self.done = 0

    def _copy(self, i):
        src = self.jobs[i][0]
        dst = self.stage_ref.at[i % STAGE_DEPTH, pl.ds(0, src.shape[0]), pl.ds(0, src.shape[1])]
        return pltpu.make_async_copy(src, dst, self.sem_ref.at[i % STAGE_DEPTH])

    def _fill(self):
        while self.started < min(self.done + STAGE_DEPTH, len(self.jobs)):
            self._copy(self.started).start()
            self.started += 1

    def need(self, n):
        self._fill()
        while self.done < n:
            src, store = self.jobs[self.done]
            self._copy(self.done).wait()
            store(self.stage_ref[self.done % STAGE_DEPTH, :src.shape[0], :src.shape[1]])
            self.done += 1
            self._fill()


def _stage_weights(jobs, stage_ref, sem_ref):
    _WeightStager(jobs, stage_ref, sem_ref).need(len(jobs))


def _col_tiles(w_hbm, layer, col0, ncols, store):
    jobs = []
    for c in range(0, ncols, STAGE_COLS):
        width = min(STAGE_COLS, ncols - c)
        jobs.append((w_hbm.at[layer, :, pl.ds(col0 + c, width)], functools.partial(store, c, width)))
    return jobs


def _norm_mod(h, g, shift, scale):
    ms = jnp.mean(h * h, axis=-1, keepdims=True)
    y = h * lax.rsqrt(ms + NORM_EPS) * g
    return y * (1.0 + scale) + shift


def _mod_part(mod_ref, k):
    return mod_ref[:, k * D_MODEL:(k + 1) * D_MODEL]


def _mod_spec(n_ctx_rows, seq, tm, tile0=0):
    def idx(i):
        r0 = (i + tile0) * tm
        grp = jnp.where(r0 < n_ctx_rows, 0, 1 + (r0 - n_ctx_rows) // seq)
        return (grp, 0, 0)
    return pl.BlockSpec((None, 1, 6 * D_MODEL), idx)


def _row_spec(tm, width, tile0=0):
    return pl.BlockSpec((tm, width), lambda i: (i + tile0, 0))


def _h_specs(tm, split, tile0=0):
    if not split:
        return [_row_spec(tm, D_MODEL, tile0)]
    return [pl.BlockSpec((tm, D_MODEL), lambda i: (jnp.maximum(i + tile0 - 1, 0), 0)),
            _resident((tm, D_MODEL))]


def _load_h(h_refs, tile0=0, cols=slice(None)):
    if len(h_refs) == 1:
        return h_refs[0][:, cols]
    lat_ref, ctx_ref = h_refs
    if tile0 > 0:
        return lat_ref[:, cols]
    return jnp.where(pl.program_id(0) == 0, ctx_ref[:, cols], lat_ref[:, cols])


def _ada_kernel(cs_ref, w_ref, b_ref, o_ref):
    cs = cs_ref[...]
    s = (cs * jax.nn.sigmoid(cs)).astype(_BF16)
    o_ref[...] = jnp.dot(s, w_ref[...].astype(_BF16), preferred_element_type=_F32) + b_ref[...]


def _ada_table(cs, ada_w, ada_b):
    depth = ada_w.shape[0]
    n = ada_w.shape[2]
    return pl.pallas_call(
        _ada_kernel,
        out_shape=jax.ShapeDtypeStruct((depth, MOD_ROWS, n), _F32),
        grid=(depth, n // ADA_COL_TILE),
        in_specs=[
            pl.BlockSpec((MOD_ROWS, D_MODEL), lambda l, j: (0, 0)),
            pl.BlockSpec((None, D_MODEL, ADA_COL_TILE), lambda l, j: (l, 0, j)),
            pl.BlockSpec((None, 1, ADA_COL_TILE), lambda l, j: (l, 0, j)),
        ],
        out_specs=pl.BlockSpec((None, MOD_ROWS, ADA_COL_TILE), lambda l, j: (l, 0, j)),
        compiler_params=_cparams("arbitrary", "arbitrary"),
        name="ada_table",
    )(cs, ada_w, ada_b.reshape(depth, 1, n))


def _a_proj_kernel(*refs, n_h, layer):
    h_refs, (g_ref, mod_ref, win_hbm, wgt_ref, bg_ref,
             q_ref, kt_ref, v_ref, o_ref, gr_ref, gc_ref,
             wqvo_ref, wkt_ref, stage_ref, sem_ref) = refs[:n_h], refs[n_h:]

    @pl.when(pl.program_id(0) == 0)
    def _():
        def store_qvo(dst0, c, width, tile):
            wqvo_ref[:, dst0 + c:dst0 + c + width] = tile.astype(_BF16)

        def store_kt(c, width, tile):
            wkt_ref[c:c + width, :] = tile.T.astype(_BF16)

        _stage_weights(
            _col_tiles(win_hbm, layer, 0, A_QK_DIM, functools.partial(store_qvo, 0))
            + _col_tiles(win_hbm, layer, A_QK_DIM, A_QK_DIM, store_kt)
            + _col_tiles(win_hbm, layer, 2 * A_QK_DIM, 2 * D_MODEL, functools.partial(store_qvo, A_QK_DIM)),
            stage_ref, sem_ref)
        wkt_ref[A_QK_DIM:, :] = wgt_ref[...]

    x = _norm_mod(_load_h(h_refs), g_ref[...], _mod_part(mod_ref, 0), _mod_part(mod_ref, 1)).astype(_BF16)
    nt = (((1,), (1,)), ((), ()))
    ktg = lax.dot_general(wkt_ref[...], x, nt, preferred_element_type=_F32)
    kt_ref[...] = ktg[:A_QK_DIM].astype(_BF16)
    gt = ktg[A_QK_DIM:] + bg_ref[...]

    n_chunks = gt.shape[1] // A_CHUNK

    def by_chunk(rows):
        return jnp.concatenate([rows[:, c * A_CHUNK:(c + 1) * A_CHUNK] for c in range(n_chunks)], axis=0)

    def by_token(x):
        return jnp.concatenate([x[c * A_HEADS:(c + 1) * A_HEADS] for c in range(n_chunks)], axis=1)

    def log_sigmoid(z):
        return jnp.minimum(z, 0.0) - jnp.log1p(jnp.exp(-jnp.abs(z)))

    row_form, col_cm, col_b = [], [], []
    for d in range(2):
        li = by_chunk(gt[2 * d * A_HEADS:(2 * d + 1) * A_HEADS]) * LOG2E
        lf = log_sigmoid(by_chunk(gt[(2 * d + 1) * A_HEADS:(2 * d + 2) * A_HEADS])) * LOG2E
        b = _lane_scan(lf, jnp.add, 0.0, d == 1)
        a = li - b
        cm = _lane_scan(a, jnp.maximum, -jnp.inf, d == 1)
        row_form += [by_token(a), by_token(b)]
        col_cm.append(cm)
        col_b.append(b)
    gr_ref[...] = jnp.concatenate(row_form, axis=0)
    pad = jnp.zeros((LANES - 2 * A_HEADS, LANES), _F32)
    for c in range(n_chunks):
        rows = slice(c * A_HEADS, (c + 1) * A_HEADS)
        toks = slice(c * A_CHUNK, (c + 1) * A_CHUNK)
        gc_ref[toks, :LANES] = jnp.concatenate([col_cm[0][rows], col_cm[1][rows], pad], axis=0).T
        gc_ref[toks, LANES:] = jnp.concatenate([col_b[0][rows], col_b[1][rows], pad], axis=0).T

    y = jnp.dot(x, wqvo_ref[...], preferred_element_type=_F32)
    q_ref[...] = (y[:, :A_QK_DIM] * (A_DK ** -0.5)).astype(_BF16)
    v_ref[...] = y[:, A_QK_DIM:A_QK_DIM + D_MODEL].astype(_BF16)
    o_ref[...] = y[:, A_QK_DIM + D_MODEL:]


def _a_proj(h_parts, g, mod_l, w_in, layer, wgt, bgate, dims):
    rows = dims.rows
    tm = ROW_TILE
    n_gate = 4 * A_HEADS
    return pl.pallas_call(
        functools.partial(_a_proj_kernel, n_h=len(h_parts), layer=layer),
        out_shape=(
            jax.ShapeDtypeStruct((rows, A_QK_DIM), _BF16),
            jax.ShapeDtypeStruct((A_QK_DIM, rows), _BF16),
            jax.ShapeDtypeStruct((rows, D_MODEL), _BF16),
            jax.ShapeDtypeStruct((rows, D_MODEL), _F32),
            jax.ShapeDtypeStruct((n_gate, rows), _F32),
            jax.ShapeDtypeStruct((rows, 2 * LANES), _F32),
        ),
        grid=(rows // tm,),
        in_specs=_h_specs(tm, len(h_parts) == 2) + [
            _resident((1, D_MODEL)),
            _mod_spec(dims.n_ctx_rows, dims.seq, tm),
            _weight_spec(),
            _resident(wgt.shape),
            _resident((n_gate, 1)),
        ],
        out_specs=(
            _row_spec(tm, A_QK_DIM),
            pl.BlockSpec((A_QK_DIM, tm), lambda i: (0, i)),
            _row_spec(tm, D_MODEL),
            _row_spec(tm, D_MODEL),
            pl.BlockSpec((n_gate, tm), lambda i: (0, i)),
            _row_spec(tm, 2 * LANES),
        ),
        scratch_shapes=[pltpu.VMEM((D_MODEL, A_QK_DIM + 2 * D_MODEL), _BF16),
                        pltpu.VMEM((A_QK_DIM + 4 * A_HEADS, D_MODEL), _BF16)] + _stage_scratch(),
        compiler_params=_cparams("arbitrary"),
        name="a_proj",
    )(*h_parts, g, mod_l, w_in, wgt, bgate)


def _lane_scan(x, op, fill, reverse):
    lane = lax.broadcasted_iota(jnp.int32, x.shape, 1)
    k = 1
    while k < LANES:
        if reverse:
            shifted = jnp.where(lane < LANES - k, pltpu.roll(x, LANES - k, axis=1), fill)
        else:
            shifted = jnp.where(lane >= k, pltpu.roll(x, k, axis=1), fill)
        x = op(x, shifted)
        k *= 2
    return x


def _a_scan_kernel(*refs, step_chunks, has_init, emit_state):
    n_state = 5
    (qf_ref, ktf_ref, vf_ref, grf_ref, gcf_ref, qb_ref, ktb_ref, vb_ref, grb_ref, gcb_ref), refs = refs[:10], refs[10:]
    init_refs, refs = (refs[:n_state], refs[n_state:]) if has_init else ((), refs)
    (hf_ref, hb_ref), refs = refs[:2], refs[2:]
    final_refs, refs = (refs[:n_state], refs[n_state:]) if emit_state else ((), refs)
    cf_ref, cb_ref, mrf_ref, mrb_ref, mc_ref = state_refs = refs

    @pl.when(pl.program_id(1) == 0)
    def _():
        if has_init:
            for ref, init in zip(state_refs, init_refs):
                ref[...] = init[...]
        else:
            cf_ref[...] = jnp.zeros(cf_ref.shape, _F32)
            cb_ref[...] = jnp.zeros(cb_ref.shape, _F32)
            mrf_ref[...] = jnp.full(mrf_ref.shape, -jnp.inf, _F32)
            mrb_ref[...] = jnp.full(mrb_ref.shape, -jnp.inf, _F32)
            mc_ref[...] = jnp.full(mc_ref.shape, -jnp.inf, _F32)

    t_idx = lax.broadcasted_iota(jnp.int32, (A_CHUNK, A_CHUNK), 0)
    s_idx = lax.broadcasted_iota(jnp.int32, (A_CHUNK, A_CHUNK), 1)
    ones_blk = jnp.ones((A_CHUNK, A_DV), _BF16)
    lane = lax.broadcasted_iota(jnp.int32, (1, LANES), 1)
    for sub in range(step_chunks):
        toks = (slice(sub * A_CHUNK, (sub + 1) * A_CHUNK),
                slice((step_chunks - 1 - sub) * A_CHUNK, (step_chunks - sub) * A_CHUNK))
        _a_chunk_pair(toks, (qf_ref, qb_ref), (ktf_ref, ktb_ref), (vf_ref, vb_ref), (grf_ref, grb_ref),
                      (gcf_ref, gcb_ref), (hf_ref, hb_ref), (cf_ref, cb_ref), (mrf_ref, mrb_ref), mc_ref,
                      t_idx, s_idx, ones_blk, lane)

    if emit_state:
        @pl.when(pl.program_id(1) == pl.num_programs(1) - 1)
        def _():
            for ref, final in zip(state_refs, final_refs):
                final[...] = ref[...]


def _a_chunk_pair(toks, q_refs, kt_refs, v_refs, gr_refs, gc_refs, h_refs, c_refs, mr_refs, mc_ref,
                  t_idx, s_idx, ones_blk, lane):
    m_col = mc_ref[0:1, :]
    m_col_new = []
    dirs = []
    for d in range(2):
        gr_ref, gc_ref, mr_ref, tok = gr_refs[d], gc_refs[d], mr_refs[d], toks[d]
        last = A_CHUNK - 1 if d == 0 else 0
        a = gr_ref[2 * d * A_HEADS:(2 * d + 1) * A_HEADS, tok]
        b = gr_ref[(2 * d + 1) * A_HEADS:(2 * d + 2) * A_HEADS, tok]
        m_row = mr_ref[:, 0:1]
        gg_end = jnp.maximum(m_row, jnp.max(a, axis=1, keepdims=True))
        ws = jnp.exp2(a - gg_end)
        decay = jnp.exp2(m_row - gg_end)
        mr_ref[...] = jnp.broadcast_to(b[:, last:last + 1] + gg_end, mr_ref.shape)
        gg_c = jnp.maximum(m_col, gc_ref[tok, :LANES])
        b_plus_gg = gc_ref[tok, LANES:] + gg_c
        clamp_c = jnp.exp2(-b_plus_gg)
        m_col_new.append(b_plus_gg[last:last + 1, :])
        dirs.append((a, m_row, ws, decay, gg_c, clamp_c))
    mc_ref[...] = jnp.broadcast_to(jnp.where(lane < A_HEADS, m_col_new[0], m_col_new[1]), mc_ref.shape)

    pair_lane = lax.broadcasted_iota(jnp.int32, (A_CHUNK, 2 * A_DK), 1)
    zeros_kt = jnp.zeros((A_DK, A_CHUNK), _BF16)
    zeros_c = jnp.zeros((A_DK, 2 * A_DV), _BF16)
    for hp in range(A_HEADS // 2):
        for d in range(2):
            q_ref, kt_ref, v_ref, h_ref, c_ref, tok = q_refs[d], kt_refs[d], v_refs[d], h_refs[d], c_refs[d], toks[d]
            a, m_row, ws, decay, gg_c, clamp_c = dirs[d]
            mask = (s_idx <= t_idx) if d == 0 else (s_idx >= t_idx)
            heads = (2 * hp, 2 * hp + 1)
            q2 = q_ref[tok, 2 * hp * A_DK:(2 * hp + 2) * A_DK]
            kts = [kt_ref[hd * A_DK:(hd + 1) * A_DK, tok] for hd in heads]
            kt_bd = jnp.concatenate([jnp.concatenate([kts[0], zeros_kt], axis=1),
                                     jnp.concatenate([zeros_kt, kts[1]], axis=1)], axis=0)
            s2 = jnp.dot(q2, kt_bd, preferred_element_type=_F32)
            ggs = [jnp.broadcast_to(gg_c[:, d * A_HEADS + hd:d * A_HEADS + hd + 1], (A_CHUNK, A_CHUNK))
                   for hd in heads]
            e_state = jnp.exp2(jnp.where(pair_lane < A_DK, m_row[heads[0]:heads[0] + 1, :] - ggs[0],
                                         m_row[heads[1]:heads[1] + 1, :] - ggs[1]))
            qe = q2.astype(_F32) * e_state
            for i, hd in enumerate(heads):
                col = d * A_HEADS + hd
                clamp = jnp.broadcast_to(clamp_c[:, col:col + 1], (A_CHUNK, A_DV))
                v = v_ref[tok, hd * A_DV:(hd + 1) * A_DV]
                v_ext = jnp.concatenate([v, ones_blk], axis=1)
                c_old = c_ref[hd]
                e_keys = jnp.exp2(jnp.where(mask, a[hd:hd + 1, :] - ggs[i], -jnp.inf))
                p = jnp.concatenate([s2[:, i * A_CHUNK:(i + 1) * A_CHUNK] * e_keys, qe], axis=1).astype(_BF16)
                c_rows = [c_old.astype(_BF16), zeros_c] if i == 0 else [zeros_c, c_old.astype(_BF16)]
                rhs = jnp.concatenate([v_ext] + c_rows, axis=0)
                num = jnp.dot(p, rhs, preferred_element_type=_F32)
                den = jnp.maximum(jnp.abs(num[:, A_DV:]), clamp)
                h_ref[tok, hd * A_DV:(hd + 1) * A_DV] = num[:, :A_DV] / den
                kw = (kts[i].astype(_F32) * ws[hd:hd + 1, :]).astype(_BF16)
                c_ref[hd] = decay[hd:hd + 1, :] * c_old + jnp.dot(kw, v_ext, preferred_element_type=_F32)


def _a_scan_call(q, kt, v, gr, gc, dims, step_chunks, first_row, seq_rows, init=None, emit_state=False):
    blk_rows = step_chunks * A_CHUNK
    n_steps = seq_rows // blk_rows
    assert seq_rows % blk_rows == 0 and first_row % blk_rows == 0
    blk0 = first_row // blk_rows

    def fwd_blk(b, j):
        return blk0 + b * n_steps + j

    def bwd_blk(b, j):
        return blk0 + b * n_steps + (n_steps - 1 - j)

    def specs(blk):
        return [
            pl.BlockSpec((blk_rows, A_QK_DIM), lambda b, j: (blk(b, j), 0)),
            pl.BlockSpec((A_QK_DIM, blk_rows), lambda b, j: (0, blk(b, j))),
            pl.BlockSpec((blk_rows, D_MODEL), lambda b, j: (blk(b, j), 0)),
            pl.BlockSpec((4 * A_HEADS, blk_rows), lambda b, j: (0, blk(b, j))),
            pl.BlockSpec((blk_rows, 2 * LANES), lambda b, j: (blk(b, j), 0)),
        ]

    state_shapes = [(A_HEADS, A_DK, 2 * A_DV)] * 2 + [(A_HEADS, LANES)] * 3
    state_specs = [pl.BlockSpec((None,) + shp, lambda b, j, n=len(shp): (b,) + (0,) * n) for shp in state_shapes]
    h_shape = jax.ShapeDtypeStruct((dims.batch * seq_rows, D_MODEL), _F32)
    out_shape = [h_shape, h_shape]
    out_specs = [pl.BlockSpec((blk_rows, D_MODEL), lambda b, j: (b * n_steps + j, 0)),
                 pl.BlockSpec((blk_rows, D_MODEL), lambda b, j: (b * n_steps + (n_steps - 1 - j), 0))]
    if emit_state:
        out_shape += [jax.ShapeDtypeStruct((dims.batch,) + shp, _F32) for shp in state_shapes]
        out_specs += state_specs
    return pl.pallas_call(
        functools.partial(_a_scan_kernel, step_chunks=step_chunks, has_init=init is not None,
                          emit_state=emit_state),
        out_shape=tuple(out_shape),
        grid=(dims.batch, n_steps),
        in_specs=specs(fwd_blk) + specs(bwd_blk) + (state_specs if init is not None else []),
        out_specs=tuple(out_specs),
        scratch_shapes=[pltpu.VMEM(shp, _F32) for shp in state_shapes],
        compiler_params=_cparams("arbitrary", "arbitrary"),
        name="a_scan",
    )(q, kt, v, gr, gc, q, kt, v, gr, gc, *(init or ()))


def _a_scan(q, kt, v, gr, gc, dims):
    hf_c, hb_c, *state = _a_scan_call(q, kt, v, gr, gc, dims, A_STEP_CHUNKS_CTX, 0, dims.ctx_len,
                                      emit_state=True)
    hf_l, hb_l = _a_scan_call(q, kt, v, gr, gc, dims, A_STEP_CHUNKS_LAT, dims.n_ctx_rows, dims.seq,
                              init=state)
    return (hf_l, hf_c), (hb_l, hb_c)


def _a_mix(hf_lat_ref, hf_ctx_ref, hb_lat_ref, hb_ctx_ref, o_ref, hg_ref, tile0, cols):
    hs = _load_h((hf_lat_ref, hf_ctx_ref), tile0, cols) + _load_h((hb_lat_ref, hb_ctx_ref), tile0, cols)
    parts = []
    for c0 in range(0, hs.shape[1], A_DV):
        x = hs[:, c0:c0 + A_DV]
        parts.append(x * lax.rsqrt(jnp.mean(x * x, axis=-1, keepdims=True) + NORM_EPS))
    y = jnp.concatenate(parts, axis=1) * hg_ref[:, cols]
    return (jax.nn.sigmoid(o_ref[:, cols]) * y).astype(_BF16)


ROPE_HALF = B_HEAD_DIM // 4


def _swap_halves_lanes(x):
    lane = lax.broadcasted_iota(jnp.int32, x.shape, 1)
    fwd = pltpu.roll(x, LANES - ROPE_HALF, axis=1)
    back = pltpu.roll(x, ROPE_HALF, axis=1)
    return jnp.where(lane % (2 * ROPE_HALF) < ROPE_HALF, fwd, back)


def _swap_halves_rows(x):
    parts = []
    for r0 in range(0, x.shape[0], 2 * ROPE_HALF):
        parts += [x[r0 + ROPE_HALF:r0 + 2 * ROPE_HALF], x[r0:r0 + ROPE_HALF]]
    return jnp.concatenate(parts, axis=0)


def _b_proj_kernel(h_ref, g_ref, mod_ref, wqkv_hbm, cos_ref, sin_ref, cost_ref, sint_ref,
                   qt_ref, k_ref, vt_ref, wqvt_ref, wk_ref, stage_ref, sem_ref, *, layer):
    @pl.when(pl.program_id(0) == 0)
    def _():
        def store_t(dst0, c, width, tile):
            wqvt_ref[dst0 + c:dst0 + c + width, :] = tile.T.astype(_BF16)

        def store_k(c, width, tile):
            wk_ref[:, c:c + width] = tile.astype(_BF16)

        _stage_weights(
            _col_tiles(wqkv_hbm, layer, 0, D_MODEL, functools.partial(store_t, 0))
            + _col_tiles(wqkv_hbm, layer, D_MODEL, B_KV_DIM, store_k)
            + _col_tiles(wqkv_hbm, layer, D_MODEL + B_KV_DIM, B_KV_DIM, functools.partial(store_t, D_MODEL)),
            stage_ref, sem_ref)

    x = _norm_mod(h_ref[...], g_ref[...], _mod_part(mod_ref, 0), _mod_part(mod_ref, 1)).astype(_BF16)
    nt = (((1,), (1,)), ((), ()))
    yt = lax.dot_general(wqvt_ref[...], x, nt, preferred_element_type=_F32)
    cost, sint = cost_ref[...], sint_ref[...]
    for hd in range(B_Q_HEADS):
        rows = slice(hd * B_HEAD_DIM, (hd + 1) * B_HEAD_DIM)
        qt = yt[rows, :]
        qt = (qt * cost + _swap_halves_rows(qt) * sint) * (B_HEAD_DIM ** -0.5 * LOG2E)
        qt_ref[rows, :] = qt.astype(_BF16)
    vt_ref[...] = yt[D_MODEL:, :].astype(_BF16)
    cos, sin = cos_ref[...], sin_ref[...]
    k = jnp.dot(x, wk_ref[...], preferred_element_type=_F32)
    for c0 in range(0, B_KV_DIM, LANES):
        kc = k[:, c0:c0 + LANES]
        k_ref[:, c0:c0 + LANES] = (kc * cos + _swap_halves_lanes(kc) * sin).astype(_BF16)


def _b_proj(h, g, mod_l, w_qkv, layer, rope, dims):
    rows = h.shape[0]
    tm = ROW_TILE
    cos, sin, cost, sint = rope
    return pl.pallas_call(
        functools.partial(_b_proj_kernel, layer=layer),
        out_shape=(
            jax.ShapeDtypeStruct((D_MODEL, rows), _BF16),
            jax.ShapeDtypeStruct((rows, B_KV_DIM), _BF16),
            jax.ShapeDtypeStruct((B_KV_DIM, rows), _BF16),
        ),
        grid=(rows // tm,),
        in_specs=[
            _row_spec(tm, D_MODEL),
            _resident((1, D_MODEL)),
            _mod_spec(dims.n_ctx_rows, dims.seq, tm),
            _weight_spec(),
            _row_spec(tm, LANES),
            _row_spec(tm, LANES),
            pl.BlockSpec((B_HEAD_DIM, tm), lambda i: (0, i)),
            pl.BlockSpec((B_HEAD_DIM, tm), lambda i: (0, i)),
        ],
        out_specs=(
            pl.BlockSpec((D_MODEL, tm), lambda i: (0, i)),
            _row_spec(tm, B_KV_DIM),
            pl.BlockSpec((B_KV_DIM, tm), lambda i: (0, i)),
        ),
        scratch_shapes=[pltpu.VMEM((D_MODEL + B_KV_DIM, D_MODEL), _BF16),
                        pltpu.VMEM((D_MODEL, B_KV_DIM), _BF16)] + _stage_scratch(),
        compiler_params=_cparams("arbitrary"),
        name="b_proj",
    )(h, g, mod_l, w_qkv, cos, sin, cost, sint)


B_ONES_ROWS = 16
B_STEP_BLOCKS_CTX = 2
B_STEP_BLOCKS_LAT = 4


def _b_attn_kernel(sink_ref, win_l_ref, win_r_ref, qt_ref, kl_ref, kc_ref, kr_ref, kx_ref,
                   vtl_ref, vtc_ref, vtr_ref, vtx_ref, o_ref, s_ref,
                   *, n_ctx_blocks, blocks_per_seq, ctx_len, step_blocks, first_block):
    first = first_block + pl.program_id(0) * step_blocks
    is_lat = first >= n_ctx_blocks
    n = (first - n_ctx_blocks) % blocks_per_seq
    neg = -jnp.inf
    win_l = jnp.where(is_lat, win_l_ref[...], neg)
    win_r = jnp.where(is_lat, win_r_ref[...], neg)
    bias_c = jnp.where(is_lat, 0.0, neg)
    bias_first_l = jnp.where(jnp.logical_and(is_lat, n >= 1), win_l_ref[...], neg)
    bias_last_r = jnp.where(jnp.logical_and(is_lat, n + step_blocks <= blocks_per_seq - 1), win_r_ref[...], neg)

    n_q = B_GROUP * B_BLOCK
    head_of_lane = lax.broadcasted_iota(jnp.int32, (1, n_q), 1) // B_BLOCK
    ones_rows = jnp.ones((B_ONES_ROWS, B_BLOCK), _BF16)
    n_ctx_tiles = ctx_len // B_BLOCK

    def local_tiles(blk):
        if first_block < n_ctx_blocks:
            return []
        own = slice(blk * B_BLOCK, (blk + 1) * B_BLOCK)
        if blk == 0:
            left = (kl_ref, vtl_ref, slice(0, B_BLOCK), bias_first_l)
        else:
            left = (kc_ref, vtc_ref, slice((blk - 1) * B_BLOCK, blk * B_BLOCK), win_l)
        if blk == step_blocks - 1:
            right = (kr_ref, vtr_ref, slice(0, B_BLOCK), bias_last_r)
        else:
            right = (kc_ref, vtc_ref, slice((blk + 1) * B_BLOCK, (blk + 2) * B_BLOCK), win_r)
        return [left, (kc_ref, vtc_ref, own, bias_c), right]

    def scores(blk, g):
        cols = slice(blk * B_BLOCK, (blk + 1) * B_BLOCK)
        qt = jnp.concatenate(
            [qt_ref[(g * B_GROUP + j) * B_HEAD_DIM:(g * B_GROUP + j + 1) * B_HEAD_DIM, cols]
             for j in range(B_GROUP)], axis=1)
        sink = jnp.zeros((1, n_q), _F32)
        for j in range(B_GROUP):
            sink = jnp.where(head_of_lane == j, sink_ref[g * B_GROUP + j] * LOG2E, sink)
        ks = slice(g * B_HEAD_DIM, (g + 1) * B_HEAD_DIM)
        k_tiles = [(k_ref[rows, ks], bias) for k_ref, _, rows, bias in local_tiles(blk)]
        for t in range(n_ctx_tiles):
            k_tiles.append((kx_ref[t * B_BLOCK:(t + 1) * B_BLOCK, ks], None))
        m_tile = None
        s_all = jnp.dot(jnp.concatenate([k for k, _ in k_tiles], axis=0), qt, preferred_element_type=_F32)
        for t, (_, bias) in enumerate(k_tiles):
            s = s_all[t * B_BLOCK:(t + 1) * B_BLOCK]
            if bias is not None:
                s = s + bias
            s_ref[blk, g, t] = s
            m_tile = s if m_tile is None else jnp.maximum(m_tile, s)
        return jnp.maximum(sink, jnp.max(m_tile, axis=0, keepdims=True)), sink

    def attend(blk, g, m, sink):
        ks = slice(g * B_HEAD_DIM, (g + 1) * B_HEAD_DIM)
        vt_tiles = [vt_ref[ks, cols] for _, vt_ref, cols, _ in local_tiles(blk)]
        for t in range(n_ctx_tiles):
            vt_tiles.append(vtx_ref[ks, t * B_BLOCK:(t + 1) * B_BLOCK])
        acc = jnp.zeros((B_HEAD_DIM + B_ONES_ROWS, n_q), _F32)
        for t, vt in enumerate(vt_tiles):
            p = jnp.exp2(s_ref[blk, g, t] - m).astype(_BF16)
            vt_ext = jnp.concatenate([vt, ones_rows], axis=0)
            acc = acc + jnp.dot(vt_ext, p, preferred_element_type=_F32)
        denom = jnp.exp2(sink - m) + acc[B_HEAD_DIM:B_HEAD_DIM + 1, :]
        out_t = acc[:B_HEAD_DIM, :] * (1.0 / denom)
        rows = slice(blk * B_BLOCK, (blk + 1) * B_BLOCK)
        for pair in range(B_GROUP // 2):
            two = jnp.concatenate([out_t[:, (2 * pair) * B_BLOCK:(2 * pair + 1) * B_BLOCK],
                                   out_t[:, (2 * pair + 1) * B_BLOCK:(2 * pair + 2) * B_BLOCK]], axis=0)
            c0 = (g * B_GROUP + 2 * pair) * B_HEAD_DIM
            o_ref[rows, c0:c0 + 2 * B_HEAD_DIM] = two.T.astype(_BF16)

    units = [(blk, g) for blk in range(step_blocks) for g in range(B_KV_HEADS)]
    pending = [scores(*units[0]), scores(*units[1])]
    for i, unit in enumerate(units):
        if i + 2 < len(units):
            pending.append(scores(*units[i + 2]))
        attend(*unit, *pending[i])


def _window_bias():
    key = np.arange(B_BLOCK)[:, None]
    qry = np.arange(B_BLOCK)[None, :]
    prev_blk = np.where(key >= qry, 0.0, -np.inf).astype(np.float32)
    next_blk = np.where(key <= qry, 0.0, -np.inf).astype(np.float32)
    return jnp.asarray(np.tile(prev_blk, (1, B_GROUP))), jnp.asarray(np.tile(next_blk, (1, B_GROUP)))


def _b_attn_call(sinks, qt, k, vt, dims, step_blocks, first_block, n_blocks):
    step_rows = step_blocks * B_BLOCK
    ncc = dims.ctx_len // B_BLOCK
    ncl = dims.seq // B_BLOCK
    lat0 = dims.batch * ncc
    nblk = k.shape[0] // B_BLOCK
    assert first_block % step_blocks == 0 and n_blocks % step_blocks == 0
    assert ncc % step_blocks == 0 if first_block < lat0 else ncl % step_blocks == 0
    assert first_block >= lat0 or first_block + n_blocks <= lat0
    off = first_block // step_blocks

    def batch_of(s):
        i = first_block + s * step_blocks
        return jnp.where(i < lat0, i // ncc, (i - lat0) // ncl)

    def left(s):
        return jnp.maximum(first_block + s * step_blocks - 1, 0)

    def right(s):
        return jnp.minimum(first_block + (s + 1) * step_blocks, nblk - 1)

    n_tiles = 3 + ncc
    return pl.pallas_call(
        functools.partial(_b_attn_kernel, n_ctx_blocks=lat0, blocks_per_seq=ncl, ctx_len=dims.ctx_len,
                          step_blocks=step_blocks, first_block=first_block),
        out_shape=jax.ShapeDtypeStruct((n_blocks * B_BLOCK, D_MODEL), _BF16),
        grid=(n_blocks // step_blocks,),
        in_specs=[
            pl.BlockSpec(memory_space=pltpu.SMEM),
            _resident((B_BLOCK, B_GROUP * B_BLOCK)),
            _resident((B_BLOCK, B_GROUP * B_BLOCK)),
            pl.BlockSpec((D_MODEL, step_rows), lambda s: (0, off + s)),
            pl.BlockSpec((B_BLOCK, B_KV_DIM), lambda s: (left(s), 0)),
            pl.BlockSpec((step_rows, B_KV_DIM), lambda s: (off + s, 0)),
            pl.BlockSpec((B_BLOCK, B_KV_DIM), lambda s: (right(s), 0)),
            pl.BlockSpec((dims.ctx_len, B_KV_DIM), lambda s: (batch_of(s), 0)),
            pl.BlockSpec((B_KV_DIM, B_BLOCK), lambda s: (0, left(s))),
            pl.BlockSpec((B_KV_DIM, step_rows), lambda s: (0, off + s)),
            pl.BlockSpec((B_KV_DIM, B_BLOCK), lambda s: (0, right(s))),
            pl.BlockSpec((B_KV_DIM, dims.ctx_len), lambda s: (0, batch_of(s))),
        ],
        out_specs=pl.BlockSpec((step_rows, D_MODEL), lambda s: (s, 0)),
        scratch_shapes=[pltpu.VMEM((step_blocks, B_KV_HEADS, n_tiles, B_BLOCK, B_GROUP * B_BLOCK), _F32)],
        compiler_params=_cparams("arbitrary"),
        name="b_attn",
    )(sinks, *_window_bias(), qt, k, k, k, k, vt, vt, vt, vt)


def _b_attn(sinks, qt, k, vt, dims):
    lat0 = dims.n_ctx_rows // B_BLOCK
    nblk = k.shape[0] // B_BLOCK
    y_ctx = _b_attn_call(sinks, qt, k, vt, dims, B_STEP_BLOCKS_CTX, 0, lat0)
    y_lat = _b_attn_call(sinks, qt, k, vt, dims, B_STEP_BLOCKS_LAT, lat0, nblk - lat0)
    return y_lat, y_ctx


def _c_proj_kernel(h_hbm, g_ref, mod_hbm, win_hbm, bg_hbm, u_hbm, w_ref, stage_ref, sem_ref, *, layer, dims):
    def store(c, width, tile):
        w_ref[:, c:c + width] = tile.astype(_BF16)

    _stage_weights(_col_tiles(win_hbm, layer, 0, 3 * D_MODEL, store), stage_ref, sem_ref)

    def tile_body(h_ref, mod_ref, bg_ref, u_ref):
        x = _norm_mod(h_ref[...], g_ref[...], _mod_part(mod_ref, 0), _mod_part(mod_ref, 1)).astype(_BF16)
        y = jnp.dot(x, w_ref[...], preferred_element_type=_F32)
        bg_ref[...] = y[:, :D_MODEL]
        u_ref[...] = y[:, D_MODEL:2 * D_MODEL] * y[:, 2 * D_MODEL:]

    tm = ROW_TILE
    pltpu.emit_pipeline(
        tile_body,
        grid=(h_hbm.shape[0] // tm,),
        in_specs=[_row_spec(tm, D_MODEL), _mod_spec(dims.n_ctx_rows, dims.seq, tm)],
        out_specs=[_row_spec(tm, D_MODEL)] * 2,
    )(h_hbm, mod_hbm, bg_hbm, u_hbm)


def _c_proj(h, g, mod_l, w_in, layer, dims):
    rows = h.shape[0]
    in_hbm = pl.BlockSpec(memory_space=pl.ANY)
    return pl.pallas_call(
        functools.partial(_c_proj_kernel, layer=layer, dims=dims),
        out_shape=(jax.ShapeDtypeStruct((rows, D_MODEL), _F32),) * 2,
        in_specs=[in_hbm, pl.BlockSpec(memory_space=pltpu.VMEM), in_hbm, _weight_spec()],
        out_specs=(in_hbm,) * 2,
        scratch_shapes=[pltpu.VMEM((D_MODEL, 3 * D_MODEL), _BF16)] + _stage_scratch(),
        compiler_params=_cparams(),
        name="c_proj",
    )(h, g, mod_l, w_in)


def _c_mix(bg_ref, u_ref, up_ref, un_ref, cw_ref, cb_ref, tile, dims, cols):
    tm = u_ref.shape[0]
    u = u_ref[:, cols]
    row = lax.broadcasted_iota(jnp.int32, (tm, 1), 0)
    g_row = tile * tm + row
    in_ctx = g_row < dims.n_ctx_rows
    pos = jnp.where(in_ctx, g_row % dims.ctx_len, (g_row - dims.n_ctx_rows) % dims.seq)
    length = jnp.where(in_ctx, dims.ctx_len, dims.seq)
    prev = jnp.where(row == 0, up_ref[SUBLANES - 1:SUBLANES, cols], pltpu.roll(u, 1, axis=0))
    nxt = jnp.where(row == tm - 1, un_ref[0:1, cols], pltpu.roll(u, tm - 1, axis=0))
    prev = jnp.where(pos == 0, 0.0, prev)
    nxt = jnp.where(pos == length - 1, 0.0, nxt)
    conv = prev * cw_ref[0:1, cols] + u * cw_ref[1:2, cols] + nxt * cw_ref[2:3, cols] + cb_ref[:, cols]
    return (bg_ref[:, cols] * conv).astype(_BF16)


_N_MIX_REFS = (6, 2, 6)


def _post_kernel(*refs, kind, n_h, tile0, final_norm, dims, out_layer, layer):
    n_mix = _N_MIX_REFS[kind]
    h_refs, mod_ref = refs[:n_h], refs[n_h]
    mix_refs = refs[n_h + 1:n_h + 1 + n_mix]
    (wo_hbm, g_ref, w1_hbm, w2_hbm, fg_ref, out_ref,
     wo_ref, w1_ref, w2_ref, stage_ref, sem_ref) = refs[n_h + 1 + n_mix:]
    n_chunks = MLP_HIDDEN // MLP_HIDDEN_CHUNK
    tiles_wo = D_MODEL // STAGE_COLS
    tiles_w1 = MLP_HIDDEN_CHUNK // STAGE_COLS
    tiles_w2 = D_MODEL // STAGE_COLS

    def compute(need):
        proj = None
        for c0 in range(0, D_MODEL, OUT_PROJ_COLS):
            cols = slice(c0, c0 + OUT_PROJ_COLS)
            if kind == 0:
                y = _a_mix(*mix_refs, tile0, cols)
            elif kind == 1:
                y = _load_h(mix_refs, 0, cols)
            else:
                y = _c_mix(*mix_refs, pl.program_id(0) + tile0, dims, cols)
            if c0 == 0:
                need(tiles_wo)
            part = jnp.dot(y, wo_ref[cols, :], preferred_element_type=_F32)
            proj = part if proj is None else proj + part
        h = _load_h(h_refs, tile0) + _mod_part(mod_ref, 2) * proj
        x = _norm_mod(h, g_ref[...], _mod_part(mod_ref, 3), _mod_part(mod_ref, 4)).astype(_BF16)
        acc = jnp.zeros(h.shape, _F32)
        for c in range(n_chunks):
            cols = slice(c * MLP_HIDDEN_CHUNK, (c + 1) * MLP_HIDDEN_CHUNK)
            need(tiles_wo + c * (tiles_w1 + tiles_w2) + tiles_w1)
            u = jnp.dot(x, w1_ref[:, cols], preferred_element_type=_F32)
            u = jnp.square(jnp.maximum(u, 0.0)).astype(_BF16)
            need(tiles_wo + (c + 1) * (tiles_w1 + tiles_w2))
            acc = acc + jnp.dot(u, w2_ref[cols, :], preferred_element_type=_F32)
        out = h + _mod_part(mod_ref, 5) * acc
        if final_norm:
            ms = jnp.mean(out * out, axis=-1, keepdims=True)
            out = out * lax.rsqrt(ms + NORM_EPS) * fg_ref[...]
        out_ref[...] = out

    @pl.when(pl.program_id(0) == 0)
    def _():
        def store_to(dst_ref, row0, c, width, tile):
            dst_ref[row0:row0 + tile.shape[0], c:c + width] = tile.astype(_BF16)

        jobs = _col_tiles(wo_hbm, out_layer, 0, D_MODEL, functools.partial(store_to, wo_ref, 0))
        for r0 in range(0, MLP_HIDDEN, MLP_HIDDEN_CHUNK):
            jobs += _col_tiles(w1_hbm, layer, r0, MLP_HIDDEN_CHUNK,
                               lambda c, width, tile, r0=r0: store_to(w1_ref, 0, r0 + c, width, tile))
            for c in range(0, D_MODEL, STAGE_COLS):
                jobs.append((w2_hbm.at[layer, pl.ds(r0, MLP_HIDDEN_CHUNK), pl.ds(c, STAGE_COLS)],
                             functools.partial(store_to, w2_ref, r0, c, STAGE_COLS)))
        compute(_WeightStager(jobs, stage_ref, sem_ref).need)

    @pl.when(pl.program_id(0) != 0)
    def _():
        compute(lambda n: None)


def _post(kind, h_parts, mod_l, mix, w_out, out_layer, g, w1, w2, layer, final_g, dims, tile0=0,
          final_norm=False):
    tm = ROW_TILE
    n_tiles = dims.rows // tm - tile0
    row = functools.partial(_row_spec, tm, D_MODEL, tile0)
    if kind == 0:
        (hf_l, hf_c), (hb_l, hb_c), o, head_g = mix
        mix = (hf_l, hf_c, hb_l, hb_c, o, head_g)
        mix_specs = _h_specs(tm, True, tile0) * 2 + [row(), _resident((1, D_MODEL))]
    elif kind == 1:
        assert tile0 == 0
        mix_specs = _h_specs(tm, True)
    else:
        bg, u, conv_w, conv_b = mix
        per = tm // SUBLANES
        last = dims.rows // SUBLANES - 1
        mix = (bg, u, u, u, conv_w, conv_b)
        mix_specs = [
            row(), row(),
            pl.BlockSpec((SUBLANES, D_MODEL), lambda i: (jnp.maximum((i + tile0) * per - 1, 0), 0)),
            pl.BlockSpec((SUBLANES, D_MODEL), lambda i: (jnp.minimum((i + tile0 + 1) * per, last), 0)),
            _resident(conv_w.shape), _resident((1, D_MODEL)),
        ]
    assert len(h_parts) == 1 or tile0 == 0
    h_specs = _h_specs(tm, True) if len(h_parts) == 2 else [row()]
    return pl.pallas_call(
        functools.partial(_post_kernel, kind=kind, n_h=len(h_parts), tile0=tile0,
                          final_norm=final_norm, dims=dims, out_layer=out_layer, layer=layer),
        out_shape=jax.ShapeDtypeStruct((n_tiles * tm, D_MODEL), _F32),
        grid=(n_tiles,),
        in_specs=h_specs + [_mod_spec(dims.n_ctx_rows, dims.seq, tm, tile0)] + mix_specs + [
            _weight_spec(),
            _resident((1, D_MODEL)),
            _weight_spec(),
            _weight_spec(),
            _resident((1, D_MODEL)),
        ],
        out_specs=_row_spec(tm, D_MODEL),
        scratch_shapes=[pltpu.VMEM((D_MODEL, D_MODEL), _BF16), pltpu.VMEM((D_MODEL, MLP_HIDDEN), _BF16),
                        pltpu.VMEM((MLP_HIDDEN, D_MODEL), _BF16)] + _stage_scratch(),
        compiler_params=_cparams("arbitrary"),
        name="post",
    )(*h_parts, mod_l, *mix, w_out, g, w1, w2, final_g)


class _Dims:
    def __init__(self, batch, seq, ctx_len):
        self.batch = batch
        self.seq = seq
        self.ctx_len = ctx_len
        self.n_ctx_rows = batch * ctx_len
        self.rows = self.n_ctx_rows + batch * seq


def _rope_tables(dims):
    n_freq = B_HEAD_DIM // 4
    t = np.arange(dims.seq)
    inv_freq = ROPE_BASE ** (-np.arange(n_freq, dtype=np.float64) / n_freq)
    ang_row = (t // GRID_W)[:, None] * inv_freq
    ang_col = (t % GRID_W)[:, None] * inv_freq
    ang = np.concatenate([ang_row, ang_row, ang_col, ang_col], axis=1)
    sign = np.tile(np.concatenate([-np.ones(n_freq), np.ones(n_freq)]), 2)
    cos = np.tile(np.cos(ang), (dims.batch, 1))
    sin = np.tile(np.sin(ang) * sign, (dims.batch, 1))
    cos = np.concatenate([np.ones((dims.n_ctx_rows, B_HEAD_DIM)), cos], axis=0).astype(np.float32)
    sin = np.concatenate([np.zeros((dims.n_ctx_rows, B_HEAD_DIM)), sin], axis=0).astype(np.float32)
    tables = (np.tile(cos, (1, 2)), np.tile(sin, (1, 2)), np.ascontiguousarray(cos.T), np.ascontiguousarray(sin.T))
    return tuple(jnp.asarray(tab) for tab in tables)


def kernel(x, c, ctx, c_ctx, ada_w, ada_b, norm_g, final_g, mlp_w1, mlp_w2,
           a_w_in, a_w_gate, a_b_gate, a_head_g, a_w_out,
           b_w_qkv, b_sinks, b_w_out, c_w_in, c_conv_w, c_conv_b, c_w_out):
    batch, seq, d = x.shape
    ctx_len = ctx.shape[1]
    depth = ada_w.shape[0]
    dims = _Dims(batch, seq, ctx_len)
    assert MLP_HIDDEN_CHUNK == D_MODEL and d == D_MODEL and seq % ROW_TILE == 0 and dims.n_ctx_rows % ROW_TILE == 0
    assert 1 + batch <= MOD_ROWS

    assert dims.n_ctx_rows == ROW_TILE
    h_parts = (x.reshape(-1, d), ctx.reshape(-1, d))
    cs = jnp.concatenate([c_ctx[None], c, jnp.zeros((MOD_ROWS - 1 - batch, d), _F32)], axis=0)
    mod = _ada_table(cs, ada_w, ada_b).reshape(depth, MOD_ROWS, 1, 6 * d)
    rope = _rope_tables(dims)
    fg = final_g.reshape(1, d)

    for l in range(depth):
        kind, j = l % N_MIXERS, l // N_MIXERS
        mod_l = mod[l]
        g0 = norm_g[l, 0].reshape(1, d)
        g1 = norm_g[l, 1].reshape(1, d)
        last_layer = l == depth - 1
        if kind == 0:
            wgt = a_w_gate[j].T.astype(_BF16)
            q, kt, v, o, gr, gc = _a_proj(h_parts, g0, mod_l, a_w_in, j, wgt, a_b_gate[j].reshape(-1, 1), dims)
            hf, hb = _a_scan(q, kt, v, gr, gc, dims)
            mix = (hf, hb, o, a_head_g[j].reshape(1, d))
            w_out = a_w_out
        elif kind == 1:
            (h,) = h_parts
            qt, k, vt = _b_proj(h, g0, mod_l, b_w_qkv, j, rope, dims)
            mix = _b_attn(b_sinks[j], qt, k, vt, dims)
            w_out = b_w_out
        else:
            (h,) = h_parts
            bg, u = _c_proj(h, g0, mod_l, c_w_in, j, dims)
            mix = (bg, u, c_conv_w[j], c_conv_b[j].reshape(1, d))
            w_out = c_w_out
        tile0 = dims.n_ctx_rows // ROW_TILE if last_layer else 0
        h_parts = (_post(kind, h_parts, mod_l, mix, w_out, j, g1, mlp_w1, mlp_w2, l, fg, dims,
                         tile0=tile0, final_norm=last_layer),)
    return h_parts[0].reshape(batch, seq, d)
```

```python
import functools

import jax
import jax.numpy as jnp
import numpy as np
from jax import lax
from jax.experimental import pallas as pl
from jax.experimental.pallas import tpu as pltpu

D_MODEL = 1024
GRID_W = 64
N_MIXERS = 3
A_HEADS = 8
A_QK_DIM = D_MODEL // 2
A_DK = A_QK_DIM // A_HEADS
A_DV = D_MODEL // A_HEADS
A_CHUNK = 128
A_STEP_CHUNKS_CTX = 2
A_STEP_CHUNKS_LAT = 4
B_Q_HEADS = 16
B_KV_HEADS = 4
B_HEAD_DIM = D_MODEL // B_Q_HEADS
B_GROUP = B_Q_HEADS // B_KV_HEADS
B_KV_DIM = B_KV_HEADS * B_HEAD_DIM
B_BLOCK = 128
ROPE_BASE = 10000.0
MLP_HIDDEN = 4 * D_MODEL
NORM_EPS = 1e-6
LOG2E = 1.4426950408889634

LANES = 128
SUBLANES = 8
VMEM_LIMIT_BYTES = 60 * 1024 * 1024

ROW_TILE = 512
MLP_HIDDEN_CHUNK = 1024
MOD_ROWS = 8
OUT_PROJ_COLS = 256
ADA_COL_TILE = 1536

_BF16 = jnp.bfloat16
_F32 = jnp.float32


def _cparams(*sem):
    return pltpu.CompilerParams(dimension_semantics=sem, vmem_limit_bytes=VMEM_LIMIT_BYTES)


def _resident(shape):
    nd = len(shape)
    return pl.BlockSpec(shape, lambda *_: (0,) * nd, pipeline_mode=pl.Buffered(1))


STAGE_COLS = 512
STAGE_DEPTH = 3


def _weight_spec():
    return pl.BlockSpec(memory_space=pl.ANY)


def _stage_scratch():
    return [pltpu.VMEM((STAGE_DEPTH, D_MODEL, STAGE_COLS), _F32), pltpu.SemaphoreType.DMA((STAGE_DEPTH,))]


class _WeightStager:
    def __init__(self, jobs, stage_ref, sem_ref):
        self.jobs, self.stage_ref, self.sem_ref = jobs, stage_ref, sem_ref
        self.started = self.done = 0

    def _copy(self, i):
        src = self.jobs[i][0]
        dst = self.stage_ref.at[i % STAGE_DEPTH, pl.ds(0, src.shape[0]), pl.ds(0, src.shape[1])]
        return pltpu.make_async_copy(src, dst, self.sem_ref.at[i % STAGE_DEPTH])

    def _fill(self):
        while self.started < min(self.done + STAGE_DEPTH, len(self.jobs)):
            self._copy(self.started).start()
            self.started += 1

    def need(self, n):
        self._fill()
        while self.done < n:
            src, store = self.jobs[self.done]
            self._copy(self.done).wait()
            store(self.stage_ref[self.done % STAGE_DEPTH, :src.shape[0], :src.shape[1]])
            self.done += 1
            self._fill()


def _stage_weights(jobs, stage_ref, sem_ref):
    _WeightStager(jobs, stage_ref, sem_ref).need(len(jobs))


def _col_tiles(w_hbm, layer, col0, ncols, store):
    jobs = []
    for c in range(0, ncols, STAGE_COLS):
        width = min(STAGE_COLS, ncols - c)
        jobs.append((w_hbm.at[layer, :, pl.ds(col0 + c, width)], functools.partial(store, c, width)))
    return jobs


def _norm_mod(h, g, shift, scale):
    ms = jnp.mean(h * h, axis=-1, keepdims=True)
    y = h * lax.rsqrt(ms + NORM_EPS) * g
    return y * (1.0 + scale) + shift


def _mod_part(mod_ref, k):
    return mod_ref[:, k * D_MODEL:(k + 1) * D_MODEL]


def _mod_spec(n_ctx_rows, seq, tm, tile0=0):
    def idx(i):
        r0 = (i + tile0) * tm
        grp = jnp.where(r0 < n_ctx_rows, 0, 1 + (r0 - n_ctx_rows) // seq)
        return (grp, 0, 0)
    return pl.BlockSpec((None, 1, 6 * D_MODEL), idx)


def _row_spec(tm, width, tile0=0):
    return pl.BlockSpec((tm, width), lambda i: (i + tile0, 0))


def _h_specs(tm, split, tile0=0):
    if not split:
        return [_row_spec(tm, D_MODEL, tile0)]
    return [pl.BlockSpec((tm, D_MODEL), lambda i: (jnp.maximum(i + tile0 - 1, 0), 0)),
            _resident((tm, D_MODEL))]


def _load_h(h_refs, tile0=0, cols=slice(None)):
    if len(h_refs) == 1:
        return h_refs[0][:, cols]
    lat_ref, ctx_ref = h_refs
    if tile0 > 0:
        return lat_ref[:, cols]
    return jnp.where(pl.program_id(0) == 0, ctx_ref[:, cols], lat_ref[:, cols])


def _ada_kernel(cs_ref, w_ref, b_ref, o_ref):
    cs = cs_ref[...]
    s = (cs * jax.nn.sigmoid(cs)).astype(_BF16)
    o_ref[...] = jnp.dot(s, w_ref[...].astype(_BF16), preferred_element_type=_F32) + b_ref[...]


def _ada_table(cs, ada_w, ada_b):
    depth = ada_w.shape[0]
    n = ada_w.shape[2]
    return pl.pallas_call(
        _ada_kernel,
        out_shape=jax.ShapeDtypeStruct((depth, MOD_ROWS, n), _F32),
        grid=(depth, n // ADA_COL_TILE),
        in_specs=[
            pl.BlockSpec((MOD_ROWS, D_MODEL), lambda l, j: (0, 0)),
            pl.BlockSpec((None, D_MODEL, ADA_COL_TILE), lambda l, j: (l, 0, j)),
            pl.BlockSpec((None, 1, ADA_COL_TILE), lambda l, j: (l, 0, j)),
        ],
        out_specs=pl.BlockSpec((None, MOD_ROWS, ADA_COL_TILE), lambda l, j: (l, 0, j)),
        compiler_params=_cparams("arbitrary", "arbitrary"),
        name="ada_table",
    )(cs, ada_w, ada_b.reshape(depth, 1, n))


def _a_proj_kernel(*refs, n_h, layer):
    h_refs, (g_ref, mod_ref, win_hbm, wgt_ref, bg_ref,
             q_ref, kt_ref, v_ref, o_ref, gr_ref, gc_ref,
             wqvo_ref, wkt_ref, stage_ref, sem_ref) = refs[:n_h], refs[n_h:]

    @pl.when(pl.program_id(0) == 0)
    def _():
        def store_qvo(dst0, c, width, tile):
            wqvo_ref[:, dst0 + c:dst0 + c + width] = tile.astype(_BF16)

        def store_kt(c, width, tile):
            wkt_ref[c:c + width, :] = tile.T.astype(_BF16)

        _stage_weights(
            _col_tiles(win_hbm, layer, 0, A_QK_DIM, functools.partial(store_qvo, 0))
            + _col_tiles(win_hbm, layer, A_QK_DIM, A_QK_DIM, store_kt)
            + _col_tiles(win_hbm, layer, 2 * A_QK_DIM, 2 * D_MODEL, functools.partial(store_qvo, A_QK_DIM)),
            stage_ref, sem_ref)
        wkt_ref[A_QK_DIM:, :] = wgt_ref[...]

    x = _norm_mod(_load_h(h_refs), g_ref[...], _mod_part(mod_ref, 0), _mod_part(mod_ref, 1)).astype(_BF16)
    nt = (((1,), (1,)), ((), ()))
    ktg = lax.dot_general(wkt_ref[...], x, nt, preferred_element_type=_F32)
    kt_ref[...] = ktg[:A_QK_DIM].astype(_BF16)
    gt = ktg[A_QK_DIM:] + bg_ref[...]

    n_chunks = gt.shape[1] // A_CHUNK

    def by_chunk(rows):
        return jnp.concatenate([rows[:, c * A_CHUNK:(c + 1) * A_CHUNK] for c in range(n_chunks)], axis=0)

    def by_token(x):
        return jnp.concatenate([x[c * A_HEADS:(c + 1) * A_HEADS] for c in range(n_chunks)], axis=1)

    def log_sigmoid(z):
        return jnp.minimum(z, 0.0) - jnp.log1p(jnp.exp(-jnp.abs(z)))

    row_form, col_cm, col_b = [], [], []
    for d in range(2):
        li = by_chunk(gt[2 * d * A_HEADS:(2 * d + 1) * A_HEADS]) * LOG2E
        lf = log_sigmoid(by_chunk(gt[(2 * d + 1) * A_HEADS:(2 * d + 2) * A_HEADS])) * LOG2E
        b = _lane_scan(lf, jnp.add, 0.0, d == 1)
        a = li - b
        cm = _lane_scan(a, jnp.maximum, -jnp.inf, d == 1)
        row_form += [by_token(a), by_token(b)]
        col_cm.append(cm)
        col_b.append(b)
    gr_ref[...] = jnp.concatenate(row_form, axis=0)
    pad = jnp.zeros((LANES - 2 * A_HEADS, LANES), _F32)
    for c in range(n_chunks):
        rows = slice(c * A_HEADS, (c + 1) * A_HEADS)
        toks = slice(c * A_CHUNK, (c + 1) * A_CHUNK)
        gc_ref[toks, :LANES] = jnp.concatenate([col_cm[0][rows], col_cm[1][rows], pad], axis=0).T
        gc_ref[toks, LANES:] = jnp.concatenate([col_b[0][rows], col_b[1][rows], pad], axis=0).T

    y = jnp.dot(x, wqvo_ref[...], preferred_element_type=_F32)
    q_ref[...] = (y[:, :A_QK_DIM] * (A_DK ** -0.5)).astype(_BF16)
    v_ref[...] = y[:, A_QK_DIM:A_QK_DIM + D_MODEL].astype(_BF16)
    o_ref[...] = y[:, A_QK_DIM + D_MODEL:]


def _a_proj(h_parts, g, mod_l, w_in, layer, wgt, bgate, dims):
    rows = dims.rows
    tm = ROW_TILE
    n_gate = 4 * A_HEADS
    return pl.pallas_call(
        functools.partial(_a_proj_kernel, n_h=len(h_parts), layer=layer),
        out_shape=(
            jax.ShapeDtypeStruct((rows, A_QK_DIM), _BF16),
            jax.ShapeDtypeStruct((A_QK_DIM, rows), _BF16),
            jax.ShapeDtypeStruct((rows, D_MODEL), _BF16),
            jax.ShapeDtypeStruct((rows, D_MODEL), _F32),
            jax.ShapeDtypeStruct((n_gate, rows), _F32),
            jax.ShapeDtypeStruct((rows, 2 * LANES), _F32),
        ),
        grid=(rows // tm,),
        in_specs=_h_specs(tm, len(h_parts) == 2) + [
            _resident((1, D_MODEL)),
            _mod_spec(dims.n_ctx_rows, dims.seq, tm),
            _weight_spec(),
            _resident(wgt.shape),
            _resident((n_gate, 1)),
        ],
        out_specs=(
            _row_spec(tm, A_QK_DIM),
            pl.BlockSpec((A_QK_DIM, tm), lambda i: (0, i)),
            _row_spec(tm, D_MODEL),
            _row_spec(tm, D_MODEL),
            pl.BlockSpec((n_gate, tm), lambda i: (0, i)),
            _row_spec(tm, 2 * LANES),
        ),
        scratch_shapes=[pltpu.VMEM((D_MODEL, A_QK_DIM + 2 * D_MODEL), _BF16),
                        pltpu.VMEM((A_QK_DIM + 4 * A_HEADS, D_MODEL), _BF16)] + _stage_scratch(),
        compiler_params=_cparams("arbitrary"),
        name="a_proj",
    )(*h_parts, g, mod_l, w_in, wgt, bgate)


def _lane_scan(x, op, fill, reverse):
    lane = lax.broadcasted_iota(jnp.int32, x.shape, 1)
    k = 1
    while k < LANES:
        if reverse:
            shifted = jnp.where(lane < LANES - k, pltpu.roll(x, LANES - k, axis=1), fill)
        else:
            shifted = jnp.where(lane >= k, pltpu.roll(x, k, axis=1), fill)
        x = op(x, shifted)
        k *= 2
    return x


def _a_scan_kernel(*refs, step_chunks, has_init, emit_state):
    n_state = 5
    (qf_ref, ktf_ref, vf_ref, grf_ref, gcf_ref, qb_ref, ktb_ref, vb_ref, grb_ref, gcb_ref), refs = refs[:10], refs[10:]
    init_refs, refs = (refs[:n_state], refs[n_state:]) if has_init else ((), refs)
    (hf_ref, hb_ref), refs = refs[:2], refs[2:]
    final_refs, refs = (refs[:n_state], refs[n_state:]) if emit_state else ((), refs)
    cf_ref, cb_ref, mrf_ref, mrb_ref, mc_ref = state_refs = refs

    @pl.when(pl.program_id(1) == 0)
    def _():
        if has_init:
            for ref, init in zip(state_refs, init_refs):
                ref[...] = init[...]
        else:
            cf_ref[...] = jnp.zeros(cf_ref.shape, _F32)
            cb_ref[...] = jnp.zeros(cb_ref.shape, _F32)
            mrf_ref[...] = jnp.full(mrf_ref.shape, -jnp.inf, _F32)
            mrb_ref[...] = jnp.full(mrb_ref.shape, -jnp.inf, _F32)
            mc_ref[...] = jnp.full(mc_ref.shape, -jnp.inf, _F32)

    t_idx = lax.broadcasted_iota(jnp.int32, (A_CHUNK, A_CHUNK), 0)
    s_idx = lax.broadcasted_iota(jnp.int32, (A_CHUNK, A_CHUNK), 1)
    ones_blk = jnp.ones((A_CHUNK, A_DV), _BF16)
    lane = lax.broadcasted_iota(jnp.int32, (1, LANES), 1)
    for sub in range(step_chunks):
        toks = (slice(sub * A_CHUNK, (sub + 1) * A_CHUNK),
                slice((step_chunks - 1 - sub) * A_CHUNK, (step_chunks - sub) * A_CHUNK))
        _a_chunk_pair(toks, (qf_ref, qb_ref), (ktf_ref, ktb_ref), (vf_ref, vb_ref), (grf_ref, grb_ref),
                      (gcf_ref, gcb_ref), (hf_ref, hb_ref), (cf_ref, cb_ref), (mrf_ref, mrb_ref), mc_ref,
                      t_idx, s_idx, ones_blk, lane)

    if emit_state:
        @pl.when(pl.program_id(1) == pl.num_programs(1) - 1)
        def _():
            for ref, final in zip(state_refs, final_refs):
                final[...] = ref[...]


def _a_chunk_pair(toks, q_refs, kt_refs, v_refs, gr_refs, gc_refs, h_refs, c_refs, mr_refs, mc_ref,
                  t_idx, s_idx, ones_blk, lane):
    m_col = mc_ref[0:1, :]
    m_col_new = []
    dirs = []
    for d in range(2):
        gr_ref, gc_ref, mr_ref, tok = gr_refs[d], gc_refs[d], mr_refs[d], toks[d]
        last = A_CHUNK - 1 if d == 0 else 0
        a = gr_ref[2 * d * A_HEADS:(2 * d + 1) * A_HEADS, tok]
        b = gr_ref[(2 * d + 1) * A_HEADS:(2 * d + 2) * A_HEADS, tok]
        m_row = mr_ref[:, 0:1]
        gg_end = jnp.maximum(m_row, jnp.max(a, axis=1, keepdims=True))
        ws = jnp.exp2(a - gg_end)
        decay = jnp.exp2(m_row - gg_end)
        mr_ref[...] = jnp.broadcast_to(b[:, last:last + 1] + gg_end, mr_ref.shape)
        gg_c = jnp.maximum(m_col, gc_ref[tok, :LANES])
        b_plus_gg = gc_ref[tok, LANES:] + gg_c
        clamp_c = jnp.exp2(-b_plus_gg)
        m_col_new.append(b_plus_gg[last:last + 1, :])
        dirs.append((a, m_row, ws, decay, gg_c, clamp_c))
    mc_ref[...] = jnp.broadcast_to(jnp.where(lane < A_HEADS, m_col_new[0], m_col_new[1]), mc_ref.shape)

    pair_lane = lax.broadcasted_iota(jnp.int32, (A_CHUNK, 2 * A_DK), 1)
    zeros_kt = jnp.zeros((A_DK, A_CHUNK), _BF16)
    zeros_c = jnp.zeros((A_DK, 2 * A_DV), _BF16)
    for hp in range(A_HEADS // 2):
        for d in range(2):
            q_ref, kt_ref, v_ref, h_ref, c_ref, tok = q_refs[d], kt_refs[d], v_refs[d], h_refs[d], c_refs[d], toks[d]
            a, m_row, ws, decay, gg_c, clamp_c = dirs[d]
            mask = (s_idx <= t_idx) if d == 0 else (s_idx >= t_idx)
            heads = (2 * hp, 2 * hp + 1)
            q2 = q_ref[tok, 2 * hp * A_DK:(2 * hp + 2) * A_DK]
            kts = [kt_ref[hd * A_DK:(hd + 1) * A_DK, tok] for hd in heads]
            kt_bd = jnp.concatenate([jnp.concatenate([kts[0], zeros_kt], axis=1),
                                     jnp.concatenate([zeros_kt, kts[1]], axis=1)], axis=0)
            s2 = jnp.dot(q2, kt_bd, preferred_element_type=_F32)
            ggs = [jnp.broadcast_to(gg_c[:, d * A_HEADS + hd:d * A_HEADS + hd + 1], (A_CHUNK, A_CHUNK))
                   for hd in heads]
            e_state = jnp.exp2(jnp.where(pair_lane < A_DK, m_row[heads[0]:heads[0] + 1, :] - ggs[0],
                                         m_row[heads[1]:heads[1] + 1, :] - ggs[1]))
            qe = q2.astype(_F32) * e_state
            for i, hd in enumerate(heads):
                col = d * A_HEADS + hd
                clamp = jnp.broadcast_to(clamp_c[:, col:col + 1], (A_CHUNK, A_DV))
                v = v_ref[tok, hd * A_DV:(hd + 1) * A_DV]
                v_ext = jnp.concatenate([v, ones_blk], axis=1)
                c_old = c_ref[hd]
                e_keys = jnp.exp2(jnp.where(mask, a[hd:hd + 1, :] - ggs[i], -jnp.inf))
                p = jnp.concatenate([s2[:, i * A_CHUNK:(i + 1) * A_CHUNK] * e_keys, qe], axis=1).astype(_BF16)
                c_rows = [c_old.astype(_BF16), zeros_c] if i == 0 else [zeros_c, c_old.astype(_BF16)]
                rhs = jnp.concatenate([v_ext] + c_rows, axis=0)
                num = jnp.dot(p, rhs, preferred_element_type=_F32)
                den = jnp.maximum(jnp.abs(num[:, A_DV:]), clamp)
                h_ref[tok, hd * A_DV:(hd + 1) * A_DV] = num[:, :A_DV] / den
                kw = (kts[i].astype(_F32) * ws[hd:hd + 1, :]).astype(_BF16)
                c_ref[hd] = decay[hd:hd + 1, :] * c_old + jnp.dot(kw, v_ext, preferred_element_type=_F32)


def _a_scan_call(q, kt, v, gr, gc, dims, step_chunks, first_row, seq_rows, init=None, emit_state=False):
    blk_rows = step_chunks * A_CHUNK
    n_steps = seq_rows // blk_rows
    assert seq_rows % blk_rows == 0 and first_row % blk_rows == 0
    blk0 = first_row // blk_rows

    def fwd_blk(b, j):
        return blk0 + b * n_steps + j

    def bwd_blk(b, j):
        return blk0 + b * n_steps + (n_steps - 1 - j)

    def specs(blk):
        return [
            pl.BlockSpec((blk_rows, A_QK_DIM), lambda b, j: (blk(b, j), 0)),
            pl.BlockSpec((A_QK_DIM, blk_rows), lambda b, j: (0, blk(b, j))),
            pl.BlockSpec((blk_rows, D_MODEL), lambda b, j: (blk(b, j), 0)),
            pl.BlockSpec((4 * A_HEADS, blk_rows), lambda b, j: (0, blk(b, j))),
            pl.BlockSpec((blk_rows, 2 * LANES), lambda b, j: (blk(b, j), 0)),
        ]

    state_shapes = [(A_HEADS, A_DK, 2 * A_DV)] * 2 + [(A_HEADS, LANES)] * 3
    state_specs = [pl.BlockSpec((None,) + shp, lambda b, j, n=len(shp): (b,) + (0,) * n) for shp in state_shapes]
    h_shape = jax.ShapeDtypeStruct((dims.batch * seq_rows, D_MODEL), _F32)
    out_shape = [h_shape, h_shape]
    out_specs = [pl.BlockSpec((blk_rows, D_MODEL), lambda b, j: (b * n_steps + j, 0)),
                 pl.BlockSpec((blk_rows, D_MODEL), lambda b, j: (b * n_steps + (n_steps - 1 - j), 0))]
    if emit_state:
        out_shape += [jax.ShapeDtypeStruct((dims.batch,) + shp, _F32) for shp in state_shapes]
        out_specs += state_specs
    return pl.pallas_call(
        functools.partial(_a_scan_kernel, step_chunks=step_chunks, has_init=init is not None,
                          emit_state=emit_state),
        out_shape=tuple(out_shape),
        grid=(dims.batch, n_steps),
        in_specs=specs(fwd_blk) + specs(bwd_blk) + (state_specs if init is not None else []),
        out_specs=tuple(out_specs),
        scratch_shapes=[pltpu.VMEM(shp, _F32) for shp in state_shapes],
        compiler_params=_cparams("arbitrary", "arbitrary"),
        name="a_scan",
    )(q, kt, v, gr, gc, q, kt, v, gr, gc, *(init or ()))


def _a_scan(q, kt, v, gr, gc, dims):
    hf_c, hb_c, *state = _a_scan_call(q, kt, v, gr, gc, dims, A_STEP_CHUNKS_CTX, 0, dims.ctx_len,
                                      emit_state=True)
    hf_l, hb_l = _a_scan_call(q, kt, v, gr, gc, dims, A_STEP_CHUNKS_LAT, dims.n_ctx_rows, dims.seq,
                              init=state)
    return (hf_l, hf_c), (hb_l, hb_c)


def _a_mix(hf_lat_ref, hf_ctx_ref, hb_lat_ref, hb_ctx_ref, o_ref, hg_ref, tile0, cols):
    hs = _load_h((hf_lat_ref, hf_ctx_ref), tile0, cols) + _load_h((hb_lat_ref, hb_ctx_ref), tile0, cols)
    parts = []
    for c0 in range(0, hs.shape[1], A_DV):
        x = hs[:, c0:c0 + A_DV]
        parts.append(x * lax.rsqrt(jnp.mean(x * x, axis=-1, keepdims=True) + NORM_EPS))
    y = jnp.concatenate(parts, axis=1) * hg_ref[:, cols]
    return (jax.nn.sigmoid(o_ref[:, cols]) * y).astype(_BF16)


ROPE_HALF = B_HEAD_DIM // 4


def _swap_halves_lanes(x):
    lane = lax.broadcasted_iota(jnp.int32, x.shape, 1)
    fwd = pltpu.roll(x, LANES - ROPE_HALF, axis=1)
    back = pltpu.roll(x, ROPE_HALF, axis=1)
    return jnp.where(lane % (2 * ROPE_HALF) < ROPE_HALF, fwd, back)


def _swap_halves_rows(x):
    parts = []
    for r0 in range(0, x.shape[0], 2 * ROPE_HALF):
        parts += [x[r0 + ROPE_HALF:r0 + 2 * ROPE_HALF], x[r0:r0 + ROPE_HALF]]
    return jnp.concatenate(parts, axis=0)


def _b_proj_kernel(h_ref, g_ref, mod_ref, wqkv_hbm, cos_ref, sin_ref, cost_ref, sint_ref,
                   qt_ref, k_ref, vt_ref, wqvt_ref, wk_ref, stage_ref, sem_ref, *, layer):
    @pl.when(pl.program_id(0) == 0)
    def _():
        def store_t(dst0, c, width, tile):
            wqvt_ref[dst0 + c:dst0 + c + width, :] = tile.T.astype(_BF16)

        def store_k(c, width, tile):
            wk_ref[:, c:c + width] = tile.astype(_BF16)

        _stage_weights(
            _col_tiles(wqkv_hbm, layer, 0, D_MODEL, functools.partial(store_t, 0))
            + _col_tiles(wqkv_hbm, layer, D_MODEL, B_KV_DIM, store_k)
            + _col_tiles(wqkv_hbm, layer, D_MODEL + B_KV_DIM, B_KV_DIM, functools.partial(store_t, D_MODEL)),
            stage_ref, sem_ref)

    x = _norm_mod(h_ref[...], g_ref[...], _mod_part(mod_ref, 0), _mod_part(mod_ref, 1)).astype(_BF16)
    nt = (((1,), (1,)), ((), ()))
    yt = lax.dot_general(wqvt_ref[...], x, nt, preferred_element_type=_F32)
    cost, sint = cost_ref[...], sint_ref[...]
    for hd in range(B_Q_HEADS):
        rows = slice(hd * B_HEAD_DIM, (hd + 1) * B_HEAD_DIM)
        qt = yt[rows, :]
        qt = (qt * cost + _swap_halves_rows(qt) * sint) * (B_HEAD_DIM ** -0.5 * LOG2E)
        qt_ref[rows, :] = qt.astype(_BF16)
    vt_ref[...] = yt[D_MODEL:, :].astype(_BF16)
    cos, sin = cos_ref[...], sin_ref[...]
    k = jnp.dot(x, wk_ref[...], preferred_element_type=_F32)
    for c0 in range(0, B_KV_DIM, LANES):
        kc = k[:, c0:c0 + LANES]
        k_ref[:, c0:c0 + LANES] = (kc * cos + _swap_halves_lanes(kc) * sin).astype(_BF16)


def _b_proj(h, g, mod_l, w_qkv, layer, rope, dims):
    rows = h.shape[0]
    tm = ROW_TILE
    cos, sin, cost, sint = rope
    return pl.pallas_call(
        functools.partial(_b_proj_kernel, layer=layer),
        out_shape=(
            jax.ShapeDtypeStruct((D_MODEL, rows), _BF16),
            jax.ShapeDtypeStruct((rows, B_KV_DIM), _BF16),
            jax.ShapeDtypeStruct((B_KV_DIM, rows), _BF16),
        ),
        grid=(rows // tm,),
        in_specs=[
            _row_spec(tm, D_MODEL),
            _resident((1, D_MODEL)),
            _mod_spec(dims.n_ctx_rows, dims.seq, tm),
            _weight_spec(),
            _row_spec(tm, LANES),
            _row_spec(tm, LANES),
            pl.BlockSpec((B_HEAD_DIM, tm), lambda i: (0, i)),
            pl.BlockSpec((B_HEAD_DIM, tm), lambda i: (0, i)),
        ],
        out_specs=(
            pl.BlockSpec((D_MODEL, tm), lambda i: (0, i)),
            _row_spec(tm, B_KV_DIM),
            pl.BlockSpec((B_KV_DIM, tm), lambda i: (0, i)),
        ),
        scratch_shapes=[pltpu.VMEM((D_MODEL + B_KV_DIM, D_MODEL), _BF16),
                        pltpu.VMEM((D_MODEL, B_KV_DIM), _BF16)] + _stage_scratch(),
        compiler_params=_cparams("arbitrary"),
        name="b_proj",
    )(h, g, mod_l, w_qkv, cos, sin, cost, sint)


B_ONES_ROWS = 16
B_STEP_BLOCKS_CTX = 2
B_STEP_BLOCKS_LAT = 4


def _b_attn_kernel(sink_ref, win_l_ref, win_r_ref, qt_ref, kl_ref, kc_ref, kr_ref, kx_ref,
                   vtl_ref, vtc_ref, vtr_ref, vtx_ref, o_ref, s_ref,
                   *, n_ctx_blocks, blocks_per_seq, ctx_len, step_blocks, first_block):
    first = first_block + pl.program_id(0) * step_blocks
    is_lat = first >= n_ctx_blocks
    n = (first - n_ctx_blocks) % blocks_per_seq
    neg = -jnp.inf
    win_l = jnp.where(is_lat, win_l_ref[...], neg)
    win_r = jnp.where(is_lat, win_r_ref[...], neg)
    bias_c = jnp.where(is_lat, 0.0, neg)
    bias_first_l = jnp.where(jnp.logical_and(is_lat, n >= 1), win_l_ref[...], neg)
    bias_last_r = jnp.where(jnp.logical_and(is_lat, n + step_blocks <= blocks_per_seq - 1), win_r_ref[...], neg)

    n_q = B_GROUP * B_BLOCK
    head_of_lane = lax.broadcasted_iota(jnp.int32, (1, n_q), 1) // B_BLOCK
    ones_rows = jnp.ones((B_ONES_ROWS, B_BLOCK), _BF16)
    n_ctx_tiles = ctx_len // B_BLOCK

    def local_tiles(blk):
        if first_block < n_ctx_blocks:
            return []
        own = slice(blk * B_BLOCK, (blk + 1) * B_BLOCK)
        if blk == 0:
            left = (kl_ref, vtl_ref, slice(0, B_BLOCK), bias_first_l)
        else:
            left = (kc_ref, vtc_ref, slice((blk - 1) * B_BLOCK, blk * B_BLOCK), win_l)
        if blk == step_blocks - 1:
            right = (kr_ref, vtr_ref, slice(0, B_BLOCK), bias_last_r)
        else:
            right = (kc_ref, vtc_ref, slice((blk + 1) * B_BLOCK, (blk + 2) * B_BLOCK), win_r)
        return [left, (kc_ref, vtc_ref, own, bias_c), right]

    def scores(blk, g):
        cols = slice(blk * B_BLOCK, (blk + 1) * B_BLOCK)
        qt = jnp.concatenate(
            [qt_ref[(g * B_GROUP + j) * B_HEAD_DIM:(g * B_GROUP + j + 1) * B_HEAD_DIM, cols]
             for j in range(B_GROUP)], axis=1)
        sink = jnp.zeros((1, n_q), _F32)
        for j in range(B_GROUP):
            sink = jnp.where(head_of_lane == j, sink_ref[g * B_GROUP + j] * LOG2E, sink)
        ks = slice(g * B_HEAD_DIM, (g + 1) * B_HEAD_DIM)
        k_tiles = [(k_ref[rows, ks], bias) for k_ref, _, rows, bias in local_tiles(blk)]
        for t in range(n_ctx_tiles):
            k_tiles.append((kx_ref[t * B_BLOCK:(t + 1) * B_BLOCK, ks], None))
        m_tile = None
        s_all = jnp.dot(jnp.concatenate([k for k, _ in k_tiles], axis=0), qt, preferred_element_type=_F32)
        for t, (_, bias) in enumerate(k_tiles):
            s = s_all[t * B_BLOCK:(t + 1) * B_BLOCK]
            if bias is not None:
                s = s + bias
            s_ref[blk, g, t] = s
            m_tile = s if m_tile is None else jnp.maximum(m_tile, s)
        return jnp.maximum(sink, jnp.max(m_tile, axis=0, keepdims=True)), sink

    def attend(blk, g, m, sink):
        ks = slice(g * B_HEAD_DIM, (g + 1) * B_HEAD_DIM)
        vt_tiles = [vt_ref[ks, cols] for _, vt_ref, cols, _ in local_tiles(blk)]
        for t in range(n_ctx_tiles):
            vt_tiles.append(vtx_ref[ks, t * B_BLOCK:(t + 1) * B_BLOCK])
        acc = jnp.zeros((B_HEAD_DIM + B_ONES_ROWS, n_q), _F32)
        for t, vt in enumerate(vt_tiles):
            p = jnp.exp2(s_ref[blk, g, t] - m).astype(_BF16)
            vt_ext = jnp.concatenate([vt, ones_rows], axis=0)
            acc = acc + jnp.dot(vt_ext, p, preferred_element_type=_F32)
        denom = jnp.exp2(sink - m) + acc[B_HEAD_DIM:B_HEAD_DIM + 1, :]
        out_t = acc[:B_HEAD_DIM, :] * (1.0 / denom)
        rows = slice(blk * B_BLOCK, (blk + 1) * B_BLOCK)
        for pair in range(B_GROUP // 2):
            two = jnp.concatenate([out_t[:, (2 * pair) * B_BLOCK:(2 * pair + 1) * B_BLOCK],
                                   out_t[:, (2 * pair + 1) * B_BLOCK:(2 * pair + 2) * B_BLOCK]], axis=0)
            c0 = (g * B_GROUP + 2 * pair) * B_HEAD_DIM
            o_ref[rows, c0:c0 + 2 * B_HEAD_DIM] = two.T.astype(_BF16)

    units = [(blk, g) for blk in range(step_blocks) for g in range(B_KV_HEADS)]
    pending = [scores(*units[0]), scores(*units[1])]
    for i, unit in enumerate(units):
        if i + 2 < len(units):
            pending.append(scores(*units[i + 2]))
        attend(*unit, *pending[i])


def _window_bias():
    key = np.arange(B_BLOCK)[:, None]
    qry = np.arange(B_BLOCK)[None, :]
    prev_blk = np.where(key >= qry, 0.0, -np.inf).astype(np.float32)
    next_blk = np.where(key <= qry, 0.0, -np.inf).astype(np.float32)
    return jnp.asarray(np.tile(prev_blk, (1, B_GROUP))), jnp.asarray(np.tile(next_blk, (1, B_GROUP)))


def _b_attn_call(sinks, qt, k, vt, dims, step_blocks, first_block, n_blocks):
    step_rows = step_blocks * B_BLOCK
    ncc = dims.ctx_len // B_BLOCK
    ncl = dims.seq // B_BLOCK
    lat0 = dims.batch * ncc
    nblk = k.shape[0] // B_BLOCK
    assert first_block % step_blocks == 0 and n_blocks % step_blocks == 0
    assert ncc % step_blocks == 0 if first_block < lat0 else ncl % step_blocks == 0
    assert first_block >= lat0 or first_block + n_blocks <= lat0
    off = first_block // step_blocks

    def batch_of(s):
        i = first_block + s * step_blocks
        return jnp.where(i < lat0, i // ncc, (i - lat0) // ncl)

    def left(s):
        return jnp.maximum(first_block + s * step_blocks - 1, 0)

    def right(s):
        return jnp.minimum(first_block + (s + 1) * step_blocks, nblk - 1)

    n_tiles = 3 + ncc
    return pl.pallas_call(
        functools.partial(_b_attn_kernel, n_ctx_blocks=lat0, blocks_per_seq=ncl, ctx_len=dims.ctx_len,
                          step_blocks=step_blocks, first_block=first_block),
        out_shape=jax.ShapeDtypeStruct((n_blocks * B_BLOCK, D_MODEL), _BF16),
        grid=(n_blocks // step_blocks,),
        in_specs=[
            pl.BlockSpec(memory_space=pltpu.SMEM),
            _resident((B_BLOCK, B_GROUP * B_BLOCK)),
            _resident((B_BLOCK, B_GROUP * B_BLOCK)),
            pl.BlockSpec((D_MODEL, step_rows), lambda s: (0, off + s)),
            pl.BlockSpec((B_BLOCK, B_KV_DIM), lambda s: (left(s), 0)),
            pl.BlockSpec((step_rows, B_KV_DIM), lambda s: (off + s, 0)),
            pl.BlockSpec((B_BLOCK, B_KV_DIM), lambda s: (right(s), 0)),
            pl.BlockSpec((dims.ctx_len, B_KV_DIM), lambda s: (batch_of(s), 0)),
            pl.BlockSpec((B_KV_DIM, B_BLOCK), lambda s: (0, left(s))),
            pl.BlockSpec((B_KV_DIM, step_rows), lambda s: (0, off + s)),
            pl.BlockSpec((B_KV_DIM, B_BLOCK), lambda s: (0, right(s))),
            pl.BlockSpec((B_KV_DIM, dims.ctx_len), lambda s: (0, batch_of(s))),
        ],
        out_specs=pl.BlockSpec((step_rows, D_MODEL), lambda s: (s, 0)),
        scratch_shapes=[pltpu.VMEM((step_blocks, B_KV_HEADS, n_tiles, B_BLOCK, B_GROUP * B_BLOCK), _F32)],
        compiler_params=_cparams("arbitrary"),
        name="b_attn",
    )(sinks, *_window_bias(), qt, k, k, k, k, vt, vt, vt, vt)


def _b_attn(sinks, qt, k, vt, dims):
    lat0 = dims.n_ctx_rows // B_BLOCK
    nblk = k.shape[0] // B_BLOCK
    y_ctx = _b_attn_call(sinks, qt, k, vt, dims, B_STEP_BLOCKS_CTX, 0, lat0)
    y_lat = _b_attn_call(sinks, qt, k, vt, dims, B_STEP_BLOCKS_LAT, lat0, nblk - lat0)
    return y_lat, y_ctx


C_LATENT_TILE = 2 * ROW_TILE


def _c_proj_kernel(h_hbm, g_ref, mod_hbm, win_hbm, bg_hbm, u_hbm, w_ref, stage_ref, sem_ref, *, layer, dims):
    def store(c, width, tile):
        w_ref[:, c:c + width] = tile.astype(_BF16)

    _stage_weights(_col_tiles(win_hbm, layer, 0, 3 * D_MODEL, store), stage_ref, sem_ref)

    def tile_body(h_ref, mod_ref, bg_ref, u_ref):
        x = _norm_mod(h_ref[...], g_ref[...], _mod_part(mod_ref, 0), _mod_part(mod_ref, 1)).astype(_BF16)
        y = jnp.dot(x, w_ref[...], preferred_element_type=_F32)
        bg_ref[...] = y[:, :D_MODEL]
        u_ref[...] = y[:, D_MODEL:2 * D_MODEL] * y[:, 2 * D_MODEL:]

    def run(row0, nrows, tm, mod_group):
        rows = lambda ref: ref.at[pl.ds(row0, nrows)]
        pltpu.emit_pipeline(
            tile_body,
            grid=(nrows // tm,),
            in_specs=[_row_spec(tm, D_MODEL),
                      pl.BlockSpec((None, 1, 6 * D_MODEL), lambda i: (mod_group(i), 0, 0))],
            out_specs=[_row_spec(tm, D_MODEL)] * 2,
        )(rows(h_hbm), mod_hbm, rows(bg_hbm), rows(u_hbm))

    n_ctx = dims.n_ctx_rows
    assert n_ctx % ROW_TILE == 0 and dims.seq % C_LATENT_TILE == 0
    run(0, n_ctx, ROW_TILE, lambda i: 0)
    run(n_ctx, h_hbm.shape[0] - n_ctx, C_LATENT_TILE, lambda i: 1 + i * C_LATENT_TILE // dims.seq)


def _c_proj(h, g, mod_l, w_in, layer, dims):
    rows = h.shape[0]
    in_hbm = pl.BlockSpec(memory_space=pl.ANY)
    return pl.pallas_call(
        functools.partial(_c_proj_kernel, layer=layer, dims=dims),
        out_shape=(jax.ShapeDtypeStruct((rows, D_MODEL), _F32),) * 2,
        in_specs=[in_hbm, pl.BlockSpec(memory_space=pltpu.VMEM), in_hbm, _weight_spec()],
        out_specs=(in_hbm,) * 2,
        scratch_shapes=[pltpu.VMEM((D_MODEL, 3 * D_MODEL), _BF16)] + _stage_scratch(),
        compiler_params=_cparams(),
        name="c_proj",
    )(h, g, mod_l, w_in)


def _c_mix(bg_ref, u_ref, up_ref, un_ref, cw_ref, cb_ref, tile, dims, cols):
    tm = u_ref.shape[0]
    u = u_ref[:, cols]
    row = lax.broadcasted_iota(jnp.int32, (tm, 1), 0)
    g_row = tile * tm + row
    in_ctx = g_row < dims.n_ctx_rows
    pos = jnp.where(in_ctx, g_row % dims.ctx_len, (g_row - dims.n_ctx_rows) % dims.seq)
    length = jnp.where(in_ctx, dims.ctx_len, dims.seq)
    prev = jnp.where(row == 0, up_ref[SUBLANES - 1:SUBLANES, cols], pltpu.roll(u, 1, axis=0))
    nxt = jnp.where(row == tm - 1, un_ref[0:1, cols], pltpu.roll(u, tm - 1, axis=0))
    prev = jnp.where(pos == 0, 0.0, prev)
    nxt = jnp.where(pos == length - 1, 0.0, nxt)
    conv = prev * cw_ref[0:1, cols] + u * cw_ref[1:2, cols] + nxt * cw_ref[2:3, cols] + cb_ref[:, cols]
    return (bg_ref[:, cols] * conv).astype(_BF16)


_N_MIX_REFS = (6, 2, 6)


def _post_kernel(*refs, kind, n_h, tile0, final_norm, dims, out_layer, layer):
    n_mix = _N_MIX_REFS[kind]
    h_refs, mod_ref = refs[:n_h], refs[n_h]
    mix_refs = refs[n_h + 1:n_h + 1 + n_mix]
    (wo_hbm, g_ref, w1_hbm, w2_hbm, fg_ref, out_ref,
     wo_ref, w1_ref, w2_ref, stage_ref, sem_ref) = refs[n_h + 1 + n_mix:]
    n_chunks = MLP_HIDDEN // MLP_HIDDEN_CHUNK
    tiles_wo = D_MODEL // STAGE_COLS
    tiles_w1 = MLP_HIDDEN_CHUNK // STAGE_COLS
    tiles_w2 = D_MODEL // STAGE_COLS

    def compute(need):
        proj = None
        for c0 in range(0, D_MODEL, OUT_PROJ_COLS):
            cols = slice(c0, c0 + OUT_PROJ_COLS)
            if kind == 0:
                y = _a_mix(*mix_refs, tile0, cols)
            elif kind == 1:
                y = _load_h(mix_refs, 0, cols)
            else:
                y = _c_mix(*mix_refs, pl.program_id(0) + tile0, dims, cols)
            if c0 == 0:
                need(tiles_wo)
            part = jnp.dot(y, wo_ref[cols, :], preferred_element_type=_F32)
            proj = part if proj is None else proj + part
        h = _load_h(h_refs, tile0) + _mod_part(mod_ref, 2) * proj
        x = _norm_mod(h, g_ref[...], _mod_part(mod_ref, 3), _mod_part(mod_ref, 4)).astype(_BF16)
        acc = jnp.zeros(h.shape, _F32)
        for c in range(n_chunks):
            cols = slice(c * MLP_HIDDEN_CHUNK, (c + 1) * MLP_HIDDEN_CHUNK)
            need(tiles_wo + c * (tiles_w1 + tiles_w2) + tiles_w1)
            u = jnp.dot(x, w1_ref[:, cols], preferred_element_type=_F32)
            u = jnp.square(jnp.maximum(u, 0.0)).astype(_BF16)
            need(tiles_wo + (c + 1) * (tiles_w1 + tiles_w2))
            acc = acc + jnp.dot(u, w2_ref[cols, :], preferred_element_type=_F32)
        out = h + _mod_part(mod_ref, 5) * acc
        if final_norm:
            ms = jnp.mean(out * out, axis=-1, keepdims=True)
            out = out * lax.rsqrt(ms + NORM_EPS) * fg_ref[...]
        out_ref[...] = out

    @pl.when(pl.program_id(0) == 0)
    def _():
        def store_to(dst_ref, row0, c, width, tile):
            dst_ref[row0:row0 + tile.shape[0], c:c + width] = tile.astype(_BF16)

        jobs = _col_tiles(wo_hbm, out_layer, 0, D_MODEL, functools.partial(store_to, wo_ref, 0))
        for r0 in range(0, MLP_HIDDEN, MLP_HIDDEN_CHUNK):
            jobs += _col_tiles(w1_hbm, layer, r0, MLP_HIDDEN_CHUNK,
                               lambda c, width, tile, r0=r0: store_to(w1_ref, 0, r0 + c, width, tile))
            for c in range(0, D_MODEL, STAGE_COLS):
                jobs.append((w2_hbm.at[layer, pl.ds(r0, MLP_HIDDEN_CHUNK), pl.ds(c, STAGE_COLS)],
                             functools.partial(store_to, w2_ref, r0, c, STAGE_COLS)))
        compute(_WeightStager(jobs, stage_ref, sem_ref).need)

    @pl.when(pl.program_id(0) != 0)
    def _():
        compute(lambda n: None)


def _post(kind, h_parts, mod_l, mix, w_out, out_layer, g, w1, w2, layer, final_g, dims, tile0=0,
          final_norm=False):
    tm = ROW_TILE
    n_tiles = dims.rows // tm - tile0
    row = functools.partial(_row_spec, tm, D_MODEL, tile0)
    if kind == 0:
        (hf_l, hf_c), (hb_l, hb_c), o, head_g = mix
        mix = (hf_l, hf_c, hb_l, hb_c, o, head_g)
        mix_specs = _h_specs(tm, True, tile0) * 2 + [row(), _resident((1, D_MODEL))]
    elif kind == 1:
        assert tile0 == 0
        mix_specs = _h_specs(tm, True)
    else:
        bg, u, conv_w, conv_b = mix
        per = tm // SUBLANES
        last = dims.rows // SUBLANES - 1
        mix = (bg, u, u, u, conv_w, conv_b)
        mix_specs = [
            row(), row(),
            pl.BlockSpec((SUBLANES, D_MODEL), lambda i: (jnp.maximum((i + tile0) * per - 1, 0), 0)),
            pl.BlockSpec((SUBLANES, D_MODEL), lambda i: (jnp.minimum((i + tile0 + 1) * per, last), 0)),
            _resident(conv_w.shape), _resident((1, D_MODEL)),
        ]
    assert len(h_parts) == 1 or tile0 == 0
    h_specs = _h_specs(tm, True) if len(h_parts) == 2 else [row()]
    return pl.pallas_call(
        functools.partial(_post_kernel, kind=kind, n_h=len(h_parts), tile0=tile0,
                          final_norm=final_norm, dims=dims, out_layer=out_layer, layer=layer),
        out_shape=jax.ShapeDtypeStruct((n_tiles * tm, D_MODEL), _F32),
        grid=(n_tiles,),
        in_specs=h_specs + [_mod_spec(dims.n_ctx_rows, dims.seq, tm, tile0)] + mix_specs + [
            _weight_spec(),
            _resident((1, D_MODEL)),
            _weight_spec(),
            _weight_spec(),
            _resident((1, D_MODEL)),
        ],
        out_specs=_row_spec(tm, D_MODEL),
        scratch_shapes=[pltpu.VMEM((D_MODEL, D_MODEL), _BF16), pltpu.VMEM((D_MODEL, MLP_HIDDEN), _BF16),
                        pltpu.VMEM((MLP_HIDDEN, D_MODEL), _BF16)] + _stage_scratch(),
        compiler_params=_cparams("arbitrary"),
        name="post",
    )(*h_parts, mod_l, *mix, w_out, g, w1, w2, final_g)


class _Dims:
    def __init__(self, batch, seq, ctx_len):
        self.batch = batch
        self.seq = seq
        self.ctx_len = ctx_len
        self.n_ctx_rows = batch * ctx_len
        self.rows = self.n_ctx_rows + batch * seq


def _rope_tables(dims):
    n_freq = B_HEAD_DIM // 4
    t = np.arange(dims.seq)
    inv_freq = ROPE_BASE ** (-np.arange(n_freq, dtype=np.float64) / n_freq)
    ang_row = (t // GRID_W)[:, None] * inv_freq
    ang_col = (t % GRID_W)[:, None] * inv_freq
    ang = np.concatenate([ang_row, ang_row, ang_col, ang_col], axis=1)
    sign = np.tile(np.concatenate([-np.ones(n_freq), np.ones(n_freq)]), 2)
    cos = np.tile(np.cos(ang), (dims.batch, 1))
    sin = np.tile(np.sin(ang) * sign, (dims.batch, 1))
    cos = np.concatenate([np.ones((dims.n_ctx_rows, B_HEAD_DIM)), cos], axis=0).astype(np.float32)
    sin = np.concatenate([np.zeros((dims.n_ctx_rows, B_HEAD_DIM)), sin], axis=0).astype(np.float32)
    tables = (np.tile(cos, (1, 2)), np.tile(sin, (1, 2)), np.ascontiguousarray(cos.T), np.ascontiguousarray(sin.T))
    return tuple(jnp.asarray(tab) for tab in tables)


def kernel(x, c, ctx, c_ctx, ada_w, ada_b, norm_g, final_g, mlp_w1, mlp_w2,
           a_w_in, a_w_gate, a_b_gate, a_head_g, a_w_out,
           b_w_qkv, b_sinks, b_w_out, c_w_in, c_conv_w, c_conv_b, c_w_out):
    batch, seq, d = x.shape
    ctx_len = ctx.shape[1]
    depth = ada_w.shape[0]
    dims = _Dims(batch, seq, ctx_len)
    assert MLP_HIDDEN_CHUNK == D_MODEL and d == D_MODEL and seq % ROW_TILE == 0 and dims.n_ctx_rows % ROW_TILE == 0
    assert 1 + batch <= MOD_ROWS

    assert dims.n_ctx_rows == ROW_TILE
    h_parts = (x.reshape(-1, d), ctx.reshape(-1, d))
    cs = jnp.concatenate([c_ctx[None], c, jnp.zeros((MOD_ROWS - 1 - batch, d), _F32)], axis=0)
    mod = _ada_table(cs, ada_w, ada_b).reshape(depth, MOD_ROWS, 1, 6 * d)
    rope = _rope_tables(dims)
    fg = final_g.reshape(1, d)

    for l in range(depth):
        kind, j = l % N_MIXERS, l // N_MIXERS
        mod_l = mod[l]
        g0 = norm_g[l, 0].reshape(1, d)
        g1 = norm_g[l, 1].reshape(1, d)
        last_layer = l == depth - 1
        if kind == 0:
            wgt = a_w_gate[j].T.astype(_BF16)
            q, kt, v, o, gr, gc = _a_proj(h_parts, g0, mod_l, a_w_in, j, wgt, a_b_gate[j].reshape(-1, 1), dims)
            hf, hb = _a_scan(q, kt, v, gr, gc, dims)
            mix = (hf, hb, o, a_head_g[j].reshape(1, d))
            w_out = a_w_out
        elif kind == 1:
            (h,) = h_parts
            qt, k, vt = _b_proj(h, g0, mod_l, b_w_qkv, j, rope, dims)
            mix = _b_attn(b_sinks[j], qt, k, vt, dims)
            w_out = b_w_out
        else:
            (h,) = h_parts
            bg, u = _c_proj(h, g0, mod_l, c_w_in, j, dims)
            mix = (bg, u, c_conv_w[j], c_conv_b[j].reshape(1, d))
            w_out = c_w_out
        tile0 = dims.n_ctx_rows // ROW_TILE if last_layer else 0
        h_parts = (_post(kind, h_parts, mod_l, mix, w_out, j, g1, mlp_w1, mlp_w2, l, fg, dims,
                         tile0=tile0, final_norm=last_layer),)
    return h_parts[0].reshape(batch, seq, d)
```

```python
import functools

import jax
import jax.numpy as jnp
import numpy as np
from jax import lax
from jax.experimental import pallas as pl
from jax.experimental.pallas import tpu as pltpu

D_MODEL = 1024
GRID_W = 64
N_MIXERS = 3
A_HEADS = 8
A_QK_DIM = D_MODEL // 2
A_DK = A_QK_DIM // A_HEADS
A_DV = D_MODEL // A_HEADS
A_CHUNK = 128
A_STEP_CHUNKS_CTX = 2
A_STEP_CHUNKS_LAT = 4
B_Q_HEADS = 16
B_KV_HEADS = 4
B_HEAD_DIM = D_MODEL // B_Q_HEADS
B_GROUP = B_Q_HEADS // B_KV_HEADS
B_KV_DIM = B_KV_HEADS * B_HEAD_DIM
B_BLOCK = 128
ROPE_BASE = 10000.0
MLP_HIDDEN = 4 * D_MODEL
NORM_EPS = 1e-6
LOG2E = 1.4426950408889634

LANES = 128
SUBLANES = 8
VMEM_LIMIT_BYTES = 60 * 1024 * 1024

ROW_TILE = 512
MLP_HIDDEN_CHUNK = 1024
MOD_ROWS = 8
OUT_PROJ_COLS = 256
ADA_COL_TILE = 1536

_BF16 = jnp.bfloat16
_F32 = jnp.float32


def _cparams(*sem):
    return pltpu.CompilerParams(dimension_semantics=sem, vmem_limit_bytes=VMEM_LIMIT_BYTES)


def _resident(shape):
    nd = len(shape)
    return pl.BlockSpec(shape, lambda *_: (0,) * nd, pipeline_mode=pl.Buffered(1))


STAGE_COLS = 512
STAGE_DEPTH = 3


def _weight_spec():
    return pl.BlockSpec(memory_space=pl.ANY)


def _stage_scratch():
    return [pltpu.VMEM((STAGE_DEPTH, D_MODEL, STAGE_COLS), _F32), pltpu.SemaphoreType.DMA((STAGE_DEPTH,))]


class _WeightStager:
    def __init__(self, jobs, stage_ref, sem_ref):
        self.jobs, self.stage_ref, self.sem_ref = jobs, stage_ref, sem_ref
        self.started = self.done = 0

    def _copy(self, i):
        src = self.jobs[i][0]
        dst = self.stage_ref.at[i % STAGE_DEPTH, pl.ds(0, src.shape[0]), pl.ds(0, src.shape[1])]
        return pltpu.make_async_copy(src, dst, self.sem_ref.at[i % STAGE_DEPTH])

    def _fill(self):
        while self.started < min(self.done + STAGE_DEPTH, len(self.jobs)):
            self._copy(self.started).start()
            self.started += 1

    def need(self, n):
        self._fill()
        while self.done < n:
            src, store = self.jobs[self.done]
            self._copy(self.done).wait()
            store(self.stage_ref[self.done % STAGE_DEPTH, :src.shape[0], :src.shape[1]])
            self.done += 1
            self._fill()


def _stage_weights(jobs, stage_ref, sem_ref):
    _WeightStager(jobs, stage_ref, sem_ref).need(len(jobs))


def _col_tiles(w_hbm, layer, col0, ncols, store):
    jobs = []
    for c in range(0, ncols, STAGE_COLS):
        width = min(STAGE_COLS, ncols - c)
        jobs.append((w_hbm.at[layer, :, pl.ds(col0 + c, width)], functools.partial(store, c, width)))
    return jobs


def _norm_mod(h, g, shift, scale):
    ms = jnp.mean(h * h, axis=-1, keepdims=True)
    y = h * lax.rsqrt(ms + NORM_EPS) * g
    return y * (1.0 + scale) + shift


def _mod_part(mod_ref, k):
    return mod_ref[:, k * D_MODEL:(k + 1) * D_MODEL]


def _mod_spec(n_ctx_rows, seq, tm, tile0=0):
    def idx(i):
        r0 = (i + tile0) * tm
        grp = jnp.where(r0 < n_ctx_rows, 0, 1 + (r0 - n_ctx_rows) // seq)
        return (grp, 0, 0)
    return pl.BlockSpec((None, 1, 6 * D_MODEL), idx)


def _row_spec(tm, width, tile0=0):
    return pl.BlockSpec((tm, width), lambda i: (i + tile0, 0))


def _h_specs(tm, split, tile0=0):
    if not split:
        return [_row_spec(tm, D_MODEL, tile0)]
    return [pl.BlockSpec((tm, D_MODEL), lambda i: (jnp.maximum(i + tile0 - 1, 0), 0)),
            _resident((tm, D_MODEL))]


def _load_h(h_refs, tile0=0, cols=slice(None)):
    if len(h_refs) == 1:
        return h_refs[0][:, cols]
    lat_ref, ctx_ref = h_refs
    if tile0 > 0:
        return lat_ref[:, cols]
    return jnp.where(pl.program_id(0) == 0, ctx_ref[:, cols], lat_ref[:, cols])


def _ada_kernel(cs_ref, w_ref, b_ref, o_ref):
    cs = cs_ref[...]
    s = (cs * jax.nn.sigmoid(cs)).astype(_BF16)
    o_ref[...] = jnp.dot(s, w_ref[...].astype(_BF16), preferred_element_type=_F32) + b_ref[...]


def _ada_table(cs, ada_w, ada_b):
    depth = ada_w.shape[0]
    n = ada_w.shape[2]
    return pl.pallas_call(
        _ada_kernel,
        out_shape=jax.ShapeDtypeStruct((depth, MOD_ROWS, n), _F32),
        grid=(depth, n // ADA_COL_TILE),
        in_specs=[
            pl.BlockSpec((MOD_ROWS, D_MODEL), lambda l, j: (0, 0)),
            pl.BlockSpec((None, D_MODEL, ADA_COL_TILE), lambda l, j: (l, 0, j)),
            pl.BlockSpec((None, 1, ADA_COL_TILE), lambda l, j: (l, 0, j)),
        ],
        out_specs=pl.BlockSpec((None, MOD_ROWS, ADA_COL_TILE), lambda l, j: (l, 0, j)),
        compiler_params=_cparams("arbitrary", "arbitrary"),
        name="ada_table",
    )(cs, ada_w, ada_b.reshape(depth, 1, n))


def _a_proj_kernel(*refs, n_h, layer):
    h_refs, (g_ref, mod_ref, win_hbm, wgt_ref, bg_ref,
             q_ref, kt_ref, v_ref, o_ref, gr_ref, gc_ref,
             wqvo_ref, wkt_ref, stage_ref, sem_ref) = refs[:n_h], refs[n_h:]

    @pl.when(pl.program_id(0) == 0)
    def _():
        def store_qvo(dst0, c, width, tile):
            wqvo_ref[:, dst0 + c:dst0 + c + width] = tile.astype(_BF16)

        def store_kt(c, width, tile):
            wkt_ref[c:c + width, :] = tile.T.astype(_BF16)

        _stage_weights(
            _col_tiles(win_hbm, layer, 0, A_QK_DIM, functools.partial(store_qvo, 0))
            + _col_tiles(win_hbm, layer, A_QK_DIM, A_QK_DIM, store_kt)
            + _col_tiles(win_hbm, layer, 2 * A_QK_DIM, 2 * D_MODEL, functools.partial(store_qvo, A_QK_DIM)),
            stage_ref, sem_ref)
        wkt_ref[A_QK_DIM:, :] = wgt_ref[...]

    x = _norm_mod(_load_h(h_refs), g_ref[...], _mod_part(mod_ref, 0), _mod_part(mod_ref, 1)).astype(_BF16)
    nt = (((1,), (1,)), ((), ()))
    ktg = lax.dot_general(wkt_ref[...], x, nt, preferred_element_type=_F32)
    kt_ref[...] = ktg[:A_QK_DIM].astype(_BF16)
    gt = ktg[A_QK_DIM:] + bg_ref[...]

    n_chunks = gt.shape[1] // A_CHUNK

    def by_chunk(rows):
        return jnp.concatenate([rows[:, c * A_CHUNK:(c + 1) * A_CHUNK] for c in range(n_chunks)], axis=0)

    def by_token(x):
        return jnp.concatenate([x[c * A_HEADS:(c + 1) * A_HEADS] for c in range(n_chunks)], axis=1)

    def log_sigmoid(z):
        return jnp.minimum(z, 0.0) - jnp.log1p(jnp.exp(-jnp.abs(z)))

    row_form, col_cm, col_b = [], [], []
    for d in range(2):
        li = by_chunk(gt[2 * d * A_HEADS:(2 * d + 1) * A_HEADS]) * LOG2E
        lf = log_sigmoid(by_chunk(gt[(2 * d + 1) * A_HEADS:(2 * d + 2) * A_HEADS])) * LOG2E
        b = _lane_scan(lf, jnp.add, 0.0, d == 1)
        a = li - b
        cm = _lane_scan(a, jnp.maximum, -jnp.inf, d == 1)
        row_form += [by_token(a), by_token(b)]
        col_cm.append(cm)
        col_b.append(b)
    gr_ref[...] = jnp.concatenate(row_form, axis=0)
    pad = jnp.zeros((LANES - 2 * A_HEADS, LANES), _F32)
    for c in range(n_chunks):
        rows = slice(c * A_HEADS, (c + 1) * A_HEADS)
        toks = slice(c * A_CHUNK, (c + 1) * A_CHUNK)
        gc_ref[toks, :LANES] = jnp.concatenate([col_cm[0][rows], col_cm[1][rows], pad], axis=0).T
        gc_ref[toks, LANES:] = jnp.concatenate([col_b[0][rows], col_b[1][rows], pad], axis=0).T

    y = jnp.dot(x, wqvo_ref[...], preferred_element_type=_F32)
    q_ref[...] = (y[:, :A_QK_DIM] * (A_DK ** -0.5)).astype(_BF16)
    v_ref[...] = y[:, A_QK_DIM:A_QK_DIM + D_MODEL].astype(_BF16)
    o_ref[...] = y[:, A_QK_DIM + D_MODEL:]


def _a_proj(h_parts, g, mod_l, w_in, layer, wgt, bgate, dims):
    rows = dims.rows
    tm = ROW_TILE
    n_gate = 4 * A_HEADS
    return pl.pallas_call(
        functools.partial(_a_proj_kernel, n_h=len(h_parts), layer=layer),
        out_shape=(
            jax.ShapeDtypeStruct((rows, A_QK_DIM), _BF16),
            jax.ShapeDtypeStruct((A_QK_DIM, rows), _BF16),
            jax.ShapeDtypeStruct((rows, D_MODEL), _BF16),
            jax.ShapeDtypeStruct((rows, D_MODEL), _F32),
            jax.ShapeDtypeStruct((n_gate, rows), _F32),
            jax.ShapeDtypeStruct((rows, 2 * LANES), _F32),
        ),
        grid=(rows // tm,),
        in_specs=_h_specs(tm, len(h_parts) == 2) + [
            _resident((1, D_MODEL)),
            _mod_spec(dims.n_ctx_rows, dims.seq, tm),
            _weight_spec(),
            _resident(wgt.shape),
            _resident((n_gate, 1)),
        ],
        out_specs=(
            _row_spec(tm, A_QK_DIM),
            pl.BlockSpec((A_QK_DIM, tm), lambda i: (0, i)),
            _row_spec(tm, D_MODEL),
            _row_spec(tm, D_MODEL),
            pl.BlockSpec((n_gate, tm), lambda i: (0, i)),
            _row_spec(tm, 2 * LANES),
        ),
        scratch_shapes=[pltpu.VMEM((D_MODEL, A_QK_DIM + 2 * D_MODEL), _BF16),
                        pltpu.VMEM((A_QK_DIM + 4 * A_HEADS, D_MODEL), _BF16)] + _stage_scratch(),
        compiler_params=_cparams("arbitrary"),
        name="a_proj",
    )(*h_parts, g, mod_l, w_in, wgt, bgate)


def _lane_scan(x, op, fill, reverse):
    lane = lax.broadcasted_iota(jnp.int32, x.shape, 1)
    k = 1
    while k < LANES:
        if reverse:
            shifted = jnp.where(lane < LANES - k, pltpu.roll(x, LANES - k, axis=1), fill)
        else:
            shifted = jnp.where(lane >= k, pltpu.roll(x, k, axis=1), fill)
        x = op(x, shifted)
        k *= 2
    return x


def _a_scan_kernel(*refs, step_chunks, has_init, emit_state):
    n_state = 5
    (qf_ref, ktf_ref, vf_ref, grf_ref, gcf_ref, qb_ref, ktb_ref, vb_ref, grb_ref, gcb_ref), refs = refs[:10], refs[10:]
    init_refs, refs = (refs[:n_state], refs[n_state:]) if has_init else ((), refs)
    (hf_ref, hb_ref), refs = refs[:2], refs[2:]
    final_refs, refs = (refs[:n_state], refs[n_state:]) if emit_state else ((), refs)
    cf_ref, cb_ref, mrf_ref, mrb_ref, mc_ref = state_refs = refs

    @pl.when(pl.program_id(1) == 0)
    def _():
        if has_init:
            for ref, init in zip(state_refs, init_refs):
                ref[...] = init[...]
        else:
            cf_ref[...] = jnp.zeros(cf_ref.shape, _F32)
            cb_ref[...] = jnp.zeros(cb_ref.shape, _F32)
            mrf_ref[...] = jnp.full(mrf_ref.shape, -jnp.inf, _F32)
            mrb_ref[...] = jnp.full(mrb_ref.shape, -jnp.inf, _F32)
            mc_ref[...] = jnp.full(mc_ref.shape, -jnp.inf, _F32)

    t_idx = lax.broadcasted_iota(jnp.int32, (A_CHUNK, A_CHUNK), 0)
    s_idx = lax.broadcasted_iota(jnp.int32, (A_CHUNK, A_CHUNK), 1)
    ones_blk = jnp.ones((A_CHUNK, A_DV), _BF16)
    lane = lax.broadcasted_iota(jnp.int32, (1, LANES), 1)
    for sub in range(step_chunks):
        toks = (slice(sub * A_CHUNK, (sub + 1) * A_CHUNK),
                slice((step_chunks - 1 - sub) * A_CHUNK, (step_chunks - sub) * A_CHUNK))
        _a_chunk_pair(toks, (qf_ref, qb_ref), (ktf_ref, ktb_ref), (vf_ref, vb_ref), (grf_ref, grb_ref),
                      (gcf_ref, gcb_ref), (hf_ref, hb_ref), (cf_ref, cb_ref), (mrf_ref, mrb_ref), mc_ref,
                      t_idx, s_idx, ones_blk, lane)

    if emit_state:
        @pl.when(pl.program_id(1) == pl.num_programs(1) - 1)
        def _():
            for ref, final in zip(state_refs, final_refs):
                final[...] = ref[...]


def _a_chunk_pair(toks, q_refs, kt_refs, v_refs, gr_refs, gc_refs, h_refs, c_refs, mr_refs, mc_ref,
                  t_idx, s_idx, ones_blk, lane):
    m_col = mc_ref[0:1, :]
    m_col_new = []
    dirs = []
    for d in range(2):
        gr_ref, gc_ref, mr_ref, tok = gr_refs[d], gc_refs[d], mr_refs[d], toks[d]
        last = A_CHUNK - 1 if d == 0 else 0
        a = gr_ref[2 * d * A_HEADS:(2 * d + 1) * A_HEADS, tok]
        b = gr_ref[(2 * d + 1) * A_HEADS:(2 * d + 2) * A_HEADS, tok]
        m_row = mr_ref[:, 0:1]
        gg_end = jnp.maximum(m_row, jnp.max(a, axis=1, keepdims=True))
        ws = jnp.exp2(a - gg_end)
        decay = jnp.exp2(m_row - gg_end)
        mr_ref[...] = jnp.broadcast_to(b[:, last:last + 1] + gg_end, mr_ref.shape)
        gg_c = jnp.maximum(m_col, gc_ref[tok, :LANES])
        b_plus_gg = gc_ref[tok, LANES:] + gg_c
        clamp_c = jnp.exp2(-b_plus_gg)
        m_col_new.append(b_plus_gg[last:last + 1, :])
        dirs.append((a, m_row, ws, decay, gg_c, clamp_c))
    mc_ref[...] = jnp.broadcast_to(jnp.where(lane < A_HEADS, m_col_new[0], m_col_new[1]), mc_ref.shape)

    pair_lane = lax.broadcasted_iota(jnp.int32, (A_CHUNK, 2 * A_DK), 1)
    zeros_kt = jnp.zeros((A_DK, A_CHUNK), _BF16)
    zeros_c = jnp.zeros((A_DK, 2 * A_DV), _BF16)
    for hp in range(A_HEADS // 2):
        for d in range(2):
            q_ref, kt_ref, v_ref, h_ref, c_ref, tok = q_refs[d], kt_refs[d], v_refs[d], h_refs[d], c_refs[d], toks[d]
            a, m_row, ws, decay, gg_c, clamp_c = dirs[d]
            mask = (s_idx <= t_idx) if d == 0 else (s_idx >= t_idx)
            heads = (2 * hp, 2 * hp + 1)
            q2 = q_ref[tok, 2 * hp * A_DK:(2 * hp + 2) * A_DK]
            kts = [kt_ref[hd * A_DK:(hd + 1) * A_DK, tok] for hd in heads]
            kt_bd = jnp.concatenate([jnp.concatenate([kts[0], zeros_kt], axis=1),
                                     jnp.concatenate([zeros_kt, kts[1]], axis=1)], axis=0)
            s2 = jnp.dot(q2, kt_bd, preferred_element_type=_F32)
            ggs = [jnp.broadcast_to(gg_c[:, d * A_HEADS + hd:d * A_HEADS + hd + 1], (A_CHUNK, A_CHUNK))
                   for hd in heads]
            e_state = jnp.exp2(jnp.where(pair_lane < A_DK, m_row[heads[0]:heads[0] + 1, :] - ggs[0],
                                         m_row[heads[1]:heads[1] + 1, :] - ggs[1]))
            qe = q2.astype(_F32) * e_state
            for i, hd in enumerate(heads):
                col = d * A_HEADS + hd
                clamp = jnp.broadcast_to(clamp_c[:, col:col + 1], (A_CHUNK, A_DV))
                v = v_ref[tok, hd * A_DV:(hd + 1) * A_DV]
                v_ext = jnp.concatenate([v, ones_blk], axis=1)
                c_old = c_ref[hd]
                e_keys = jnp.exp2(jnp.where(mask, a[hd:hd + 1, :] - ggs[i], -jnp.inf))
                p = jnp.concatenate([s2[:, i * A_CHUNK:(i + 1) * A_CHUNK] * e_keys, qe], axis=1).astype(_BF16)
                c_rows = [c_old.astype(_BF16), zeros_c] if i == 0 else [zeros_c, c_old.astype(_BF16)]
                rhs = jnp.concatenate([v_ext] + c_rows, axis=0)
                num = jnp.dot(p, rhs, preferred_element_type=_F32)
                den = jnp.maximum(jnp.abs(num[:, A_DV:]), clamp)
                h_ref[tok, hd * A_DV:(hd + 1) * A_DV] = num[:, :A_DV] / den
                kw = (kts[i].astype(_F32) * ws[hd:hd + 1, :]).astype(_BF16)
                c_ref[hd] = decay[hd:hd + 1, :] * c_old + jnp.dot(kw, v_ext, preferred_element_type=_F32)


def _a_scan_call(q, kt, v, gr, gc, dims, step_chunks, first_row, seq_rows, init=None, emit_state=False):
    blk_rows = step_chunks * A_CHUNK
    n_steps = seq_rows // blk_rows
    assert seq_rows % blk_rows == 0 and first_row % blk_rows == 0
    blk0 = first_row // blk_rows

    def fwd_blk(b, j):
        return blk0 + b * n_steps + j

    def bwd_blk(b, j):
        return blk0 + b * n_steps + (n_steps - 1 - j)

    def specs(blk):
        return [
            pl.BlockSpec((blk_rows, A_QK_DIM), lambda b, j: (blk(b, j), 0)),
            pl.BlockSpec((A_QK_DIM, blk_rows), lambda b, j: (0, blk(b, j))),
            pl.BlockSpec((blk_rows, D_MODEL), lambda b, j: (blk(b, j), 0)),
            pl.BlockSpec((4 * A_HEADS, blk_rows), lambda b, j: (0, blk(b, j))),
            pl.BlockSpec((blk_rows, 2 * LANES), lambda b, j: (blk(b, j), 0)),
        ]

    state_shapes = [(A_HEADS, A_DK, 2 * A_DV)] * 2 + [(A_HEADS, LANES)] * 3
    state_specs = [pl.BlockSpec((None,) + shp, lambda b, j, n=len(shp): (b,) + (0,) * n) for shp in state_shapes]
    h_shape = jax.ShapeDtypeStruct((dims.batch * seq_rows, D_MODEL), _F32)
    out_shape = [h_shape, h_shape]
    out_specs = [pl.BlockSpec((blk_rows, D_MODEL), lambda b, j: (b * n_steps + j, 0)),
                 pl.BlockSpec((blk_rows, D_MODEL), lambda b, j: (b * n_steps + (n_steps - 1 - j), 0))]
    if emit_state:
        out_shape += [jax.ShapeDtypeStruct((dims.batch,) + shp, _F32) for shp in state_shapes]
        out_specs += state_specs
    return pl.pallas_call(
        functools.partial(_a_scan_kernel, step_chunks=step_chunks, has_init=init is not None,
                          emit_state=emit_state),
        out_shape=tuple(out_shape),
        grid=(dims.batch, n_steps),
        in_specs=specs(fwd_blk) + specs(bwd_blk) + (state_specs if init is not None else []),
        out_specs=tuple(out_specs),
        scratch_shapes=[pltpu.VMEM(shp, _F32) for shp in state_shapes],
        compiler_params=_cparams("arbitrary", "arbitrary"),
        name="a_scan",
    )(q, kt, v, gr, gc, q, kt, v, gr, gc, *(init or ()))


def _a_scan(q, kt, v, gr, gc, dims):
    hf_c, hb_c, *state = _a_scan_call(q, kt, v, gr, gc, dims, A_STEP_CHUNKS_CTX, 0, dims.ctx_len,
                                      emit_state=True)
    hf_l, hb_l = _a_scan_call(q, kt, v, gr, gc, dims, A_STEP_CHUNKS_LAT, dims.n_ctx_rows, dims.seq,
                              init=state)
    return (hf_l, hf_c), (hb_l, hb_c)


def _a_mix(hf_lat_ref, hf_ctx_ref, hb_lat_ref, hb_ctx_ref, o_ref, hg_ref, tile0, cols):
    hs = _load_h((hf_lat_ref, hf_ctx_ref), tile0, cols) + _load_h((hb_lat_ref, hb_ctx_ref), tile0, cols)
    parts = []
    for c0 in range(0, hs.shape[1], A_DV):
        x = hs[:, c0:c0 + A_DV]
        parts.append(x * lax.rsqrt(jnp.mean(x * x, axis=-1, keepdims=True) + NORM_EPS))
    y = jnp.concatenate(parts, axis=1) * hg_ref[:, cols]
    return (jax.nn.sigmoid(o_ref[:, cols]) * y).astype(_BF16)


ROPE_HALF = B_HEAD_DIM // 4


def _swap_halves_lanes(x):
    lane = lax.broadcasted_iota(jnp.int32, x.shape, 1)
    fwd = pltpu.roll(x, LANES - ROPE_HALF, axis=1)
    back = pltpu.roll(x, ROPE_HALF, axis=1)
    return jnp.where(lane % (2 * ROPE_HALF) < ROPE_HALF, fwd, back)


def _swap_halves_rows(x):
    parts = []
    for r0 in range(0, x.shape[0], 2 * ROPE_HALF):
        parts += [x[r0 + ROPE_HALF:r0 + 2 * ROPE_HALF], x[r0:r0 + ROPE_HALF]]
    return jnp.concatenate(parts, axis=0)


def _b_proj_kernel(h_ref, g_ref, mod_ref, wqkv_hbm, cos_ref, sin_ref, cost_ref, sint_ref,
                   qt_ref, k_ref, vt_ref, wqvt_ref, wk_ref, stage_ref, sem_ref, *, layer):
    @pl.when(pl.program_id(0) == 0)
    def _():
        def store_t(dst0, c, width, tile):
            wqvt_ref[dst0 + c:dst0 + c + width, :] = tile.T.astype(_BF16)

        def store_k(c, width, tile):
            wk_ref[:, c:c + width] = tile.astype(_BF16)

        _stage_weights(
            _col_tiles(wqkv_hbm, layer, 0, D_MODEL, functools.partial(store_t, 0))
            + _col_tiles(wqkv_hbm, layer, D_MODEL, B_KV_DIM, store_k)
            + _col_tiles(wqkv_hbm, layer, D_MODEL + B_KV_DIM, B_KV_DIM, functools.partial(store_t, D_MODEL)),
            stage_ref, sem_ref)

    x = _norm_mod(h_ref[...], g_ref[...], _mod_part(mod_ref, 0), _mod_part(mod_ref, 1)).astype(_BF16)
    nt = (((1,), (1,)), ((), ()))
    yt = lax.dot_general(wqvt_ref[...], x, nt, preferred_element_type=_F32)
    cost, sint = cost_ref[...], sint_ref[...]
    for hd in range(B_Q_HEADS):
        rows = slice(hd * B_HEAD_DIM, (hd + 1) * B_HEAD_DIM)
        qt = yt[rows, :]
        qt = (qt * cost + _swap_halves_rows(qt) * sint) * (B_HEAD_DIM ** -0.5 * LOG2E)
        qt_ref[rows, :] = qt.astype(_BF16)
    vt_ref[...] = yt[D_MODEL:, :].astype(_BF16)
    cos, sin = cos_ref[...], sin_ref[...]
    k = jnp.dot(x, wk_ref[...], preferred_element_type=_F32)
    for c0 in range(0, B_KV_DIM, LANES):
        kc = k[:, c0:c0 + LANES]
        k_ref[:, c0:c0 + LANES] = (kc * cos + _swap_halves_lanes(kc) * sin).astype(_BF16)


def _b_proj(h, g, mod_l, w_qkv, layer, rope, dims):
    rows = h.shape[0]
    tm = ROW_TILE
    cos, sin, cost, sint = rope
    return pl.pallas_call(
        functools.partial(_b_proj_kernel, layer=layer),
        out_shape=(
            jax.ShapeDtypeStruct((D_MODEL, rows), _BF16),
            jax.ShapeDtypeStruct((rows, B_KV_DIM), _BF16),
            jax.ShapeDtypeStruct((B_KV_DIM, rows), _BF16),
        ),
        grid=(rows // tm,),
        in_specs=[
            _row_spec(tm, D_MODEL),
            _resident((1, D_MODEL)),
            _mod_spec(dims.n_ctx_rows, dims.seq, tm),
            _weight_spec(),
            _row_spec(tm, LANES),
            _row_spec(tm, LANES),
            pl.BlockSpec((B_HEAD_DIM, tm), lambda i: (0, i)),
            pl.BlockSpec((B_HEAD_DIM, tm), lambda i: (0, i)),
        ],
        out_specs=(
            pl.BlockSpec((D_MODEL, tm), lambda i: (0, i)),
            _row_spec(tm, B_KV_DIM),
            pl.BlockSpec((B_KV_DIM, tm), lambda i: (0, i)),
        ),
        scratch_shapes=[pltpu.VMEM((D_MODEL + B_KV_DIM, D_MODEL), _BF16),
                        pltpu.VMEM((D_MODEL, B_KV_DIM), _BF16)] + _stage_scratch(),
        compiler_params=_cparams("arbitrary"),
        name="b_proj",
    )(h, g, mod_l, w_qkv, cos, sin, cost, sint)


B_ONES_ROWS = 16
B_STEP_BLOCKS_CTX = 2
B_STEP_BLOCKS_LAT = 4


def _b_attn_kernel(sink_ref, win_l_ref, win_r_ref, qt_ref, kl_ref, kc_ref, kr_ref, kx_ref,
                   vtl_ref, vtc_ref, vtr_ref, vtx_ref, o_ref, s_ref,
                   *, n_ctx_blocks, blocks_per_seq, ctx_len, step_blocks, first_block):
    first = first_block + pl.program_id(0) * step_blocks
    is_lat = first >= n_ctx_blocks
    n = (first - n_ctx_blocks) % blocks_per_seq
    neg = -jnp.inf
    win_l = jnp.where(is_lat, win_l_ref[...], neg)
    win_r = jnp.where(is_lat, win_r_ref[...], neg)
    bias_c = jnp.where(is_lat, 0.0, neg)
    bias_first_l = jnp.where(jnp.logical_and(is_lat, n >= 1), win_l_ref[...], neg)
    bias_last_r = jnp.where(jnp.logical_and(is_lat, n + step_blocks <= blocks_per_seq - 1), win_r_ref[...], neg)

    n_q = B_GROUP * B_BLOCK
    head_of_lane = lax.broadcasted_iota(jnp.int32, (1, n_q), 1) // B_BLOCK
    ones_rows = jnp.ones((B_ONES_ROWS, B_BLOCK), _BF16)
    n_ctx_tiles = ctx_len // B_BLOCK

    def local_tiles(blk):
        if first_block < n_ctx_blocks:
            return []
        own = slice(blk * B_BLOCK, (blk + 1) * B_BLOCK)
        if blk == 0:
            left = (kl_ref, vtl_ref, slice(0, B_BLOCK), bias_first_l)
        else:
            left = (kc_ref, vtc_ref, slice((blk - 1) * B_BLOCK, blk * B_BLOCK), win_l)
        if blk == step_blocks - 1:
            right = (kr_ref, vtr_ref, slice(0, B_BLOCK), bias_last_r)
        else:
            right = (kc_ref, vtc_ref, slice((blk + 1) * B_BLOCK, (blk + 2) * B_BLOCK), win_r)
        return [left, (kc_ref, vtc_ref, own, bias_c), right]

    def scores(blk, g):
        cols = slice(blk * B_BLOCK, (blk + 1) * B_BLOCK)
        qt = jnp.concatenate(
            [qt_ref[(g * B_GROUP + j) * B_HEAD_DIM:(g * B_GROUP + j + 1) * B_HEAD_DIM, cols]
             for j in range(B_GROUP)], axis=1)
        sink = jnp.zeros((1, n_q), _F32)
        for j in range(B_GROUP):
            sink = jnp.where(head_of_lane == j, sink_ref[g * B_GROUP + j] * LOG2E, sink)
        ks = slice(g * B_HEAD_DIM, (g + 1) * B_HEAD_DIM)
        k_tiles = [(k_ref[rows, ks], bias) for k_ref, _, rows, bias in local_tiles(blk)]
        for t in range(n_ctx_tiles):
            k_tiles.append((kx_ref[t * B_BLOCK:(t + 1) * B_BLOCK, ks], None))
        m_tile = None
        s_all = jnp.dot(jnp.concatenate([k for k, _ in k_tiles], axis=0), qt, preferred_element_type=_F32)
        for t, (_, bias) in enumerate(k_tiles):
            s = s_all[t * B_BLOCK:(t + 1) * B_BLOCK]
            if bias is not None:
                s = s + bias
            s_ref[blk, g, t] = s
            m_tile = s if m_tile is None else jnp.maximum(m_tile, s)
        return jnp.maximum(sink, jnp.max(m_tile, axis=0, keepdims=True)), sink

    def attend(blk, g, m, sink):
        ks = slice(g * B_HEAD_DIM, (g + 1) * B_HEAD_DIM)
        vt_tiles = [vt_ref[ks, cols] for _, vt_ref, cols, _ in local_tiles(blk)]
        for t in range(n_ctx_tiles):
            vt_tiles.append(vtx_ref[ks, t * B_BLOCK:(t + 1) * B_BLOCK])
        acc = jnp.zeros((B_HEAD_DIM + B_ONES_ROWS, n_q), _F32)
        for t, vt in enumerate(vt_tiles):
            p = jnp.exp2(s_ref[blk, g, t] - m).astype(_BF16)
            vt_ext = jnp.concatenate([vt, ones_rows], axis=0)
            acc = acc + jnp.dot(vt_ext, p, preferred_element_type=_F32)
        denom = jnp.exp2(sink - m) + acc[B_HEAD_DIM:B_HEAD_DIM + 1, :]
        out_t = acc[:B_HEAD_DIM, :] * (1.0 / denom)
        rows = slice(blk * B_BLOCK, (blk + 1) * B_BLOCK)
        for pair in range(B_GROUP // 2):
            two = jnp.concatenate([out_t[:, (2 * pair) * B_BLOCK:(2 * pair + 1) * B_BLOCK],
                                   out_t[:, (2 * pair + 1) * B_BLOCK:(2 * pair + 2) * B_BLOCK]], axis=0)
            c0 = (g * B_GROUP + 2 * pair) * B_HEAD_DIM
            o_ref[rows, c0:c0 + 2 * B_HEAD_DIM] = two.T.astype(_BF16)

    units = [(blk, g) for blk in range(step_blocks) for g in range(B_KV_HEADS)]
    pending = [scores(*units[0]), scores(*units[1])]
    for i, unit in enumerate(units):
        if i + 2 < len(units):
            pending.append(scores(*units[i + 2]))
        attend(*unit, *pending[i])


def _window_bias():
    key = np.arange(B_BLOCK)[:, None]
    qry = np.arange(B_BLOCK)[None, :]
    prev_blk = np.where(key >= qry, 0.0, -np.inf).astype(np.float32)
    next_blk = np.where(key <= qry, 0.0, -np.inf).astype(np.float32)
    return jnp.asarray(np.tile(prev_blk, (1, B_GROUP))), jnp.asarray(np.tile(next_blk, (1, B_GROUP)))


def _b_attn_call(sinks, qt, k, vt, dims, step_blocks, first_block, n_blocks):
    step_rows = step_blocks * B_BLOCK
    ncc = dims.ctx_len // B_BLOCK
    ncl = dims.seq // B_BLOCK
    lat0 = dims.batch * ncc
    nblk = k.shape[0] // B_BLOCK
    assert first_block % step_blocks == 0 and n_blocks % step_blocks == 0
    assert ncc % step_blocks == 0 if first_block < lat0 else ncl % step_blocks == 0
    assert first_block >= lat0 or first_block + n_blocks <= lat0
    off = first_block // step_blocks

    def batch_of(s):
        i = first_block + s * step_blocks
        return jnp.where(i < lat0, i // ncc, (i - lat0) // ncl)

    def left(s):
        return jnp.maximum(first_block + s * step_blocks - 1, 0)

    def right(s):
        return jnp.minimum(first_block + (s + 1) * step_blocks, nblk - 1)

    n_tiles = 3 + ncc
    return pl.pallas_call(
        functools.partial(_b_attn_kernel, n_ctx_blocks=lat0, blocks_per_seq=ncl, ctx_len=dims.ctx_len,
                          step_blocks=step_blocks, first_block=first_block),
        out_shape=jax.ShapeDtypeStruct((n_blocks * B_BLOCK, D_MODEL), _BF16),
        grid=(n_blocks // step_blocks,),
        in_specs=[
            pl.BlockSpec(memory_space=pltpu.SMEM),
            _resident((B_BLOCK, B_GROUP * B_BLOCK)),
            _resident((B_BLOCK, B_GROUP * B_BLOCK)),
            pl.BlockSpec((D_MODEL, step_rows), lambda s: (0, off + s)),
            pl.BlockSpec((B_BLOCK, B_KV_DIM), lambda s: (left(s), 0)),
            pl.BlockSpec((step_rows, B_KV_DIM), lambda s: (off + s, 0)),
            pl.BlockSpec((B_BLOCK, B_KV_DIM), lambda s: (right(s), 0)),
            pl.BlockSpec((dims.ctx_len, B_KV_DIM), lambda s: (batch_of(s), 0)),
            pl.BlockSpec((B_KV_DIM, B_BLOCK), lambda s: (0, left(s))),
            pl.BlockSpec((B_KV_DIM, step_rows), lambda s: (0, off + s)),
            pl.BlockSpec((B_KV_DIM, B_BLOCK), lambda s: (0, right(s))),
            pl.BlockSpec((B_KV_DIM, dims.ctx_len), lambda s: (0, batch_of(s))),
        ],
        out_specs=pl.BlockSpec((step_rows, D_MODEL), lambda s: (s, 0)),
        scratch_shapes=[pltpu.VMEM((step_blocks, B_KV_HEADS, n_tiles, B_BLOCK, B_GROUP * B_BLOCK), _F32)],
        compiler_params=_cparams("arbitrary"),
        name="b_attn",
    )(sinks, *_window_bias(), qt, k, k, k, k, vt, vt, vt, vt)


def _b_attn(sinks, qt, k, vt, dims):
    lat0 = dims.n_ctx_rows // B_BLOCK
    nblk = k.shape[0] // B_BLOCK
    y_ctx = _b_attn_call(sinks, qt, k, vt, dims, B_STEP_BLOCKS_CTX, 0, lat0)
    y_lat = _b_attn_call(sinks, qt, k, vt, dims, B_STEP_BLOCKS_LAT, lat0, nblk - lat0)
    return y_lat, y_ctx


def _c_proj_kernel(h_hbm, g_ref, mod_hbm, win_hbm, bg_hbm, u_hbm, w_ref, stage_ref, sem_ref, *, layer, dims):
    def store(c, width, tile):
        w_ref[:, c:c + width] = tile.astype(_BF16)

    _stage_weights(_col_tiles(win_hbm, layer, 0, 3 * D_MODEL, store), stage_ref, sem_ref)

    def tile_body(h_ref, mod_ref, bg_ref, u_ref):
        x = _norm_mod(h_ref[...], g_ref[...], _mod_part(mod_ref, 0), _mod_part(mod_ref, 1)).astype(_BF16)
        y = jnp.dot(x, w_ref[...], preferred_element_type=_F32)
        bg_ref[...] = y[:, :D_MODEL]
        u_ref[...] = y[:, D_MODEL:2 * D_MODEL] * y[:, 2 * D_MODEL:]

    tm = ROW_TILE
    pltpu.emit_pipeline(
        tile_body,
        grid=(h_hbm.shape[0] // tm,),
        in_specs=[pl.BlockSpec((tm, D_MODEL), lambda i: (i, 0), pipeline_mode=pl.Buffered(3)),
                  _mod_spec(dims.n_ctx_rows, dims.seq, tm)],
        out_specs=[_row_spec(tm, D_MODEL)] * 2,
    )(h_hbm, mod_hbm, bg_hbm, u_hbm)


def _c_proj(h, g, mod_l, w_in, layer, dims):
    rows = h.shape[0]
    in_hbm = pl.BlockSpec(memory_space=pl.ANY)
    return pl.pallas_call(
        functools.partial(_c_proj_kernel, layer=layer, dims=dims),
        out_shape=(jax.ShapeDtypeStruct((rows, D_MODEL), _F32),) * 2,
        in_specs=[in_hbm, pl.BlockSpec(memory_space=pltpu.VMEM), in_hbm, _weight_spec()],
        out_specs=(in_hbm,) * 2,
        scratch_shapes=[pltpu.VMEM((D_MODEL, 3 * D_MODEL), _BF16)] + _stage_scratch(),
        compiler_params=_cparams(),
        name="c_proj",
    )(h, g, mod_l, w_in)


def _c_mix(bg_ref, u_ref, up_ref, un_ref, cw_ref, cb_ref, tile, dims, cols):
    tm = u_ref.shape[0]
    u = u_ref[:, cols]
    row = lax.broadcasted_iota(jnp.int32, (tm, 1), 0)
    g_row = tile * tm + row
    in_ctx = g_row < dims.n_ctx_rows
    pos = jnp.where(in_ctx, g_row % dims.ctx_len, (g_row - dims.n_ctx_rows) % dims.seq)
    length = jnp.where(in_ctx, dims.ctx_len, dims.seq)
    prev = jnp.where(row == 0, up_ref[SUBLANES - 1:SUBLANES, cols], pltpu.roll(u, 1, axis=0))
    nxt = jnp.where(row == tm - 1, un_ref[0:1, cols], pltpu.roll(u, tm - 1, axis=0))
    prev = jnp.where(pos == 0, 0.0, prev)
    nxt = jnp.where(pos == length - 1, 0.0, nxt)
    conv = prev * cw_ref[0:1, cols] + u * cw_ref[1:2, cols] + nxt * cw_ref[2:3, cols] + cb_ref[:, cols]
    return (bg_ref[:, cols] * conv).astype(_BF16)


_N_MIX_REFS = (6, 2, 6)


def _post_kernel(*refs, kind, n_h, tile0, final_norm, dims, out_layer, layer):
    n_mix = _N_MIX_REFS[kind]
    h_refs, mod_ref = refs[:n_h], refs[n_h]
    mix_refs = refs[n_h + 1:n_h + 1 + n_mix]
    (wo_hbm, g_ref, w1_hbm, w2_hbm, fg_ref, out_ref,
     wo_ref, w1_ref, w2_ref, stage_ref, sem_ref) = refs[n_h + 1 + n_mix:]
    n_chunks = MLP_HIDDEN // MLP_HIDDEN_CHUNK
    tiles_wo = D_MODEL // STAGE_COLS
    tiles_w1 = MLP_HIDDEN_CHUNK // STAGE_COLS
    tiles_w2 = D_MODEL // STAGE_COLS

    def compute(need):
        proj = None
        for c0 in range(0, D_MODEL, OUT_PROJ_COLS):
            cols = slice(c0, c0 + OUT_PROJ_COLS)
            if kind == 0:
                y = _a_mix(*mix_refs, tile0, cols)
            elif kind == 1:
                y = _load_h(mix_refs, 0, cols)
            else:
                y = _c_mix(*mix_refs, pl.program_id(0) + tile0, dims, cols)
            if c0 == 0:
                need(tiles_wo)
            part = jnp.dot(y, wo_ref[cols, :], preferred_element_type=_F32)
            proj = part if proj is None else proj + part
        h = _load_h(h_refs, tile0) + _mod_part(mod_ref, 2) * proj
        x = _norm_mod(h, g_ref[...], _mod_part(mod_ref, 3), _mod_part(mod_ref, 4)).astype(_BF16)
        acc = jnp.zeros(h.shape, _F32)
        for c in range(n_chunks):
            cols = slice(c * MLP_HIDDEN_CHUNK, (c + 1) * MLP_HIDDEN_CHUNK)
            need(tiles_wo + c * (tiles_w1 + tiles_w2) + tiles_w1)
            u = jnp.dot(x, w1_ref[:, cols], preferred_element_type=_F32)
            u = jnp.square(jnp.maximum(u, 0.0)).astype(_BF16)
            need(tiles_wo + (c + 1) * (tiles_w1 + tiles_w2))
            acc = acc + jnp.dot(u, w2_ref[cols, :], preferred_element_type=_F32)
        out = h + _mod_part(mod_ref, 5) * acc
        if final_norm:
            ms = jnp.mean(out * out, axis=-1, keepdims=True)
            out = out * lax.rsqrt(ms + NORM_EPS) * fg_ref[...]
        out_ref[...] = out

    @pl.when(pl.program_id(0) == 0)
    def _():
        def store_to(dst_ref, row0, c, width, tile):
            dst_ref[row0:row0 + tile.shape[0], c:c + width] = tile.astype(_BF16)

        jobs = _col_tiles(wo_hbm, out_layer, 0, D_MODEL, functools.partial(store_to, wo_ref, 0))
        for r0 in range(0, MLP_HIDDEN, MLP_HIDDEN_CHUNK):
            jobs += _col_tiles(w1_hbm, layer, r0, MLP_HIDDEN_CHUNK,
                               lambda c, width, tile, r0=r0: store_to(w1_ref, 0, r0 + c, width, tile))
            for c in range(0, D_MODEL, STAGE_COLS):
                jobs.append((w2_hbm.at[layer, pl.ds(r0, MLP_HIDDEN_CHUNK), pl.ds(c, STAGE_COLS)],
                             functools.partial(store_to, w2_ref, r0, c, STAGE_COLS)))
        compute(_WeightStager(jobs, stage_ref, sem_ref).need)

    @pl.when(pl.program_id(0) != 0)
    def _():
        compute(lambda n: None)


def _post(kind, h_parts, mod_l, mix, w_out, out_layer, g, w1, w2, layer, final_g, dims, tile0=0,
          final_norm=False):
    tm = ROW_TILE
    n_tiles = dims.rows // tm - tile0
    row = functools.partial(_row_spec, tm, D_MODEL, tile0)
    if kind == 0:
        (hf_l, hf_c), (hb_l, hb_c), o, head_g = mix
        mix = (hf_l, hf_c, hb_l, hb_c, o, head_g)
        mix_specs = _h_specs(tm, True, tile0) * 2 + [row(), _resident((1, D_MODEL))]
    elif kind == 1:
        assert tile0 == 0
        mix_specs = _h_specs(tm, True)
    else:
        bg, u, conv_w, conv_b = mix
        per = tm // SUBLANES
        last = dims.rows // SUBLANES - 1
        mix = (bg, u, u, u, conv_w, conv_b)
        mix_specs = [
            row(), row(),
            pl.BlockSpec((SUBLANES, D_MODEL), lambda i: (jnp.maximum((i + tile0) * per - 1, 0), 0)),
            pl.BlockSpec((SUBLANES, D_MODEL), lambda i: (jnp.minimum((i + tile0 + 1) * per, last), 0)),
            _resident(conv_w.shape), _resident((1, D_MODEL)),
        ]
    assert len(h_parts) == 1 or tile0 == 0
    h_specs = _h_specs(tm, True) if len(h_parts) == 2 else [row()]
    return pl.pallas_call(
        functools.partial(_post_kernel, kind=kind, n_h=len(h_parts), tile0=tile0,
                          final_norm=final_norm, dims=dims, out_layer=out_layer, layer=layer),
        out_shape=jax.ShapeDtypeStruct((n_tiles * tm, D_MODEL), _F32),
        grid=(n_tiles,),
        in_specs=h_specs + [_mod_spec(dims.n_ctx_rows, dims.seq, tm, tile0)] + mix_specs + [
            _weight_spec(),
            _resident((1, D_MODEL)),
            _weight_spec(),
            _weight_spec(),
            _resident((1, D_MODEL)),
        ],
        out_specs=_row_spec(tm, D_MODEL),
        scratch_shapes=[pltpu.VMEM((D_MODEL, D_MODEL), _BF16), pltpu.VMEM((D_MODEL, MLP_HIDDEN), _BF16),
                        pltpu.VMEM((MLP_HIDDEN, D_MODEL), _BF16)] + _stage_scratch(),
        compiler_params=_cparams("arbitrary"),
        name="post",
    )(*h_parts, mod_l, *mix, w_out, g, w1, w2, final_g)


class _Dims:
    def __init__(self, batch, seq, ctx_len):
        self.batch = batch
        self.seq = seq
        self.ctx_len = ctx_len
        self.n_ctx_rows = batch * ctx_len
        self.rows = self.n_ctx_rows + batch * seq


def _rope_tables(dims):
    n_freq = B_HEAD_DIM // 4
    t = np.arange(dims.seq)
    inv_freq = ROPE_BASE ** (-np.arange(n_freq, dtype=np.float64) / n_freq)
    ang_row = (t // GRID_W)[:, None] * inv_freq
    ang_col = (t % GRID_W)[:, None] * inv_freq
    ang = np.concatenate([ang_row, ang_row, ang_col, ang_col], axis=1)
    sign = np.tile(np.concatenate([-np.ones(n_freq), np.ones(n_freq)]), 2)
    cos = np.tile(np.cos(ang), (dims.batch, 1))
    sin = np.tile(np.sin(ang) * sign, (dims.batch, 1))
    cos = np.concatenate([np.ones((dims.n_ctx_rows, B_HEAD_DIM)), cos], axis=0).astype(np.float32)
    sin = np.concatenate([np.zeros((dims.n_ctx_rows, B_HEAD_DIM)), sin], axis=0).astype(np.float32)
    tables = (np.tile(cos, (1, 2)), np.tile(sin, (1, 2)), np.ascontiguousarray(cos.T), np.ascontiguousarray(sin.T))
    return tuple(jnp.asarray(tab) for tab in tables)


def kernel(x, c, ctx, c_ctx, ada_w, ada_b, norm_g, final_g, mlp_w1, mlp_w2,
           a_w_in, a_w_gate, a_b_gate, a_head_g, a_w_out,
           b_w_qkv, b_sinks, b_w_out, c_w_in, c_conv_w, c_conv_b, c_w_out):
    batch, seq, d = x.shape
    ctx_len = ctx.shape[1]
    depth = ada_w.shape[0]
    dims = _Dims(batch, seq, ctx_len)
    assert MLP_HIDDEN_CHUNK == D_MODEL and d == D_MODEL and seq % ROW_TILE == 0 and dims.n_ctx_rows % ROW_TILE == 0
    assert 1 + batch <= MOD_ROWS

    assert dims.n_ctx_rows == ROW_TILE
    h_parts = (x.reshape(-1, d), ctx.reshape(-1, d))
    cs = jnp.concatenate([c_ctx[None], c, jnp.zeros((MOD_ROWS - 1 - batch, d), _F32)], axis=0)
    mod = _ada_table(cs, ada_w, ada_b).reshape(depth, MOD_ROWS, 1, 6 * d)
    rope = _rope_tables(dims)
    fg = final_g.reshape(1, d)

    for l in range(depth):
        kind, j = l % N_MIXERS, l // N_MIXERS
        mod_l = mod[l]
        g0 = norm_g[l, 0].reshape(1, d)
        g1 = norm_g[l, 1].reshape(1, d)
        last_layer = l == depth - 1
        if kind == 0:
            wgt = a_w_gate[j].T.astype(_BF16)
            q, kt, v, o, gr, gc = _a_proj(h_parts, g0, mod_l, a_w_in, j, wgt, a_b_gate[j].reshape(-1, 1), dims)
            hf, hb = _a_scan(q, kt, v, gr, gc, dims)
            mix = (hf, hb, o, a_head_g[j].reshape(1, d))
            w_out = a_w_out
        elif kind == 1:
            (h,) = h_parts
            qt, k, vt = _b_proj(h, g0, mod_l, b_w_qkv, j, rope, dims)
            mix = _b_attn(b_sinks[j], qt, k, vt, dims)
            w_out = b_w_out
        else:
            (h,) = h_parts
            bg, u = _c_proj(h, g0, mod_l, c_w_in, j, dims)
            mix = (bg, u, c_conv_w[j], c_conv_b[j].reshape(1, d))
            w_out = c_w_out
        tile0 = dims.n_ctx_rows // ROW_TILE if last_layer else 0
        h_parts = (_post(kind, h_parts, mod_l, mix, w_out, j, g1, mlp_w1, mlp_w2, l, fg, dims,
                         tile0=tile0, final_norm=last_layer),)
    return h_parts[0].reshape(batch, seq, d)
```
